```python
import jax, jax.numpy as jnp
from jax import lax
import numpy as np

D_MODEL = 1024
BATCH = 8
SEQ = 2048
DEPTH = 1
DEC_BATCH = 128
DEC_SEQ = 1
PAST_LEN = 8192
PAGE_SIZE = 128

POOL_WINDOWS = (2, 4, 8, 16)
POOL_GROUPS = len(POOL_WINDOWS)
POOL_GROUP_WIDTH = 128
POOL_WIDTH = POOL_GROUPS * POOL_GROUP_WIDTH
POOL_BUF = max(POOL_WINDOWS) - 1
N_HEADS = 8
N_KV_HEADS = 2
HEAD_DIM = 64
GROUP = N_HEADS // N_KV_HEADS
Q_WIDTH = N_HEADS * HEAD_DIM
KV_WIDTH = N_KV_HEADS * HEAD_DIM
WINDOW = 128
BLOCK = WINDOW
D_FF = ((8 * D_MODEL + 3 * 256 - 1) // (3 * 256)) * 256
IN_WIDTH = POOL_WIDTH + Q_WIDTH + 2 * KV_WIDTH + 2 * D_MODEL
EPS = 1e-6
NEG = -1e30

kernel_name = "gated_pool_swa_hybrid_step"


def rms_norm(x, g):
    xf = x.astype(jnp.float32)
    y = xf * lax.rsqrt(jnp.mean(xf * xf, axis=-1, keepdims=True) + EPS)
    return (y * g.astype(jnp.float32)).astype(x.dtype)


def split_in(h, w_in, q_norm, k_norm):
    z = h @ w_in
    idx = np.cumsum([POOL_WIDTH, Q_WIDTH, KV_WIDTH, KV_WIDTH, D_MODEL]).tolist()
    u, q, k, v, ga, gb = jnp.split(z, idx, axis=-1)
    lead = z.shape[:-1]
    q = rms_norm(q.reshape(*lead, N_HEADS, HEAD_DIM), q_norm)
    k = rms_norm(k.reshape(*lead, N_KV_HEADS, HEAD_DIM), k_norm)
    v = v.reshape(*lead, N_KV_HEADS, HEAD_DIM)
    return u, q, k, v, ga, gb


def pool_mix(u_ext, pos, mix_w, scale):
    n, _, c = u_ext.shape
    t = pos.shape[0]
    uf = u_ext.astype(jnp.float32)
    cs = jnp.concatenate([jnp.zeros((n, 1, c), jnp.float32), jnp.cumsum(uf, axis=1)], axis=1)
    end = cs[:, POOL_BUF + 1:]
    u_new = uf[:, POOL_BUF:]
    outs = []
    for g, w in enumerate(POOL_WINDOWS):
        sl = slice(g * POOL_GROUP_WIDTH, (g + 1) * POOL_GROUP_WIDTH)
        start = cs[:, POOL_BUF + 1 - w: POOL_BUF + 1 - w + t, sl]
        cnt = jnp.minimum(pos + 1, w).astype(jnp.float32)[None, :, None]
        outs.append((end[..., sl] - start) / cnt - u_new[..., sl])
    y = jnp.stack(outs, axis=2).astype(u_ext.dtype)
    y = jnp.einsum("ntgc,gcd->ntgd", y, mix_w).reshape(n, t, POOL_WIDTH)
    return y * scale


def sink_attention(q, k, v, mask, sinks):
    n, tq = q.shape[:2]
    qg = q.reshape(n, tq, N_KV_HEADS, GROUP, HEAD_DIM)
    s = jnp.einsum("nqkgd,nskd->nkgqs", qg, k).astype(jnp.float32) * (HEAD_DIM ** -0.5)
    s = jnp.where(mask, s, NEG)
    sink = sinks.astype(jnp.float32).reshape(N_KV_HEADS, GROUP)[None, :, :, None, None]
    m = jnp.maximum(jnp.max(s, axis=-1, keepdims=True), sink)
    p = jnp.exp(s - m)
    denom = jnp.sum(p, axis=-1, keepdims=True) + jnp.exp(sink - m)
    o = jnp.einsum("nkgqs,nskd->nqkgd", (p / denom).astype(v.dtype), v)
    return o.reshape(n, tq, Q_WIDTH)


def merge_branches(y_pool, y_attn, ga, gb, w_pool_proj, w_attn_proj, w_out):
    merged = jax.nn.sigmoid(ga) * (y_pool @ w_pool_proj) + jax.nn.sigmoid(gb) * (y_attn @ w_attn_proj)
    return merged @ w_out


def swiglu(h, w_gate, w_up, w_down):
    return (jax.nn.silu(h @ w_gate) * (h @ w_up)) @ w_down


def prompt_layer(x, w_buf, norm1, w_in, q_norm, k_norm, sinks, pool_mix_w, pool_scale,
                 w_pool_proj, w_attn_proj, w_out, norm2, w_gate, w_up, w_down):
    b, s, _ = x.shape
    h = rms_norm(x, norm1)
    u, q, k, v, ga, gb = split_in(h, w_in, q_norm, k_norm)
    pos = jnp.arange(s)
    u_ext = jnp.concatenate([jnp.zeros((b, POOL_BUF, POOL_WIDTH), u.dtype), u], axis=1)
    y_pool = pool_mix(u_ext, pos, pool_mix_w, pool_scale)
    nb = s // BLOCK
    qb = q.reshape(b * nb, BLOCK, N_HEADS, HEAD_DIM)

    def band(a):
        ab = a.reshape(b, nb, BLOCK, N_KV_HEADS, HEAD_DIM)
        prev = jnp.concatenate([jnp.zeros_like(ab[:, :1]), ab[:, :-1]], axis=1)
        return jnp.concatenate([prev, ab], axis=2).reshape(b * nb, 2 * BLOCK, N_KV_HEADS, HEAD_DIM)

    kk, vv = band(k), band(v)
    blk = jnp.arange(nb)[:, None]
    qpos = blk * BLOCK + jnp.arange(BLOCK)[None, :]
    kpos = (blk - 1) * BLOCK + jnp.arange(2 * BLOCK)[None, :]
    d = qpos[:, :, None] - kpos[:, None, :]
    mask = (kpos[:, None, :] >= 0) & (d >= 0) & (d <= WINDOW)
    mask = jnp.broadcast_to(mask[None], (b, nb, BLOCK, 2 * BLOCK)).reshape(b * nb, 1, 1, BLOCK, 2 * BLOCK)
    y_attn = sink_attention(qb, kk, vv, mask, sinks).reshape(b, s, Q_WIDTH)
    x = x + merge_branches(y_pool, y_attn, ga, gb, w_pool_proj, w_attn_proj, w_out)
    x = x + swiglu(rms_norm(x, norm2), w_gate, w_up, w_down)
    return x, k[:, s - w_buf:], v[:, s - w_buf:], u[:, s - POOL_BUF:]


def sample_layer(x, cache_k, cache_v, state_pool, norm1, w_in, q_norm, k_norm, sinks, pool_mix_w,
                 pool_scale, w_pool_proj, w_attn_proj, w_out, norm2, w_gate, w_up, w_down):
    _, t, _ = x.shape
    w_buf = cache_k.shape[1]
    h = rms_norm(x, norm1)
    u, q, k, v, ga, gb = split_in(h, w_in, q_norm, k_norm)
    pos = PAST_LEN + jnp.arange(t)
    u_ext = jnp.concatenate([state_pool, u], axis=1)
    y_pool = pool_mix(u_ext, pos, pool_mix_w, pool_scale)
    kk = jnp.concatenate([cache_k, k], axis=1)
    vv = jnp.concatenate([cache_v, v], axis=1)
    kpos = jnp.concatenate([PAST_LEN - w_buf + jnp.arange(w_buf), pos])
    d = pos[:, None] - kpos[None, :]
    mask = ((d >= 0) & (d <= WINDOW))[None, None, None]
    y_attn = sink_attention(q, kk, vv, mask, sinks)
    x = x + merge_branches(y_pool, y_attn, ga, gb, w_pool_proj, w_attn_proj, w_out)
    x = x + swiglu(rms_norm(x, norm2), w_gate, w_up, w_down)
    return x, kk[:, -w_buf:], vv[:, -w_buf:], u_ext[:, -POOL_BUF:]


def setup_inputs(seed: int = 0) -> dict:
    key = jax.random.key(seed)
    ks = jax.random.split(key, 20)
    w_buf = min(WINDOW, PAST_LEN)
    f32 = jnp.float32

    def nrm(k, shape, scale):
        return jax.random.normal(k, shape, f32) * scale

    return {
        "x_prompt": nrm(ks[0], (BATCH, SEQ, D_MODEL), 1.0),
        "x_sample": nrm(ks[1], (DEC_BATCH, DEC_SEQ, D_MODEL), 1.0),
        "cache_k": nrm(ks[2], (DEPTH, DEC_BATCH, w_buf, N_KV_HEADS, HEAD_DIM), 1.0),
        "cache_v": nrm(ks[3], (DEPTH, DEC_BATCH, w_buf, N_KV_HEADS, HEAD_DIM), 1.0),
        "state_pool": nrm(ks[4], (DEPTH, DEC_BATCH, POOL_BUF, POOL_WIDTH), 1.0),
        "norm1": 1.0 + nrm(ks[5], (DEPTH, D_MODEL), 0.05),
        "w_in": nrm(ks[6], (DEPTH, D_MODEL, IN_WIDTH), D_MODEL ** -0.5),
        "q_norm": 1.0 + nrm(ks[7], (DEPTH, HEAD_DIM), 0.05),
        "k_norm": 1.0 + nrm(ks[8], (DEPTH, HEAD_DIM), 0.05),
        "sinks": nrm(ks[9], (DEPTH, N_HEADS), 0.5),
        "pool_mix_w": nrm(ks[10], (DEPTH, POOL_GROUPS, POOL_GROUP_WIDTH, POOL_GROUP_WIDTH), POOL_GROUP_WIDTH ** -0.5),
        "pool_scale": 1.0 + nrm(ks[11], (DEPTH, POOL_WIDTH), 0.1),
        "w_pool_proj": nrm(ks[12], (DEPTH, POOL_WIDTH, D_MODEL), POOL_WIDTH ** -0.5),
        "w_attn_proj": nrm(ks[13], (DEPTH, Q_WIDTH, D_MODEL), Q_WIDTH ** -0.5),
        "w_out": nrm(ks[14], (DEPTH, D_MODEL, D_MODEL), D_MODEL ** -0.5),
        "norm2": 1.0 + nrm(ks[15], (DEPTH, D_MODEL), 0.05),
        "w_gate": nrm(ks[16], (DEPTH, D_MODEL, D_FF), D_MODEL ** -0.5),
        "w_up": nrm(ks[17], (DEPTH, D_MODEL, D_FF), D_MODEL ** -0.5),
        "w_down": nrm(ks[18], (DEPTH, D_FF, D_MODEL), D_FF ** -0.5),
    }


def reference(x_prompt, x_sample, cache_k, cache_v, state_pool, norm1, w_in, q_norm, k_norm, sinks,
              pool_mix_w, pool_scale, w_pool_proj, w_attn_proj, w_out, norm2, w_gate, w_up, w_down):
    w_buf = cache_k.shape[2]
    xp, xs = x_prompt, x_sample
    kp, vp, pp, ksn, vsn, psn = [], [], [], [], [], []
    for l in range(DEPTH):
        shared = (norm1[l], w_in[l], q_norm[l], k_norm[l], sinks[l], pool_mix_w[l], pool_scale[l],
                  w_pool_proj[l], w_attn_proj[l], w_out[l], norm2[l], w_gate[l], w_up[l], w_down[l])
        xp, k_l, v_l, p_l = prompt_layer(xp, w_buf, *shared)
        xs, ks_l, vs_l, ps_l = sample_layer(xs, cache_k[l], cache_v[l], state_pool[l], *shared)
        kp.append(k_l); vp.append(v_l); pp.append(p_l)
        ksn.append(ks_l); vsn.append(vs_l); psn.append(ps_l)
    return (xp, xs, jnp.stack(kp), jnp.stack(vp), jnp.stack(pp), jnp.stack(ksn), jnp.stack(vsn), jnp.stack(psn))
```

```python
import functools

import numpy as np
import jax
import jax.numpy as jnp
from jax import lax
from jax.experimental import pallas as pl
from jax.experimental.pallas import tpu as pltpu

D_MODEL = 1024
POOL_WINDOWS = (2, 4, 8, 16)
POOL_GROUP_WIDTH = 128
POOL_WIDTH = 512
POOL_BUF = 15
N_HEADS = 8
N_KV_HEADS = 2
HEAD_DIM = 64
GROUP = N_HEADS // N_KV_HEADS
Q_WIDTH = 512
KV_WIDTH = 128
WINDOW = 128
D_FF = 2816
EPS = 1e-6
NEG = -1e30

LANES = 128
ATT_BLOCK = WINDOW
SEQ_TILE = 512
FF_CHUNK = 512
GATE_CHUNK = 256
SAMPLE_BLOCK = 16
PREFIX = 16
VMEM_LIMIT_BYTES = 58 * 1024 * 1024

OFF_U = 0
OFF_Q = OFF_U + POOL_WIDTH
OFF_K = OFF_Q + Q_WIDTH
OFF_V = OFF_K + KV_WIDTH
OFF_GA = OFF_V + KV_WIDTH
OFF_GB = OFF_GA + D_MODEL

HEAD_ORDER = tuple(g + GROUP * half for g in range(GROUP) for half in range(N_KV_HEADS))

BF16 = jnp.bfloat16
F32 = jnp.float32


def _dot(a, b):
    return jnp.dot(a, b, preferred_element_type=F32)


def _dot_nt(a, b):
    return lax.dot_general(a, b, (((1,), (1,)), ((), ())), preferred_element_type=F32)


def _rms_scale(x):
    return lax.rsqrt(jnp.mean(x * x, axis=-1, keepdims=True) + EPS)


def _head_norm(x, seg, gain):
    ss = _dot((x * x).astype(BF16), seg)
    return (x * lax.rsqrt(ss * (1.0 / HEAD_DIM) + EPS)) * gain


def _ffn(h2, x1, wg_ref, wu_ref, wd_ref, out_ref, out_idx):
    start = 0
    while start < D_FF:
        width = min(FF_CHUNK, D_FF - start)
        g = _dot(h2, wg_ref[:, start:start + width])
        u = _dot(h2, wu_ref[:, start:start + width])
        a = (g * jax.nn.sigmoid(g) * u).astype(BF16)
        contrib = _dot(a, wd_ref[start:start + width, :])
        if start == 0:
            out_ref[out_idx] = x1 + contrib
        else:
            out_ref[out_idx] += contrib
        start += width


def _merge(h, ypool, yattn, w_in_ref, wpp_ref, wap_ref, merged_ref):
    for c in range(D_MODEL // GATE_CHUNK):
        lo = c * GATE_CHUNK
        ga = _dot(h, w_in_ref[:, OFF_GA + lo:OFF_GA + lo + GATE_CHUNK])
        gb = _dot(h, w_in_ref[:, OFF_GB + lo:OFF_GB + lo + GATE_CHUNK])
        pp = _dot(ypool, wpp_ref[:, lo:lo + GATE_CHUNK])
        ap = _dot(yattn, wap_ref[:, lo:lo + GATE_CHUNK])
        merged = jax.nn.sigmoid(ga) * pp + jax.nn.sigmoid(gb) * ap
        merged_ref[:, lo:lo + GATE_CHUNK] = merged.astype(BF16)


def _prompt_kernel(sinks_ref, x_ref, n1_ref, w_in_ref, qn_ref, kn_ref, seg_ref, mixw_ref, pscale_ref,
                   wpp_ref, wap_ref, wout_ref, n2_ref, wg_ref, wu_ref, wd_ref,
                   y_ref, knew_ref, vnew_ref, pnew_ref,
                   h_s, u_s, q_s, k_s, v_s, s_s, p_s, ypool_s, yattn_s, merged_s, x1_s, h2_s):
    T = SEQ_TILE
    j = pl.program_id(1)

    @pl.when(j == 0)
    def _():
        u_s[0:PREFIX, :] = jnp.zeros((PREFIX, POOL_WIDTH), F32)
        k_s[0:ATT_BLOCK, :] = jnp.zeros((ATT_BLOCK, KV_WIDTH), BF16)
        v_s[0:ATT_BLOCK, :] = jnp.zeros((ATT_BLOCK, KV_WIDTH), BF16)

    x = x_ref[0]
    h_s[...] = ((x * _rms_scale(x)) * n1_ref[...]).astype(BF16)
    h = h_s[...]

    u = _dot(h, w_in_ref[:, OFF_U:OFF_U + POOL_WIDTH])
    u_s[PREFIX:PREFIX + T, :] = u
    pnew_ref[0] = u_s[PREFIX + T - POOL_BUF:PREFIX + T, :]

    q = _dot(h, w_in_ref[:, OFF_Q:OFF_Q + Q_WIDTH])
    q_s[...] = _head_norm(q, seg_ref[...], qn_ref[...]).astype(BF16)

    k = _dot(h, w_in_ref[:, OFF_K:OFF_K + KV_WIDTH])
    kn = _head_norm(k, seg_ref[0:KV_WIDTH, 0:KV_WIDTH], kn_ref[...])
    v = _dot(h, w_in_ref[:, OFF_V:OFF_V + KV_WIDTH])
    k_s[ATT_BLOCK:ATT_BLOCK + T, :] = kn.astype(BF16)
    v_s[ATT_BLOCK:ATT_BLOCK + T, :] = v.astype(BF16)
    knew_ref[0] = kn[T - ATT_BLOCK:, :]
    vnew_ref[0] = v[T - ATT_BLOCK:, :]

    pos1 = j * T + lax.broadcasted_iota(jnp.int32, (T, 1), 0) + 1
    for g, w in enumerate(POOL_WINDOWS):
        cols = slice(g * POOL_GROUP_WIDTH, (g + 1) * POOL_GROUP_WIDTH)
        a = u_s[:, cols]
        s = a
        shift = 1
        while shift < w:
            s = s + pltpu.roll(s, shift, 0)
            shift *= 2
        inv_cnt = 1.0 / jnp.minimum(pos1, w).astype(F32)
        yg = (s[PREFIX:, :] * inv_cnt - a[PREFIX:, :]).astype(BF16)
        ypool_s[:, cols] = (_dot(yg, mixw_ref[g]) * pscale_ref[:, cols]).astype(BF16)

    lane = lax.broadcasted_iota(jnp.int32, (ATT_BLOCK, LANES), 1)
    row = lax.broadcasted_iota(jnp.int32, (ATT_BLOCK, 2 * ATT_BLOCK), 0)
    col = lax.broadcasted_iota(jnp.int32, (ATT_BLOCK, 2 * ATT_BLOCK), 1)
    band = (col >= row) & (col <= row + WINDOW)
    first_lo = jnp.where(j == 0, ATT_BLOCK, 0)
    for b in range(T // ATT_BLOCK):
        r0 = b * ATT_BLOCK
        kc = k_s[r0:r0 + 2 * ATT_BLOCK, :]
        vc = v_s[r0:r0 + 2 * ATT_BLOCK, :]
        parts = []
        for g in range(GROUP):
            qcol = q_s[r0:r0 + ATT_BLOCK, g * LANES:(g + 1) * LANES]
            parts.append(jnp.where(lane < HEAD_DIM, qcol, jnp.zeros_like(qcol)))
            parts.append(jnp.where(lane >= HEAD_DIM, qcol, jnp.zeros_like(qcol)))
        s_s[...] = _dot_nt(jnp.concatenate(parts, axis=0), kc)
        valid = band & (col >= first_lo) if b == 0 else band
        for i, head in enumerate(HEAD_ORDER):
            sh = jnp.where(valid, s_s[i * ATT_BLOCK:(i + 1) * ATT_BLOCK, :], NEG)
            sink = sinks_ref[head]
            m = jnp.maximum(jnp.max(sh, axis=-1, keepdims=True), sink)
            p = jnp.exp(sh - m)
            denom = jnp.sum(p, axis=-1, keepdims=True) + jnp.exp(sink - m)
            p_s[i * ATT_BLOCK:(i + 1) * ATT_BLOCK, :] = (p * (1.0 / denom)).astype(BF16)
        o = _dot(p_s[...], vc)
        for g in range(GROUP):
            o_lo = o[(2 * g) * ATT_BLOCK:(2 * g + 1) * ATT_BLOCK, :]
            o_hi = o[(2 * g + 1) * ATT_BLOCK:(2 * g + 2) * ATT_BLOCK, :]
            yattn_s[r0:r0 + ATT_BLOCK, g * LANES:(g + 1) * LANES] = jnp.where(lane < HEAD_DIM, o_lo, o_hi).astype(BF16)

    u_s[0:PREFIX, :] = u_s[T:T + PREFIX, :]
    k_s[0:ATT_BLOCK, :] = k_s[T:T + ATT_BLOCK, :]
    v_s[0:ATT_BLOCK, :] = v_s[T:T + ATT_BLOCK, :]

    _merge(h, ypool_s[...], yattn_s[...], w_in_ref, wpp_ref, wap_ref, merged_s)
    x1 = x_ref[0] + _dot(merged_s[...], wout_ref[...])
    x1_s[...] = x1
    h2_s[...] = ((x1 * _rms_scale(x1)) * n2_ref[...]).astype(BF16)

    _ffn(h2_s[...], x1_s[...], wg_ref, wu_ref, wd_ref, y_ref, 0)


def _const_spec(shape):
    nd = len(shape)
    return pl.BlockSpec(shape, lambda *_: (0,) * nd, pipeline_mode=pl.Buffered(1))


def _prompt_call(x, sinks, weights):
    B, S, _ = x.shape
    T = SEQ_TILE
    in_specs = [pl.BlockSpec(memory_space=pltpu.SMEM),
                pl.BlockSpec((1, T, D_MODEL), lambda b, j: (b, j, 0))]
    in_specs += [_const_spec(w.shape) for w in weights]
    out_shape = (jax.ShapeDtypeStruct((B, S, D_MODEL), F32),
                 jax.ShapeDtypeStruct((B, ATT_BLOCK, KV_WIDTH), F32),
                 jax.ShapeDtypeStruct((B, ATT_BLOCK, KV_WIDTH), F32),
                 jax.ShapeDtypeStruct((B, POOL_BUF, POOL_WIDTH), F32))
    out_specs = (pl.BlockSpec((1, T, D_MODEL), lambda b, j: (b, j, 0)),
                 pl.BlockSpec((1, ATT_BLOCK, KV_WIDTH), lambda b, j: (b, 0, 0)),
                 pl.BlockSpec((1, ATT_BLOCK, KV_WIDTH), lambda b, j: (b, 0, 0)),
                 pl.BlockSpec((1, POOL_BUF, POOL_WIDTH), lambda b, j: (b, 0, 0)))
    scratch = [pltpu.VMEM((T, D_MODEL), BF16),
               pltpu.VMEM((PREFIX + T, POOL_WIDTH), F32),
               pltpu.VMEM((T, Q_WIDTH), BF16),
               pltpu.VMEM((ATT_BLOCK + T, KV_WIDTH), BF16),
               pltpu.VMEM((ATT_BLOCK + T, KV_WIDTH), BF16),
               pltpu.VMEM((N_HEADS * ATT_BLOCK, 2 * ATT_BLOCK), F32),
               pltpu.VMEM((N_HEADS * ATT_BLOCK, 2 * ATT_BLOCK), BF16),
               pltpu.VMEM((T, POOL_WIDTH), BF16),
               pltpu.VMEM((T, Q_WIDTH), BF16),
               pltpu.VMEM((T, D_MODEL), BF16),
               pltpu.VMEM((T, D_MODEL), F32),
               pltpu.VMEM((T, D_MODEL), BF16)]
    return pl.pallas_call(
        _prompt_kernel,
        grid=(B, S // T),
        in_specs=in_specs,
        out_specs=out_specs,
        out_shape=out_shape,
        scratch_shapes=scratch,
        compiler_params=pltpu.CompilerParams(
            dimension_semantics=("arbitrary", "arbitrary"),
            vmem_limit_bytes=VMEM_LIMIT_BYTES),
        name="prompt_layer",
    )(sinks, x, *weights)


def _sample_kernel(x_ref, ck_ref, cv_ref, sp_ref, sinkl_ref, sel_ref, selt_ref,
                   n1_ref, w_in_ref, qn_ref, kn_ref, seg_ref, mixw_ref, pscale_ref,
                   wpp_ref, wap_ref, wout_ref, n2_ref, wg_ref, wu_ref, wd_ref,
                   y_ref, knew_ref, vnew_ref, pnew_ref,
                   h_s, u_s, q_s, k_s, v_s, lhs_s, ypre_s, yattn_s, merged_s):
    SB = SAMPLE_BLOCK
    W = WINDOW
    i = pl.program_id(0)
    nsteps = pl.num_programs(0)

    @pl.when(i == 0)
    def _():
        x = x_ref[...]
        h_s[...] = ((x * _rms_scale(x)) * n1_ref[...]).astype(BF16)
        h = h_s[...]
        u_s[...] = _dot(h, w_in_ref[:, OFF_U:OFF_U + POOL_WIDTH])
        q = _dot(h, w_in_ref[:, OFF_Q:OFF_Q + Q_WIDTH])
        q_s[...] = _head_norm(q, seg_ref[...], qn_ref[...])
        k = _dot(h, w_in_ref[:, OFF_K:OFF_K + KV_WIDTH])
        k_s[...] = _head_norm(k, seg_ref[0:KV_WIDTH, 0:KV_WIDTH], kn_ref[...])
        v_s[...] = _dot(h, w_in_ref[:, OFF_V:OFF_V + KV_WIDTH])

    rows = pl.ds(pl.multiple_of(i * SB, SB), SB)
    qb = q_s[rows, :]
    knew = k_s[rows, :]
    vnew = v_s[rows, :]
    unew = u_s[rows, :]

    for g, w in enumerate(POOL_WINDOWS):
        cols = slice(g * POOL_GROUP_WIDTH, (g + 1) * POOL_GROUP_WIDTH)
        s = unew[:, cols]
        for r in range(POOL_BUF - (w - 1), POOL_BUF):
            s = s + sp_ref[:, r * POOL_WIDTH + g * POOL_GROUP_WIDTH:r * POOL_WIDTH + (g + 1) * POOL_GROUP_WIDTH]
        ypre_s[rows, cols] = (s * (1.0 / w) - unew[:, cols]).astype(BF16)
    pnew_ref[:, 0:(POOL_BUF - 1) * POOL_WIDTH] = sp_ref[:, POOL_WIDTH:]
    pnew_ref[:, (POOL_BUF - 1) * POOL_WIDTH:] = unew

    qcols = [qb[:, g * LANES:(g + 1) * LANES] for g in range(GROUP)]
    for s_idx in range(W):
        kpos = ck_ref[:, s_idx * KV_WIDTH:(s_idx + 1) * KV_WIDTH]
        for g in range(GROUP):
            lhs_s[s_idx * SB:(s_idx + 1) * SB, g * LANES:(g + 1) * LANES] = (kpos * qcols[g]).astype(BF16)
    for g in range(GROUP):
        lhs_s[W * SB:(W + 1) * SB, g * LANES:(g + 1) * LANES] = (knew * qcols[g]).astype(BF16)
    sc = _dot(lhs_s[...], sel_ref[...])
    sc3 = sc.reshape(W + 1, SB, LANES)
    sink = sinkl_ref[...]
    m = jnp.maximum(jnp.max(sc3, axis=0), sink)
    p3 = jnp.exp(sc3 - m[None])
    denom = jnp.sum(p3, axis=0) + jnp.exp(sink - m)
    pn = (p3 * (1.0 / denom)[None]).astype(BF16).reshape((W + 1) * SB, LANES)
    pb = _dot(pn, selt_ref[...])
    o = [jnp.zeros((SB, LANES), F32) for _ in range(GROUP)]
    for s_idx in range(W + 1):
        vpos = cv_ref[:, s_idx * KV_WIDTH:(s_idx + 1) * KV_WIDTH] if s_idx < W else vnew
        for g in range(GROUP):
            o[g] = o[g] + pb[s_idx * SB:(s_idx + 1) * SB, g * LANES:(g + 1) * LANES] * vpos
    for g in range(GROUP):
        yattn_s[rows, g * LANES:(g + 1) * LANES] = o[g].astype(BF16)

    knew_ref[:, 0:(W - 1) * KV_WIDTH] = ck_ref[:, KV_WIDTH:]
    knew_ref[:, (W - 1) * KV_WIDTH:] = knew
    vnew_ref[:, 0:(W - 1) * KV_WIDTH] = cv_ref[:, KV_WIDTH:]
    vnew_ref[:, (W - 1) * KV_WIDTH:] = vnew

    @pl.when(i == nsteps - 1)
    def _():
        h = h_s[...]
        ypool = jnp.concatenate(
            [(_dot(ypre_s[:, g * POOL_GROUP_WIDTH:(g + 1) * POOL_GROUP_WIDTH], mixw_ref[g])
              * pscale_ref[:, g * POOL_GROUP_WIDTH:(g + 1) * POOL_GROUP_WIDTH]).astype(BF16)
             for g in range(len(POOL_WINDOWS))], axis=1)
        _merge(h, ypool, yattn_s[...], w_in_ref, wpp_ref, wap_ref, merged_s)
        x1 = x_ref[...] + _dot(merged_s[...], wout_ref[...])
        h2 = ((x1 * _rms_scale(x1)) * n2_ref[...]).astype(BF16)
        _ffn(h2, x1, wg_ref, wu_ref, wd_ref, y_ref, Ellipsis)


def _sample_call(x, ck, cv, sp, sinkl, sel, selt, weights):
    N = x.shape[0]
    SB = SAMPLE_BLOCK
    W = WINDOW
    consts = (sinkl, sel, selt) + tuple(weights)
    in_specs = [_const_spec(x.shape),
                pl.BlockSpec((SB, W * KV_WIDTH), lambda i: (i, 0)),
                pl.BlockSpec((SB, W * KV_WIDTH), lambda i: (i, 0)),
                pl.BlockSpec((SB, POOL_BUF * POOL_WIDTH), lambda i: (i, 0))]
    in_specs += [_const_spec(w.shape) for w in consts]
    out_shape = (jax.ShapeDtypeStruct((N, D_MODEL), F32),
                 jax.ShapeDtypeStruct((N, W * KV_WIDTH), F32),
                 jax.ShapeDtypeStruct((N, W * KV_WIDTH), F32),
                 jax.ShapeDtypeStruct((N, POOL_BUF * POOL_WIDTH), F32))
    out_specs = (pl.BlockSpec((N, D_MODEL), lambda i: (0, 0)),
                 pl.BlockSpec((SB, W * KV_WIDTH), lambda i: (i, 0)),
                 pl.BlockSpec((SB, W * KV_WIDTH), lambda i: (i, 0)),
                 pl.BlockSpec((SB, POOL_BUF * POOL_WIDTH), lambda i: (i, 0)))
    scratch = [pltpu.VMEM((N, D_MODEL), BF16),
               pltpu.VMEM((N, POOL_WIDTH), F32),
               pltpu.VMEM((N, Q_WIDTH), F32),
               pltpu.VMEM((N, KV_WIDTH), F32),
               pltpu.VMEM((N, KV_WIDTH), F32),
               pltpu.VMEM(((W + 1) * SB, Q_WIDTH), BF16),
               pltpu.VMEM((N, POOL_WIDTH), BF16),
               pltpu.VMEM((N, Q_WIDTH), BF16),
               pltpu.VMEM((N, D_MODEL), BF16)]
    return pl.pallas_call(
        _sample_kernel,
        grid=(N // SB,),
        in_specs=in_specs,
        out_specs=out_specs,
        out_shape=out_shape,
        scratch_shapes=scratch,
        compiler_params=pltpu.CompilerParams(
            dimension_semantics=("arbitrary",),
            vmem_limit_bytes=VMEM_LIMIT_BYTES),
        name="sample_layer",
    )(x, ck, cv, sp, *consts)


def _head_sum_matrix():
    idx = np.arange(Q_WIDTH) // HEAD_DIM
    return jnp.asarray(idx[:, None] == idx[None, :], dtype=BF16)


def _head_select_matrix():
    sel = np.zeros((Q_WIDTH, LANES), np.float32)
    sel[np.arange(Q_WIDTH), np.arange(Q_WIDTH) // HEAD_DIM] = 1.0
    return sel


@jax.jit
def _forward(x_prompt, x_sample, cache_k, cache_v, state_pool, norm1, w_in, q_norm, k_norm, sinks,
             pool_mix_w, pool_scale, w_pool_proj, w_attn_proj, w_out, norm2, w_gate, w_up, w_down):
    depth = w_in.shape[0]
    assert depth == 1, "single-layer trunk"
    l = 0
    B = x_prompt.shape[0]
    N = x_sample.shape[0]
    W = cache_k.shape[2]
    assert W == WINDOW and x_sample.shape[1] == 1

    q_perm = np.concatenate([np.arange(h * HEAD_DIM, (h + 1) * HEAD_DIM) for h in HEAD_ORDER])
    in_perm = np.concatenate([np.arange(OFF_Q), OFF_Q + q_perm, np.arange(OFF_K, OFF_GB + D_MODEL)])
    w_in_b = w_in[l][:, in_perm].astype(BF16)
    wap_b = w_attn_proj[l][q_perm, :].astype(BF16)
    qn = (jnp.tile(q_norm[l], N_HEADS) * (HEAD_DIM ** -0.5)).reshape(1, Q_WIDTH)
    kn = jnp.tile(k_norm[l], N_KV_HEADS).reshape(1, KV_WIDTH)
    weights = (norm1[l].reshape(1, D_MODEL), w_in_b, qn, kn, _head_sum_matrix(),
               pool_mix_w[l].astype(BF16), pool_scale[l].reshape(1, POOL_WIDTH),
               w_pool_proj[l].astype(BF16), wap_b, w_out[l].astype(BF16),
               norm2[l].reshape(1, D_MODEL), w_gate[l].astype(BF16), w_up[l].astype(BF16),
               w_down[l].astype(BF16))

    y_p, k_p, v_p, pool_p = _prompt_call(x_prompt, sinks[l], weights)

    sel = _head_select_matrix()
    sink_lanes = jnp.zeros((1, LANES), F32).at[0, :N_HEADS].set(sinks[l][np.asarray(HEAD_ORDER)])
    y_s, k_s, v_s, pool_s = _sample_call(
        x_sample.reshape(N, D_MODEL),
        cache_k[l].reshape(N, W * KV_WIDTH), cache_v[l].reshape(N, W * KV_WIDTH),
        state_pool[l].reshape(N, POOL_BUF * POOL_WIDTH),
        sink_lanes, jnp.asarray(sel, BF16), jnp.asarray(sel.T, BF16), weights)

    return (y_p, y_s.reshape(N, 1, D_MODEL),
            k_p.reshape(1, B, W, N_KV_HEADS, HEAD_DIM), v_p.reshape(1, B, W, N_KV_HEADS, HEAD_DIM),
            pool_p.reshape(1, B, POOL_BUF, POOL_WIDTH),
            k_s.reshape(1, N, W, N_KV_HEADS, HEAD_DIM), v_s.reshape(1, N, W, N_KV_HEADS, HEAD_DIM),
            pool_s.reshape(1, N, POOL_BUF, POOL_WIDTH))


def kernel(x_prompt, x_sample, cache_k, cache_v, state_pool, norm1, w_in, q_norm, k_norm, sinks, pool_mix_w,
           pool_scale, w_pool_proj, w_attn_proj, w_out, norm2, w_gate, w_up, w_down):
    return _forward(x_prompt, x_sample, cache_k, cache_v, state_pool, norm1, w_in, q_norm, k_norm, sinks,
                    pool_mix_w, pool_scale, w_pool_proj, w_attn_proj, w_out, norm2, w_gate, w_up, w_down)
```

```python
import numpy as np
import jax
import jax.numpy as jnp
from jax import lax
from jax.experimental import pallas as pl
from jax.experimental.pallas import tpu as pltpu

D_MODEL = 1024
POOL_WINDOWS = (2, 4, 8, 16)
POOL_GROUP_WIDTH = 128
POOL_WIDTH = 512
POOL_BUF = 15
N_HEADS = 8
N_KV_HEADS = 2
HEAD_DIM = 64
GROUP = N_HEADS // N_KV_HEADS
Q_WIDTH = 512
KV_WIDTH = 128
WINDOW = 128
D_FF = 2816
EPS = 1e-6
NEG = -1e30

LANES = 128
SUBLANES = 8
MXU_DIM = 256
ATT_BLOCK = WINDOW
SEQ_TILE = 512
FF_CHUNK = 512
GATE_CHUNK = 256
SAMPLE_BLOCK = 16
PREFIX = 16
VMEM_LIMIT_BYTES = 58 * 1024 * 1024
Q_COLS = Q_WIDTH // LANES
HEADS_PER_COL = LANES // HEAD_DIM

OFF_U = 0
OFF_Q = OFF_U + POOL_WIDTH
OFF_K = OFF_Q + Q_WIDTH
OFF_V = OFF_K + KV_WIDTH
OFF_GA = OFF_V + KV_WIDTH
OFF_GB = OFF_GA + D_MODEL

BF16 = jnp.bfloat16
F32 = jnp.float32


def _dot(a, b):
    return jnp.dot(a, b, preferred_element_type=F32)


def _dot_nt(a, b):
    return lax.dot_general(a, b, (((1,), (1,)), ((), ())), preferred_element_type=F32)


def _rms_scale(x):
    return lax.rsqrt(jnp.mean(x * x, axis=-1, keepdims=True) + EPS)


def _head_norm(x, seg, gain):
    width = seg.shape[0]
    sq = (x * x).astype(BF16)
    ss = jnp.concatenate([_dot(sq[:, c:c + width], seg) for c in range(0, x.shape[1], width)], axis=1)
    return (x * lax.rsqrt(ss * (1.0 / HEAD_DIM) + EPS)) * gain


def _kv_head(head):
    return head // GROUP


def _ffn(h2, x1, wg_ref, wu_ref, wd_ref, out_ref, out_idx):
    start = 0
    while start < D_FF:
        width = min(FF_CHUNK, D_FF - start)
        g = _dot(h2, wg_ref[:, start:start + width])
        u = _dot(h2, wu_ref[:, start:start + width])
        a = (g * jax.nn.sigmoid(g) * u).astype(BF16)
        contrib = _dot(a, wd_ref[start:start + width, :])
        if start == 0:
            out_ref[out_idx] = x1 + contrib
        else:
            out_ref[out_idx] += contrib
        start += width


def _merge(h, ypool, yattn, w_in_ref, wpp_ref, wap_ref, merged_ref):
    for c in range(D_MODEL // GATE_CHUNK):
        lo = c * GATE_CHUNK
        ga = _dot(h, w_in_ref[:, OFF_GA + lo:OFF_GA + lo + GATE_CHUNK])
        gb = _dot(h, w_in_ref[:, OFF_GB + lo:OFF_GB + lo + GATE_CHUNK])
        pp = _dot(ypool, wpp_ref[:, lo:lo + GATE_CHUNK])
        ap = _dot(yattn, wap_ref[:, lo:lo + GATE_CHUNK])
        merged = jax.nn.sigmoid(ga) * pp + jax.nn.sigmoid(gb) * ap
        merged_ref[:, lo:lo + GATE_CHUNK] = merged.astype(BF16)


def _prompt_kernel(sinks_ref, x_ref, n1_ref, w_in_ref, qn_ref, kn_ref, seg_ref, mixw_ref, pscale_ref,
                   wpp_ref, wap_ref, wout_ref, n2_ref, wg_ref, wu_ref, wd_ref,
                   y_ref, knew_ref, vnew_ref, pnew_ref,
                   h_s, u_s, q_s, kd_s, vd_s, s_s, p_s, ypool_s, yattn_s, merged_s, x1_s, h2_s):
    T = SEQ_TILE
    j = pl.program_id(1)
    rows_per_kv = GROUP * ATT_BLOCK

    @pl.when(j == 0)
    def _():
        u_s[0:PREFIX, :] = jnp.zeros((PREFIX, POOL_WIDTH), F32)
        kd_s[:, 0:ATT_BLOCK, :] = jnp.zeros((N_KV_HEADS, ATT_BLOCK, KV_WIDTH), BF16)
        vd_s[:, 0:ATT_BLOCK, :] = jnp.zeros((N_KV_HEADS, ATT_BLOCK, KV_WIDTH), BF16)

    x = x_ref[0]
    h_s[...] = ((x * _rms_scale(x)) * n1_ref[...]).astype(BF16)
    h = h_s[...]

    u_s[PREFIX:PREFIX + T, :] = _dot(h, w_in_ref[:, OFF_U:OFF_U + POOL_WIDTH])
    pnew_ref[0] = u_s[PREFIX + T - POOL_BUF:PREFIX + T, :]

    q = _dot(h, w_in_ref[:, OFF_Q:OFF_Q + Q_WIDTH])
    q_s[...] = _head_norm(q, seg_ref[...], qn_ref[...]).astype(BF16)

    k = _dot(h, w_in_ref[:, OFF_K:OFF_K + KV_WIDTH])
    kn = _head_norm(k, seg_ref[0:KV_WIDTH, 0:KV_WIDTH], kn_ref[...])
    v = _dot(h, w_in_ref[:, OFF_V:OFF_V + KV_WIDTH])
    lane_t = lax.broadcasted_iota(jnp.int32, (T, LANES), 1)
    for src, dst in ((kn, kd_s), (v, vd_s)):
        swapped = pltpu.roll(src, HEAD_DIM, 1)
        dst[0, ATT_BLOCK:ATT_BLOCK + T, :] = jnp.where(lane_t < HEAD_DIM, src, swapped).astype(BF16)
        dst[1, ATT_BLOCK:ATT_BLOCK + T, :] = jnp.where(lane_t < HEAD_DIM, swapped, src).astype(BF16)

    @pl.when(j == pl.num_programs(1) - 1)
    def _():
        knew_ref[0] = kn[T - ATT_BLOCK:, :].T
        vnew_ref[0] = v[T - ATT_BLOCK:, :].T

    pos1 = j * T + lax.broadcasted_iota(jnp.int32, (T, 1), 0) + 1
    for g, w in enumerate(POOL_WINDOWS):
        cols = slice(g * POOL_GROUP_WIDTH, (g + 1) * POOL_GROUP_WIDTH)
        a = u_s[:, cols]
        s = a
        shift = 1
        while shift < w:
            s = s + pltpu.roll(s, shift, 0)
            shift *= 2
        inv_cnt = 1.0 / jnp.minimum(pos1, w).astype(F32)
        yg = (s[PREFIX:, :] * inv_cnt - a[PREFIX:, :]).astype(BF16)
        ypool_s[:, cols] = (_dot(yg, mixw_ref[g]) * pscale_ref[:, cols]).astype(BF16)

    lane = lax.broadcasted_iota(jnp.int32, (ATT_BLOCK, LANES), 1)
    row = lax.broadcasted_iota(jnp.int32, (ATT_BLOCK, 2 * ATT_BLOCK), 0)
    col = lax.broadcasted_iota(jnp.int32, (ATT_BLOCK, 2 * ATT_BLOCK), 1)
    band = (col >= row) & (col <= row + WINDOW)
    first_lo = jnp.where(j == 0, ATT_BLOCK, 0)
    for b in range(T // ATT_BLOCK):
        r0 = b * ATT_BLOCK
        for c in range(N_KV_HEADS):
            parts = []
            for p in range(c * Q_COLS // N_KV_HEADS, (c + 1) * Q_COLS // N_KV_HEADS):
                qcol = q_s[r0:r0 + ATT_BLOCK, p * LANES:(p + 1) * LANES]
                parts.append(jnp.where(lane < HEAD_DIM, qcol, jnp.zeros_like(qcol)))
                parts.append(jnp.where(lane >= HEAD_DIM, qcol, jnp.zeros_like(qcol)))
            s_s[c * rows_per_kv:(c + 1) * rows_per_kv, :] = _dot_nt(
                jnp.concatenate(parts, axis=0), kd_s[c, r0:r0 + 2 * ATT_BLOCK, :])
        valid = band & (col >= first_lo) if b == 0 else band
        for head in range(N_HEADS):
            sh = jnp.where(valid, s_s[head * ATT_BLOCK:(head + 1) * ATT_BLOCK, :], NEG)
            sink = sinks_ref[head]
            m = jnp.maximum(jnp.max(sh, axis=-1, keepdims=True), sink)
            p = jnp.exp(sh - m)
            denom = jnp.sum(p, axis=-1, keepdims=True) + jnp.exp(sink - m)
            p_s[head * ATT_BLOCK:(head + 1) * ATT_BLOCK, :] = (p * (1.0 / denom)).astype(BF16)
        for c in range(N_KV_HEADS):
            o = _dot(p_s[c * rows_per_kv:(c + 1) * rows_per_kv, :], vd_s[c, r0:r0 + 2 * ATT_BLOCK, :])
            for pp in range(Q_COLS // N_KV_HEADS):
                p_col = c * Q_COLS // N_KV_HEADS + pp
                o_lo = o[(2 * pp) * ATT_BLOCK:(2 * pp + 1) * ATT_BLOCK, :]
                o_hi = o[(2 * pp + 1) * ATT_BLOCK:(2 * pp + 2) * ATT_BLOCK, :]
                yattn_s[r0:r0 + ATT_BLOCK, p_col * LANES:(p_col + 1) * LANES] = (
                    jnp.where(lane < HEAD_DIM, o_lo, o_hi).astype(BF16))

    u_s[0:PREFIX, :] = u_s[T:T + PREFIX, :]
    kd_s[:, 0:ATT_BLOCK, :] = kd_s[:, T:T + ATT_BLOCK, :]
    vd_s[:, 0:ATT_BLOCK, :] = vd_s[:, T:T + ATT_BLOCK, :]

    _merge(h, ypool_s[...], yattn_s[...], w_in_ref, wpp_ref, wap_ref, merged_s)
    x1 = x_ref[0] + _dot(merged_s[...], wout_ref[...])
    x1_s[...] = x1
    h2_s[...] = ((x1 * _rms_scale(x1)) * n2_ref[...]).astype(BF16)

    _ffn(h2_s[...], x1_s[...], wg_ref, wu_ref, wd_ref, y_ref, 0)


def _const_spec(shape):
    nd = len(shape)
    return pl.BlockSpec(shape, lambda *_: (0,) * nd, pipeline_mode=pl.Buffered(1))


def _prompt_call(x, sinks, weights):
    B, S, _ = x.shape
    T = SEQ_TILE
    in_specs = [pl.BlockSpec(memory_space=pltpu.SMEM),
                pl.BlockSpec((1, T, D_MODEL), lambda b, j: (b, j, 0))]
    in_specs += [_const_spec(w.shape) for w in weights]
    out_shape = (jax.ShapeDtypeStruct((B, S, D_MODEL), F32),
                 jax.ShapeDtypeStruct((B, KV_WIDTH, ATT_BLOCK), F32),
                 jax.ShapeDtypeStruct((B, KV_WIDTH, ATT_BLOCK), F32),
                 jax.ShapeDtypeStruct((B, POOL_BUF, POOL_WIDTH), F32))
    out_specs = (pl.BlockSpec((1, T, D_MODEL), lambda b, j: (b, j, 0)),
                 pl.BlockSpec((1, KV_WIDTH, ATT_BLOCK), lambda b, j: (b, 0, 0)),
                 pl.BlockSpec((1, KV_WIDTH, ATT_BLOCK), lambda b, j: (b, 0, 0)),
                 pl.BlockSpec((1, POOL_BUF, POOL_WIDTH), lambda b, j: (b, 0, 0)))
    scratch = [pltpu.VMEM((T, D_MODEL), BF16),
               pltpu.VMEM((PREFIX + T, POOL_WIDTH), F32),
               pltpu.VMEM((T, Q_WIDTH), BF16),
               pltpu.VMEM((N_KV_HEADS, ATT_BLOCK + T, KV_WIDTH), BF16),
               pltpu.VMEM((N_KV_HEADS, ATT_BLOCK + T, KV_WIDTH), BF16),
               pltpu.VMEM((N_HEADS * ATT_BLOCK, 2 * ATT_BLOCK), F32),
               pltpu.VMEM((N_HEADS * ATT_BLOCK, 2 * ATT_BLOCK), BF16),
               pltpu.VMEM((T, POOL_WIDTH), BF16),
               pltpu.VMEM((T, Q_WIDTH), BF16),
               pltpu.VMEM((T, D_MODEL), BF16),
               pltpu.VMEM((T, D_MODEL), F32),
               pltpu.VMEM((T, D_MODEL), BF16)]
    return pl.pallas_call(
        _prompt_kernel,
        grid=(B, S // T),
        in_specs=in_specs,
        out_specs=out_specs,
        out_shape=out_shape,
        scratch_shapes=scratch,
        compiler_params=pltpu.CompilerParams(
            dimension_semantics=("arbitrary", "arbitrary"),
            vmem_limit_bytes=VMEM_LIMIT_BYTES),
        name="prompt_layer",
    )(sinks, x, *weights)


def _sample_kernel(x_ref, ck_ref, cv_ref, sp_ref, sinkc_ref,
                   n1_ref, w_in_ref, qn_ref, kn_ref, seg_ref, mixw_ref, pscale_ref,
                   wpp_ref, wap_ref, wout_ref, n2_ref, wg_ref, wu_ref, wd_ref,
                   y_ref, knew_ref, vnew_ref, pnew_ref,
                   h_s, u_s, k_s, v_s, kt_s, vt_s, q8_s, o8_s, ypre_s, merged_s):
    SB = SAMPLE_BLOCK
    N = x_ref.shape[0]
    i = pl.program_id(0)
    nsteps = pl.num_programs(0)
    lane = lax.broadcasted_iota(jnp.int32, (N, LANES), 1)

    @pl.when(i == 0)
    def _():
        x = x_ref[...]
        h_s[...] = ((x * _rms_scale(x)) * n1_ref[...]).astype(BF16)
        h = h_s[...]
        u_s[...] = _dot(h, w_in_ref[:, OFF_U:OFF_U + POOL_WIDTH])
        q = _head_norm(_dot(h, w_in_ref[:, OFF_Q:OFF_Q + Q_WIDTH]), seg_ref[...], qn_ref[...])
        kn = _head_norm(_dot(h, w_in_ref[:, OFF_K:OFF_K + KV_WIDTH]), seg_ref[0:KV_WIDTH, 0:KV_WIDTH], kn_ref[...])
        v = _dot(h, w_in_ref[:, OFF_V:OFF_V + KV_WIDTH])
        k_s[...] = kn
        v_s[...] = v
        kt_s[...] = kn.T
        vt_s[...] = v.T
        for p in range(Q_COLS):
            qcol = q[:, p * LANES:(p + 1) * LANES]
            qswap = pltpu.roll(qcol, HEAD_DIM, 1)
            for e in range(HEADS_PER_COL):
                head = p * HEADS_PER_COL + e
                c = _kv_head(head)
                src = qcol if e == c else qswap
                keep = (lane < HEAD_DIM) if c == 0 else (lane >= HEAD_DIM)
                q8_s[pl.ds(head, N, stride=N_HEADS), :] = jnp.where(keep, src, 0.0)

    base = i * SB
    rows = pl.ds(pl.multiple_of(base, SB), SB)

    unew = u_s[rows, :]
    for g, w in enumerate(POOL_WINDOWS):
        cols = slice(g * POOL_GROUP_WIDTH, (g + 1) * POOL_GROUP_WIDTH)
        s = unew[:, cols]
        for r in range(POOL_BUF - (w - 1), POOL_BUF):
            s = s + sp_ref[r, :, cols]
        ypre_s[rows, cols] = (s * (1.0 / w) - unew[:, cols]).astype(BF16)
    pnew_ref[0:POOL_BUF - 1] = sp_ref[1:POOL_BUF]
    pnew_ref[POOL_BUF - 1] = unew

    lane_c = lax.broadcasted_iota(jnp.int32, (KV_WIDTH, WINDOW), 1)
    sink = sinkc_ref[:, 0:1]
    kt_all = kt_s[...]
    vt_all = vt_s[...]
    for bb in range(SB):
        b = base + bb
        lhs = q8_s[pl.ds(pl.multiple_of(b * N_HEADS, N_HEADS), N_HEADS), :]
        kt = ck_ref[bb]
        vt = cv_ref[bb]
        knew = k_s[pl.ds(b, 1), :]
        vnew = v_s[pl.ds(b, 1), :]
        sc = _dot(lhs.astype(BF16), kt.astype(BF16))
        sc_self = jnp.sum(lhs * knew, axis=-1, keepdims=True)
        m = jnp.maximum(jnp.maximum(jnp.max(sc, axis=-1, keepdims=True), sc_self), sink)
        p = jnp.exp(sc - m)
        p_self = jnp.exp(sc_self - m)
        denom = jnp.sum(p, axis=-1, keepdims=True) + p_self + jnp.exp(sink - m)
        inv = 1.0 / denom
        o = _dot_nt((p * inv).astype(BF16), vt.astype(BF16)) + (p_self * inv) * vnew
        o8_s[pl.ds(pl.multiple_of(b * N_HEADS, N_HEADS), N_HEADS), :] = o
        put = (WINDOW - 1) - b
        knew_ref[bb] = jnp.where(lane_c == WINDOW - 1, pltpu.roll(kt_all, put, 1), pltpu.roll(kt, WINDOW - 1, 1))
        vnew_ref[bb] = jnp.where(lane_c == WINDOW - 1, pltpu.roll(vt_all, put, 1), pltpu.roll(vt, WINDOW - 1, 1))

    @pl.when(i == nsteps - 1)
    def _():
        h = h_s[...]
        ypool = jnp.concatenate(
            [(_dot(ypre_s[:, g * POOL_GROUP_WIDTH:(g + 1) * POOL_GROUP_WIDTH], mixw_ref[g])
              * pscale_ref[:, g * POOL_GROUP_WIDTH:(g + 1) * POOL_GROUP_WIDTH]).astype(BF16)
             for g in range(len(POOL_WINDOWS))], axis=1)
        ycols = []
        for p in range(Q_COLS):
            halves = []
            for e in range(HEADS_PER_COL):
                head = p * HEADS_PER_COL + e
                oh = o8_s[pl.ds(head, N, stride=N_HEADS), :]
                halves.append(oh if e == _kv_head(head) else pltpu.roll(oh, HEAD_DIM, 1))
            ycols.append(jnp.where(lane < HEAD_DIM, halves[0], halves[1]).astype(BF16))
        yattn = jnp.concatenate(ycols, axis=1)
        _merge(h, ypool, yattn, w_in_ref, wpp_ref, wap_ref, merged_s)
        x1 = x_ref[...] + _dot(merged_s[...], wout_ref[...])
        h2 = ((x1 * _rms_scale(x1)) * n2_ref[...]).astype(BF16)
        _ffn(h2, x1, wg_ref, wu_ref, wd_ref, y_ref, Ellipsis)


def _sample_call(x, ck, cv, sp, sinkc, weights):
    N = x.shape[0]
    SB = SAMPLE_BLOCK
    W = WINDOW
    consts = (sinkc,) + tuple(weights)
    in_specs = [_const_spec(x.shape),
                pl.BlockSpec((SB, KV_WIDTH, W), lambda i: (i, 0, 0)),
                pl.BlockSpec((SB, KV_WIDTH, W), lambda i: (i, 0, 0)),
                pl.BlockSpec((POOL_BUF, SB, POOL_WIDTH), lambda i: (0, i, 0))]
    in_specs += [_const_spec(w.shape) for w in consts]
    out_shape = (jax.ShapeDtypeStruct((N, D_MODEL), F32),
                 jax.ShapeDtypeStruct((N, KV_WIDTH, W), F32),
                 jax.ShapeDtypeStruct((N, KV_WIDTH, W), F32),
                 jax.ShapeDtypeStruct((POOL_BUF, N, POOL_WIDTH), F32))
    out_specs = (pl.BlockSpec((N, D_MODEL), lambda i: (0, 0)),
                 pl.BlockSpec((SB, KV_WIDTH, W), lambda i: (i, 0, 0)),
                 pl.BlockSpec((SB, KV_WIDTH, W), lambda i: (i, 0, 0)),
                 pl.BlockSpec((POOL_BUF, SB, POOL_WIDTH), lambda i: (0, i, 0)))
    scratch = [pltpu.VMEM((N, D_MODEL), BF16),
               pltpu.VMEM((N, POOL_WIDTH), F32),
               pltpu.VMEM((N, KV_WIDTH), F32),
               pltpu.VMEM((N, KV_WIDTH), F32),
               pltpu.VMEM((KV_WIDTH, N), F32),
               pltpu.VMEM((KV_WIDTH, N), F32),
               pltpu.VMEM((N * N_HEADS, LANES), F32),
               pltpu.VMEM((N * N_HEADS, LANES), F32),
               pltpu.VMEM((N, POOL_WIDTH), BF16),
               pltpu.VMEM((N, D_MODEL), BF16)]
    return pl.pallas_call(
        _sample_kernel,
        grid=(N // SB,),
        in_specs=in_specs,
        out_specs=out_specs,
        out_shape=out_shape,
        scratch_shapes=scratch,
        compiler_params=pltpu.CompilerParams(
            dimension_semantics=("arbitrary",),
            vmem_limit_bytes=VMEM_LIMIT_BYTES),
        name="sample_layer",
    )(x, ck, cv, sp, *consts)


def _head_sum_matrix():
    idx = np.arange(MXU_DIM) // HEAD_DIM
    return jnp.asarray(idx[:, None] == idx[None, :], dtype=BF16)


def _cache_to_device_order(c):
    n, w = c.shape[0], c.shape[1]
    return jnp.transpose(c, (0, 2, 3, 1)).reshape(n, KV_WIDTH, w)


def _cache_from_device_order(c):
    n, _, w = c.shape
    return jnp.transpose(c.reshape(n, N_KV_HEADS, HEAD_DIM, w), (0, 3, 1, 2))[None]


@jax.jit
def _forward(x_prompt, x_sample, cache_k, cache_v, state_pool, norm1, w_in, q_norm, k_norm, sinks,
             pool_mix_w, pool_scale, w_pool_proj, w_attn_proj, w_out, norm2, w_gate, w_up, w_down):
    depth = w_in.shape[0]
    assert depth == 1, "single-layer trunk"
    l = 0
    N = x_sample.shape[0]
    assert cache_k.shape[2] == WINDOW and x_sample.shape[1] == 1

    qn = (jnp.tile(q_norm[l], N_HEADS) * (HEAD_DIM ** -0.5)).reshape(1, Q_WIDTH)
    kn = jnp.tile(k_norm[l], N_KV_HEADS).reshape(1, KV_WIDTH)
    weights = (norm1[l].reshape(1, D_MODEL), w_in[l].astype(BF16), qn, kn, _head_sum_matrix(),
               pool_mix_w[l].astype(BF16), pool_scale[l].reshape(1, POOL_WIDTH),
               w_pool_proj[l].astype(BF16), w_attn_proj[l].astype(BF16), w_out[l].astype(BF16),
               norm2[l].reshape(1, D_MODEL), w_gate[l].astype(BF16), w_up[l].astype(BF16),
               w_down[l].astype(BF16))

    y_p, k_p, v_p, pool_p = _prompt_call(x_prompt, sinks[l], weights)

    sink_cols = jnp.broadcast_to(sinks[l][:, None], (N_HEADS, LANES))
    y_s, k_s, v_s, pool_s = _sample_call(
        x_sample.reshape(N, D_MODEL),
        _cache_to_device_order(cache_k[l]), _cache_to_device_order(cache_v[l]),
        jnp.transpose(state_pool[l], (1, 0, 2)), sink_cols, weights)

    return (y_p, y_s.reshape(N, 1, D_MODEL),
            _cache_from_device_order(k_p), _cache_from_device_order(v_p), pool_p[None],
            _cache_from_device_order(k_s), _cache_from_device_order(v_s),
            jnp.transpose(pool_s, (1, 0, 2))[None])


def kernel(x_prompt, x_sample, cache_k, cache_v, state_pool, norm1, w_in, q_norm, k_norm, sinks, pool_mix_w,
           pool_scale, w_pool_proj, w_attn_proj, w_out, norm2, w_gate, w_up, w_down):
    return _forward(x_prompt, x_sample, cache_k, cache_v, state_pool, norm1, w_in, q_norm, k_norm, sinks,
                    pool_mix_w, pool_scale, w_pool_proj, w_attn_proj, w_out, norm2, w_gate, w_up, w_down)
```

```python
import numpy as np
import jax
import jax.numpy as jnp
from jax import lax
from jax.experimental import pallas as pl
from jax.experimental.pallas import tpu as pltpu

D_MODEL = 1024
POOL_WINDOWS = (2, 4, 8, 16)
POOL_GROUP_WIDTH = 128
POOL_WIDTH = 512
POOL_BUF = 15
N_HEADS = 8
N_KV_HEADS = 2
HEAD_DIM = 64
GROUP = N_HEADS // N_KV_HEADS
Q_WIDTH = 512
KV_WIDTH = 128
WINDOW = 128
D_FF = 2816
EPS = 1e-6
NEG = -1e30

LANES = 128
SUBLANES = 8
MXU_DIM = 256
ATT_BLOCK = WINDOW
SEQ_TILE = 512
SUB_TILE = 256
FF_CHUNK = 512
GATE_CHUNK = 256
SAMPLE_BLOCK = 16
PREFIX = 16
VMEM_LIMIT_BYTES = 58 * 1024 * 1024
Q_COLS = Q_WIDTH // LANES
HEADS_PER_COL = LANES // HEAD_DIM
N_GATE_CHUNKS = D_MODEL // GATE_CHUNK

OFF_U = 0
OFF_Q = OFF_U + POOL_WIDTH
OFF_K = OFF_Q + Q_WIDTH
OFF_V = OFF_K + KV_WIDTH
OFF_GA = OFF_V + KV_WIDTH
OFF_GB = OFF_GA + D_MODEL

BF16 = jnp.bfloat16
F32 = jnp.float32


def _dot(a, b):
    return jnp.dot(a, b, preferred_element_type=F32)


def _dot_nt(a, b):
    return lax.dot_general(a, b, (((1,), (1,)), ((), ())), preferred_element_type=F32)


def _rms_scale(x):
    return lax.rsqrt(jnp.mean(x * x, axis=-1, keepdims=True) + EPS)


def _head_norm(x, seg, gain):
    width = seg.shape[0]
    sq = (x * x).astype(BF16)
    ss = jnp.concatenate([_dot(sq[:, c:c + width], seg) for c in range(0, x.shape[1], width)], axis=1)
    return (x * lax.rsqrt(ss * (1.0 / HEAD_DIM) + EPS)) * gain


def _kv_head(head):
    return head // GROUP


def _ffn_act(h2, wg_ref, wu_ref, act_ref, rows):
    for start in range(0, D_FF, FF_CHUNK):
        width = min(FF_CHUNK, D_FF - start)
        g = _dot(h2, wg_ref[:, start:start + width])
        u = _dot(h2, wu_ref[:, start:start + width])
        act_ref[rows, start:start + width] = (g * jax.nn.sigmoid(g) * u).astype(BF16)


def _gate_chunk(h, c, w_in_ref, sga_ref, sgb_ref, rows):
    lo = c * GATE_CHUNK
    sga_ref[rows, lo:lo + GATE_CHUNK] = jax.nn.sigmoid(_dot(h, w_in_ref[:, OFF_GA + lo:OFF_GA + lo + GATE_CHUNK]))
    sgb_ref[rows, lo:lo + GATE_CHUNK] = jax.nn.sigmoid(_dot(h, w_in_ref[:, OFF_GB + lo:OFF_GB + lo + GATE_CHUNK]))


def _merge(ypool, yattn, sga_ref, sgb_ref, wpp_ref, wap_ref, merged_ref, rows):
    for c in range(N_GATE_CHUNKS):
        lo = c * GATE_CHUNK
        pp = _dot(ypool, wpp_ref[:, lo:lo + GATE_CHUNK])
        ap = _dot(yattn, wap_ref[:, lo:lo + GATE_CHUNK])
        merged = sga_ref[rows, lo:lo + GATE_CHUNK] * pp + sgb_ref[rows, lo:lo + GATE_CHUNK] * ap
        merged_ref[rows, lo:lo + GATE_CHUNK] = merged.astype(BF16)


def _prompt_kernel(sinks_ref, x_ref, n1_ref, w_in_ref, qn_ref, kn_ref, seg_ref, mixw_ref, pscale_ref,
                   wpp_ref, wap_ref, wout_ref, n2_ref, wg_ref, wu_ref, wd_ref,
                   y_ref, knew_ref, vnew_ref, pnew_ref,
                   h_s, u_s, q_s, kd_s, vd_s, s_s, p_s, ypool_s, yattn_s, sga_s, sgb_s, merged_s, x1_s, h2_s, act_s):
    T = SEQ_TILE
    R = SUB_TILE
    subs = [slice(r0, r0 + R) for r0 in range(0, T, R)]
    n_blocks = T // ATT_BLOCK
    j = pl.program_id(1)
    rows_per_kv = GROUP * ATT_BLOCK

    @pl.when(j == 0)
    def _():
        u_s[0:PREFIX, :] = jnp.zeros((PREFIX, POOL_WIDTH), F32)
        kd_s[:, 0:ATT_BLOCK, :] = jnp.zeros((N_KV_HEADS, ATT_BLOCK, KV_WIDTH), BF16)
        vd_s[:, 0:ATT_BLOCK, :] = jnp.zeros((N_KV_HEADS, ATT_BLOCK, KV_WIDTH), BF16)

    lane_r = lax.broadcasted_iota(jnp.int32, (R, LANES), 1)
    lane = lax.broadcasted_iota(jnp.int32, (ATT_BLOCK, LANES), 1)
    row = lax.broadcasted_iota(jnp.int32, (ATT_BLOCK, 2 * ATT_BLOCK), 0)
    col = lax.broadcasted_iota(jnp.int32, (ATT_BLOCK, 2 * ATT_BLOCK), 1)
    band = (col >= row) & (col <= row + WINDOW)
    first_lo = jnp.where(j == 0, ATT_BLOCK, 0)

    def norm1(rows):
        x = x_ref[0, rows, :]
        h_s[rows, :] = ((x * _rms_scale(x)) * n1_ref[...]).astype(BF16)

    def in_proj(rows):
        h = h_s[rows, :]
        u_s[PREFIX + rows.start:PREFIX + rows.stop, :] = _dot(h, w_in_ref[:, OFF_U:OFF_U + POOL_WIDTH])
        q = _dot(h, w_in_ref[:, OFF_Q:OFF_Q + Q_WIDTH])
        q_s[rows, :] = _head_norm(q, seg_ref[...], qn_ref[...]).astype(BF16)
        k = _dot(h, w_in_ref[:, OFF_K:OFF_K + KV_WIDTH])
        kn = _head_norm(k, seg_ref[0:KV_WIDTH, 0:KV_WIDTH], kn_ref[...])
        v = _dot(h, w_in_ref[:, OFF_V:OFF_V + KV_WIDTH])
        dst_rows = slice(ATT_BLOCK + rows.start, ATT_BLOCK + rows.stop)
        for src, dst in ((kn, kd_s), (v, vd_s)):
            swapped = pltpu.roll(src, HEAD_DIM, 1)
            dst[0, dst_rows, :] = jnp.where(lane_r < HEAD_DIM, src, swapped).astype(BF16)
            dst[1, dst_rows, :] = jnp.where(lane_r < HEAD_DIM, swapped, src).astype(BF16)
        if rows.stop == T:
            knew_ref[0] = kn[R - ATT_BLOCK:, :].T
            vnew_ref[0] = v[R - ATT_BLOCK:, :].T
            pnew_ref[0] = u_s[PREFIX + T - POOL_BUF:PREFIX + T, :]

    def pool(rows):
        pos1 = j * T + rows.start + lax.broadcasted_iota(jnp.int32, (R, 1), 0) + 1
        for g, w in enumerate(POOL_WINDOWS):
            cols = slice(g * POOL_GROUP_WIDTH, (g + 1) * POOL_GROUP_WIDTH)
            a = u_s[rows.start:rows.stop + PREFIX, cols]
            s = a
            shift = 1
            while shift < w:
                s = s + pltpu.roll(s, shift, 0)
                shift *= 2
            inv_cnt = 1.0 / jnp.minimum(pos1, w).astype(F32)
            yg = (s[PREFIX:, :] * inv_cnt - a[PREFIX:, :]).astype(BF16)
            ypool_s[rows, cols] = (_dot(yg, mixw_ref[g]) * pscale_ref[:, cols]).astype(BF16)

    def gates(rows, chunks):
        h = h_s[rows, :]
        for c in chunks:
            _gate_chunk(h, c, w_in_ref, sga_s, sgb_s, rows)

    def scores(b):
        r0 = b * ATT_BLOCK
        for c in range(N_KV_HEADS):
            parts = []
            for p in range(c * Q_COLS // N_KV_HEADS, (c + 1) * Q_COLS // N_KV_HEADS):
                qcol = q_s[r0:r0 + ATT_BLOCK, p * LANES:(p + 1) * LANES]
                parts.append(jnp.where(lane < HEAD_DIM, qcol, jnp.zeros_like(qcol)))
                parts.append(jnp.where(lane >= HEAD_DIM, qcol, jnp.zeros_like(qcol)))
            s_s[b % 2, c * rows_per_kv:(c + 1) * rows_per_kv, :] = _dot_nt(
                jnp.concatenate(parts, axis=0), kd_s[c, r0:r0 + 2 * ATT_BLOCK, :])

    def softmax(b):
        valid = band & (col >= first_lo) if b == 0 else band
        for head in range(N_HEADS):
            sh = jnp.where(valid, s_s[b % 2, head * ATT_BLOCK:(head + 1) * ATT_BLOCK, :], NEG)
            sink = sinks_ref[head]
            m = jnp.maximum(jnp.max(sh, axis=-1, keepdims=True), sink)
            p = jnp.exp(sh - m)
            denom = jnp.sum(p, axis=-1, keepdims=True) + jnp.exp(sink - m)
            p_s[b % 2, head * ATT_BLOCK:(head + 1) * ATT_BLOCK, :] = (p * (1.0 / denom)).astype(BF16)

    def weighted_values(b):
        r0 = b * ATT_BLOCK
        for c in range(N_KV_HEADS):
            o = _dot(p_s[b % 2, c * rows_per_kv:(c + 1) * rows_per_kv, :], vd_s[c, r0:r0 + 2 * ATT_BLOCK, :])
            for pp in range(Q_COLS // N_KV_HEADS):
                p_col = c * Q_COLS // N_KV_HEADS + pp
                o_lo = o[(2 * pp) * ATT_BLOCK:(2 * pp + 1) * ATT_BLOCK, :]
                o_hi = o[(2 * pp + 1) * ATT_BLOCK:(2 * pp + 2) * ATT_BLOCK, :]
                yattn_s[r0:r0 + ATT_BLOCK, p_col * LANES:(p_col + 1) * LANES] = (
                    jnp.where(lane < HEAD_DIM, o_lo, o_hi).astype(BF16))

    def out_proj(rows):
        _merge(ypool_s[rows, :], yattn_s[rows, :], sga_s, sgb_s, wpp_ref, wap_ref, merged_s, rows)
        x1 = x_ref[0, rows, :] + _dot(merged_s[rows, :], wout_ref[...])
        x1_s[rows, :] = x1
        h2_s[rows, :] = ((x1 * _rms_scale(x1)) * n2_ref[...]).astype(BF16)

    for rows in subs:
        norm1(rows)
    for rows in subs:
        in_proj(rows)
    pool(subs[0])
    gates(subs[0], range(N_GATE_CHUNKS))
    for rows in subs[1:]:
        pool(rows)
    later_gates = [(rows, c) for rows in subs[1:] for c in range(N_GATE_CHUNKS)]
    per_block = -(-len(later_gates) // n_blocks)
    scores(0)
    for b in range(n_blocks):
        if b + 1 < n_blocks:
            scores(b + 1)
        for rows, c in later_gates[b * per_block:(b + 1) * per_block]:
            gates(rows, [c])
        softmax(b)
        weighted_values(b)

    u_s[0:PREFIX, :] = u_s[T:T + PREFIX, :]
    kd_s[:, 0:ATT_BLOCK, :] = kd_s[:, T:T + ATT_BLOCK, :]
    vd_s[:, 0:ATT_BLOCK, :] = vd_s[:, T:T + ATT_BLOCK, :]

    for rows in subs:
        out_proj(rows)
    for rows in subs:
        _ffn_act(h2_s[rows, :], wg_ref, wu_ref, act_s, rows)
    for rows in subs:
        y_ref[0, rows, :] = x1_s[rows, :] + _dot(act_s[rows, :], wd_ref[...])


def _const_spec(shape):
    nd = len(shape)
    return pl.BlockSpec(shape, lambda *_: (0,) * nd, pipeline_mode=pl.Buffered(1))


def _prompt_call(x, sinks, weights):
    B, S, _ = x.shape
    T = SEQ_TILE
    in_specs = [pl.BlockSpec(memory_space=pltpu.SMEM),
                pl.BlockSpec((1, T, D_MODEL), lambda b, j: (b, j, 0))]
    in_specs += [_const_spec(w.shape) for w in weights]
    out_shape = (jax.ShapeDtypeStruct((B, S, D_MODEL), F32),
                 jax.ShapeDtypeStruct((B, KV_WIDTH, ATT_BLOCK), F32),
                 jax.ShapeDtypeStruct((B, KV_WIDTH, ATT_BLOCK), F32),
                 jax.ShapeDtypeStruct((B, POOL_BUF, POOL_WIDTH), F32))
    out_specs = (pl.BlockSpec((1, T, D_MODEL), lambda b, j: (b, j, 0)),
                 pl.BlockSpec((1, KV_WIDTH, ATT_BLOCK), lambda b, j: (b, 0, 0)),
                 pl.BlockSpec((1, KV_WIDTH, ATT_BLOCK), lambda b, j: (b, 0, 0)),
                 pl.BlockSpec((1, POOL_BUF, POOL_WIDTH), lambda b, j: (b, 0, 0)))
    scratch = [pltpu.VMEM((T, D_MODEL), BF16),
               pltpu.VMEM((PREFIX + T, POOL_WIDTH), F32),
               pltpu.VMEM((T, Q_WIDTH), BF16),
               pltpu.VMEM((N_KV_HEADS, ATT_BLOCK + T, KV_WIDTH), BF16),
               pltpu.VMEM((N_KV_HEADS, ATT_BLOCK + T, KV_WIDTH), BF16),
               pltpu.VMEM((2, N_HEADS * ATT_BLOCK, 2 * ATT_BLOCK), F32),
               pltpu.VMEM((2, N_HEADS * ATT_BLOCK, 2 * ATT_BLOCK), BF16),
               pltpu.VMEM((T, POOL_WIDTH), BF16),
               pltpu.VMEM((T, Q_WIDTH), BF16),
               pltpu.VMEM((T, D_MODEL), F32),
               pltpu.VMEM((T, D_MODEL), F32),
               pltpu.VMEM((T, D_MODEL), BF16),
               pltpu.VMEM((T, D_MODEL), F32),
               pltpu.VMEM((T, D_MODEL), BF16),
               pltpu.VMEM((T, D_FF), BF16)]
    return pl.pallas_call(
        _prompt_kernel,
        grid=(B, S // T),
        in_specs=in_specs,
        out_specs=out_specs,
        out_shape=out_shape,
        scratch_shapes=scratch,
        compiler_params=pltpu.CompilerParams(
            dimension_semantics=("arbitrary", "arbitrary"),
            vmem_limit_bytes=VMEM_LIMIT_BYTES),
        name="prompt_layer",
    )(sinks, x, *weights)


def _sample_kernel(x_ref, ck_ref, cv_ref, sp_ref, sinkc_ref,
                   n1_ref, w_in_ref, qn_ref, kn_ref, seg_ref, mixw_ref, pscale_ref,
                   wpp_ref, wap_ref, wout_ref, n2_ref, wg_ref, wu_ref, wd_ref,
                   y_ref, knew_ref, vnew_ref, pnew_ref,
                   u_s, k_s, v_s, kt_s, vt_s, q8_s, o8_s, ypre_s, sga_s, sgb_s, merged_s, act_s):
    SB = SAMPLE_BLOCK
    N = x_ref.shape[0]
    i = pl.program_id(0)
    nsteps = pl.num_programs(0)
    lane = lax.broadcasted_iota(jnp.int32, (N, LANES), 1)
    all_rows = slice(0, N)

    @pl.when(i == 0)
    def _():
        x = x_ref[...]
        h = ((x * _rms_scale(x)) * n1_ref[...]).astype(BF16)
        u_s[...] = _dot(h, w_in_ref[:, OFF_U:OFF_U + POOL_WIDTH])
        for gc in range(N_GATE_CHUNKS):
            _gate_chunk(h, gc, w_in_ref, sga_s, sgb_s, all_rows)
        q = _head_norm(_dot(h, w_in_ref[:, OFF_Q:OFF_Q + Q_WIDTH]), seg_ref[...], qn_ref[...])
        kn = _head_norm(_dot(h, w_in_ref[:, OFF_K:OFF_K + KV_WIDTH]), seg_ref[0:KV_WIDTH, 0:KV_WIDTH], kn_ref[...])
        v = _dot(h, w_in_ref[:, OFF_V:OFF_V + KV_WIDTH])
        k_s[...] = kn
        v_s[...] = v
        kt_s[...] = kn.T
        vt_s[...] = v.T
        for p in range(Q_COLS):
            qcol = q[:, p * LANES:(p + 1) * LANES]
            qswap = pltpu.roll(qcol, HEAD_DIM, 1)
            for e in range(HEADS_PER_COL):
                head = p * HEADS_PER_COL + e
                c = _kv_head(head)
                src = qcol if e == c else qswap
                keep = (lane < HEAD_DIM) if c == 0 else (lane >= HEAD_DIM)
                q8_s[pl.ds(head, N, stride=N_HEADS), :] = jnp.where(keep, src, 0.0)

    base = i * SB
    rows = pl.ds(pl.multiple_of(base, SB), SB)

    unew = u_s[rows, :]
    for g, w in enumerate(POOL_WINDOWS):
        cols = slice(g * POOL_GROUP_WIDTH, (g + 1) * POOL_GROUP_WIDTH)
        s = unew[:, cols]
        for r in range(POOL_BUF - (w - 1), POOL_BUF):
            s = s + sp_ref[r, :, cols]
        ypre_s[rows, cols] = (s * (1.0 / w) - unew[:, cols]).astype(BF16)
    pnew_ref[0:POOL_BUF - 1] = sp_ref[1:POOL_BUF]
    pnew_ref[POOL_BUF - 1] = unew

    lane_c = lax.broadcasted_iota(jnp.int32, (KV_WIDTH, WINDOW), 1)
    sink = sinkc_ref[:, 0:1]
    kt_all = kt_s[...]
    vt_all = vt_s[...]
    for bb in range(SB):
        b = base + bb
        lhs = q8_s[pl.ds(pl.multiple_of(b * N_HEADS, N_HEADS), N_HEADS), :]
        kt = ck_ref[bb]
        vt = cv_ref[bb]
        knew = k_s[pl.ds(b, 1), :]
        vnew = v_s[pl.ds(b, 1), :]
        sc = _dot(lhs.astype(BF16), kt.astype(BF16))
        sc_self = jnp.sum(lhs * knew, axis=-1, keepdims=True)
        m = jnp.maximum(jnp.maximum(jnp.max(sc, axis=-1, keepdims=True), sc_self), sink)
        p = jnp.exp(sc - m)
        p_self = jnp.exp(sc_self - m)
        denom = jnp.sum(p, axis=-1, keepdims=True) + p_self + jnp.exp(sink - m)
        inv = 1.0 / denom
        o = _dot_nt((p * inv).astype(BF16), vt.astype(BF16)) + (p_self * inv) * vnew
        o8_s[pl.ds(pl.multiple_of(b * N_HEADS, N_HEADS), N_HEADS), :] = o
        put = (WINDOW - 1) - b
        knew_ref[bb] = jnp.where(lane_c == WINDOW - 1, pltpu.roll(kt_all, put, 1), pltpu.roll(kt, WINDOW - 1, 1))
        vnew_ref[bb] = jnp.where(lane_c == WINDOW - 1, pltpu.roll(vt_all, put, 1), pltpu.roll(vt, WINDOW - 1, 1))

    @pl.when(i == nsteps - 1)
    def _():
        ypool = jnp.concatenate(
            [(_dot(ypre_s[:, g * POOL_GROUP_WIDTH:(g + 1) * POOL_GROUP_WIDTH], mixw_ref[g])
              * pscale_ref[:, g * POOL_GROUP_WIDTH:(g + 1) * POOL_GROUP_WIDTH]).astype(BF16)
             for g in range(len(POOL_WINDOWS))], axis=1)
        ycols = []
        for p in range(Q_COLS):
            halves = []
            for e in range(HEADS_PER_COL):
                head = p * HEADS_PER_COL + e
                oh = o8_s[pl.ds(head, N, stride=N_HEADS), :]
                halves.append(oh if e == _kv_head(head) else pltpu.roll(oh, HEAD_DIM, 1))
            ycols.append(jnp.where(lane < HEAD_DIM, halves[0], halves[1]).astype(BF16))
        yattn = jnp.concatenate(ycols, axis=1)
        _merge(ypool, yattn, sga_s, sgb_s, wpp_ref, wap_ref, merged_s, all_rows)
        x1 = x_ref[...] + _dot(merged_s[...], wout_ref[...])
        h2 = ((x1 * _rms_scale(x1)) * n2_ref[...]).astype(BF16)
        _ffn_act(h2, wg_ref, wu_ref, act_s, all_rows)
        y_ref[...] = x1 + _dot(act_s[...], wd_ref[...])


def _sample_call(x, ck, cv, sp, sinkc, weights):
    N = x.shape[0]
    SB = SAMPLE_BLOCK
    W = WINDOW
    consts = (sinkc,) + tuple(weights)
    in_specs = [_const_spec(x.shape),
                pl.BlockSpec((SB, KV_WIDTH, W), lambda i: (i, 0, 0)),
                pl.BlockSpec((SB, KV_WIDTH, W), lambda i: (i, 0, 0)),
                pl.BlockSpec((POOL_BUF, SB, POOL_WIDTH), lambda i: (0, i, 0))]
    in_specs += [_const_spec(w.shape) for w in consts]
    out_shape = (jax.ShapeDtypeStruct((N, D_MODEL), F32),
                 jax.ShapeDtypeStruct((N, KV_WIDTH, W), F32),
                 jax.ShapeDtypeStruct((N, KV_WIDTH, W), F32),
                 jax.ShapeDtypeStruct((POOL_BUF, N, POOL_WIDTH), F32))
    out_specs = (pl.BlockSpec((N, D_MODEL), lambda i: (0, 0)),
                 pl.BlockSpec((SB, KV_WIDTH, W), lambda i: (i, 0, 0)),
                 pl.BlockSpec((SB, KV_WIDTH, W), lambda i: (i, 0, 0)),
                 pl.BlockSpec((POOL_BUF, SB, POOL_WIDTH), lambda i: (0, i, 0)))
    scratch = [pltpu.VMEM((N, POOL_WIDTH), F32),
               pltpu.VMEM((N, KV_WIDTH), F32),
               pltpu.VMEM((N, KV_WIDTH), F32),
               pltpu.VMEM((KV_WIDTH, N), F32),
               pltpu.VMEM((KV_WIDTH, N), F32),
               pltpu.VMEM((N * N_HEADS, LANES), F32),
               pltpu.VMEM((N * N_HEADS, LANES), F32),
               pltpu.VMEM((N, POOL_WIDTH), BF16),
               pltpu.VMEM((N, D_MODEL), F32),
               pltpu.VMEM((N, D_MODEL), F32),
               pltpu.VMEM((N, D_MODEL), BF16),
               pltpu.VMEM((N, D_FF), BF16)]
    return pl.pallas_call(
        _sample_kernel,
        grid=(N // SB,),
        in_specs=in_specs,
        out_specs=out_specs,
        out_shape=out_shape,
        scratch_shapes=scratch,
        compiler_params=pltpu.CompilerParams(
            dimension_semantics=("arbitrary",),
            vmem_limit_bytes=VMEM_LIMIT_BYTES),
        name="sample_layer",
    )(x, ck, cv, sp, *consts)


def _head_sum_matrix():
    idx = np.arange(MXU_DIM) // HEAD_DIM
    return jnp.asarray(idx[:, None] == idx[None, :], dtype=BF16)


def _cache_to_device_order(c):
    n, w = c.shape[0], c.shape[1]
    return jnp.transpose(c, (0, 2, 3, 1)).reshape(n, KV_WIDTH, w)


def _cache_from_device_order(c):
    n, _, w = c.shape
    return jnp.transpose(c.reshape(n, N_KV_HEADS, HEAD_DIM, w), (0, 3, 1, 2))[None]


@jax.jit
def _forward(x_prompt, x_sample, cache_k, cache_v, state_pool, norm1, w_in, q_norm, k_norm, sinks,
             pool_mix_w, pool_scale, w_pool_proj, w_attn_proj, w_out, norm2, w_gate, w_up, w_down):
    depth = w_in.shape[0]
    assert depth == 1, "single-layer trunk"
    l = 0
    N = x_sample.shape[0]
    assert cache_k.shape[2] == WINDOW and x_sample.shape[1] == 1

    qn = (jnp.tile(q_norm[l], N_HEADS) * (HEAD_DIM ** -0.5)).reshape(1, Q_WIDTH)
    kn = jnp.tile(k_norm[l], N_KV_HEADS).reshape(1, KV_WIDTH)
    weights = (norm1[l].reshape(1, D_MODEL), w_in[l].astype(BF16), qn, kn, _head_sum_matrix(),
               pool_mix_w[l].astype(BF16), pool_scale[l].reshape(1, POOL_WIDTH),
               w_pool_proj[l].astype(BF16), w_attn_proj[l].astype(BF16), w_out[l].astype(BF16),
               norm2[l].reshape(1, D_MODEL), w_gate[l].astype(BF16), w_up[l].astype(BF16),
               w_down[l].astype(BF16))

    y_p, k_p, v_p, pool_p = _prompt_call(x_prompt, sinks[l], weights)

    sink_cols = jnp.broadcast_to(sinks[l][:, None], (N_HEADS, LANES))
    y_s, k_s, v_s, pool_s = _sample_call(
        x_sample.reshape(N, D_MODEL),
        _cache_to_device_order(cache_k[l]), _cache_to_device_order(cache_v[l]),
        jnp.transpose(state_pool[l], (1, 0, 2)), sink_cols, weights)

    return (y_p, y_s.reshape(N, 1, D_MODEL),
            _cache_from_device_order(k_p), _cache_from_device_order(v_p), pool_p[None],
            _cache_from_device_order(k_s), _cache_from_device_order(v_s),
            jnp.transpose(pool_s, (1, 0, 2))[None])


def kernel(x_prompt, x_sample, cache_k, cache_v, state_pool, norm1, w_in, q_norm, k_norm, sinks, pool_mix_w,
           pool_scale, w_pool_proj, w_attn_proj, w_out, norm2, w_gate, w_up, w_down):
    return _forward(x_prompt, x_sample, cache_k, cache_v, state_pool, norm1, w_in, q_norm, k_norm, sinks,
                    pool_mix_w, pool_scale, w_pool_proj, w_attn_proj, w_out, norm2, w_gate, w_up, w_down)
```

```python
import numpy as np
import jax
import jax.numpy as jnp
from jax import lax
from jax.experimental import pallas as pl
from jax.experimental.pallas import tpu as pltpu

D_MODEL = 1024
POOL_WINDOWS = (2, 4, 8, 16)
POOL_GROUP_WIDTH = 128
POOL_WIDTH = 512
POOL_BUF = 15
N_HEADS = 8
N_KV_HEADS = 2
HEAD_DIM = 64
GROUP = N_HEADS // N_KV_HEADS
Q_WIDTH = 512
KV_WIDTH = 128
WINDOW = 128
D_FF = 2816
EPS = 1e-6
NEG = -1e30

LANES = 128
SUBLANES = 8
MXU_DIM = 256
ATT_BLOCK = WINDOW
SEQ_TILE = 512
SUB_TILE = 256
FF_CHUNK = 512
GATE_CHUNK = 256
SAMPLE_BLOCK = 16
PREFIX = 16
VMEM_LIMIT_BYTES = 58 * 1024 * 1024
Q_COLS = Q_WIDTH // LANES
HEADS_PER_COL = LANES // HEAD_DIM
N_GATE_CHUNKS = D_MODEL // GATE_CHUNK

OFF_U = 0
OFF_Q = OFF_U + POOL_WIDTH
OFF_K = OFF_Q + Q_WIDTH
OFF_V = OFF_K + KV_WIDTH
OFF_GA = OFF_V + KV_WIDTH
OFF_GB = OFF_GA + D_MODEL

BF16 = jnp.bfloat16
F32 = jnp.float32


def _dot(a, b):
    return jnp.dot(a, b, preferred_element_type=F32)


def _dot_nt(a, b):
    return lax.dot_general(a, b, (((1,), (1,)), ((), ())), preferred_element_type=F32)


def _rms_scale(x):
    return lax.rsqrt(jnp.mean(x * x, axis=-1, keepdims=True) + EPS)


def _head_norm(x, seg, gain):
    width = seg.shape[0]
    sq = (x * x).astype(BF16)
    ss = jnp.concatenate([_dot(sq[:, c:c + width], seg) for c in range(0, x.shape[1], width)], axis=1)
    return (x * lax.rsqrt(ss * (1.0 / HEAD_DIM) + EPS)) * gain


def _kv_head(head):
    return head // GROUP


def _ffn_act(h2, wg_ref, wu_ref, act_ref, rows):
    for start in range(0, D_FF, FF_CHUNK):
        width = min(FF_CHUNK, D_FF - start)
        g = _dot(h2, wg_ref[:, start:start + width])
        u = _dot(h2, wu_ref[:, start:start + width])
        act_ref[rows, start:start + width] = (g * jax.nn.sigmoid(g) * u).astype(BF16)


def _gate_chunk(h, c, w_in_ref, sga_ref, sgb_ref, rows):
    lo = c * GATE_CHUNK
    sga_ref[rows, lo:lo + GATE_CHUNK] = jax.nn.sigmoid(_dot(h, w_in_ref[:, OFF_GA + lo:OFF_GA + lo + GATE_CHUNK]))
    sgb_ref[rows, lo:lo + GATE_CHUNK] = jax.nn.sigmoid(_dot(h, w_in_ref[:, OFF_GB + lo:OFF_GB + lo + GATE_CHUNK]))


def _merge(ypool, yattn, sga_ref, sgb_ref, wpp_ref, wap_ref, merged_ref, rows):
    for c in range(N_GATE_CHUNKS):
        lo = c * GATE_CHUNK
        pp = _dot(ypool, wpp_ref[:, lo:lo + GATE_CHUNK])
        ap = _dot(yattn, wap_ref[:, lo:lo + GATE_CHUNK])
        merged = sga_ref[rows, lo:lo + GATE_CHUNK] * pp + sgb_ref[rows, lo:lo + GATE_CHUNK] * ap
        merged_ref[rows, lo:lo + GATE_CHUNK] = merged.astype(BF16)


def _prompt_kernel(sinks_ref, x_ref, n1_ref, w_in_ref, qn_ref, kn_ref, seg_ref, mixw_ref, pscale_ref,
                   wpp_ref, wap_ref, wout_ref, n2_ref, wg_ref, wu_ref, wd_ref,
                   y_ref, knew_ref, vnew_ref, pnew_ref,
                   h_s, u_s, q_s, kd_s, vd_s, s_s, p_s, ypool_s, yattn_s, sga_s, sgb_s, merged_s, x1_s, h2_s, act_s):
    T = SEQ_TILE
    R = SUB_TILE
    subs = [slice(r0, r0 + R) for r0 in range(0, T, R)]
    n_blocks = T // ATT_BLOCK
    j = pl.program_id(1)
    rows_per_kv = GROUP * ATT_BLOCK

    @pl.when(j == 0)
    def _():
        u_s[0:PREFIX, :] = jnp.zeros((PREFIX, POOL_WIDTH), F32)
        kd_s[:, 0:ATT_BLOCK, :] = jnp.zeros((N_KV_HEADS, ATT_BLOCK, KV_WIDTH), BF16)
        vd_s[:, 0:ATT_BLOCK, :] = jnp.zeros((N_KV_HEADS, ATT_BLOCK, KV_WIDTH), BF16)

    lane_r = lax.broadcasted_iota(jnp.int32, (R, LANES), 1)
    lane = lax.broadcasted_iota(jnp.int32, (ATT_BLOCK, LANES), 1)
    row = lax.broadcasted_iota(jnp.int32, (ATT_BLOCK, 2 * ATT_BLOCK), 0)
    col = lax.broadcasted_iota(jnp.int32, (ATT_BLOCK, 2 * ATT_BLOCK), 1)
    band = (col >= row) & (col <= row + WINDOW)
    first_lo = jnp.where(j == 0, ATT_BLOCK, 0)

    def norm1(rows):
        x = x_ref[0, rows, :]
        h_s[rows, :] = ((x * _rms_scale(x)) * n1_ref[...]).astype(BF16)

    def in_proj(rows):
        h = h_s[rows, :]
        u_s[PREFIX + rows.start:PREFIX + rows.stop, :] = _dot(h, w_in_ref[:, OFF_U:OFF_U + POOL_WIDTH])
        q = _dot(h, w_in_ref[:, OFF_Q:OFF_Q + Q_WIDTH])
        q_s[rows, :] = _head_norm(q, seg_ref[...], qn_ref[...]).astype(BF16)
        k = _dot(h, w_in_ref[:, OFF_K:OFF_K + KV_WIDTH])
        kn = _head_norm(k, seg_ref[0:KV_WIDTH, 0:KV_WIDTH], kn_ref[...])
        v = _dot(h, w_in_ref[:, OFF_V:OFF_V + KV_WIDTH])
        dst_rows = slice(ATT_BLOCK + rows.start, ATT_BLOCK + rows.stop)
        for src, dst in ((kn, kd_s), (v, vd_s)):
            swapped = pltpu.roll(src, HEAD_DIM, 1)
            dst[0, dst_rows, :] = jnp.where(lane_r < HEAD_DIM, src, swapped).astype(BF16)
            dst[1, dst_rows, :] = jnp.where(lane_r < HEAD_DIM, swapped, src).astype(BF16)
        if rows.stop == T:
            knew_ref[0] = kn[R - ATT_BLOCK:, :].T
            vnew_ref[0] = v[R - ATT_BLOCK:, :].T
            pnew_ref[0] = u_s[PREFIX + T - POOL_BUF:PREFIX + T, :]

    def pool(rows):
        pos1 = j * T + rows.start + lax.broadcasted_iota(jnp.int32, (R, 1), 0) + 1
        for g, w in enumerate(POOL_WINDOWS):
            cols = slice(g * POOL_GROUP_WIDTH, (g + 1) * POOL_GROUP_WIDTH)
            a = u_s[rows.start:rows.stop + PREFIX, cols]
            s = a
            shift = 1
            while shift < w:
                s = s + pltpu.roll(s, shift, 0)
                shift *= 2
            inv_cnt = 1.0 / jnp.minimum(pos1, w).astype(F32)
            yg = (s[PREFIX:, :] * inv_cnt - a[PREFIX:, :]).astype(BF16)
            ypool_s[rows, cols] = (_dot(yg, mixw_ref[g]) * pscale_ref[:, cols]).astype(BF16)

    def gates(rows, chunks):
        h = h_s[rows, :]
        for c in chunks:
            _gate_chunk(h, c, w_in_ref, sga_s, sgb_s, rows)

    def scores(b):
        r0 = b * ATT_BLOCK
        for c in range(N_KV_HEADS):
            parts = []
            for p in range(c * Q_COLS // N_KV_HEADS, (c + 1) * Q_COLS // N_KV_HEADS):
                qcol = q_s[r0:r0 + ATT_BLOCK, p * LANES:(p + 1) * LANES]
                parts.append(jnp.where(lane < HEAD_DIM, qcol, jnp.zeros_like(qcol)))
                parts.append(jnp.where(lane >= HEAD_DIM, qcol, jnp.zeros_like(qcol)))
            s_s[b % 2, c * rows_per_kv:(c + 1) * rows_per_kv, :] = _dot_nt(
                jnp.concatenate(parts, axis=0), kd_s[c, r0:r0 + 2 * ATT_BLOCK, :])

    def softmax(b):
        valid = band & (col >= first_lo) if b == 0 else band
        for head in range(N_HEADS):
            sh = jnp.where(valid, s_s[b % 2, head * ATT_BLOCK:(head + 1) * ATT_BLOCK, :], NEG)
            sink = sinks_ref[head]
            m = jnp.maximum(jnp.max(sh, axis=-1, keepdims=True), sink)
            p = jnp.exp(sh - m)
            denom = jnp.sum(p, axis=-1, keepdims=True) + jnp.exp(sink - m)
            p_s[b % 2, head * ATT_BLOCK:(head + 1) * ATT_BLOCK, :] = (p * (1.0 / denom)).astype(BF16)

    def weighted_values(b):
        r0 = b * ATT_BLOCK
        for c in range(N_KV_HEADS):
            o = _dot(p_s[b % 2, c * rows_per_kv:(c + 1) * rows_per_kv, :], vd_s[c, r0:r0 + 2 * ATT_BLOCK, :])
            for pp in range(Q_COLS // N_KV_HEADS):
                p_col = c * Q_COLS // N_KV_HEADS + pp
                o_lo = o[(2 * pp) * ATT_BLOCK:(2 * pp + 1) * ATT_BLOCK, :]
                o_hi = o[(2 * pp + 1) * ATT_BLOCK:(2 * pp + 2) * ATT_BLOCK, :]
                yattn_s[r0:r0 + ATT_BLOCK, p_col * LANES:(p_col + 1) * LANES] = (
                    jnp.where(lane < HEAD_DIM, o_lo, o_hi).astype(BF16))

    def out_proj(rows):
        _merge(ypool_s[rows, :], yattn_s[rows, :], sga_s, sgb_s, wpp_ref, wap_ref, merged_s, rows)
        x1 = x_ref[0, rows, :] + _dot(merged_s[rows, :], wout_ref[...])
        x1_s[rows, :] = x1
        h2_s[rows, :] = ((x1 * _rms_scale(x1)) * n2_ref[...]).astype(BF16)

    for rows in subs:
        norm1(rows)
    for rows in subs:
        in_proj(rows)
    pool(subs[0])
    gates(subs[0], range(N_GATE_CHUNKS))
    for rows in subs[1:]:
        pool(rows)
    later_gates = [(rows, c) for rows in subs[1:] for c in range(N_GATE_CHUNKS)]
    per_block = -(-len(later_gates) // n_blocks)
    scores(0)
    for b in range(n_blocks):
        if b + 1 < n_blocks:
            scores(b + 1)
        for rows, c in later_gates[b * per_block:(b + 1) * per_block]:
            gates(rows, [c])
        softmax(b)
        weighted_values(b)

    u_s[0:PREFIX, :] = u_s[T:T + PREFIX, :]
    kd_s[:, 0:ATT_BLOCK, :] = kd_s[:, T:T + ATT_BLOCK, :]
    vd_s[:, 0:ATT_BLOCK, :] = vd_s[:, T:T + ATT_BLOCK, :]

    for rows in subs:
        out_proj(rows)
    for rows in subs:
        _ffn_act(h2_s[rows, :], wg_ref, wu_ref, act_s, rows)
    for rows in subs:
        y_ref[0, rows, :] = x1_s[rows, :] + _dot(act_s[rows, :], wd_ref[...])


def _const_spec(shape):
    nd = len(shape)
    return pl.BlockSpec(shape, lambda *_: (0,) * nd, pipeline_mode=pl.Buffered(1))


def _prompt_call(x, sinks, weights):
    B, S, _ = x.shape
    T = SEQ_TILE
    in_specs = [pl.BlockSpec(memory_space=pltpu.SMEM),
                pl.BlockSpec((1, T, D_MODEL), lambda b, j: (b, j, 0))]
    in_specs += [_const_spec(w.shape) for w in weights]
    out_shape = (jax.ShapeDtypeStruct((B, S, D_MODEL), F32),
                 jax.ShapeDtypeStruct((B, KV_WIDTH, ATT_BLOCK), F32),
                 jax.ShapeDtypeStruct((B, KV_WIDTH, ATT_BLOCK), F32),
                 jax.ShapeDtypeStruct((B, POOL_BUF, POOL_WIDTH), F32))
    out_specs = (pl.BlockSpec((1, T, D_MODEL), lambda b, j: (b, j, 0)),
                 pl.BlockSpec((1, KV_WIDTH, ATT_BLOCK), lambda b, j: (b, 0, 0)),
                 pl.BlockSpec((1, KV_WIDTH, ATT_BLOCK), lambda b, j: (b, 0, 0)),
                 pl.BlockSpec((1, POOL_BUF, POOL_WIDTH), lambda b, j: (b, 0, 0)))
    scratch = [pltpu.VMEM((T, D_MODEL), BF16),
               pltpu.VMEM((PREFIX + T, POOL_WIDTH), F32),
               pltpu.VMEM((T, Q_WIDTH), BF16),
               pltpu.VMEM((N_KV_HEADS, ATT_BLOCK + T, KV_WIDTH), BF16),
               pltpu.VMEM((N_KV_HEADS, ATT_BLOCK + T, KV_WIDTH), BF16),
               pltpu.VMEM((2, N_HEADS * ATT_BLOCK, 2 * ATT_BLOCK), F32),
               pltpu.VMEM((2, N_HEADS * ATT_BLOCK, 2 * ATT_BLOCK), BF16),
               pltpu.VMEM((T, POOL_WIDTH), BF16),
               pltpu.VMEM((T, Q_WIDTH), BF16),
               pltpu.VMEM((T, D_MODEL), F32),
               pltpu.VMEM((T, D_MODEL), F32),
               pltpu.VMEM((T, D_MODEL), BF16),
               pltpu.VMEM((T, D_MODEL), F32),
               pltpu.VMEM((T, D_MODEL), BF16),
               pltpu.VMEM((T, D_FF), BF16)]
    return pl.pallas_call(
        _prompt_kernel,
        grid=(B, S // T),
        in_specs=in_specs,
        out_specs=out_specs,
        out_shape=out_shape,
        scratch_shapes=scratch,
        compiler_params=pltpu.CompilerParams(
            dimension_semantics=("arbitrary", "arbitrary"),
            vmem_limit_bytes=VMEM_LIMIT_BYTES),
        name="prompt_layer",
    )(sinks, x, *weights)


def _sample_kernel(x_ref, ck_ref, cv_ref, sp_ref, sinkc_ref,
                   n1_ref, w_in_ref, qn_ref, kn_ref, seg_ref, mixw_ref, pscale_ref,
                   wpp_ref, wap_ref, wout_ref, n2_ref, wg_ref, wu_ref, wd_ref,
                   y_ref, knew_ref, vnew_ref, pnew_ref,
                   u_s, kt_s, vt_s, q8_s, k8_s, v8_s, o8_s, ypre_s, sga_s, sgb_s, merged_s, act_s):
    SB = SAMPLE_BLOCK
    N = x_ref.shape[0]
    i = pl.program_id(0)
    nsteps = pl.num_programs(0)
    lane = lax.broadcasted_iota(jnp.int32, (N, LANES), 1)
    all_rows = slice(0, N)

    @pl.when(i == 0)
    def _():
        x = x_ref[...]
        h = ((x * _rms_scale(x)) * n1_ref[...]).astype(BF16)
        u_s[...] = _dot(h, w_in_ref[:, OFF_U:OFF_U + POOL_WIDTH])
        for gc in range(N_GATE_CHUNKS):
            _gate_chunk(h, gc, w_in_ref, sga_s, sgb_s, all_rows)
        q = _head_norm(_dot(h, w_in_ref[:, OFF_Q:OFF_Q + Q_WIDTH]), seg_ref[...], qn_ref[...])
        kn = _head_norm(_dot(h, w_in_ref[:, OFF_K:OFF_K + KV_WIDTH]), seg_ref[0:KV_WIDTH, 0:KV_WIDTH], kn_ref[...])
        v = _dot(h, w_in_ref[:, OFF_V:OFF_V + KV_WIDTH])
        kt_s[...] = kn.T
        vt_s[...] = v.T
        for head in range(N_HEADS):
            k8_s[pl.ds(head, N, stride=N_HEADS), :] = kn
            v8_s[pl.ds(head, N, stride=N_HEADS), :] = v
        for p in range(Q_COLS):
            qcol = q[:, p * LANES:(p + 1) * LANES]
            qswap = pltpu.roll(qcol, HEAD_DIM, 1)
            for e in range(HEADS_PER_COL):
                head = p * HEADS_PER_COL + e
                c = _kv_head(head)
                src = qcol if e == c else qswap
                keep = (lane < HEAD_DIM) if c == 0 else (lane >= HEAD_DIM)
                q8_s[pl.ds(head, N, stride=N_HEADS), :] = jnp.where(keep, src, 0.0)

    base = i * SB
    rows = pl.ds(pl.multiple_of(base, SB), SB)

    unew = u_s[rows, :]
    for g, w in enumerate(POOL_WINDOWS):
        cols = slice(g * POOL_GROUP_WIDTH, (g + 1) * POOL_GROUP_WIDTH)
        s = unew[:, cols]
        for r in range(POOL_BUF - (w - 1), POOL_BUF):
            s = s + sp_ref[r, :, cols]
        ypre_s[rows, cols] = (s * (1.0 / w) - unew[:, cols]).astype(BF16)
    pnew_ref[0:POOL_BUF - 1] = sp_ref[1:POOL_BUF]
    pnew_ref[POOL_BUF - 1] = unew

    hrows = pl.ds(pl.multiple_of(base * N_HEADS, SB * N_HEADS), SB * N_HEADS)
    lhs = q8_s[hrows, :]
    sc = jnp.concatenate(
        [_dot(lhs[bb * N_HEADS:(bb + 1) * N_HEADS, :].astype(BF16), ck_ref[bb].astype(BF16)) for bb in range(SB)],
        axis=0)
    sc_self = jnp.sum(lhs * k8_s[hrows, :], axis=-1, keepdims=True)
    sink = sinkc_ref[:, 0:1]
    m = jnp.maximum(jnp.maximum(jnp.max(sc, axis=-1, keepdims=True), sc_self), sink)
    p = jnp.exp(sc - m)
    p_self = jnp.exp(sc_self - m)
    denom = jnp.sum(p, axis=-1, keepdims=True) + p_self + jnp.exp(sink - m)
    inv = 1.0 / denom
    pn = p * inv
    o = jnp.concatenate(
        [_dot_nt(pn[bb * N_HEADS:(bb + 1) * N_HEADS, :].astype(BF16), cv_ref[bb].astype(BF16)) for bb in range(SB)],
        axis=0)
    o8_s[hrows, :] = o + (p_self * inv) * v8_s[hrows, :]

    lane_c = lax.broadcasted_iota(jnp.int32, (KV_WIDTH, WINDOW), 1)
    kt_all = kt_s[...]
    vt_all = vt_s[...]
    for bb in range(SB):
        put = (WINDOW - 1) - (base + bb)
        knew_ref[bb] = jnp.where(lane_c == WINDOW - 1, pltpu.roll(kt_all, put, 1),
                                 pltpu.roll(ck_ref[bb], WINDOW - 1, 1))
        vnew_ref[bb] = jnp.where(lane_c == WINDOW - 1, pltpu.roll(vt_all, put, 1),
                                 pltpu.roll(cv_ref[bb], WINDOW - 1, 1))

    @pl.when(i == nsteps - 1)
    def _():
        ypool = jnp.concatenate(
            [(_dot(ypre_s[:, g * POOL_GROUP_WIDTH:(g + 1) * POOL_GROUP_WIDTH], mixw_ref[g])
              * pscale_ref[:, g * POOL_GROUP_WIDTH:(g + 1) * POOL_GROUP_WIDTH]).astype(BF16)
             for g in range(len(POOL_WINDOWS))], axis=1)
        ycols = []
        for p in range(Q_COLS):
            halves = []
            for e in range(HEADS_PER_COL):
                head = p * HEADS_PER_COL + e
                oh = o8_s[pl.ds(head, N, stride=N_HEADS), :]
                halves.append(oh if e == _kv_head(head) else pltpu.roll(oh, HEAD_DIM, 1))
            ycols.append(jnp.where(lane < HEAD_DIM, halves[0], halves[1]).astype(BF16))
        yattn = jnp.concatenate(ycols, axis=1)
        _merge(ypool, yattn, sga_s, sgb_s, wpp_ref, wap_ref, merged_s, all_rows)
        x1 = x_ref[...] + _dot(merged_s[...], wout_ref[...])
        h2 = ((x1 * _rms_scale(x1)) * n2_ref[...]).astype(BF16)
        _ffn_act(h2, wg_ref, wu_ref, act_s, all_rows)
        y_ref[...] = x1 + _dot(act_s[...], wd_ref[...])


def _sample_call(x, ck, cv, sp, sinkc, weights):
    N = x.shape[0]
    SB = SAMPLE_BLOCK
    W = WINDOW
    consts = (sinkc,) + tuple(weights)
    in_specs = [_const_spec(x.shape),
                pl.BlockSpec((SB, KV_WIDTH, W), lambda i: (i, 0, 0)),
                pl.BlockSpec((SB, KV_WIDTH, W), lambda i: (i, 0, 0)),
                pl.BlockSpec((POOL_BUF, SB, POOL_WIDTH), lambda i: (0, i, 0))]
    in_specs += [_const_spec(w.shape) for w in consts]
    out_shape = (jax.ShapeDtypeStruct((N, D_MODEL), F32),
                 jax.ShapeDtypeStruct((N, KV_WIDTH, W), F32),
                 jax.ShapeDtypeStruct((N, KV_WIDTH, W), F32),
                 jax.ShapeDtypeStruct((POOL_BUF, N, POOL_WIDTH), F32))
    out_specs = (pl.BlockSpec((N, D_MODEL), lambda i: (0, 0)),
                 pl.BlockSpec((SB, KV_WIDTH, W), lambda i: (i, 0, 0)),
                 pl.BlockSpec((SB, KV_WIDTH, W), lambda i: (i, 0, 0)),
                 pl.BlockSpec((POOL_BUF, SB, POOL_WIDTH), lambda i: (0, i, 0)))
    scratch = [pltpu.VMEM((N, POOL_WIDTH), F32),
               pltpu.VMEM((KV_WIDTH, N), F32),
               pltpu.VMEM((KV_WIDTH, N), F32),
               pltpu.VMEM((N * N_HEADS, LANES), F32),
               pltpu.VMEM((N * N_HEADS, LANES), F32),
               pltpu.VMEM((N * N_HEADS, LANES), F32),
               pltpu.VMEM((N * N_HEADS, LANES), F32),
               pltpu.VMEM((N, POOL_WIDTH), BF16),
               pltpu.VMEM((N, D_MODEL), F32),
               pltpu.VMEM((N, D_MODEL), F32),
               pltpu.VMEM((N, D_MODEL), BF16),
               pltpu.VMEM((N, D_FF), BF16)]
    return pl.pallas_call(
        _sample_kernel,
        grid=(N // SB,),
        in_specs=in_specs,
        out_specs=out_specs,
        out_shape=out_shape,
        scratch_shapes=scratch,
        compiler_params=pltpu.CompilerParams(
            dimension_semantics=("arbitrary",),
            vmem_limit_bytes=VMEM_LIMIT_BYTES),
        name="sample_layer",
    )(x, ck, cv, sp, *consts)


def _head_sum_matrix():
    idx = np.arange(MXU_DIM) // HEAD_DIM
    return jnp.asarray(idx[:, None] == idx[None, :], dtype=BF16)


def _cache_to_device_order(c):
    n, w = c.shape[0], c.shape[1]
    return jnp.transpose(c, (0, 2, 3, 1)).reshape(n, KV_WIDTH, w)


def _cache_from_device_order(c):
    n, _, w = c.shape
    return jnp.transpose(c.reshape(n, N_KV_HEADS, HEAD_DIM, w), (0, 3, 1, 2))[None]


@jax.jit
def _forward(x_prompt, x_sample, cache_k, cache_v, state_pool, norm1, w_in, q_norm, k_norm, sinks,
             pool_mix_w, pool_scale, w_pool_proj, w_attn_proj, w_out, norm2, w_gate, w_up, w_down):
    depth = w_in.shape[0]
    assert depth == 1, "single-layer trunk"
    l = 0
    N = x_sample.shape[0]
    assert cache_k.shape[2] == WINDOW and x_sample.shape[1] == 1

    qn = (jnp.tile(q_norm[l], N_HEADS) * (HEAD_DIM ** -0.5)).reshape(1, Q_WIDTH)
    kn = jnp.tile(k_norm[l], N_KV_HEADS).reshape(1, KV_WIDTH)
    weights = (norm1[l].reshape(1, D_MODEL), w_in[l].astype(BF16), qn, kn, _head_sum_matrix(),
               pool_mix_w[l].astype(BF16), pool_scale[l].reshape(1, POOL_WIDTH),
               w_pool_proj[l].astype(BF16), w_attn_proj[l].astype(BF16), w_out[l].astype(BF16),
               norm2[l].reshape(1, D_MODEL), w_gate[l].astype(BF16), w_up[l].astype(BF16),
               w_down[l].astype(BF16))

    y_p, k_p, v_p, pool_p = _prompt_call(x_prompt, sinks[l], weights)

    sink_cols = jnp.broadcast_to(jnp.tile(sinks[l], SAMPLE_BLOCK)[:, None], (SAMPLE_BLOCK * N_HEADS, LANES))
    y_s, k_s, v_s, pool_s = _sample_call(
        x_sample.reshape(N, D_MODEL),
        _cache_to_device_order(cache_k[l]), _cache_to_device_order(cache_v[l]),
        jnp.transpose(state_pool[l], (1, 0, 2)), sink_cols, weights)

    return (y_p, y_s.reshape(N, 1, D_MODEL),
            _cache_from_device_order(k_p), _cache_from_device_order(v_p), pool_p[None],
            _cache_from_device_order(k_s), _cache_from_device_order(v_s),
            jnp.transpose(pool_s, (1, 0, 2))[None])


def kernel(x_prompt, x_sample, cache_k, cache_v, state_pool, norm1, w_in, q_norm, k_norm, sinks, pool_mix_w,
           pool_scale, w_pool_proj, w_attn_proj, w_out, norm2, w_gate, w_up, w_down):
    return _forward(x_prompt, x_sample, cache_k, cache_v, state_pool, norm1, w_in, q_norm, k_norm, sinks,
                    pool_mix_w, pool_scale, w_pool_proj, w_attn_proj, w_out, norm2, w_gate, w_up, w_down)
```

```python
import numpy as np
import jax
import jax.numpy as jnp
from jax import lax
from jax.experimental import pallas as pl
from jax.experimental.pallas import tpu as pltpu

D_MODEL = 1024
POOL_WINDOWS = (2, 4, 8, 16)
POOL_GROUP_WIDTH = 128
POOL_WIDTH = 512
POOL_BUF = 15
N_HEADS = 8
N_KV_HEADS = 2
HEAD_DIM = 64
GROUP = N_HEADS // N_KV_HEADS
Q_WIDTH = 512
KV_WIDTH = 128
WINDOW = 128
D_FF = 2816
EPS = 1e-6
NEG = -1e30

LANES = 128
SUBLANES = 8
MXU_DIM = 256
ATT_BLOCK = WINDOW
SEQ_TILE = 512
SUB_TILE = 256
FF_CHUNK = 512
GATE_CHUNK = 256
SAMPLE_BLOCK = 16
PREFIX = 16
VMEM_LIMIT_BYTES = 58 * 1024 * 1024
Q_COLS = Q_WIDTH // LANES
HEADS_PER_COL = LANES // HEAD_DIM
N_GATE_CHUNKS = D_MODEL // GATE_CHUNK

OFF_U = 0
OFF_Q = OFF_U + POOL_WIDTH
OFF_K = OFF_Q + Q_WIDTH
OFF_V = OFF_K + KV_WIDTH
OFF_GA = OFF_V + KV_WIDTH
OFF_GB = OFF_GA + D_MODEL

BF16 = jnp.bfloat16
F32 = jnp.float32


def _dot(a, b):
    return jnp.dot(a, b, preferred_element_type=F32)


def _dot_nt(a, b):
    return lax.dot_general(a, b, (((1,), (1,)), ((), ())), preferred_element_type=F32)


def _rms_scale(x):
    return lax.rsqrt(jnp.mean(x * x, axis=-1, keepdims=True) + EPS)


def _head_norm(x, seg, gain):
    width = seg.shape[0]
    sq = (x * x).astype(BF16)
    ss = jnp.concatenate([_dot(sq[:, c:c + width], seg) for c in range(0, x.shape[1], width)], axis=1)
    return (x * lax.rsqrt(ss * (1.0 / HEAD_DIM) + EPS)) * gain


def _kv_head(head):
    return head // GROUP


def _ffn_act(h2, wg_ref, wu_ref, act_ref, rows):
    for start in range(0, D_FF, FF_CHUNK):
        width = min(FF_CHUNK, D_FF - start)
        g = _dot(h2, wg_ref[:, start:start + width])
        u = _dot(h2, wu_ref[:, start:start + width])
        act_ref[rows, start:start + width] = (g * jax.nn.sigmoid(g) * u).astype(BF16)


def _gate_chunk(h, c, w_in_ref, sga_ref, sgb_ref, rows):
    lo = c * GATE_CHUNK
    sga_ref[rows, lo:lo + GATE_CHUNK] = jax.nn.sigmoid(_dot(h, w_in_ref[:, OFF_GA + lo:OFF_GA + lo + GATE_CHUNK]))
    sgb_ref[rows, lo:lo + GATE_CHUNK] = jax.nn.sigmoid(_dot(h, w_in_ref[:, OFF_GB + lo:OFF_GB + lo + GATE_CHUNK]))


def _merge(ypool, yattn, sga_ref, sgb_ref, wpp_ref, wap_ref, merged_ref, rows):
    for c in range(N_GATE_CHUNKS):
        lo = c * GATE_CHUNK
        pp = _dot(ypool, wpp_ref[:, lo:lo + GATE_CHUNK])
        ap = _dot(yattn, wap_ref[:, lo:lo + GATE_CHUNK])
        merged = sga_ref[rows, lo:lo + GATE_CHUNK] * pp + sgb_ref[rows, lo:lo + GATE_CHUNK] * ap
        merged_ref[rows, lo:lo + GATE_CHUNK] = merged.astype(BF16)


def _prompt_kernel(sinks_ref, x_ref, n1_ref, w_in_ref, qn_ref, kn_ref, seg_ref, mixw_ref, pscale_ref,
                   wpp_ref, wap_ref, wout_ref, n2_ref, wg_ref, wu_ref, wd_ref,
                   y_ref, knew_ref, vnew_ref, pnew_ref,
                   h_s, u_s, q_s, kd_s, vd_s, s_s, p_s, ypool_s, yattn_s, sga_s, sgb_s, merged_s, x1_s, h2_s, act_s):
    T = SEQ_TILE
    R = SUB_TILE
    subs = [slice(r0, r0 + R) for r0 in range(0, T, R)]
    n_blocks = T // ATT_BLOCK
    j = pl.program_id(1)
    rows_per_kv = GROUP * ATT_BLOCK

    @pl.when(j == 0)
    def _():
        u_s[0:PREFIX, :] = jnp.zeros((PREFIX, POOL_WIDTH), F32)
        kd_s[:, 0:ATT_BLOCK, :] = jnp.zeros((N_KV_HEADS, ATT_BLOCK, KV_WIDTH), BF16)
        vd_s[:, 0:ATT_BLOCK, :] = jnp.zeros((N_KV_HEADS, ATT_BLOCK, KV_WIDTH), BF16)

    lane_r = lax.broadcasted_iota(jnp.int32, (R, LANES), 1)
    lane = lax.broadcasted_iota(jnp.int32, (ATT_BLOCK, LANES), 1)
    row = lax.broadcasted_iota(jnp.int32, (ATT_BLOCK, 2 * ATT_BLOCK), 0)
    col = lax.broadcasted_iota(jnp.int32, (ATT_BLOCK, 2 * ATT_BLOCK), 1)
    band = (col >= row) & (col <= row + WINDOW)
    first_lo = jnp.where(j == 0, ATT_BLOCK, 0)

    def norm1(rows):
        x = x_ref[0, rows, :]
        h_s[rows, :] = ((x * _rms_scale(x)) * n1_ref[...]).astype(BF16)

    def in_proj(rows):
        h = h_s[rows, :]
        u_s[PREFIX + rows.start:PREFIX + rows.stop, :] = _dot(h, w_in_ref[:, OFF_U:OFF_U + POOL_WIDTH])
        q = _dot(h, w_in_ref[:, OFF_Q:OFF_Q + Q_WIDTH])
        q_s[rows, :] = _head_norm(q, seg_ref[...], qn_ref[...]).astype(BF16)
        k = _dot(h, w_in_ref[:, OFF_K:OFF_K + KV_WIDTH])
        kn = _head_norm(k, seg_ref[0:KV_WIDTH, 0:KV_WIDTH], kn_ref[...])
        v = _dot(h, w_in_ref[:, OFF_V:OFF_V + KV_WIDTH])
        dst_rows = slice(ATT_BLOCK + rows.start, ATT_BLOCK + rows.stop)
        for src, dst in ((kn, kd_s), (v, vd_s)):
            swapped = pltpu.roll(src, HEAD_DIM, 1)
            dst[0, dst_rows, :] = jnp.where(lane_r < HEAD_DIM, src, swapped).astype(BF16)
            dst[1, dst_rows, :] = jnp.where(lane_r < HEAD_DIM, swapped, src).astype(BF16)
        if rows.stop == T:
            knew_ref[0] = kn[R - ATT_BLOCK:, :].T
            vnew_ref[0] = v[R - ATT_BLOCK:, :].T
            pnew_ref[0] = u_s[PREFIX + T - POOL_BUF:PREFIX + T, :]

    def pool(rows):
        pos1 = j * T + rows.start + lax.broadcasted_iota(jnp.int32, (R, 1), 0) + 1
        for g, w in enumerate(POOL_WINDOWS):
            cols = slice(g * POOL_GROUP_WIDTH, (g + 1) * POOL_GROUP_WIDTH)
            a = u_s[rows.start:rows.stop + PREFIX, cols]
            s = a
            shift = 1
            while shift < w:
                s = s + pltpu.roll(s, shift, 0)
                shift *= 2
            inv_cnt = 1.0 / jnp.minimum(pos1, w).astype(F32)
            yg = (s[PREFIX:, :] * inv_cnt - a[PREFIX:, :]).astype(BF16)
            ypool_s[rows, cols] = (_dot(yg, mixw_ref[g]) * pscale_ref[:, cols]).astype(BF16)

    def gates(rows, chunks):
        h = h_s[rows, :]
        for c in chunks:
            _gate_chunk(h, c, w_in_ref, sga_s, sgb_s, rows)

    def scores(b):
        r0 = b * ATT_BLOCK
        for c in range(N_KV_HEADS):
            parts = []
            for p in range(c * Q_COLS // N_KV_HEADS, (c + 1) * Q_COLS // N_KV_HEADS):
                qcol = q_s[r0:r0 + ATT_BLOCK, p * LANES:(p + 1) * LANES]
                parts.append(jnp.where(lane < HEAD_DIM, qcol, jnp.zeros_like(qcol)))
                parts.append(jnp.where(lane >= HEAD_DIM, qcol, jnp.zeros_like(qcol)))
            s_s[b % 2, c * rows_per_kv:(c + 1) * rows_per_kv, :] = _dot_nt(
                jnp.concatenate(parts, axis=0), kd_s[c, r0:r0 + 2 * ATT_BLOCK, :])

    def softmax(b):
        valid = band & (col >= first_lo) if b == 0 else band
        for head in range(N_HEADS):
            sh = jnp.where(valid, s_s[b % 2, head * ATT_BLOCK:(head + 1) * ATT_BLOCK, :], NEG)
            sink = sinks_ref[head]
            m = jnp.maximum(jnp.max(sh, axis=-1, keepdims=True), sink)
            p = jnp.exp(sh - m)
            denom = jnp.sum(p, axis=-1, keepdims=True) + jnp.exp(sink - m)
            p_s[b % 2, head * ATT_BLOCK:(head + 1) * ATT_BLOCK, :] = (p * (1.0 / denom)).astype(BF16)

    def weighted_values(b):
        r0 = b * ATT_BLOCK
        for c in range(N_KV_HEADS):
            o = _dot(p_s[b % 2, c * rows_per_kv:(c + 1) * rows_per_kv, :], vd_s[c, r0:r0 + 2 * ATT_BLOCK, :])
            for pp in range(Q_COLS // N_KV_HEADS):
                p_col = c * Q_COLS // N_KV_HEADS + pp
                o_lo = o[(2 * pp) * ATT_BLOCK:(2 * pp + 1) * ATT_BLOCK, :]
                o_hi = o[(2 * pp + 1) * ATT_BLOCK:(2 * pp + 2) * ATT_BLOCK, :]
                yattn_s[r0:r0 + ATT_BLOCK, p_col * LANES:(p_col + 1) * LANES] = (
                    jnp.where(lane < HEAD_DIM, o_lo, o_hi).astype(BF16))

    def out_proj(rows):
        _merge(ypool_s[rows, :], yattn_s[rows, :], sga_s, sgb_s, wpp_ref, wap_ref, merged_s, rows)
        x1 = x_ref[0, rows, :] + _dot(merged_s[rows, :], wout_ref[...])
        x1_s[rows, :] = x1
        h2_s[rows, :] = ((x1 * _rms_scale(x1)) * n2_ref[...]).astype(BF16)

    for rows in subs:
        norm1(rows)
    for rows in subs:
        in_proj(rows)
    pool(subs[0])
    gates(subs[0], range(N_GATE_CHUNKS))
    for rows in subs[1:]:
        pool(rows)
    later_gates = [(rows, c) for rows in subs[1:] for c in range(N_GATE_CHUNKS)]
    per_block = -(-len(later_gates) // n_blocks)
    scores(0)
    for b in range(n_blocks):
        if b + 1 < n_blocks:
            scores(b + 1)
        for rows, c in later_gates[b * per_block:(b + 1) * per_block]:
            gates(rows, [c])
        softmax(b)
        weighted_values(b)

    u_s[0:PREFIX, :] = u_s[T:T + PREFIX, :]
    kd_s[:, 0:ATT_BLOCK, :] = kd_s[:, T:T + ATT_BLOCK, :]
    vd_s[:, 0:ATT_BLOCK, :] = vd_s[:, T:T + ATT_BLOCK, :]

    for rows in subs:
        out_proj(rows)
    for rows in subs:
        _ffn_act(h2_s[rows, :], wg_ref, wu_ref, act_s, rows)
    for rows in subs:
        y_ref[0, rows, :] = x1_s[rows, :] + _dot(act_s[rows, :], wd_ref[...])


def _const_spec(shape):
    nd = len(shape)
    return pl.BlockSpec(shape, lambda *_: (0,) * nd, pipeline_mode=pl.Buffered(1))


def _prompt_call(x, sinks, weights):
    B, S, _ = x.shape
    T = SEQ_TILE
    in_specs = [pl.BlockSpec(memory_space=pltpu.SMEM),
                pl.BlockSpec((1, T, D_MODEL), lambda b, j: (b, j, 0))]
    in_specs += [_const_spec(w.shape) for w in weights]
    out_shape = (jax.ShapeDtypeStruct((B, S, D_MODEL), F32),
                 jax.ShapeDtypeStruct((B, KV_WIDTH, ATT_BLOCK), F32),
                 jax.ShapeDtypeStruct((B, KV_WIDTH, ATT_BLOCK), F32),
                 jax.ShapeDtypeStruct((B, POOL_BUF, POOL_WIDTH), F32))
    out_specs = (pl.BlockSpec((1, T, D_MODEL), lambda b, j: (b, j, 0)),
                 pl.BlockSpec((1, KV_WIDTH, ATT_BLOCK), lambda b, j: (b, 0, 0)),
                 pl.BlockSpec((1, KV_WIDTH, ATT_BLOCK), lambda b, j: (b, 0, 0)),
                 pl.BlockSpec((1, POOL_BUF, POOL_WIDTH), lambda b, j: (b, 0, 0)))
    scratch = [pltpu.VMEM((T, D_MODEL), BF16),
               pltpu.VMEM((PREFIX + T, POOL_WIDTH), F32),
               pltpu.VMEM((T, Q_WIDTH), BF16),
               pltpu.VMEM((N_KV_HEADS, ATT_BLOCK + T, KV_WIDTH), BF16),
               pltpu.VMEM((N_KV_HEADS, ATT_BLOCK + T, KV_WIDTH), BF16),
               pltpu.VMEM((2, N_HEADS * ATT_BLOCK, 2 * ATT_BLOCK), F32),
               pltpu.VMEM((2, N_HEADS * ATT_BLOCK, 2 * ATT_BLOCK), BF16),
               pltpu.VMEM((T, POOL_WIDTH), BF16),
               pltpu.VMEM((T, Q_WIDTH), BF16),
               pltpu.VMEM((T, D_MODEL), F32),
               pltpu.VMEM((T, D_MODEL), F32),
               pltpu.VMEM((T, D_MODEL), BF16),
               pltpu.VMEM((T, D_MODEL), F32),
               pltpu.VMEM((T, D_MODEL), BF16),
               pltpu.VMEM((T, D_FF), BF16)]
    return pl.pallas_call(
        _prompt_kernel,
        grid=(B, S // T),
        in_specs=in_specs,
        out_specs=out_specs,
        out_shape=out_shape,
        scratch_shapes=scratch,
        compiler_params=pltpu.CompilerParams(
            dimension_semantics=("arbitrary", "arbitrary"),
            vmem_limit_bytes=VMEM_LIMIT_BYTES),
        name="prompt_layer",
    )(sinks, x, *weights)


BIG_WEIGHTS = (("w_in", D_MODEL, OFF_GB + D_MODEL), ("mix", len(POOL_WINDOWS) * POOL_GROUP_WIDTH, POOL_GROUP_WIDTH),
               ("wpp", POOL_WIDTH, D_MODEL), ("wap", Q_WIDTH, D_MODEL), ("wout", D_MODEL, D_MODEL),
               ("wg", D_MODEL, D_FF), ("wu", D_MODEL, D_FF), ("wd", D_FF, D_MODEL))
CONVERT_ROWS = 256
STAGE_COLS = max(cols for _, _, cols in BIG_WEIGHTS)
WEIGHT_CHUNKS = tuple((w, r0, min(CONVERT_ROWS, nrows - r0))
                      for w, (_, nrows, _) in enumerate(BIG_WEIGHTS) for r0 in range(0, nrows, CONVERT_ROWS))


def _chunks_per_step(nsteps):
    first = [n for n, (w, _, _) in enumerate(WEIGHT_CHUNKS) if w == 0]
    rest = [n for n, (w, _, _) in enumerate(WEIGHT_CHUNKS) if w != 0]
    size = lambda n: WEIGHT_CHUNKS[n][2] * BIG_WEIGHTS[WEIGHT_CHUNKS[n][0]][2]
    total = sum(size(n) for n in rest)
    steps = [first] + [[] for _ in range(nsteps - 1)]
    done = 0
    for n in rest:
        steps[1 + min(nsteps - 2, done * (nsteps - 1) // total)].append(n)
        done += size(n)
    return steps


def _sample_kernel(*refs):
    nw = len(BIG_WEIGHTS)
    (x_ref, ck_ref, cv_ref, sp_ref, sinkc_ref, n1_ref, qn_ref, kn_ref, seg_ref, pscale_ref, n2_ref) = refs[:11]
    w_hbm = refs[11:11 + nw]
    y_ref, knew_ref, vnew_ref, pnew_ref = refs[11 + nw:15 + nw]
    w_bf_hbm = refs[15 + nw:15 + 2 * nw]
    w_v = refs[15 + 2 * nw:15 + 3 * nw]
    (stage, stage_sem, out_sem,
     u_s, kt_s, vt_s, q8_s, k8_s, v8_s, o8_s, ypre_s, sga_s, sgb_s, merged_s, act_s) = refs[15 + 3 * nw:]
    w_in_ref, mix_ref, wpp_ref, wap_ref, wout_ref, wg_ref, wu_ref, wd_ref = w_v

    SB = SAMPLE_BLOCK
    N = x_ref.shape[0]
    nsteps = N // SB
    i = pl.program_id(0)
    lane = lax.broadcasted_iota(jnp.int32, (N, LANES), 1)
    all_rows = slice(0, N)

    def chunk_copy(n):
        w, r0, rows = WEIGHT_CHUNKS[n]
        cols = BIG_WEIGHTS[w][2]
        return pltpu.make_async_copy(w_hbm[w].at[pl.ds(r0, rows), :],
                                     stage.at[n % 2, pl.ds(0, rows), pl.ds(0, cols)], stage_sem.at[n % 2])

    def writeback_copy(w):
        return pltpu.make_async_copy(w_v[w], w_bf_hbm[w], out_sem.at[w])

    def convert(n):
        w, r0, rows = WEIGHT_CHUNKS[n]
        cols = BIG_WEIGHTS[w][2]
        w_v[w][r0:r0 + rows, :] = stage[n % 2, 0:rows, 0:cols].astype(BF16)
        if r0 + rows == BIG_WEIGHTS[w][1]:
            writeback_copy(w).start()

    for step, chunk_ids in enumerate(_chunks_per_step(nsteps)):
        @pl.when(i == step)
        def _(chunk_ids=chunk_ids):
            for n in chunk_ids:
                if n == 0:
                    chunk_copy(0).start()
                if n + 1 < len(WEIGHT_CHUNKS):
                    chunk_copy(n + 1).start()
                chunk_copy(n).wait()
                convert(n)

    @pl.when(i == 0)
    def _():
        x = x_ref[...]
        h = ((x * _rms_scale(x)) * n1_ref[...]).astype(BF16)
        u_s[...] = _dot(h, w_in_ref[:, OFF_U:OFF_U + POOL_WIDTH])
        for gc in range(N_GATE_CHUNKS):
            _gate_chunk(h, gc, w_in_ref, sga_s, sgb_s, all_rows)
        q = _head_norm(_dot(h, w_in_ref[:, OFF_Q:OFF_Q + Q_WIDTH]), seg_ref[...], qn_ref[...])
        kn = _head_norm(_dot(h, w_in_ref[:, OFF_K:OFF_K + KV_WIDTH]), seg_ref[0:KV_WIDTH, 0:KV_WIDTH], kn_ref[...])
        v = _dot(h, w_in_ref[:, OFF_V:OFF_V + KV_WIDTH])
        kt_s[...] = kn.T
        vt_s[...] = v.T
        for head in range(N_HEADS):
            k8_s[pl.ds(head, N, stride=N_HEADS), :] = kn
            v8_s[pl.ds(head, N, stride=N_HEADS), :] = v
        for p in range(Q_COLS):
            qcol = q[:, p * LANES:(p + 1) * LANES]
            qswap = pltpu.roll(qcol, HEAD_DIM, 1)
            for e in range(HEADS_PER_COL):
                head = p * HEADS_PER_COL + e
                c = _kv_head(head)
                src = qcol if e == c else qswap
                keep = (lane < HEAD_DIM) if c == 0 else (lane >= HEAD_DIM)
                q8_s[pl.ds(head, N, stride=N_HEADS), :] = jnp.where(keep, src, 0.0)

    base = i * SB
    rows = pl.ds(pl.multiple_of(base, SB), SB)

    unew = u_s[rows, :]
    for g, w in enumerate(POOL_WINDOWS):
        cols = slice(g * POOL_GROUP_WIDTH, (g + 1) * POOL_GROUP_WIDTH)
        s = unew[:, cols]
        for r in range(POOL_BUF - (w - 1), POOL_BUF):
            s = s + sp_ref[r, :, cols]
        ypre_s[rows, cols] = (s * (1.0 / w) - unew[:, cols]).astype(BF16)
    pnew_ref[0:POOL_BUF - 1] = sp_ref[1:POOL_BUF]
    pnew_ref[POOL_BUF - 1] = unew

    hrows = pl.ds(pl.multiple_of(base * N_HEADS, SB * N_HEADS), SB * N_HEADS)
    lhs = q8_s[hrows, :]
    sc = jnp.concatenate(
        [_dot(lhs[bb * N_HEADS:(bb + 1) * N_HEADS, :].astype(BF16), ck_ref[bb].astype(BF16)) for bb in range(SB)],
        axis=0)
    sc_self = jnp.sum(lhs * k8_s[hrows, :], axis=-1, keepdims=True)
    sink = sinkc_ref[:, 0:1]
    m = jnp.maximum(jnp.maximum(jnp.max(sc, axis=-1, keepdims=True), sc_self), sink)
    p = jnp.exp(sc - m)
    p_self = jnp.exp(sc_self - m)
    denom = jnp.sum(p, axis=-1, keepdims=True) + p_self + jnp.exp(sink - m)
    inv = 1.0 / denom
    pn = p * inv
    o = jnp.concatenate(
        [_dot_nt(pn[bb * N_HEADS:(bb + 1) * N_HEADS, :].astype(BF16), cv_ref[bb].astype(BF16)) for bb in range(SB)],
        axis=0)
    o8_s[hrows, :] = o + (p_self * inv) * v8_s[hrows, :]

    lane_c = lax.broadcasted_iota(jnp.int32, (KV_WIDTH, WINDOW), 1)
    kt_all = kt_s[...]
    vt_all = vt_s[...]
    for bb in range(SB):
        put = (WINDOW - 1) - (base + bb)
        knew_ref[bb] = jnp.where(lane_c == WINDOW - 1, pltpu.roll(kt_all, put, 1),
                                 pltpu.roll(ck_ref[bb], WINDOW - 1, 1))
        vnew_ref[bb] = jnp.where(lane_c == WINDOW - 1, pltpu.roll(vt_all, put, 1),
                                 pltpu.roll(cv_ref[bb], WINDOW - 1, 1))

    @pl.when(i == nsteps - 1)
    def _():
        ypool = jnp.concatenate(
            [(_dot(ypre_s[:, g * POOL_GROUP_WIDTH:(g + 1) * POOL_GROUP_WIDTH],
                   mix_ref[g * POOL_GROUP_WIDTH:(g + 1) * POOL_GROUP_WIDTH, :])
              * pscale_ref[:, g * POOL_GROUP_WIDTH:(g + 1) * POOL_GROUP_WIDTH]).astype(BF16)
             for g in range(len(POOL_WINDOWS))], axis=1)
        ycols = []
        for p in range(Q_COLS):
            halves = []
            for e in range(HEADS_PER_COL):
                head = p * HEADS_PER_COL + e
                oh = o8_s[pl.ds(head, N, stride=N_HEADS), :]
                halves.append(oh if e == _kv_head(head) else pltpu.roll(oh, HEAD_DIM, 1))
            ycols.append(jnp.where(lane < HEAD_DIM, halves[0], halves[1]).astype(BF16))
        yattn = jnp.concatenate(ycols, axis=1)
        _merge(ypool, yattn, sga_s, sgb_s, wpp_ref, wap_ref, merged_s, all_rows)
        x1 = x_ref[...] + _dot(merged_s[...], wout_ref[...])
        h2 = ((x1 * _rms_scale(x1)) * n2_ref[...]).astype(BF16)
        _ffn_act(h2, wg_ref, wu_ref, act_s, all_rows)
        y_ref[...] = x1 + _dot(act_s[...], wd_ref[...])
        for w in range(nw):
            writeback_copy(w).wait()


def _sample_call(x, ck, cv, sp, sinkc, small, big):
    N = x.shape[0]
    SB = SAMPLE_BLOCK
    W = WINDOW
    assert tuple(b.shape for b in big) == tuple((r, c) for _, r, c in BIG_WEIGHTS)
    consts = (sinkc,) + tuple(small)
    in_specs = [_const_spec(x.shape),
                pl.BlockSpec((SB, KV_WIDTH, W), lambda i: (i, 0, 0)),
                pl.BlockSpec((SB, KV_WIDTH, W), lambda i: (i, 0, 0)),
                pl.BlockSpec((POOL_BUF, SB, POOL_WIDTH), lambda i: (0, i, 0))]
    in_specs += [_const_spec(w.shape) for w in consts]
    in_specs += [pl.BlockSpec(memory_space=pl.ANY) for _ in big]
    out_shape = (jax.ShapeDtypeStruct((N, D_MODEL), F32),
                 jax.ShapeDtypeStruct((N, KV_WIDTH, W), F32),
                 jax.ShapeDtypeStruct((N, KV_WIDTH, W), F32),
                 jax.ShapeDtypeStruct((POOL_BUF, N, POOL_WIDTH), F32))
    out_shape += tuple(jax.ShapeDtypeStruct(b.shape, BF16) for b in big)
    out_specs = (pl.BlockSpec((N, D_MODEL), lambda i: (0, 0)),
                 pl.BlockSpec((SB, KV_WIDTH, W), lambda i: (i, 0, 0)),
                 pl.BlockSpec((SB, KV_WIDTH, W), lambda i: (i, 0, 0)),
                 pl.BlockSpec((POOL_BUF, SB, POOL_WIDTH), lambda i: (0, i, 0)))
    out_specs += tuple(pl.BlockSpec(memory_space=pl.ANY) for _ in big)
    scratch = [pltpu.VMEM(b.shape, BF16) for b in big]
    scratch += [pltpu.VMEM((2, CONVERT_ROWS, STAGE_COLS), F32),
                pltpu.SemaphoreType.DMA((2,)),
                pltpu.SemaphoreType.DMA((len(big),)),
                pltpu.VMEM((N, POOL_WIDTH), F32),
               pltpu.VMEM((KV_WIDTH, N), F32),
               pltpu.VMEM((KV_WIDTH, N), F32),
               pltpu.VMEM((N * N_HEADS, LANES), F32),
               pltpu.VMEM((N * N_HEADS, LANES), F32),
               pltpu.VMEM((N * N_HEADS, LANES), F32),
               pltpu.VMEM((N * N_HEADS, LANES), F32),
               pltpu.VMEM((N, POOL_WIDTH), BF16),
               pltpu.VMEM((N, D_MODEL), F32),
               pltpu.VMEM((N, D_MODEL), F32),
               pltpu.VMEM((N, D_MODEL), BF16),
               pltpu.VMEM((N, D_FF), BF16)]
    return pl.pallas_call(
        _sample_kernel,
        grid=(N // SB,),
        in_specs=in_specs,
        out_specs=out_specs,
        out_shape=out_shape,
        scratch_shapes=scratch,
        compiler_params=pltpu.CompilerParams(
            dimension_semantics=("arbitrary",),
            vmem_limit_bytes=VMEM_LIMIT_BYTES),
        name="sample_layer",
    )(x, ck, cv, sp, *consts, *big)


def _head_sum_matrix():
    idx = np.arange(MXU_DIM) // HEAD_DIM
    return jnp.asarray(idx[:, None] == idx[None, :], dtype=BF16)


def _cache_to_device_order(c):
    n, w = c.shape[0], c.shape[1]
    return jnp.transpose(c, (0, 2, 3, 1)).reshape(n, KV_WIDTH, w)


def _cache_from_device_order(c):
    n, _, w = c.shape
    return jnp.transpose(c.reshape(n, N_KV_HEADS, HEAD_DIM, w), (0, 3, 1, 2))[None]


@jax.jit
def _forward(x_prompt, x_sample, cache_k, cache_v, state_pool, norm1, w_in, q_norm, k_norm, sinks,
             pool_mix_w, pool_scale, w_pool_proj, w_attn_proj, w_out, norm2, w_gate, w_up, w_down):
    depth = w_in.shape[0]
    assert depth == 1, "single-layer trunk"
    l = 0
    N = x_sample.shape[0]
    assert cache_k.shape[2] == WINDOW and x_sample.shape[1] == 1

    qn = (jnp.tile(q_norm[l], N_HEADS) * (HEAD_DIM ** -0.5)).reshape(1, Q_WIDTH)
    kn = jnp.tile(k_norm[l], N_KV_HEADS).reshape(1, KV_WIDTH)
    n1 = norm1[l].reshape(1, D_MODEL)
    n2 = norm2[l].reshape(1, D_MODEL)
    seg = _head_sum_matrix()
    pscale = pool_scale[l].reshape(1, POOL_WIDTH)
    big = (w_in[l], pool_mix_w[l].reshape(len(POOL_WINDOWS) * POOL_GROUP_WIDTH, POOL_GROUP_WIDTH),
           w_pool_proj[l], w_attn_proj[l], w_out[l], w_gate[l], w_up[l], w_down[l])

    sink_cols = jnp.broadcast_to(jnp.tile(sinks[l], SAMPLE_BLOCK)[:, None], (SAMPLE_BLOCK * N_HEADS, LANES))
    y_s, k_s, v_s, pool_s, w_in_b, mix_b, wpp_b, wap_b, wout_b, wg_b, wu_b, wd_b = _sample_call(
        x_sample.reshape(N, D_MODEL),
        _cache_to_device_order(cache_k[l]), _cache_to_device_order(cache_v[l]),
        jnp.transpose(state_pool[l], (1, 0, 2)), sink_cols, (n1, qn, kn, seg, pscale, n2), big)

    weights = (n1, w_in_b, qn, kn, seg, mix_b.reshape(len(POOL_WINDOWS), POOL_GROUP_WIDTH, POOL_GROUP_WIDTH),
               pscale, wpp_b, wap_b, wout_b, n2, wg_b, wu_b, wd_b)
    y_p, k_p, v_p, pool_p = _prompt_call(x_prompt, sinks[l], weights)

    return (y_p, y_s.reshape(N, 1, D_MODEL),
            _cache_from_device_order(k_p), _cache_from_device_order(v_p), pool_p[None],
            _cache_from_device_order(k_s), _cache_from_device_order(v_s),
            jnp.transpose(pool_s, (1, 0, 2))[None])


def kernel(x_prompt, x_sample, cache_k, cache_v, state_pool, norm1, w_in, q_norm, k_norm, sinks, pool_mix_w,
           pool_scale, w_pool_proj, w_attn_proj, w_out, norm2, w_gate, w_up, w_down):
    return _forward(x_prompt, x_sample, cache_k, cache_v, state_pool, norm1, w_in, q_norm, k_norm, sinks,
                    pool_mix_w, pool_scale, w_pool_proj, w_attn_proj, w_out, norm2, w_gate, w_up, w_down)
```

```python
import numpy as np
import jax
import jax.numpy as jnp
from jax import lax
from jax.experimental import pallas as pl
from jax.experimental.pallas import tpu as pltpu

D_MODEL = 1024
POOL_WINDOWS = (2, 4, 8, 16)
POOL_GROUP_WIDTH = 128
POOL_WIDTH = 512
POOL_BUF = 15
N_HEADS = 8
N_KV_HEADS = 2
HEAD_DIM = 64
GROUP = N_HEADS // N_KV_HEADS
Q_WIDTH = 512
KV_WIDTH = 128
WINDOW = 128
D_FF = 2816
EPS = 1e-6
NEG = -1e30

LANES = 128
SUBLANES = 8
MXU_DIM = 256
ATT_BLOCK = WINDOW
SEQ_TILE = 512
SUB_TILE = 256
FF_CHUNK = 512
GATE_CHUNK = 256
SAMPLE_BLOCK = 16
PREFIX = 16
VMEM_LIMIT_BYTES = 58 * 1024 * 1024
Q_COLS = Q_WIDTH // LANES
HEADS_PER_COL = LANES // HEAD_DIM
N_GATE_CHUNKS = D_MODEL // GATE_CHUNK

OFF_U = 0
OFF_Q = OFF_U + POOL_WIDTH
OFF_K = OFF_Q + Q_WIDTH
OFF_V = OFF_K + KV_WIDTH
OFF_GA = OFF_V + KV_WIDTH
OFF_GB = OFF_GA + D_MODEL

BF16 = jnp.bfloat16
F32 = jnp.float32


def _dot(a, b):
    return jnp.dot(a, b, preferred_element_type=F32)


def _dot_nt(a, b):
    return lax.dot_general(a, b, (((1,), (1,)), ((), ())), preferred_element_type=F32)


def _rms_scale(x):
    return lax.rsqrt(jnp.mean(x * x, axis=-1, keepdims=True) + EPS)


def _head_norm(x, seg, gain):
    width = seg.shape[0]
    sq = (x * x).astype(BF16)
    ss = jnp.concatenate([_dot(sq[:, c:c + width], seg) for c in range(0, x.shape[1], width)], axis=1)
    return (x * lax.rsqrt(ss * (1.0 / HEAD_DIM) + EPS)) * gain


def _kv_head(head):
    return head // GROUP


def _ffn_act(h2, wg_ref, wu_ref, act_ref, rows):
    for start in range(0, D_FF, FF_CHUNK):
        width = min(FF_CHUNK, D_FF - start)
        g = _dot(h2, wg_ref[:, start:start + width])
        u = _dot(h2, wu_ref[:, start:start + width])
        act_ref[rows, start:start + width] = (g * jax.nn.sigmoid(g) * u).astype(BF16)


def _gate_chunk(h, c, w_in_ref, sga_ref, sgb_ref, rows):
    lo = c * GATE_CHUNK
    sga_ref[rows, lo:lo + GATE_CHUNK] = jax.nn.sigmoid(_dot(h, w_in_ref[:, OFF_GA + lo:OFF_GA + lo + GATE_CHUNK]))
    sgb_ref[rows, lo:lo + GATE_CHUNK] = jax.nn.sigmoid(_dot(h, w_in_ref[:, OFF_GB + lo:OFF_GB + lo + GATE_CHUNK]))


def _merge(ypool, yattn, sga_ref, sgb_ref, wpp_ref, wap_ref, merged_ref, rows):
    for c in range(N_GATE_CHUNKS):
        lo = c * GATE_CHUNK
        pp = _dot(ypool, wpp_ref[:, lo:lo + GATE_CHUNK])
        ap = _dot(yattn, wap_ref[:, lo:lo + GATE_CHUNK])
        merged = sga_ref[rows, lo:lo + GATE_CHUNK] * pp + sgb_ref[rows, lo:lo + GATE_CHUNK] * ap
        merged_ref[rows, lo:lo + GATE_CHUNK] = merged.astype(BF16)


def _prompt_kernel(sinks_ref, x_ref, n1_ref, w_in_ref, qn_ref, kn_ref, seg_ref, mixw_ref, pscale_ref,
                   wpp_ref, wap_ref, wout_ref, n2_ref, wg_ref, wu_ref, wd_ref,
                   y_ref, knew_ref, vnew_ref, pnew_ref,
                   h_s, u_s, q_s, kd_s, vd_s, s_s, p_s, ypool_s, yattn_s, sga_s, sgb_s, merged_s, x1_s, h2_s, act_s):
    T = SEQ_TILE
    R = SUB_TILE
    subs = [slice(r0, r0 + R) for r0 in range(0, T, R)]
    n_blocks = T // ATT_BLOCK
    j = pl.program_id(1)
    rows_per_kv = GROUP * ATT_BLOCK

    @pl.when(j == 0)
    def _():
        u_s[0:PREFIX, :] = jnp.zeros((PREFIX, POOL_WIDTH), F32)
        kd_s[:, 0:ATT_BLOCK, :] = jnp.zeros((N_KV_HEADS, ATT_BLOCK, KV_WIDTH), BF16)
        vd_s[:, 0:ATT_BLOCK, :] = jnp.zeros((N_KV_HEADS, ATT_BLOCK, KV_WIDTH), BF16)

    lane_r = lax.broadcasted_iota(jnp.int32, (R, LANES), 1)
    lane = lax.broadcasted_iota(jnp.int32, (ATT_BLOCK, LANES), 1)
    row = lax.broadcasted_iota(jnp.int32, (ATT_BLOCK, 2 * ATT_BLOCK), 0)
    col = lax.broadcasted_iota(jnp.int32, (ATT_BLOCK, 2 * ATT_BLOCK), 1)
    band = (col >= row) & (col <= row + WINDOW)
    first_lo = jnp.where(j == 0, ATT_BLOCK, 0)

    def norm1(rows):
        x = x_ref[0, rows, :]
        h_s[rows, :] = ((x * _rms_scale(x)) * n1_ref[...]).astype(BF16)

    def in_proj(rows):
        h = h_s[rows, :]
        u_s[PREFIX + rows.start:PREFIX + rows.stop, :] = _dot(h, w_in_ref[:, OFF_U:OFF_U + POOL_WIDTH])
        q = _dot(h, w_in_ref[:, OFF_Q:OFF_Q + Q_WIDTH])
        q_s[rows, :] = _head_norm(q, seg_ref[...], qn_ref[...]).astype(BF16)
        k = _dot(h, w_in_ref[:, OFF_K:OFF_K + KV_WIDTH])
        kn = _head_norm(k, seg_ref[0:KV_WIDTH, 0:KV_WIDTH], kn_ref[...])
        v = _dot(h, w_in_ref[:, OFF_V:OFF_V + KV_WIDTH])
        dst_rows = slice(ATT_BLOCK + rows.start, ATT_BLOCK + rows.stop)
        for src, dst in ((kn, kd_s), (v, vd_s)):
            swapped = pltpu.roll(src, HEAD_DIM, 1)
            dst[0, dst_rows, :] = jnp.where(lane_r < HEAD_DIM, src, swapped).astype(BF16)
            dst[1, dst_rows, :] = jnp.where(lane_r < HEAD_DIM, swapped, src).astype(BF16)
        if rows.stop == T:
            knew_ref[0] = kn[R - ATT_BLOCK:, :].T
            vnew_ref[0] = v[R - ATT_BLOCK:, :].T
            pnew_ref[0] = u_s[PREFIX + T - POOL_BUF:PREFIX + T, :]

    def pool(rows):
        pos1 = j * T + rows.start + lax.broadcasted_iota(jnp.int32, (R, 1), 0) + 1
        for g, w in enumerate(POOL_WINDOWS):
            cols = slice(g * POOL_GROUP_WIDTH, (g + 1) * POOL_GROUP_WIDTH)
            a = u_s[rows.start:rows.stop + PREFIX, cols]
            s = a
            shift = 1
            while shift < w:
                s = s + pltpu.roll(s, shift, 0)
                shift *= 2
            inv_cnt = 1.0 / jnp.minimum(pos1, w).astype(F32)
            yg = (s[PREFIX:, :] * inv_cnt - a[PREFIX:, :]).astype(BF16)
            ypool_s[rows, cols] = (_dot(yg, mixw_ref[g]) * pscale_ref[:, cols]).astype(BF16)

    def gates(rows, chunks):
        h = h_s[rows, :]
        for c in chunks:
            _gate_chunk(h, c, w_in_ref, sga_s, sgb_s, rows)

    def scores(b):
        r0 = b * ATT_BLOCK
        for c in range(N_KV_HEADS):
            parts = []
            for p in range(c * Q_COLS // N_KV_HEADS, (c + 1) * Q_COLS // N_KV_HEADS):
                qcol = q_s[r0:r0 + ATT_BLOCK, p * LANES:(p + 1) * LANES]
                parts.append(jnp.where(lane < HEAD_DIM, qcol, jnp.zeros_like(qcol)))
                parts.append(jnp.where(lane >= HEAD_DIM, qcol, jnp.zeros_like(qcol)))
            s_s[b % 2, c * rows_per_kv:(c + 1) * rows_per_kv, :] = _dot_nt(
                jnp.concatenate(parts, axis=0), kd_s[c, r0:r0 + 2 * ATT_BLOCK, :])

    def softmax(b):
        valid = band & (col >= first_lo) if b == 0 else band
        for head in range(N_HEADS):
            sh = jnp.where(valid, s_s[b % 2, head * ATT_BLOCK:(head + 1) * ATT_BLOCK, :], NEG)
            sink = sinks_ref[head]
            m = jnp.maximum(jnp.max(sh, axis=-1, keepdims=True), sink)
            p = jnp.exp(sh - m)
            denom = jnp.sum(p, axis=-1, keepdims=True) + jnp.exp(sink - m)
            p_s[b % 2, head * ATT_BLOCK:(head + 1) * ATT_BLOCK, :] = (p * (1.0 / denom)).astype(BF16)

    def weighted_values(b):
        r0 = b * ATT_BLOCK
        for c in range(N_KV_HEADS):
            o = _dot(p_s[b % 2, c * rows_per_kv:(c + 1) * rows_per_kv, :], vd_s[c, r0:r0 + 2 * ATT_BLOCK, :])
            for pp in range(Q_COLS // N_KV_HEADS):
                p_col = c * Q_COLS // N_KV_HEADS + pp
                o_lo = o[(2 * pp) * ATT_BLOCK:(2 * pp + 1) * ATT_BLOCK, :]
                o_hi = o[(2 * pp + 1) * ATT_BLOCK:(2 * pp + 2) * ATT_BLOCK, :]
                yattn_s[r0:r0 + ATT_BLOCK, p_col * LANES:(p_col + 1) * LANES] = (
                    jnp.where(lane < HEAD_DIM, o_lo, o_hi).astype(BF16))

    def out_proj(rows):
        _merge(ypool_s[rows, :], yattn_s[rows, :], sga_s, sgb_s, wpp_ref, wap_ref, merged_s, rows)
        x1 = x_ref[0, rows, :] + _dot(merged_s[rows, :], wout_ref[...])
        x1_s[rows, :] = x1
        h2_s[rows, :] = ((x1 * _rms_scale(x1)) * n2_ref[...]).astype(BF16)

    for rows in subs:
        norm1(rows)
    for rows in subs:
        in_proj(rows)
    pool(subs[0])
    gates(subs[0], range(N_GATE_CHUNKS))
    for rows in subs[1:]:
        pool(rows)
    later_gates = [(rows, c) for rows in subs[1:] for c in range(N_GATE_CHUNKS)]
    per_block = -(-len(later_gates) // n_blocks)
    scores(0)
    for b in range(n_blocks):
        if b + 1 < n_blocks:
            scores(b + 1)
        for rows, c in later_gates[b * per_block:(b + 1) * per_block]:
            gates(rows, [c])
        softmax(b)
        weighted_values(b)

    u_s[0:PREFIX, :] = u_s[T:T + PREFIX, :]
    kd_s[:, 0:ATT_BLOCK, :] = kd_s[:, T:T + ATT_BLOCK, :]
    vd_s[:, 0:ATT_BLOCK, :] = vd_s[:, T:T + ATT_BLOCK, :]

    for rows in subs:
        out_proj(rows)
    for rows in subs:
        _ffn_act(h2_s[rows, :], wg_ref, wu_ref, act_s, rows)
    for rows in subs:
        y_ref[0, rows, :] = x1_s[rows, :] + _dot(act_s[rows, :], wd_ref[...])


def _const_spec(shape):
    nd = len(shape)
    return pl.BlockSpec(shape, lambda *_: (0,) * nd, pipeline_mode=pl.Buffered(1))


def _prompt_call(x, sinks, weights):
    B, S, _ = x.shape
    T = SEQ_TILE
    in_specs = [pl.BlockSpec(memory_space=pltpu.SMEM),
                pl.BlockSpec((1, T, D_MODEL), lambda b, j: (b, j, 0))]
    in_specs += [_const_spec(w.shape) for w in weights]
    out_shape = (jax.ShapeDtypeStruct((B, S, D_MODEL), F32),
                 jax.ShapeDtypeStruct((B, KV_WIDTH, ATT_BLOCK), F32),
                 jax.ShapeDtypeStruct((B, KV_WIDTH, ATT_BLOCK), F32),
                 jax.ShapeDtypeStruct((B, POOL_BUF, POOL_WIDTH), F32))
    out_specs = (pl.BlockSpec((1, T, D_MODEL), lambda b, j: (b, j, 0)),
                 pl.BlockSpec((1, KV_WIDTH, ATT_BLOCK), lambda b, j: (b, 0, 0)),
                 pl.BlockSpec((1, KV_WIDTH, ATT_BLOCK), lambda b, j: (b, 0, 0)),
                 pl.BlockSpec((1, POOL_BUF, POOL_WIDTH), lambda b, j: (b, 0, 0)))
    scratch = [pltpu.VMEM((T, D_MODEL), BF16),
               pltpu.VMEM((PREFIX + T, POOL_WIDTH), F32),
               pltpu.VMEM((T, Q_WIDTH), BF16),
               pltpu.VMEM((N_KV_HEADS, ATT_BLOCK + T, KV_WIDTH), BF16),
               pltpu.VMEM((N_KV_HEADS, ATT_BLOCK + T, KV_WIDTH), BF16),
               pltpu.VMEM((2, N_HEADS * ATT_BLOCK, 2 * ATT_BLOCK), F32),
               pltpu.VMEM((2, N_HEADS * ATT_BLOCK, 2 * ATT_BLOCK), BF16),
               pltpu.VMEM((T, POOL_WIDTH), BF16),
               pltpu.VMEM((T, Q_WIDTH), BF16),
               pltpu.VMEM((T, D_MODEL), F32),
               pltpu.VMEM((T, D_MODEL), F32),
               pltpu.VMEM((T, D_MODEL), BF16),
               pltpu.VMEM((T, D_MODEL), F32),
               pltpu.VMEM((T, D_MODEL), BF16),
               pltpu.VMEM((T, D_FF), BF16)]
    return pl.pallas_call(
        _prompt_kernel,
        grid=(B, S // T),
        in_specs=in_specs,
        out_specs=out_specs,
        out_shape=out_shape,
        scratch_shapes=scratch,
        compiler_params=pltpu.CompilerParams(
            dimension_semantics=("arbitrary", "arbitrary"),
            vmem_limit_bytes=VMEM_LIMIT_BYTES),
        name="prompt_layer",
    )(sinks, x, *weights)


BIG_WEIGHTS = (("w_in", D_MODEL, OFF_GB + D_MODEL), ("mix", len(POOL_WINDOWS) * POOL_GROUP_WIDTH, POOL_GROUP_WIDTH),
               ("wpp", POOL_WIDTH, D_MODEL), ("wap", Q_WIDTH, D_MODEL), ("wout", D_MODEL, D_MODEL),
               ("wg", D_MODEL, D_FF), ("wu", D_MODEL, D_FF), ("wd", D_FF, D_MODEL))
CONVERT_ROWS = 128
STAGE_SLOTS = 4
STAGE_COLS = max(cols for _, _, cols in BIG_WEIGHTS)
WEIGHT_CHUNKS = tuple((w, r0, min(CONVERT_ROWS, nrows - r0))
                      for w, (_, nrows, _) in enumerate(BIG_WEIGHTS) for r0 in range(0, nrows, CONVERT_ROWS))


def _chunks_per_step(nsteps):
    first = [n for n, (w, _, _) in enumerate(WEIGHT_CHUNKS) if w == 0]
    rest = [n for n, (w, _, _) in enumerate(WEIGHT_CHUNKS) if w != 0]
    size = lambda n: WEIGHT_CHUNKS[n][2] * BIG_WEIGHTS[WEIGHT_CHUNKS[n][0]][2]
    total = sum(size(n) for n in rest)
    steps = [first] + [[] for _ in range(nsteps - 1)]
    done = 0
    for n in rest:
        steps[1 + min(nsteps - 2, done * (nsteps - 1) // total)].append(n)
        done += size(n)
    return steps


def _sample_kernel(*refs):
    nw = len(BIG_WEIGHTS)
    (x_ref, ck_ref, cv_ref, sp_ref, sinkc_ref, n1_ref, qn_ref, kn_ref, seg_ref, pscale_ref, n2_ref) = refs[:11]
    w_hbm = refs[11:11 + nw]
    y_ref, knew_ref, vnew_ref, pnew_ref = refs[11 + nw:15 + nw]
    w_bf_hbm = refs[15 + nw:15 + 2 * nw]
    w_v = refs[15 + 2 * nw:15 + 3 * nw]
    (stage, stage_sem, out_sem,
     u_s, kt_s, vt_s, q8_s, k8_s, v8_s, o8_s, ypre_s, sga_s, sgb_s, merged_s, act_s) = refs[15 + 3 * nw:]
    w_in_ref, mix_ref, wpp_ref, wap_ref, wout_ref, wg_ref, wu_ref, wd_ref = w_v

    SB = SAMPLE_BLOCK
    N = x_ref.shape[0]
    nsteps = N // SB
    i = pl.program_id(0)
    lane = lax.broadcasted_iota(jnp.int32, (N, LANES), 1)
    all_rows = slice(0, N)

    def chunk_copy(n):
        w, r0, rows = WEIGHT_CHUNKS[n]
        cols = BIG_WEIGHTS[w][2]
        slot = n % STAGE_SLOTS
        return pltpu.make_async_copy(w_hbm[w].at[pl.ds(r0, rows), :],
                                     stage.at[slot, pl.ds(0, rows), pl.ds(0, cols)], stage_sem.at[slot])

    def writeback_copy(w):
        return pltpu.make_async_copy(w_v[w], w_bf_hbm[w], out_sem.at[w])

    def convert(n):
        w, r0, rows = WEIGHT_CHUNKS[n]
        cols = BIG_WEIGHTS[w][2]
        w_v[w][r0:r0 + rows, :] = stage[n % STAGE_SLOTS, 0:rows, 0:cols].astype(BF16)
        if r0 + rows == BIG_WEIGHTS[w][1]:
            writeback_copy(w).start()

    ahead = STAGE_SLOTS - 1
    for step, chunk_ids in enumerate(_chunks_per_step(nsteps)):
        @pl.when(i == step)
        def _(chunk_ids=chunk_ids):
            for n in chunk_ids:
                if n == 0:
                    for m in range(ahead):
                        chunk_copy(m).start()
                if n + ahead < len(WEIGHT_CHUNKS):
                    chunk_copy(n + ahead).start()
                chunk_copy(n).wait()
                convert(n)

    @pl.when(i == 0)
    def _():
        x = x_ref[...]
        h = ((x * _rms_scale(x)) * n1_ref[...]).astype(BF16)
        u_s[...] = _dot(h, w_in_ref[:, OFF_U:OFF_U + POOL_WIDTH])
        for gc in range(N_GATE_CHUNKS):
            _gate_chunk(h, gc, w_in_ref, sga_s, sgb_s, all_rows)
        q = _head_norm(_dot(h, w_in_ref[:, OFF_Q:OFF_Q + Q_WIDTH]), seg_ref[...], qn_ref[...])
        kn = _head_norm(_dot(h, w_in_ref[:, OFF_K:OFF_K + KV_WIDTH]), seg_ref[0:KV_WIDTH, 0:KV_WIDTH], kn_ref[...])
        v = _dot(h, w_in_ref[:, OFF_V:OFF_V + KV_WIDTH])
        kt_s[...] = kn.T
        vt_s[...] = v.T
        for head in range(N_HEADS):
            k8_s[pl.ds(head, N, stride=N_HEADS), :] = kn
            v8_s[pl.ds(head, N, stride=N_HEADS), :] = v
        for p in range(Q_COLS):
            qcol = q[:, p * LANES:(p + 1) * LANES]
            qswap = pltpu.roll(qcol, HEAD_DIM, 1)
            for e in range(HEADS_PER_COL):
                head = p * HEADS_PER_COL + e
                c = _kv_head(head)
                src = qcol if e == c else qswap
                keep = (lane < HEAD_DIM) if c == 0 else (lane >= HEAD_DIM)
                q8_s[pl.ds(head, N, stride=N_HEADS), :] = jnp.where(keep, src, 0.0)

    base = i * SB
    rows = pl.ds(pl.multiple_of(base, SB), SB)

    unew = u_s[rows, :]
    for g, w in enumerate(POOL_WINDOWS):
        cols = slice(g * POOL_GROUP_WIDTH, (g + 1) * POOL_GROUP_WIDTH)
        s = unew[:, cols]
        for r in range(POOL_BUF - (w - 1), POOL_BUF):
            s = s + sp_ref[r, :, cols]
        ypre_s[rows, cols] = (s * (1.0 / w) - unew[:, cols]).astype(BF16)
    pnew_ref[0:POOL_BUF - 1] = sp_ref[1:POOL_BUF]
    pnew_ref[POOL_BUF - 1] = unew

    hrows = pl.ds(pl.multiple_of(base * N_HEADS, SB * N_HEADS), SB * N_HEADS)
    lhs = q8_s[hrows, :]
    sc = jnp.concatenate(
        [_dot(lhs[bb * N_HEADS:(bb + 1) * N_HEADS, :].astype(BF16), ck_ref[bb].astype(BF16)) for bb in range(SB)],
        axis=0)
    sc_self = jnp.sum(lhs * k8_s[hrows, :], axis=-1, keepdims=True)
    sink = sinkc_ref[:, 0:1]
    m = jnp.maximum(jnp.maximum(jnp.max(sc, axis=-1, keepdims=True), sc_self), sink)
    p = jnp.exp(sc - m)
    p_self = jnp.exp(sc_self - m)
    denom = jnp.sum(p, axis=-1, keepdims=True) + p_self + jnp.exp(sink - m)
    inv = 1.0 / denom
    pn = p * inv
    o = jnp.concatenate(
        [_dot_nt(pn[bb * N_HEADS:(bb + 1) * N_HEADS, :].astype(BF16), cv_ref[bb].astype(BF16)) for bb in range(SB)],
        axis=0)
    o8_s[hrows, :] = o + (p_self * inv) * v8_s[hrows, :]

    lane_c = lax.broadcasted_iota(jnp.int32, (KV_WIDTH, WINDOW), 1)
    kt_all = kt_s[...]
    vt_all = vt_s[...]
    for bb in range(SB):
        put = (WINDOW - 1) - (base + bb)
        knew_ref[bb] = jnp.where(lane_c == WINDOW - 1, pltpu.roll(kt_all, put, 1),
                                 pltpu.roll(ck_ref[bb], WINDOW - 1, 1))
        vnew_ref[bb] = jnp.where(lane_c == WINDOW - 1, pltpu.roll(vt_all, put, 1),
                                 pltpu.roll(cv_ref[bb], WINDOW - 1, 1))

    @pl.when(i == nsteps - 1)
    def _():
        ypool = jnp.concatenate(
            [(_dot(ypre_s[:, g * POOL_GROUP_WIDTH:(g + 1) * POOL_GROUP_WIDTH],
                   mix_ref[g * POOL_GROUP_WIDTH:(g + 1) * POOL_GROUP_WIDTH, :])
              * pscale_ref[:, g * POOL_GROUP_WIDTH:(g + 1) * POOL_GROUP_WIDTH]).astype(BF16)
             for g in range(len(POOL_WINDOWS))], axis=1)
        ycols = []
        for p in range(Q_COLS):
            halves = []
            for e in range(HEADS_PER_COL):
                head = p * HEADS_PER_COL + e
                oh = o8_s[pl.ds(head, N, stride=N_HEADS), :]
                halves.append(oh if e == _kv_head(head) else pltpu.roll(oh, HEAD_DIM, 1))
            ycols.append(jnp.where(lane < HEAD_DIM, halves[0], halves[1]).astype(BF16))
        yattn = jnp.concatenate(ycols, axis=1)
        _merge(ypool, yattn, sga_s, sgb_s, wpp_ref, wap_ref, merged_s, all_rows)
        x1 = x_ref[...] + _dot(merged_s[...], wout_ref[...])
        h2 = ((x1 * _rms_scale(x1)) * n2_ref[...]).astype(BF16)
        _ffn_act(h2, wg_ref, wu_ref, act_s, all_rows)
        y_ref[...] = x1 + _dot(act_s[...], wd_ref[...])
        for w in range(nw):
            writeback_copy(w).wait()


def _sample_call(x, ck, cv, sp, sinkc, small, big):
    N = x.shape[0]
    SB = SAMPLE_BLOCK
    W = WINDOW
    assert tuple(b.shape for b in big) == tuple((r, c) for _, r, c in BIG_WEIGHTS)
    consts = (sinkc,) + tuple(small)
    in_specs = [_const_spec(x.shape),
                pl.BlockSpec((SB, KV_WIDTH, W), lambda i: (i, 0, 0)),
                pl.BlockSpec((SB, KV_WIDTH, W), lambda i: (i, 0, 0)),
                pl.BlockSpec((POOL_BUF, SB, POOL_WIDTH), lambda i: (0, i, 0))]
    in_specs += [_const_spec(w.shape) for w in consts]
    in_specs += [pl.BlockSpec(memory_space=pl.ANY) for _ in big]
    out_shape = (jax.ShapeDtypeStruct((N, D_MODEL), F32),
                 jax.ShapeDtypeStruct((N, KV_WIDTH, W), F32),
                 jax.ShapeDtypeStruct((N, KV_WIDTH, W), F32),
                 jax.ShapeDtypeStruct((POOL_BUF, N, POOL_WIDTH), F32))
    out_shape += tuple(jax.ShapeDtypeStruct(b.shape, BF16) for b in big)
    out_specs = (pl.BlockSpec((N, D_MODEL), lambda i: (0, 0)),
                 pl.BlockSpec((SB, KV_WIDTH, W), lambda i: (i, 0, 0)),
                 pl.BlockSpec((SB, KV_WIDTH, W), lambda i: (i, 0, 0)),
                 pl.BlockSpec((POOL_BUF, SB, POOL_WIDTH), lambda i: (0, i, 0)))
    out_specs += tuple(pl.BlockSpec(memory_space=pl.ANY) for _ in big)
    scratch = [pltpu.VMEM(b.shape, BF16) for b in big]
    scratch += [pltpu.VMEM((STAGE_SLOTS, CONVERT_ROWS, STAGE_COLS), F32),
                pltpu.SemaphoreType.DMA((STAGE_SLOTS,)),
                pltpu.SemaphoreType.DMA((len(big),)),
                pltpu.VMEM((N, POOL_WIDTH), F32),
               pltpu.VMEM((KV_WIDTH, N), F32),
               pltpu.VMEM((KV_WIDTH, N), F32),
               pltpu.VMEM((N * N_HEADS, LANES), F32),
               pltpu.VMEM((N * N_HEADS, LANES), F32),
               pltpu.VMEM((N * N_HEADS, LANES), F32),
               pltpu.VMEM((N * N_HEADS, LANES), F32),
               pltpu.VMEM((N, POOL_WIDTH), BF16),
               pltpu.VMEM((N, D_MODEL), F32),
               pltpu.VMEM((N, D_MODEL), F32),
               pltpu.VMEM((N, D_MODEL), BF16),
               pltpu.VMEM((N, D_FF), BF16)]
    return pl.pallas_call(
        _sample_kernel,
        grid=(N // SB,),
        in_specs=in_specs,
        out_specs=out_specs,
        out_shape=out_shape,
        scratch_shapes=scratch,
        compiler_params=pltpu.CompilerParams(
            dimension_semantics=("arbitrary",),
            vmem_limit_bytes=VMEM_LIMIT_BYTES),
        name="sample_layer",
    )(x, ck, cv, sp, *consts, *big)


def _head_sum_matrix():
    idx = np.arange(MXU_DIM) // HEAD_DIM
    return jnp.asarray(idx[:, None] == idx[None, :], dtype=BF16)


def _cache_to_device_order(c):
    n, w = c.shape[0], c.shape[1]
    return jnp.transpose(c, (0, 2, 3, 1)).reshape(n, KV_WIDTH, w)


def _cache_from_device_order(c):
    n, _, w = c.shape
    return jnp.transpose(c.reshape(n, N_KV_HEADS, HEAD_DIM, w), (0, 3, 1, 2))[None]


@jax.jit
def _forward(x_prompt, x_sample, cache_k, cache_v, state_pool, norm1, w_in, q_norm, k_norm, sinks,
             pool_mix_w, pool_scale, w_pool_proj, w_attn_proj, w_out, norm2, w_gate, w_up, w_down):
    depth = w_in.shape[0]
    assert depth == 1, "single-layer trunk"
    l = 0
    N = x_sample.shape[0]
    assert cache_k.shape[2] == WINDOW and x_sample.shape[1] == 1

    qn = (jnp.tile(q_norm[l], N_HEADS) * (HEAD_DIM ** -0.5)).reshape(1, Q_WIDTH)
    kn = jnp.tile(k_norm[l], N_KV_HEADS).reshape(1, KV_WIDTH)
    n1 = norm1[l].reshape(1, D_MODEL)
    n2 = norm2[l].reshape(1, D_MODEL)
    seg = _head_sum_matrix()
    pscale = pool_scale[l].reshape(1, POOL_WIDTH)
    big = (w_in[l], pool_mix_w[l].reshape(len(POOL_WINDOWS) * POOL_GROUP_WIDTH, POOL_GROUP_WIDTH),
           w_pool_proj[l], w_attn_proj[l], w_out[l], w_gate[l], w_up[l], w_down[l])

    sink_cols = jnp.broadcast_to(jnp.tile(sinks[l], SAMPLE_BLOCK)[:, None], (SAMPLE_BLOCK * N_HEADS, LANES))
    y_s, k_s, v_s, pool_s, w_in_b, mix_b, wpp_b, wap_b, wout_b, wg_b, wu_b, wd_b = _sample_call(
        x_sample.reshape(N, D_MODEL),
        _cache_to_device_order(cache_k[l]), _cache_to_device_order(cache_v[l]),
        jnp.transpose(state_pool[l], (1, 0, 2)), sink_cols, (n1, qn, kn, seg, pscale, n2), big)

    weights = (n1, w_in_b, qn, kn, seg, mix_b.reshape(len(POOL_WINDOWS), POOL_GROUP_WIDTH, POOL_GROUP_WIDTH),
               pscale, wpp_b, wap_b, wout_b, n2, wg_b, wu_b, wd_b)
    y_p, k_p, v_p, pool_p = _prompt_call(x_prompt, sinks[l], weights)

    return (y_p, y_s.reshape(N, 1, D_MODEL),
            _cache_from_device_order(k_p), _cache_from_device_order(v_p), pool_p[None],
            _cache_from_device_order(k_s), _cache_from_device_order(v_s),
            jnp.transpose(pool_s, (1, 0, 2))[None])


def kernel(x_prompt, x_sample, cache_k, cache_v, state_pool, norm1, w_in, q_norm, k_norm, sinks, pool_mix_w,
           pool_scale, w_pool_proj, w_attn_proj, w_out, norm2, w_gate, w_up, w_down):
    return _forward(x_prompt, x_sample, cache_k, cache_v, state_pool, norm1, w_in, q_norm, k_norm, sinks,
                    pool_mix_w, pool_scale, w_pool_proj, w_attn_proj, w_out, norm2, w_gate, w_up, w_down)
```

```python
import jax
import jax.numpy as jnp
from jax import lax
from jax.experimental import pallas as pl
from jax.experimental.pallas import tpu as pltpu

D_MODEL = 1024
POOL_WINDOWS = (2, 4, 8, 16)
POOL_GROUP_WIDTH = 128
POOL_WIDTH = 512
POOL_BUF = 15
N_HEADS = 8
N_KV_HEADS = 2
HEAD_DIM = 64
GROUP = N_HEADS // N_KV_HEADS
Q_WIDTH = 512
KV_WIDTH = 128
WINDOW = 128
D_FF = 2816
EPS = 1e-6
NEG = -1e30

LANES = 128
ATT_BLOCK = WINDOW
SEQ_TILE = 512
SUB_TILE = 256
FF_CHUNK = 512
GATE_CHUNK = 256
SAMPLE_BLOCK = 16
PREFIX = 16
VMEM_LIMIT_BYTES = 58 * 1024 * 1024
Q_COLS = Q_WIDTH // LANES
HEADS_PER_COL = LANES // HEAD_DIM
N_GATE_CHUNKS = D_MODEL // GATE_CHUNK

OFF_U = 0
OFF_Q = OFF_U + POOL_WIDTH
OFF_K = OFF_Q + Q_WIDTH
OFF_V = OFF_K + KV_WIDTH
OFF_GA = OFF_V + KV_WIDTH
OFF_GB = OFF_GA + D_MODEL

BF16 = jnp.bfloat16
F32 = jnp.float32


def _dot(a, b):
    return jnp.dot(a, b, preferred_element_type=F32)


def _dot_nt(a, b):
    return lax.dot_general(a, b, (((1,), (1,)), ((), ())), preferred_element_type=F32)


def _rms_scale(x):
    return lax.rsqrt(jnp.mean(x * x, axis=-1, keepdims=True) + EPS)


def _head_norm(x, gain):
    sq = x * x
    low = lax.broadcasted_iota(jnp.int32, (x.shape[0], LANES), 1) < HEAD_DIM
    cols = []
    for c in range(0, x.shape[1], LANES):
        blk = sq[:, c:c + LANES]
        ss_lo = jnp.sum(jnp.where(low, blk, 0.0), axis=-1, keepdims=True)
        ss_hi = jnp.sum(jnp.where(low, 0.0, blk), axis=-1, keepdims=True)
        cols.append(jnp.where(low, ss_lo, ss_hi))
    ss = jnp.concatenate(cols, axis=1) if len(cols) > 1 else cols[0]
    return (x * lax.rsqrt(ss * (1.0 / HEAD_DIM) + EPS)) * gain


def _kv_head(head):
    return head // GROUP


def _ffn_act(h2, wg_ref, wu_ref, act_ref, rows):
    for start in range(0, D_FF, FF_CHUNK):
        width = min(FF_CHUNK, D_FF - start)
        g = _dot(h2, wg_ref[:, start:start + width])
        u = _dot(h2, wu_ref[:, start:start + width])
        act_ref[rows, start:start + width] = (g * jax.nn.sigmoid(g) * u).astype(BF16)


def _gate_chunk(h, c, w_in_ref, sga_ref, sgb_ref, rows):
    lo = c * GATE_CHUNK
    sga_ref[rows, lo:lo + GATE_CHUNK] = jax.nn.sigmoid(_dot(h, w_in_ref[:, OFF_GA + lo:OFF_GA + lo + GATE_CHUNK]))
    sgb_ref[rows, lo:lo + GATE_CHUNK] = jax.nn.sigmoid(_dot(h, w_in_ref[:, OFF_GB + lo:OFF_GB + lo + GATE_CHUNK]))


def _merge(ypool, yattn, sga_ref, sgb_ref, wpp_ref, wap_ref, merged_ref, rows):
    for c in range(N_GATE_CHUNKS):
        lo = c * GATE_CHUNK
        pp = _dot(ypool, wpp_ref[:, lo:lo + GATE_CHUNK])
        ap = _dot(yattn, wap_ref[:, lo:lo + GATE_CHUNK])
        merged = sga_ref[rows, lo:lo + GATE_CHUNK] * pp + sgb_ref[rows, lo:lo + GATE_CHUNK] * ap
        merged_ref[rows, lo:lo + GATE_CHUNK] = merged.astype(BF16)


def _prompt_kernel(sinks_ref, x_ref, n1_ref, w_in_ref, qn_ref, kn_ref,
                   wfold_ref, wap_ref, wout_ref, n2_ref, wg_ref, wu_ref, wd_ref,
                   y_ref, knew_ref, vnew_ref, pnew_ref,
                   h_s, u_s, q_s, kd_s, vd_s, s_s, p_s, ypool_s, yattn_s, sga_s, sgb_s, merged_s, x1_s, h2_s, act_s):
    T = SEQ_TILE
    R = SUB_TILE
    subs = [slice(r0, r0 + R) for r0 in range(0, T, R)]
    n_blocks = T // ATT_BLOCK
    j = pl.program_id(1)
    rows_per_kv = GROUP * ATT_BLOCK

    @pl.when(j == 0)
    def _():
        u_s[0:PREFIX, :] = jnp.zeros((PREFIX, POOL_WIDTH), F32)
        kd_s[:, 0:ATT_BLOCK, :] = jnp.zeros((N_KV_HEADS, ATT_BLOCK, KV_WIDTH), BF16)
        vd_s[:, 0:ATT_BLOCK, :] = jnp.zeros((N_KV_HEADS, ATT_BLOCK, KV_WIDTH), BF16)

    lane_r = lax.broadcasted_iota(jnp.int32, (R, LANES), 1)
    lane = lax.broadcasted_iota(jnp.int32, (ATT_BLOCK, LANES), 1)
    row = lax.broadcasted_iota(jnp.int32, (ATT_BLOCK, 2 * ATT_BLOCK), 0)
    col = lax.broadcasted_iota(jnp.int32, (ATT_BLOCK, 2 * ATT_BLOCK), 1)
    band = (col >= row) & (col <= row + WINDOW)
    first_lo = jnp.where(j == 0, ATT_BLOCK, 0)

    def norm1(rows):
        x = x_ref[0, rows, :]
        h_s[rows, :] = ((x * _rms_scale(x)) * n1_ref[...]).astype(BF16)

    def in_proj(rows):
        h = h_s[rows, :]
        u_s[PREFIX + rows.start:PREFIX + rows.stop, :] = _dot(h, w_in_ref[:, OFF_U:OFF_U + POOL_WIDTH])
        q = _dot(h, w_in_ref[:, OFF_Q:OFF_Q + Q_WIDTH])
        q_s[rows, :] = _head_norm(q, qn_ref[...]).astype(BF16)
        kv = _dot(h, w_in_ref[:, OFF_K:OFF_K + 2 * KV_WIDTH])
        kn = _head_norm(kv[:, 0:KV_WIDTH], kn_ref[...])
        v = kv[:, KV_WIDTH:]
        dst_rows = slice(ATT_BLOCK + rows.start, ATT_BLOCK + rows.stop)
        for src, dst in ((kn, kd_s), (v, vd_s)):
            swapped = pltpu.roll(src, HEAD_DIM, 1)
            dst[0, dst_rows, :] = jnp.where(lane_r < HEAD_DIM, src, swapped).astype(BF16)
            dst[1, dst_rows, :] = jnp.where(lane_r < HEAD_DIM, swapped, src).astype(BF16)
        if rows.stop == T:
            knew_ref[0] = kn[R - ATT_BLOCK:, :].T
            vnew_ref[0] = v[R - ATT_BLOCK:, :].T
            pnew_ref[0] = u_s[PREFIX + T - POOL_BUF:PREFIX + T, :]

    def pool(rows):
        pos1 = j * T + rows.start + lax.broadcasted_iota(jnp.int32, (R, 1), 0) + 1
        for g, w in enumerate(POOL_WINDOWS):
            cols = slice(g * POOL_GROUP_WIDTH, (g + 1) * POOL_GROUP_WIDTH)
            a = u_s[rows.start:rows.stop + PREFIX, cols]
            s = a
            shift = 1
            while shift < w:
                s = s + pltpu.roll(s, shift, 0)
                shift *= 2
            inv_cnt = 1.0 / jnp.minimum(pos1, w).astype(F32)
            ypool_s[rows, cols] = (s[PREFIX:, :] * inv_cnt - a[PREFIX:, :]).astype(BF16)

    def gates(rows, chunks):
        h = h_s[rows, :]
        for c in chunks:
            _gate_chunk(h, c, w_in_ref, sga_s, sgb_s, rows)

    def scores(b):
        r0 = b * ATT_BLOCK
        for c in range(N_KV_HEADS):
            parts = []
            for p in range(c * Q_COLS // N_KV_HEADS, (c + 1) * Q_COLS // N_KV_HEADS):
                qcol = q_s[r0:r0 + ATT_BLOCK, p * LANES:(p + 1) * LANES]
                parts.append(jnp.where(lane < HEAD_DIM, qcol, jnp.zeros_like(qcol)))
                parts.append(jnp.where(lane >= HEAD_DIM, qcol, jnp.zeros_like(qcol)))
            s_s[b % 2, c * rows_per_kv:(c + 1) * rows_per_kv, :] = _dot_nt(
                jnp.concatenate(parts, axis=0), kd_s[c, r0:r0 + 2 * ATT_BLOCK, :])

    def softmax(b):
        valid = band & (col >= first_lo) if b == 0 else band
        for head in range(N_HEADS):
            sh = jnp.where(valid, s_s[b % 2, head * ATT_BLOCK:(head + 1) * ATT_BLOCK, :], NEG)
            sink = sinks_ref[head]
            m = jnp.maximum(jnp.max(sh, axis=-1, keepdims=True), sink)
            p = jnp.exp(sh - m)
            denom = jnp.sum(p, axis=-1, keepdims=True) + jnp.exp(sink - m)
            p_s[b % 2, head * ATT_BLOCK:(head + 1) * ATT_BLOCK, :] = (p * (1.0 / denom)).astype(BF16)

    def weighted_values(b):
        r0 = b * ATT_BLOCK
        for c in range(N_KV_HEADS):
            o = _dot(p_s[b % 2, c * rows_per_kv:(c + 1) * rows_per_kv, :], vd_s[c, r0:r0 + 2 * ATT_BLOCK, :])
            for pp in range(Q_COLS // N_KV_HEADS):
                p_col = c * Q_COLS // N_KV_HEADS + pp
                o_lo = o[(2 * pp) * ATT_BLOCK:(2 * pp + 1) * ATT_BLOCK, :]
                o_hi = o[(2 * pp + 1) * ATT_BLOCK:(2 * pp + 2) * ATT_BLOCK, :]
                yattn_s[r0:r0 + ATT_BLOCK, p_col * LANES:(p_col + 1) * LANES] = (
                    jnp.where(lane < HEAD_DIM, o_lo, o_hi).astype(BF16))

    def out_proj(rows):
        _merge(ypool_s[rows, :], yattn_s[rows, :], sga_s, sgb_s, wfold_ref, wap_ref, merged_s, rows)
        x1 = x_ref[0, rows, :] + _dot(merged_s[rows, :], wout_ref[...])
        x1_s[rows, :] = x1
        h2_s[rows, :] = ((x1 * _rms_scale(x1)) * n2_ref[...]).astype(BF16)

    for rows in subs:
        norm1(rows)
    for rows in subs:
        in_proj(rows)
    pool(subs[0])
    gates(subs[0], range(N_GATE_CHUNKS))
    for rows in subs[1:]:
        pool(rows)
    later_gates = [(rows, c) for rows in subs[1:] for c in range(N_GATE_CHUNKS)]
    per_block = -(-len(later_gates) // n_blocks)
    scores(0)
    for b in range(n_blocks):
        if b + 1 < n_blocks:
            scores(b + 1)
        for rows, c in later_gates[b * per_block:(b + 1) * per_block]:
            gates(rows, [c])
        softmax(b)
        weighted_values(b)

    u_s[0:PREFIX, :] = u_s[T:T + PREFIX, :]
    kd_s[:, 0:ATT_BLOCK, :] = kd_s[:, T:T + ATT_BLOCK, :]
    vd_s[:, 0:ATT_BLOCK, :] = vd_s[:, T:T + ATT_BLOCK, :]

    for rows in subs:
        out_proj(rows)
    for rows in subs:
        _ffn_act(h2_s[rows, :], wg_ref, wu_ref, act_s, rows)
    for rows in subs:
        y_ref[0, rows, :] = x1_s[rows, :] + _dot(act_s[rows, :], wd_ref[...])


def _const_spec(shape):
    nd = len(shape)
    return pl.BlockSpec(shape, lambda *_: (0,) * nd, pipeline_mode=pl.Buffered(1))


def _prompt_call(x, sinks, weights):
    B, S, _ = x.shape
    T = SEQ_TILE
    in_specs = [pl.BlockSpec(memory_space=pltpu.SMEM),
                pl.BlockSpec((1, T, D_MODEL), lambda b, j: (b, j, 0))]
    in_specs += [_const_spec(w.shape) for w in weights]
    out_shape = (jax.ShapeDtypeStruct((B, S, D_MODEL), F32),
                 jax.ShapeDtypeStruct((B, KV_WIDTH, ATT_BLOCK), F32),
                 jax.ShapeDtypeStruct((B, KV_WIDTH, ATT_BLOCK), F32),
                 jax.ShapeDtypeStruct((B, POOL_BUF, POOL_WIDTH), F32))
    out_specs = (pl.BlockSpec((1, T, D_MODEL), lambda b, j: (b, j, 0)),
                 pl.BlockSpec((1, KV_WIDTH, ATT_BLOCK), lambda b, j: (b, 0, 0)),
                 pl.BlockSpec((1, KV_WIDTH, ATT_BLOCK), lambda b, j: (b, 0, 0)),
                 pl.BlockSpec((1, POOL_BUF, POOL_WIDTH), lambda b, j: (b, 0, 0)))
    scratch = [pltpu.VMEM((T, D_MODEL), BF16),
               pltpu.VMEM((PREFIX + T, POOL_WIDTH), F32),
               pltpu.VMEM((T, Q_WIDTH), BF16),
               pltpu.VMEM((N_KV_HEADS, ATT_BLOCK + T, KV_WIDTH), BF16),
               pltpu.VMEM((N_KV_HEADS, ATT_BLOCK + T, KV_WIDTH), BF16),
               pltpu.VMEM((2, N_HEADS * ATT_BLOCK, 2 * ATT_BLOCK), F32),
               pltpu.VMEM((2, N_HEADS * ATT_BLOCK, 2 * ATT_BLOCK), BF16),
               pltpu.VMEM((T, POOL_WIDTH), BF16),
               pltpu.VMEM((T, Q_WIDTH), BF16),
               pltpu.VMEM((T, D_MODEL), F32),
               pltpu.VMEM((T, D_MODEL), F32),
               pltpu.VMEM((T, D_MODEL), BF16),
               pltpu.VMEM((T, D_MODEL), F32),
               pltpu.VMEM((T, D_MODEL), BF16),
               pltpu.VMEM((T, D_FF), BF16)]
    return pl.pallas_call(
        _prompt_kernel,
        grid=(B, S // T),
        in_specs=in_specs,
        out_specs=out_specs,
        out_shape=out_shape,
        scratch_shapes=scratch,
        compiler_params=pltpu.CompilerParams(
            dimension_semantics=("arbitrary", "arbitrary"),
            vmem_limit_bytes=VMEM_LIMIT_BYTES),
        name="prompt_layer",
    )(sinks, x, *weights)


BIG_WEIGHTS = (("w_in", D_MODEL, OFF_GB + D_MODEL), ("mix", len(POOL_WINDOWS) * POOL_GROUP_WIDTH, POOL_GROUP_WIDTH),
               ("wpp", POOL_WIDTH, D_MODEL), ("wap", Q_WIDTH, D_MODEL), ("wout", D_MODEL, D_MODEL),
               ("wg", D_MODEL, D_FF), ("wu", D_MODEL, D_FF), ("wd", D_FF, D_MODEL))
W_MIX, W_PP = 1, 2
W_FOLD = len(BIG_WEIGHTS)
HANDED_ON = tuple(w for w in range(len(BIG_WEIGHTS)) if w not in (W_MIX, W_PP)) + (W_FOLD,)
CONVERT_ROWS = 128
STAGE_SLOTS = 4
STAGE_COLS = max(cols for _, _, cols in BIG_WEIGHTS)
WEIGHT_CHUNKS = tuple((w, r0, min(CONVERT_ROWS, nrows - r0))
                      for w, (_, nrows, _) in enumerate(BIG_WEIGHTS) for r0 in range(0, nrows, CONVERT_ROWS))


def _chunks_per_step(nsteps):
    first = [n for n, (w, _, _) in enumerate(WEIGHT_CHUNKS) if w == 0]
    rest = [n for n, (w, _, _) in enumerate(WEIGHT_CHUNKS) if w != 0]
    size = lambda n: WEIGHT_CHUNKS[n][2] * BIG_WEIGHTS[WEIGHT_CHUNKS[n][0]][2]
    total = sum(size(n) for n in rest)
    steps = [first] + [[] for _ in range(nsteps - 1)]
    done = 0
    for n in rest:
        steps[1 + min(nsteps - 2, done * (nsteps - 1) // total)].append(n)
        done += size(n)
    return steps


def _sample_kernel(*refs):
    nw = len(BIG_WEIGHTS)
    n_in = 10
    (x_ref, ck_ref, cv_ref, sp_ref, sinkc_ref, n1_ref, qn_ref, kn_ref, pscale_ref, n2_ref) = refs[:n_in]
    w_hbm = refs[n_in:n_in + nw]
    pos = n_in + nw
    y_ref, knew_ref, vnew_ref, pnew_ref = refs[pos:pos + 4]
    pos += 4
    w_bf_hbm = dict(zip(HANDED_ON, refs[pos:pos + len(HANDED_ON)]))
    pos += len(HANDED_ON)
    w_v = refs[pos:pos + nw + 1]
    pos += nw + 1
    (stage, stage_sem, out_sem, mixf_s,
     u_s, kt_s, vt_s, q8_s, k8_s, v8_s, o8_s, ypre_s, sga_s, sgb_s, merged_s, act_s) = refs[pos:]
    w_in_ref, _, wpp_ref, wap_ref, wout_ref, wg_ref, wu_ref, wd_ref, wfold_ref = w_v

    SB = SAMPLE_BLOCK
    N = x_ref.shape[0]
    nsteps = N // SB
    i = pl.program_id(0)
    lane = lax.broadcasted_iota(jnp.int32, (N, LANES), 1)
    all_rows = slice(0, N)

    def chunk_copy(n):
        w, r0, rows = WEIGHT_CHUNKS[n]
        cols = BIG_WEIGHTS[w][2]
        slot = n % STAGE_SLOTS
        return pltpu.make_async_copy(w_hbm[w].at[pl.ds(r0, rows), :],
                                     stage.at[slot, pl.ds(0, rows), pl.ds(0, cols)], stage_sem.at[slot])

    def writeback_copy(w):
        return pltpu.make_async_copy(w_v[w], w_bf_hbm[w], out_sem.at[w])

    def convert(n):
        w, r0, rows = WEIGHT_CHUNKS[n]
        cols = BIG_WEIGHTS[w][2]
        staged = stage[n % STAGE_SLOTS, 0:rows, 0:cols]
        if w == W_MIX:
            mixf_s[r0:r0 + rows, :] = staged
        else:
            w_v[w][r0:r0 + rows, :] = staged.astype(BF16)
        if r0 + rows == BIG_WEIGHTS[w][1] and w in HANDED_ON:
            writeback_copy(w).start()

    ahead = STAGE_SLOTS - 1
    for step, chunk_ids in enumerate(_chunks_per_step(nsteps)):
        @pl.when(i == step)
        def _(chunk_ids=chunk_ids):
            for n in chunk_ids:
                if n == 0:
                    for m in range(ahead):
                        chunk_copy(m).start()
                if n + ahead < len(WEIGHT_CHUNKS):
                    chunk_copy(n + ahead).start()
                chunk_copy(n).wait()
                convert(n)

    @pl.when(i == 0)
    def _():
        x = x_ref[...]
        h = ((x * _rms_scale(x)) * n1_ref[...]).astype(BF16)
        u_s[...] = _dot(h, w_in_ref[:, OFF_U:OFF_U + POOL_WIDTH])
        for gc in range(N_GATE_CHUNKS):
            _gate_chunk(h, gc, w_in_ref, sga_s, sgb_s, all_rows)
        q = _head_norm(_dot(h, w_in_ref[:, OFF_Q:OFF_Q + Q_WIDTH]), qn_ref[...])
        kv = _dot(h, w_in_ref[:, OFF_K:OFF_K + 2 * KV_WIDTH])
        kn = _head_norm(kv[:, 0:KV_WIDTH], kn_ref[...])
        v = kv[:, KV_WIDTH:]
        kt_s[...] = kn.T
        vt_s[...] = v.T
        for head in range(N_HEADS):
            k8_s[pl.ds(head, N, stride=N_HEADS), :] = kn
            v8_s[pl.ds(head, N, stride=N_HEADS), :] = v
        for p in range(Q_COLS):
            qcol = q[:, p * LANES:(p + 1) * LANES]
            qswap = pltpu.roll(qcol, HEAD_DIM, 1)
            for e in range(HEADS_PER_COL):
                head = p * HEADS_PER_COL + e
                c = _kv_head(head)
                src = qcol if e == c else qswap
                keep = (lane < HEAD_DIM) if c == 0 else (lane >= HEAD_DIM)
                q8_s[pl.ds(head, N, stride=N_HEADS), :] = jnp.where(keep, src, 0.0)

    base = i * SB
    rows = pl.ds(pl.multiple_of(base, SB), SB)

    unew = u_s[rows, :]
    for g, w in enumerate(POOL_WINDOWS):
        cols = slice(g * POOL_GROUP_WIDTH, (g + 1) * POOL_GROUP_WIDTH)
        s = unew[:, cols]
        for r in range(POOL_BUF - (w - 1), POOL_BUF):
            s = s + sp_ref[r, :, cols]
        ypre_s[rows, cols] = (s * (1.0 / w) - unew[:, cols]).astype(BF16)
    pnew_ref[0:POOL_BUF - 1] = sp_ref[1:POOL_BUF]
    pnew_ref[POOL_BUF - 1] = unew

    hrows = pl.ds(pl.multiple_of(base * N_HEADS, SB * N_HEADS), SB * N_HEADS)
    lhs = q8_s[hrows, :]
    sc = jnp.concatenate(
        [_dot(lhs[bb * N_HEADS:(bb + 1) * N_HEADS, :].astype(BF16), ck_ref[bb].astype(BF16)) for bb in range(SB)],
        axis=0)
    sc_self = jnp.sum(lhs * k8_s[hrows, :], axis=-1, keepdims=True)
    sink = sinkc_ref[:, 0:1]
    m = jnp.maximum(jnp.maximum(jnp.max(sc, axis=-1, keepdims=True), sc_self), sink)
    p = jnp.exp(sc - m)
    p_self = jnp.exp(sc_self - m)
    denom = jnp.sum(p, axis=-1, keepdims=True) + p_self + jnp.exp(sink - m)
    inv = 1.0 / denom
    pn = p * inv
    o = jnp.concatenate(
        [_dot_nt(pn[bb * N_HEADS:(bb + 1) * N_HEADS, :].astype(BF16), cv_ref[bb].astype(BF16)) for bb in range(SB)],
        axis=0)
    o8_s[hrows, :] = o + (p_self * inv) * v8_s[hrows, :]

    lane_c = lax.broadcasted_iota(jnp.int32, (KV_WIDTH, WINDOW), 1)
    kt_all = kt_s[...]
    vt_all = vt_s[...]
    for bb in range(SB):
        put = (WINDOW - 1) - (base + bb)
        knew_ref[bb] = jnp.where(lane_c == WINDOW - 1, pltpu.roll(kt_all, put, 1),
                                 pltpu.roll(ck_ref[bb], WINDOW - 1, 1))
        vnew_ref[bb] = jnp.where(lane_c == WINDOW - 1, pltpu.roll(vt_all, put, 1),
                                 pltpu.roll(cv_ref[bb], WINDOW - 1, 1))

    @pl.when(i == nsteps - 1)
    def _():
        for g in range(len(POOL_WINDOWS)):
            grp = slice(g * POOL_GROUP_WIDTH, (g + 1) * POOL_GROUP_WIDTH)
            wfold_ref[grp, :] = _dot((mixf_s[grp, :] * pscale_ref[:, grp]).astype(BF16), wpp_ref[grp, :]).astype(BF16)
        writeback_copy(W_FOLD).start()
        ycols = []
        for p in range(Q_COLS):
            halves = []
            for e in range(HEADS_PER_COL):
                head = p * HEADS_PER_COL + e
                oh = o8_s[pl.ds(head, N, stride=N_HEADS), :]
                halves.append(oh if e == _kv_head(head) else pltpu.roll(oh, HEAD_DIM, 1))
            ycols.append(jnp.where(lane < HEAD_DIM, halves[0], halves[1]).astype(BF16))
        yattn = jnp.concatenate(ycols, axis=1)
        _merge(ypre_s[...], yattn, sga_s, sgb_s, wfold_ref, wap_ref, merged_s, all_rows)
        x1 = x_ref[...] + _dot(merged_s[...], wout_ref[...])
        h2 = ((x1 * _rms_scale(x1)) * n2_ref[...]).astype(BF16)
        _ffn_act(h2, wg_ref, wu_ref, act_s, all_rows)
        y_ref[...] = x1 + _dot(act_s[...], wd_ref[...])
        for w in HANDED_ON:
            writeback_copy(w).wait()


def _sample_call(x, ck, cv, sp, sinkc, small, big):
    N = x.shape[0]
    SB = SAMPLE_BLOCK
    W = WINDOW
    assert tuple(b.shape for b in big) == tuple((r, c) for _, r, c in BIG_WEIGHTS)
    consts = (sinkc,) + tuple(small)
    in_specs = [_const_spec(x.shape),
                pl.BlockSpec((SB, KV_WIDTH, W), lambda i: (i, 0, 0)),
                pl.BlockSpec((SB, KV_WIDTH, W), lambda i: (i, 0, 0)),
                pl.BlockSpec((POOL_BUF, SB, POOL_WIDTH), lambda i: (0, i, 0))]
    in_specs += [_const_spec(w.shape) for w in consts]
    in_specs += [pl.BlockSpec(memory_space=pl.ANY) for _ in big]
    out_shape = (jax.ShapeDtypeStruct((N, D_MODEL), F32),
                 jax.ShapeDtypeStruct((N, KV_WIDTH, W), F32),
                 jax.ShapeDtypeStruct((N, KV_WIDTH, W), F32),
                 jax.ShapeDtypeStruct((POOL_BUF, N, POOL_WIDTH), F32))
    bf_shapes = [b.shape for b in big] + [(POOL_WIDTH, D_MODEL)]
    out_shape += tuple(jax.ShapeDtypeStruct(bf_shapes[w], BF16) for w in HANDED_ON)
    out_specs = (pl.BlockSpec((N, D_MODEL), lambda i: (0, 0)),
                 pl.BlockSpec((SB, KV_WIDTH, W), lambda i: (i, 0, 0)),
                 pl.BlockSpec((SB, KV_WIDTH, W), lambda i: (i, 0, 0)),
                 pl.BlockSpec((POOL_BUF, SB, POOL_WIDTH), lambda i: (0, i, 0)))
    out_specs += tuple(pl.BlockSpec(memory_space=pl.ANY) for _ in HANDED_ON)
    scratch = [pltpu.VMEM(shape, BF16) for shape in bf_shapes]
    scratch += [pltpu.VMEM((STAGE_SLOTS, CONVERT_ROWS, STAGE_COLS), F32),
                pltpu.SemaphoreType.DMA((STAGE_SLOTS,)),
                pltpu.SemaphoreType.DMA((len(bf_shapes),)),
                pltpu.VMEM(big[W_MIX].shape, F32),
                pltpu.VMEM((N, POOL_WIDTH), F32),
               pltpu.VMEM((KV_WIDTH, N), F32),
               pltpu.VMEM((KV_WIDTH, N), F32),
               pltpu.VMEM((N * N_HEADS, LANES), F32),
               pltpu.VMEM((N * N_HEADS, LANES), F32),
               pltpu.VMEM((N * N_HEADS, LANES), F32),
               pltpu.VMEM((N * N_HEADS, LANES), F32),
               pltpu.VMEM((N, POOL_WIDTH), BF16),
               pltpu.VMEM((N, D_MODEL), F32),
               pltpu.VMEM((N, D_MODEL), F32),
               pltpu.VMEM((N, D_MODEL), BF16),
               pltpu.VMEM((N, D_FF), BF16)]
    return pl.pallas_call(
        _sample_kernel,
        grid=(N // SB,),
        in_specs=in_specs,
        out_specs=out_specs,
        out_shape=out_shape,
        scratch_shapes=scratch,
        compiler_params=pltpu.CompilerParams(
            dimension_semantics=("arbitrary",),
            vmem_limit_bytes=VMEM_LIMIT_BYTES),
        name="sample_layer",
    )(x, ck, cv, sp, *consts, *big)


def _cache_to_device_order(c):
    n, w = c.shape[0], c.shape[1]
    return jnp.transpose(c, (0, 2, 3, 1)).reshape(n, KV_WIDTH, w)


def _cache_from_device_order(c):
    n, _, w = c.shape
    return jnp.transpose(c.reshape(n, N_KV_HEADS, HEAD_DIM, w), (0, 3, 1, 2))[None]


@jax.jit
def _forward(x_prompt, x_sample, cache_k, cache_v, state_pool, norm1, w_in, q_norm, k_norm, sinks,
             pool_mix_w, pool_scale, w_pool_proj, w_attn_proj, w_out, norm2, w_gate, w_up, w_down):
    depth = w_in.shape[0]
    assert depth == 1, "single-layer trunk"
    l = 0
    N = x_sample.shape[0]
    assert cache_k.shape[2] == WINDOW and x_sample.shape[1] == 1

    qn = (jnp.tile(q_norm[l], N_HEADS) * (HEAD_DIM ** -0.5)).reshape(1, Q_WIDTH)
    kn = jnp.tile(k_norm[l], N_KV_HEADS).reshape(1, KV_WIDTH)
    n1 = norm1[l].reshape(1, D_MODEL)
    n2 = norm2[l].reshape(1, D_MODEL)
    pscale = pool_scale[l].reshape(1, POOL_WIDTH)
    big = (w_in[l], pool_mix_w[l].reshape(len(POOL_WINDOWS) * POOL_GROUP_WIDTH, POOL_GROUP_WIDTH),
           w_pool_proj[l], w_attn_proj[l], w_out[l], w_gate[l], w_up[l], w_down[l])

    sink_cols = jnp.broadcast_to(jnp.tile(sinks[l], SAMPLE_BLOCK)[:, None], (SAMPLE_BLOCK * N_HEADS, LANES))
    y_s, k_s, v_s, pool_s, w_in_b, wap_b, wout_b, wg_b, wu_b, wd_b, wfold_b = _sample_call(
        x_sample.reshape(N, D_MODEL),
        _cache_to_device_order(cache_k[l]), _cache_to_device_order(cache_v[l]),
        jnp.transpose(state_pool[l], (1, 0, 2)), sink_cols, (n1, qn, kn, pscale, n2), big)

    weights = (n1, w_in_b, qn, kn, wfold_b, wap_b, wout_b, n2, wg_b, wu_b, wd_b)
    y_p, k_p, v_p, pool_p = _prompt_call(x_prompt, sinks[l], weights)

    return (y_p, y_s.reshape(N, 1, D_MODEL),
            _cache_from_device_order(k_p), _cache_from_device_order(v_p), pool_p[None],
            _cache_from_device_order(k_s), _cache_from_device_order(v_s),
            jnp.transpose(pool_s, (1, 0, 2))[None])


def kernel(x_prompt, x_sample, cache_k, cache_v, state_pool, norm1, w_in, q_norm, k_norm, sinks, pool_mix_w,
           pool_scale, w_pool_proj, w_attn_proj, w_out, norm2, w_gate, w_up, w_down):
    return _forward(x_prompt, x_sample, cache_k, cache_v, state_pool, norm1, w_in, q_norm, k_norm, sinks,
                    pool_mix_w, pool_scale, w_pool_proj, w_attn_proj, w_out, norm2, w_gate, w_up, w_down)
```

```python
import jax
import jax.numpy as jnp
from jax import lax
from jax.experimental import pallas as pl
from jax.experimental.pallas import tpu as pltpu

D_MODEL = 1024
POOL_WINDOWS = (2, 4, 8, 16)
POOL_GROUP_WIDTH = 128
POOL_WIDTH = 512
POOL_BUF = 15
N_HEADS = 8
N_KV_HEADS = 2
HEAD_DIM = 64
GROUP = N_HEADS // N_KV_HEADS
Q_WIDTH = 512
KV_WIDTH = 128
WINDOW = 128
D_FF = 2816
EPS = 1e-6
NEG = -1e30

LANES = 128
ATT_BLOCK = WINDOW
SEQ_TILE = 512
SUB_TILE = 256
FF_CHUNK = 512
GATE_CHUNK = 256
SAMPLE_BLOCK = 16
PREFIX = 16
VMEM_LIMIT_BYTES = 58 * 1024 * 1024
Q_COLS = Q_WIDTH // LANES
HEADS_PER_COL = LANES // HEAD_DIM
N_GATE_CHUNKS = D_MODEL // GATE_CHUNK

OFF_U = 0
OFF_Q = OFF_U + POOL_WIDTH
OFF_K = OFF_Q + Q_WIDTH
OFF_V = OFF_K + KV_WIDTH
OFF_GA = OFF_V + KV_WIDTH
OFF_GB = OFF_GA + D_MODEL

BF16 = jnp.bfloat16
F32 = jnp.float32


def _dot(a, b):
    return jnp.dot(a, b, preferred_element_type=F32)


def _dot_nt(a, b):
    return lax.dot_general(a, b, (((1,), (1,)), ((), ())), preferred_element_type=F32)


def _rms_scale(x):
    return lax.rsqrt(jnp.mean(x * x, axis=-1, keepdims=True) + EPS)


def _head_norm(x, gain):
    sq = x * x
    low = lax.broadcasted_iota(jnp.int32, (x.shape[0], LANES), 1) < HEAD_DIM
    cols = []
    for c in range(0, x.shape[1], LANES):
        blk = sq[:, c:c + LANES]
        ss_lo = jnp.sum(jnp.where(low, blk, 0.0), axis=-1, keepdims=True)
        ss_hi = jnp.sum(jnp.where(low, 0.0, blk), axis=-1, keepdims=True)
        cols.append(jnp.where(low, ss_lo, ss_hi))
    ss = jnp.concatenate(cols, axis=1) if len(cols) > 1 else cols[0]
    return (x * lax.rsqrt(ss * (1.0 / HEAD_DIM) + EPS)) * gain


def _kv_head(head):
    return head // GROUP


def _ffn_act(h2, wg_ref, wu_ref, act_ref, rows):
    for start in range(0, D_FF, FF_CHUNK):
        width = min(FF_CHUNK, D_FF - start)
        g = _dot(h2, wg_ref[:, start:start + width])
        u = _dot(h2, wu_ref[:, start:start + width])
        act_ref[rows, start:start + width] = (g * jax.nn.sigmoid(g) * u).astype(BF16)


def _gate_chunk(h, c, w_in_ref, sga_ref, sgb_ref, rows):
    lo = c * GATE_CHUNK
    sga_ref[rows, lo:lo + GATE_CHUNK] = jax.nn.sigmoid(_dot(h, w_in_ref[:, OFF_GA + lo:OFF_GA + lo + GATE_CHUNK]))
    sgb_ref[rows, lo:lo + GATE_CHUNK] = jax.nn.sigmoid(_dot(h, w_in_ref[:, OFF_GB + lo:OFF_GB + lo + GATE_CHUNK]))


def _merge(ypool, yattn, sga_ref, sgb_ref, wpp_ref, wap_ref, merged_ref, rows):
    for c in range(N_GATE_CHUNKS):
        lo = c * GATE_CHUNK
        pp = _dot(ypool, wpp_ref[:, lo:lo + GATE_CHUNK])
        ap = _dot(yattn, wap_ref[:, lo:lo + GATE_CHUNK])
        merged = sga_ref[rows, lo:lo + GATE_CHUNK] * pp + sgb_ref[rows, lo:lo + GATE_CHUNK] * ap
        merged_ref[rows, lo:lo + GATE_CHUNK] = merged.astype(BF16)


def _prompt_kernel(sinks_ref, x_ref, n1_ref, w_in_ref, qn_ref, kn_ref,
                   wfold_ref, wap_ref, wout_ref, n2_ref, wg_ref, wu_ref, wd_ref,
                   y_ref, knew_ref, vnew_ref, pnew_ref,
                   h_s, u_s, q_s, kd_s, vd_s, s_s, p_s, ypool_s, yattn_s, sga_s, sgb_s, merged_s, x1_s, h2_s, act_s):
    T = SEQ_TILE
    R = SUB_TILE
    subs = [slice(r0, r0 + R) for r0 in range(0, T, R)]
    n_blocks = T // ATT_BLOCK
    j = pl.program_id(1)
    rows_per_kv = GROUP * ATT_BLOCK

    @pl.when(j == 0)
    def _():
        u_s[0:PREFIX, :] = jnp.zeros((PREFIX, POOL_WIDTH), F32)
        kd_s[:, 0:ATT_BLOCK, :] = jnp.zeros((N_KV_HEADS, ATT_BLOCK, KV_WIDTH), BF16)
        vd_s[:, 0:ATT_BLOCK, :] = jnp.zeros((N_KV_HEADS, ATT_BLOCK, KV_WIDTH), BF16)

    lane_r = lax.broadcasted_iota(jnp.int32, (R, LANES), 1)
    lane = lax.broadcasted_iota(jnp.int32, (ATT_BLOCK, LANES), 1)
    row = lax.broadcasted_iota(jnp.int32, (ATT_BLOCK, 2 * ATT_BLOCK), 0)
    col = lax.broadcasted_iota(jnp.int32, (ATT_BLOCK, 2 * ATT_BLOCK), 1)
    band = (col >= row) & (col <= row + WINDOW)
    first_lo = jnp.where(j == 0, ATT_BLOCK, 0)

    def norm1(rows):
        x = x_ref[0, rows, :]
        h_s[rows, :] = ((x * _rms_scale(x)) * n1_ref[...]).astype(BF16)

    def in_proj(rows):
        h = h_s[rows, :]
        u_s[PREFIX + rows.start:PREFIX + rows.stop, :] = _dot(h, w_in_ref[:, OFF_U:OFF_U + POOL_WIDTH])
        q = _dot(h, w_in_ref[:, OFF_Q:OFF_Q + Q_WIDTH])
        q_s[rows, :] = _head_norm(q, qn_ref[...]).astype(BF16)
        kv = _dot(h, w_in_ref[:, OFF_K:OFF_K + 2 * KV_WIDTH])
        kn = _head_norm(kv[:, 0:KV_WIDTH], kn_ref[...])
        v = kv[:, KV_WIDTH:]
        dst_rows = slice(ATT_BLOCK + rows.start, ATT_BLOCK + rows.stop)
        for src, dst in ((kn, kd_s), (v, vd_s)):
            swapped = pltpu.roll(src, HEAD_DIM, 1)
            dst[0, dst_rows, :] = jnp.where(lane_r < HEAD_DIM, src, swapped).astype(BF16)
            dst[1, dst_rows, :] = jnp.where(lane_r < HEAD_DIM, swapped, src).astype(BF16)
        if rows.stop == T:
            knew_ref[0] = kn[R - ATT_BLOCK:, :].T
            vnew_ref[0] = v[R - ATT_BLOCK:, :].T
            pnew_ref[0] = u_s[PREFIX + T - POOL_BUF:PREFIX + T, :]

    def pool(rows):
        pos1 = j * T + rows.start + lax.broadcasted_iota(jnp.int32, (R, 1), 0) + 1
        for g, w in enumerate(POOL_WINDOWS):
            cols = slice(g * POOL_GROUP_WIDTH, (g + 1) * POOL_GROUP_WIDTH)
            a = u_s[rows.start:rows.stop + PREFIX, cols]
            s = a
            shift = 1
            while shift < w:
                s = s + pltpu.roll(s, shift, 0)
                shift *= 2
            inv_cnt = 1.0 / jnp.minimum(pos1, w).astype(F32)
            ypool_s[rows, cols] = (s[PREFIX:, :] * inv_cnt - a[PREFIX:, :]).astype(BF16)

    def gates(rows, chunks):
        h = h_s[rows, :]
        for c in chunks:
            _gate_chunk(h, c, w_in_ref, sga_s, sgb_s, rows)

    def scores(b):
        r0 = b * ATT_BLOCK
        for c in range(N_KV_HEADS):
            parts = []
            for p in range(c * Q_COLS // N_KV_HEADS, (c + 1) * Q_COLS // N_KV_HEADS):
                qcol = q_s[r0:r0 + ATT_BLOCK, p * LANES:(p + 1) * LANES]
                parts.append(jnp.where(lane < HEAD_DIM, qcol, jnp.zeros_like(qcol)))
                parts.append(jnp.where(lane >= HEAD_DIM, qcol, jnp.zeros_like(qcol)))
            s_s[b % 2, c * rows_per_kv:(c + 1) * rows_per_kv, :] = _dot_nt(
                jnp.concatenate(parts, axis=0), kd_s[c, r0:r0 + 2 * ATT_BLOCK, :])

    def softmax(b):
        valid = band & (col >= first_lo) if b == 0 else band
        for head in range(N_HEADS):
            sh = jnp.where(valid, s_s[b % 2, head * ATT_BLOCK:(head + 1) * ATT_BLOCK, :], NEG)
            sink = sinks_ref[head]
            m = jnp.maximum(jnp.max(sh, axis=-1, keepdims=True), sink)
            p = jnp.exp(sh - m)
            denom = jnp.sum(p, axis=-1, keepdims=True) + jnp.exp(sink - m)
            p_s[b % 2, head * ATT_BLOCK:(head + 1) * ATT_BLOCK, :] = (p * (1.0 / denom)).astype(BF16)

    def weighted_values(b):
        r0 = b * ATT_BLOCK
        for c in range(N_KV_HEADS):
            o = _dot(p_s[b % 2, c * rows_per_kv:(c + 1) * rows_per_kv, :], vd_s[c, r0:r0 + 2 * ATT_BLOCK, :])
            for pp in range(Q_COLS // N_KV_HEADS):
                p_col = c * Q_COLS // N_KV_HEADS + pp
                o_lo = o[(2 * pp) * ATT_BLOCK:(2 * pp + 1) * ATT_BLOCK, :]
                o_hi = o[(2 * pp + 1) * ATT_BLOCK:(2 * pp + 2) * ATT_BLOCK, :]
                yattn_s[r0:r0 + ATT_BLOCK, p_col * LANES:(p_col + 1) * LANES] = (
                    jnp.where(lane < HEAD_DIM, o_lo, o_hi).astype(BF16))

    def out_proj(rows):
        _merge(ypool_s[rows, :], yattn_s[rows, :], sga_s, sgb_s, wfold_ref, wap_ref, merged_s, rows)
        x1 = x_ref[0, rows, :] + _dot(merged_s[rows, :], wout_ref[...])
        x1_s[rows, :] = x1
        h2_s[rows, :] = ((x1 * _rms_scale(x1)) * n2_ref[...]).astype(BF16)

    for rows in subs:
        norm1(rows)
    for rows in subs:
        in_proj(rows)
    pool(subs[0])
    gates(subs[0], range(N_GATE_CHUNKS))
    for rows in subs[1:]:
        pool(rows)
    later_gates = [(rows, c) for rows in subs[1:] for c in range(N_GATE_CHUNKS)]
    per_block = -(-len(later_gates) // n_blocks)
    scores(0)
    for b in range(n_blocks):
        if b + 1 < n_blocks:
            scores(b + 1)
        for rows, c in later_gates[b * per_block:(b + 1) * per_block]:
            gates(rows, [c])
        softmax(b)
        weighted_values(b)

    u_s[0:PREFIX, :] = u_s[T:T + PREFIX, :]
    kd_s[:, 0:ATT_BLOCK, :] = kd_s[:, T:T + ATT_BLOCK, :]
    vd_s[:, 0:ATT_BLOCK, :] = vd_s[:, T:T + ATT_BLOCK, :]

    for rows in subs:
        out_proj(rows)
    for rows in subs:
        _ffn_act(h2_s[rows, :], wg_ref, wu_ref, act_s, rows)
    for rows in subs:
        y_ref[0, rows, :] = x1_s[rows, :] + _dot(act_s[rows, :], wd_ref[...])


def _const_spec(shape):
    nd = len(shape)
    return pl.BlockSpec(shape, lambda *_: (0,) * nd, pipeline_mode=pl.Buffered(1))


def _prompt_call(x, sinks, weights):
    B, S, _ = x.shape
    T = SEQ_TILE
    in_specs = [pl.BlockSpec(memory_space=pltpu.SMEM),
                pl.BlockSpec((1, T, D_MODEL), lambda b, j: (b, j, 0))]
    in_specs += [_const_spec(w.shape) for w in weights]
    out_shape = (jax.ShapeDtypeStruct((B, S, D_MODEL), F32),
                 jax.ShapeDtypeStruct((B, KV_WIDTH, ATT_BLOCK), F32),
                 jax.ShapeDtypeStruct((B, KV_WIDTH, ATT_BLOCK), F32),
                 jax.ShapeDtypeStruct((B, POOL_BUF, POOL_WIDTH), F32))
    out_specs = (pl.BlockSpec((1, T, D_MODEL), lambda b, j: (b, j, 0)),
                 pl.BlockSpec((1, KV_WIDTH, ATT_BLOCK), lambda b, j: (b, 0, 0)),
                 pl.BlockSpec((1, KV_WIDTH, ATT_BLOCK), lambda b, j: (b, 0, 0)),
                 pl.BlockSpec((1, POOL_BUF, POOL_WIDTH), lambda b, j: (b, 0, 0)))
    scratch = [pltpu.VMEM((T, D_MODEL), BF16),
               pltpu.VMEM((PREFIX + T, POOL_WIDTH), F32),
               pltpu.VMEM((T, Q_WIDTH), BF16),
               pltpu.VMEM((N_KV_HEADS, ATT_BLOCK + T, KV_WIDTH), BF16),
               pltpu.VMEM((N_KV_HEADS, ATT_BLOCK + T, KV_WIDTH), BF16),
               pltpu.VMEM((2, N_HEADS * ATT_BLOCK, 2 * ATT_BLOCK), F32),
               pltpu.VMEM((2, N_HEADS * ATT_BLOCK, 2 * ATT_BLOCK), BF16),
               pltpu.VMEM((T, POOL_WIDTH), BF16),
               pltpu.VMEM((T, Q_WIDTH), BF16),
               pltpu.VMEM((T, D_MODEL), F32),
               pltpu.VMEM((T, D_MODEL), F32),
               pltpu.VMEM((T, D_MODEL), BF16),
               pltpu.VMEM((T, D_MODEL), F32),
               pltpu.VMEM((T, D_MODEL), BF16),
               pltpu.VMEM((T, D_FF), BF16)]
    return pl.pallas_call(
        _prompt_kernel,
        grid=(B, S // T),
        in_specs=in_specs,
        out_specs=out_specs,
        out_shape=out_shape,
        scratch_shapes=scratch,
        compiler_params=pltpu.CompilerParams(
            dimension_semantics=("arbitrary", "arbitrary"),
            vmem_limit_bytes=VMEM_LIMIT_BYTES),
        name="prompt_layer",
    )(sinks, x, *weights)


BIG_WEIGHTS = (("w_in", D_MODEL, OFF_GB + D_MODEL), ("mix", len(POOL_WINDOWS) * POOL_GROUP_WIDTH, POOL_GROUP_WIDTH),
               ("wpp", POOL_WIDTH, D_MODEL), ("wap", Q_WIDTH, D_MODEL), ("wout", D_MODEL, D_MODEL),
               ("wg", D_MODEL, D_FF), ("wu", D_MODEL, D_FF), ("wd", D_FF, D_MODEL))
W_MIX, W_PP = 1, 2
W_FOLD = len(BIG_WEIGHTS)
HANDED_ON = tuple(w for w in range(len(BIG_WEIGHTS)) if w not in (W_MIX, W_PP)) + (W_FOLD,)
CONVERT_ROWS = 128
STAGE_SLOTS = 4
STAGE_COLS = max(cols for _, _, cols in BIG_WEIGHTS)


def _weight_chunks():
    chunks = []
    for w, (_, nrows, cols) in enumerate(BIG_WEIGHTS):
        side_by_side = max(1, STAGE_COLS // cols)
        blocks = [(r0, min(CONVERT_ROWS, nrows - r0)) for r0 in range(0, nrows, CONVERT_ROWS)]
        for first in range(0, len(blocks), side_by_side):
            group = blocks[first:first + side_by_side]
            chunks.append((w, tuple((r0, rows, k * cols) for k, (r0, rows) in enumerate(group))))
    return tuple(chunks)


WEIGHT_CHUNKS = _weight_chunks()


def _chunks_per_step(nsteps):
    first = [n for n, (w, _) in enumerate(WEIGHT_CHUNKS) if w == 0]
    rest = [n for n, (w, _) in enumerate(WEIGHT_CHUNKS) if w != 0]
    size = lambda n: sum(rows for _, rows, _ in WEIGHT_CHUNKS[n][1]) * BIG_WEIGHTS[WEIGHT_CHUNKS[n][0]][2]
    total = sum(size(n) for n in rest)
    steps = [first] + [[] for _ in range(nsteps - 1)]
    done = 0
    for n in rest:
        steps[1 + min(nsteps - 2, done * (nsteps - 1) // total)].append(n)
        done += size(n)
    return steps


def _sample_kernel(*refs):
    nw = len(BIG_WEIGHTS)
    n_in = 10
    (x_ref, ck_ref, cv_ref, sp_ref, sinkc_ref, n1_ref, qn_ref, kn_ref, pscale_ref, n2_ref) = refs[:n_in]
    w_hbm = refs[n_in:n_in + nw]
    pos = n_in + nw
    y_ref, knew_ref, vnew_ref, pnew_ref = refs[pos:pos + 4]
    pos += 4
    w_bf_hbm = dict(zip(HANDED_ON, refs[pos:pos + len(HANDED_ON)]))
    pos += len(HANDED_ON)
    w_v = refs[pos:pos + nw + 1]
    pos += nw + 1
    (stage, stage_sem, out_sem, mixf_s,
     u_s, kt_s, vt_s, q8_s, k8_s, v8_s, o8_s, ypre_s, sga_s, sgb_s, merged_s, act_s) = refs[pos:]
    w_in_ref, _, wpp_ref, wap_ref, wout_ref, wg_ref, wu_ref, wd_ref, wfold_ref = w_v

    SB = SAMPLE_BLOCK
    N = x_ref.shape[0]
    nsteps = N // SB
    i = pl.program_id(0)
    lane = lax.broadcasted_iota(jnp.int32, (N, LANES), 1)
    all_rows = slice(0, N)

    def chunk_copies(n):
        w, pieces = WEIGHT_CHUNKS[n]
        cols = BIG_WEIGHTS[w][2]
        slot = n % STAGE_SLOTS
        return [pltpu.make_async_copy(w_hbm[w].at[pl.ds(r0, rows), :],
                                      stage.at[slot, pl.ds(0, rows), pl.ds(c0, cols)], stage_sem.at[slot])
                for r0, rows, c0 in pieces]

    def writeback_copy(w):
        return pltpu.make_async_copy(w_v[w], w_bf_hbm[w], out_sem.at[w])

    def convert(n):
        w, pieces = WEIGHT_CHUNKS[n]
        cols = BIG_WEIGHTS[w][2]
        for r0, rows, c0 in pieces:
            staged = stage[n % STAGE_SLOTS, 0:rows, c0:c0 + cols]
            if w == W_MIX:
                mixf_s[r0:r0 + rows, :] = staged
            else:
                w_v[w][r0:r0 + rows, :] = staged.astype(BF16)
        last_r0, last_rows, _ = pieces[-1]
        if last_r0 + last_rows == BIG_WEIGHTS[w][1] and w in HANDED_ON:
            writeback_copy(w).start()

    ahead = STAGE_SLOTS - 1
    for step, chunk_ids in enumerate(_chunks_per_step(nsteps)):
        @pl.when(i == step)
        def _(chunk_ids=chunk_ids):
            for n in chunk_ids:
                if n == 0:
                    for m in range(ahead):
                        for copy in chunk_copies(m):
                            copy.start()
                if n + ahead < len(WEIGHT_CHUNKS):
                    for copy in chunk_copies(n + ahead):
                        copy.start()
                for copy in chunk_copies(n):
                    copy.wait()
                convert(n)

    @pl.when(i == 0)
    def _():
        x = x_ref[...]
        h = ((x * _rms_scale(x)) * n1_ref[...]).astype(BF16)
        u_s[...] = _dot(h, w_in_ref[:, OFF_U:OFF_U + POOL_WIDTH])
        for gc in range(N_GATE_CHUNKS):
            _gate_chunk(h, gc, w_in_ref, sga_s, sgb_s, all_rows)
        q = _head_norm(_dot(h, w_in_ref[:, OFF_Q:OFF_Q + Q_WIDTH]), qn_ref[...])
        kv = _dot(h, w_in_ref[:, OFF_K:OFF_K + 2 * KV_WIDTH])
        kn = _head_norm(kv[:, 0:KV_WIDTH], kn_ref[...])
        v = kv[:, KV_WIDTH:]
        kt_s[...] = kn.T
        vt_s[...] = v.T
        for head in range(N_HEADS):
            k8_s[pl.ds(head, N, stride=N_HEADS), :] = kn
            v8_s[pl.ds(head, N, stride=N_HEADS), :] = v
        for p in range(Q_COLS):
            qcol = q[:, p * LANES:(p + 1) * LANES]
            qswap = pltpu.roll(qcol, HEAD_DIM, 1)
            for e in range(HEADS_PER_COL):
                head = p * HEADS_PER_COL + e
                c = _kv_head(head)
                src = qcol if e == c else qswap
                keep = (lane < HEAD_DIM) if c == 0 else (lane >= HEAD_DIM)
                q8_s[pl.ds(head, N, stride=N_HEADS), :] = jnp.where(keep, src, 0.0)

    base = i * SB
    rows = pl.ds(pl.multiple_of(base, SB), SB)

    unew = u_s[rows, :]
    for g, w in enumerate(POOL_WINDOWS):
        cols = slice(g * POOL_GROUP_WIDTH, (g + 1) * POOL_GROUP_WIDTH)
        s = unew[:, cols]
        for r in range(POOL_BUF - (w - 1), POOL_BUF):
            s = s + sp_ref[r, :, cols]
        ypre_s[rows, cols] = (s * (1.0 / w) - unew[:, cols]).astype(BF16)
    pnew_ref[0:POOL_BUF - 1] = sp_ref[1:POOL_BUF]
    pnew_ref[POOL_BUF - 1] = unew

    hrows = pl.ds(pl.multiple_of(base * N_HEADS, SB * N_HEADS), SB * N_HEADS)
    lhs = q8_s[hrows, :]
    sc = jnp.concatenate(
        [_dot(lhs[bb * N_HEADS:(bb + 1) * N_HEADS, :].astype(BF16), ck_ref[bb].astype(BF16)) for bb in range(SB)],
        axis=0)
    sc_self = jnp.sum(lhs * k8_s[hrows, :], axis=-1, keepdims=True)
    sink = sinkc_ref[:, 0:1]
    m = jnp.maximum(jnp.maximum(jnp.max(sc, axis=-1, keepdims=True), sc_self), sink)
    p = jnp.exp(sc - m)
    p_self = jnp.exp(sc_self - m)
    denom = jnp.sum(p, axis=-1, keepdims=True) + p_self + jnp.exp(sink - m)
    inv = 1.0 / denom
    pn = p * inv
    o = jnp.concatenate(
        [_dot_nt(pn[bb * N_HEADS:(bb + 1) * N_HEADS, :].astype(BF16), cv_ref[bb].astype(BF16)) for bb in range(SB)],
        axis=0)
    o8_s[hrows, :] = o + (p_self * inv) * v8_s[hrows, :]

    lane_c = lax.broadcasted_iota(jnp.int32, (KV_WIDTH, WINDOW), 1)
    kt_all = kt_s[...]
    vt_all = vt_s[...]
    for bb in range(SB):
        put = (WINDOW - 1) - (base + bb)
        knew_ref[bb] = jnp.where(lane_c == WINDOW - 1, pltpu.roll(kt_all, put, 1),
                                 pltpu.roll(ck_ref[bb], WINDOW - 1, 1))
        vnew_ref[bb] = jnp.where(lane_c == WINDOW - 1, pltpu.roll(vt_all, put, 1),
                                 pltpu.roll(cv_ref[bb], WINDOW - 1, 1))

    @pl.when(i == nsteps - 1)
    def _():
        for g in range(len(POOL_WINDOWS)):
            grp = slice(g * POOL_GROUP_WIDTH, (g + 1) * POOL_GROUP_WIDTH)
            wfold_ref[grp, :] = _dot((mixf_s[grp, :] * pscale_ref[:, grp]).astype(BF16), wpp_ref[grp, :]).astype(BF16)
        writeback_copy(W_FOLD).start()
        ycols = []
        for p in range(Q_COLS):
            halves = []
            for e in range(HEADS_PER_COL):
                head = p * HEADS_PER_COL + e
                oh = o8_s[pl.ds(head, N, stride=N_HEADS), :]
                halves.append(oh if e == _kv_head(head) else pltpu.roll(oh, HEAD_DIM, 1))
            ycols.append(jnp.where(lane < HEAD_DIM, halves[0], halves[1]).astype(BF16))
        yattn = jnp.concatenate(ycols, axis=1)
        _merge(ypre_s[...], yattn, sga_s, sgb_s, wfold_ref, wap_ref, merged_s, all_rows)
        x1 = x_ref[...] + _dot(merged_s[...], wout_ref[...])
        h2 = ((x1 * _rms_scale(x1)) * n2_ref[...]).astype(BF16)
        _ffn_act(h2, wg_ref, wu_ref, act_s, all_rows)
        y_ref[...] = x1 + _dot(act_s[...], wd_ref[...])
        for w in HANDED_ON:
            writeback_copy(w).wait()


def _sample_call(x, ck, cv, sp, sinkc, small, big):
    N = x.shape[0]
    SB = SAMPLE_BLOCK
    W = WINDOW
    assert tuple(b.shape for b in big) == tuple((r, c) for _, r, c in BIG_WEIGHTS)
    consts = (sinkc,) + tuple(small)
    in_specs = [_const_spec(x.shape),
                pl.BlockSpec((SB, KV_WIDTH, W), lambda i: (i, 0, 0)),
                pl.BlockSpec((SB, KV_WIDTH, W), lambda i: (i, 0, 0)),
                pl.BlockSpec((POOL_BUF, SB, POOL_WIDTH), lambda i: (0, i, 0))]
    in_specs += [_const_spec(w.shape) for w in consts]
    in_specs += [pl.BlockSpec(memory_space=pl.ANY) for _ in big]
    out_shape = (jax.ShapeDtypeStruct((N, D_MODEL), F32),
                 jax.ShapeDtypeStruct((N, KV_WIDTH, W), F32),
                 jax.ShapeDtypeStruct((N, KV_WIDTH, W), F32),
                 jax.ShapeDtypeStruct((POOL_BUF, N, POOL_WIDTH), F32))
    bf_shapes = [b.shape for b in big] + [(POOL_WIDTH, D_MODEL)]
    out_shape += tuple(jax.ShapeDtypeStruct(bf_shapes[w], BF16) for w in HANDED_ON)
    out_specs = (pl.BlockSpec((N, D_MODEL), lambda i: (0, 0)),
                 pl.BlockSpec((SB, KV_WIDTH, W), lambda i: (i, 0, 0)),
                 pl.BlockSpec((SB, KV_WIDTH, W), lambda i: (i, 0, 0)),
                 pl.BlockSpec((POOL_BUF, SB, POOL_WIDTH), lambda i: (0, i, 0)))
    out_specs += tuple(pl.BlockSpec(memory_space=pl.ANY) for _ in HANDED_ON)
    scratch = [pltpu.VMEM(shape, BF16) for shape in bf_shapes]
    scratch += [pltpu.VMEM((STAGE_SLOTS, CONVERT_ROWS, STAGE_COLS), F32),
                pltpu.SemaphoreType.DMA((STAGE_SLOTS,)),
                pltpu.SemaphoreType.DMA((len(bf_shapes),)),
                pltpu.VMEM(big[W_MIX].shape, F32),
                pltpu.VMEM((N, POOL_WIDTH), F32),
               pltpu.VMEM((KV_WIDTH, N), F32),
               pltpu.VMEM((KV_WIDTH, N), F32),
               pltpu.VMEM((N * N_HEADS, LANES), F32),
               pltpu.VMEM((N * N_HEADS, LANES), F32),
               pltpu.VMEM((N * N_HEADS, LANES), F32),
               pltpu.VMEM((N * N_HEADS, LANES), F32),
               pltpu.VMEM((N, POOL_WIDTH), BF16),
               pltpu.VMEM((N, D_MODEL), F32),
               pltpu.VMEM((N, D_MODEL), F32),
               pltpu.VMEM((N, D_MODEL), BF16),
               pltpu.VMEM((N, D_FF), BF16)]
    return pl.pallas_call(
        _sample_kernel,
        grid=(N // SB,),
        in_specs=in_specs,
        out_specs=out_specs,
        out_shape=out_shape,
        scratch_shapes=scratch,
        compiler_params=pltpu.CompilerParams(
            dimension_semantics=("arbitrary",),
            vmem_limit_bytes=VMEM_LIMIT_BYTES),
        name="sample_layer",
    )(x, ck, cv, sp, *consts, *big)


def _cache_to_device_order(c):
    n, w = c.shape[0], c.shape[1]
    return jnp.transpose(c, (0, 2, 3, 1)).reshape(n, KV_WIDTH, w)


def _cache_from_device_order(c):
    n, _, w = c.shape
    return jnp.transpose(c.reshape(n, N_KV_HEADS, HEAD_DIM, w), (0, 3, 1, 2))[None]


@jax.jit
def _forward(x_prompt, x_sample, cache_k, cache_v, state_pool, norm1, w_in, q_norm, k_norm, sinks,
             pool_mix_w, pool_scale, w_pool_proj, w_attn_proj, w_out, norm2, w_gate, w_up, w_down):
    depth = w_in.shape[0]
    assert depth == 1, "single-layer trunk"
    l = 0
    N = x_sample.shape[0]
    assert cache_k.shape[2] == WINDOW and x_sample.shape[1] == 1

    qn = (jnp.tile(q_norm[l], N_HEADS) * (HEAD_DIM ** -0.5)).reshape(1, Q_WIDTH)
    kn = jnp.tile(k_norm[l], N_KV_HEADS).reshape(1, KV_WIDTH)
    n1 = norm1[l].reshape(1, D_MODEL)
    n2 = norm2[l].reshape(1, D_MODEL)
    pscale = pool_scale[l].reshape(1, POOL_WIDTH)
    big = (w_in[l], pool_mix_w[l].reshape(len(POOL_WINDOWS) * POOL_GROUP_WIDTH, POOL_GROUP_WIDTH),
           w_pool_proj[l], w_attn_proj[l], w_out[l], w_gate[l], w_up[l], w_down[l])

    sink_cols = jnp.broadcast_to(jnp.tile(sinks[l], SAMPLE_BLOCK)[:, None], (SAMPLE_BLOCK * N_HEADS, LANES))
    y_s, k_s, v_s, pool_s, w_in_b, wap_b, wout_b, wg_b, wu_b, wd_b, wfold_b = _sample_call(
        x_sample.reshape(N, D_MODEL),
        _cache_to_device_order(cache_k[l]), _cache_to_device_order(cache_v[l]),
        jnp.transpose(state_pool[l], (1, 0, 2)), sink_cols, (n1, qn, kn, pscale, n2), big)

    weights = (n1, w_in_b, qn, kn, wfold_b, wap_b, wout_b, n2, wg_b, wu_b, wd_b)
    y_p, k_p, v_p, pool_p = _prompt_call(x_prompt, sinks[l], weights)

    return (y_p, y_s.reshape(N, 1, D_MODEL),
            _cache_from_device_order(k_p), _cache_from_device_order(v_p), pool_p[None],
            _cache_from_device_order(k_s), _cache_from_device_order(v_s),
            jnp.transpose(pool_s, (1, 0, 2))[None])


def kernel(x_prompt, x_sample, cache_k, cache_v, state_pool, norm1, w_in, q_norm, k_norm, sinks, pool_mix_w,
           pool_scale, w_pool_proj, w_attn_proj, w_out, norm2, w_gate, w_up, w_down):
    return _forward(x_prompt, x_sample, cache_k, cache_v, state_pool, norm1, w_in, q_norm, k_norm, sinks,
                    pool_mix_w, pool_scale, w_pool_proj, w_attn_proj, w_out, norm2, w_gate, w_up, w_down)
```

```python
import jax
import jax.numpy as jnp
from jax import lax
from jax.experimental import pallas as pl
from jax.experimental.pallas import tpu as pltpu

D_MODEL = 1024
POOL_WINDOWS = (2, 4, 8, 16)
POOL_GROUP_WIDTH = 128
POOL_WIDTH = 512
POOL_BUF = 15
N_HEADS = 8
N_KV_HEADS = 2
HEAD_DIM = 64
GROUP = N_HEADS // N_KV_HEADS
Q_WIDTH = 512
KV_WIDTH = 128
WINDOW = 128
D_FF = 2816
EPS = 1e-6
NEG = -1e30

LANES = 128
ATT_BLOCK = WINDOW
SEQ_TILE = 512
SUB_TILE = 256
FF_CHUNK = 512
GATE_CHUNK = 256
SAMPLE_BLOCK = 16
PREFIX = 16
VMEM_LIMIT_BYTES = 58 * 1024 * 1024
Q_COLS = Q_WIDTH // LANES
HEADS_PER_COL = LANES // HEAD_DIM
N_GATE_CHUNKS = D_MODEL // GATE_CHUNK

OFF_U = 0
OFF_Q = OFF_U + POOL_WIDTH
OFF_K = OFF_Q + Q_WIDTH
OFF_V = OFF_K + KV_WIDTH
OFF_GA = OFF_V + KV_WIDTH
OFF_GB = OFF_GA + D_MODEL

BF16 = jnp.bfloat16
F32 = jnp.float32

ROW_N1, ROW_N2, ROW_QN, ROW_KN, ROW_PSCALE, PARAM_ROWS = 0, 1, 2, 3, 4, 8


def _param(params_ref, row, width):
    return params_ref[row:row + 1, 0:width]


def _dot(a, b):
    return jnp.dot(a, b, preferred_element_type=F32)


def _dot_nt(a, b):
    return lax.dot_general(a, b, (((1,), (1,)), ((), ())), preferred_element_type=F32)


def _rms_scale(x):
    return lax.rsqrt(jnp.mean(x * x, axis=-1, keepdims=True) + EPS)


def _head_norm(x, gain):
    sq = x * x
    low = lax.broadcasted_iota(jnp.int32, (x.shape[0], LANES), 1) < HEAD_DIM
    cols = []
    for c in range(0, x.shape[1], LANES):
        blk = sq[:, c:c + LANES]
        ss_lo = jnp.sum(jnp.where(low, blk, 0.0), axis=-1, keepdims=True)
        ss_hi = jnp.sum(jnp.where(low, 0.0, blk), axis=-1, keepdims=True)
        cols.append(jnp.where(low, ss_lo, ss_hi))
    ss = jnp.concatenate(cols, axis=1) if len(cols) > 1 else cols[0]
    return (x * lax.rsqrt(ss * (1.0 / HEAD_DIM) + EPS)) * gain


def _kv_head(head):
    return head // GROUP


def _ffn_act(h2, wg_ref, wu_ref, act_ref, rows):
    for start in range(0, D_FF, FF_CHUNK):
        width = min(FF_CHUNK, D_FF - start)
        g = _dot(h2, wg_ref[:, start:start + width])
        u = _dot(h2, wu_ref[:, start:start + width])
        act_ref[rows, start:start + width] = (g * jax.nn.sigmoid(g) * u).astype(BF16)


def _gate_chunk(h, c, w_in_ref, sga_ref, sgb_ref, rows):
    lo = c * GATE_CHUNK
    sga_ref[rows, lo:lo + GATE_CHUNK] = jax.nn.sigmoid(_dot(h, w_in_ref[:, OFF_GA + lo:OFF_GA + lo + GATE_CHUNK]))
    sgb_ref[rows, lo:lo + GATE_CHUNK] = jax.nn.sigmoid(_dot(h, w_in_ref[:, OFF_GB + lo:OFF_GB + lo + GATE_CHUNK]))


def _merge(ypool, yattn, sga_ref, sgb_ref, wpp_ref, wap_ref, merged_ref, rows):
    for c in range(N_GATE_CHUNKS):
        lo = c * GATE_CHUNK
        pp = _dot(ypool, wpp_ref[:, lo:lo + GATE_CHUNK])
        ap = _dot(yattn, wap_ref[:, lo:lo + GATE_CHUNK])
        merged = sga_ref[rows, lo:lo + GATE_CHUNK] * pp + sgb_ref[rows, lo:lo + GATE_CHUNK] * ap
        merged_ref[rows, lo:lo + GATE_CHUNK] = merged.astype(BF16)


def _prompt_kernel(sinks_ref, x_ref, params_ref, w_in_ref, wfold_ref, wap_ref, wout_ref, wg_ref, wu_ref, wd_ref,
                   y_ref, knew_ref, vnew_ref, pnew_ref,
                   h_s, u_s, q_s, kd_s, vd_s, s_s, p_s, ypool_s, yattn_s, sga_s, sgb_s, merged_s, x1_s, h2_s, act_s):
    T = SEQ_TILE
    R = SUB_TILE
    subs = [slice(r0, r0 + R) for r0 in range(0, T, R)]
    n_blocks = T // ATT_BLOCK
    j = pl.program_id(1)
    rows_per_kv = GROUP * ATT_BLOCK

    @pl.when(j == 0)
    def _():
        u_s[0:PREFIX, :] = jnp.zeros((PREFIX, POOL_WIDTH), F32)
        kd_s[:, 0:ATT_BLOCK, :] = jnp.zeros((N_KV_HEADS, ATT_BLOCK, KV_WIDTH), BF16)
        vd_s[:, 0:ATT_BLOCK, :] = jnp.zeros((N_KV_HEADS, ATT_BLOCK, KV_WIDTH), BF16)

    lane_r = lax.broadcasted_iota(jnp.int32, (R, LANES), 1)
    lane = lax.broadcasted_iota(jnp.int32, (ATT_BLOCK, LANES), 1)
    row = lax.broadcasted_iota(jnp.int32, (ATT_BLOCK, 2 * ATT_BLOCK), 0)
    col = lax.broadcasted_iota(jnp.int32, (ATT_BLOCK, 2 * ATT_BLOCK), 1)
    band = (col >= row) & (col <= row + WINDOW)
    first_lo = jnp.where(j == 0, ATT_BLOCK, 0)

    def norm1(rows):
        x = x_ref[0, rows, :]
        h_s[rows, :] = ((x * _rms_scale(x)) * _param(params_ref, ROW_N1, D_MODEL)).astype(BF16)

    def in_proj(rows):
        h = h_s[rows, :]
        u_s[PREFIX + rows.start:PREFIX + rows.stop, :] = _dot(h, w_in_ref[:, OFF_U:OFF_U + POOL_WIDTH])
        q = _dot(h, w_in_ref[:, OFF_Q:OFF_Q + Q_WIDTH])
        q_s[rows, :] = _head_norm(q, _param(params_ref, ROW_QN, Q_WIDTH)).astype(BF16)
        kv = _dot(h, w_in_ref[:, OFF_K:OFF_K + 2 * KV_WIDTH])
        kn = _head_norm(kv[:, 0:KV_WIDTH], _param(params_ref, ROW_KN, KV_WIDTH))
        v = kv[:, KV_WIDTH:]
        dst_rows = slice(ATT_BLOCK + rows.start, ATT_BLOCK + rows.stop)
        for src, dst in ((kn, kd_s), (v, vd_s)):
            swapped = pltpu.roll(src, HEAD_DIM, 1)
            dst[0, dst_rows, :] = jnp.where(lane_r < HEAD_DIM, src, swapped).astype(BF16)
            dst[1, dst_rows, :] = jnp.where(lane_r < HEAD_DIM, swapped, src).astype(BF16)
        if rows.stop == T:
            knew_ref[0] = kn[R - ATT_BLOCK:, :].T
            vnew_ref[0] = v[R - ATT_BLOCK:, :].T
            pnew_ref[0] = u_s[PREFIX + T - POOL_BUF:PREFIX + T, :]

    def pool(rows):
        pos1 = j * T + rows.start + lax.broadcasted_iota(jnp.int32, (R, 1), 0) + 1
        for g, w in enumerate(POOL_WINDOWS):
            cols = slice(g * POOL_GROUP_WIDTH, (g + 1) * POOL_GROUP_WIDTH)
            a = u_s[rows.start:rows.stop + PREFIX, cols]
            s = a
            shift = 1
            while shift < w:
                s = s + pltpu.roll(s, shift, 0)
                shift *= 2
            inv_cnt = 1.0 / jnp.minimum(pos1, w).astype(F32)
            ypool_s[rows, cols] = (s[PREFIX:, :] * inv_cnt - a[PREFIX:, :]).astype(BF16)

    def gates(rows, chunks):
        h = h_s[rows, :]
        for c in chunks:
            _gate_chunk(h, c, w_in_ref, sga_s, sgb_s, rows)

    def scores(b):
        r0 = b * ATT_BLOCK
        for c in range(N_KV_HEADS):
            parts = []
            for p in range(c * Q_COLS // N_KV_HEADS, (c + 1) * Q_COLS // N_KV_HEADS):
                qcol = q_s[r0:r0 + ATT_BLOCK, p * LANES:(p + 1) * LANES]
                parts.append(jnp.where(lane < HEAD_DIM, qcol, jnp.zeros_like(qcol)))
                parts.append(jnp.where(lane >= HEAD_DIM, qcol, jnp.zeros_like(qcol)))
            s_s[b % 2, c * rows_per_kv:(c + 1) * rows_per_kv, :] = _dot_nt(
                jnp.concatenate(parts, axis=0), kd_s[c, r0:r0 + 2 * ATT_BLOCK, :])

    def softmax(b):
        valid = band & (col >= first_lo) if b == 0 else band
        for head in range(N_HEADS):
            sh = jnp.where(valid, s_s[b % 2, head * ATT_BLOCK:(head + 1) * ATT_BLOCK, :], NEG)
            sink = sinks_ref[head]
            m = jnp.maximum(jnp.max(sh, axis=-1, keepdims=True), sink)
            p = jnp.exp(sh - m)
            denom = jnp.sum(p, axis=-1, keepdims=True) + jnp.exp(sink - m)
            p_s[b % 2, head * ATT_BLOCK:(head + 1) * ATT_BLOCK, :] = (p * (1.0 / denom)).astype(BF16)

    def weighted_values(b):
        r0 = b * ATT_BLOCK
        for c in range(N_KV_HEADS):
            o = _dot(p_s[b % 2, c * rows_per_kv:(c + 1) * rows_per_kv, :], vd_s[c, r0:r0 + 2 * ATT_BLOCK, :])
            for pp in range(Q_COLS // N_KV_HEADS):
                p_col = c * Q_COLS // N_KV_HEADS + pp
                o_lo = o[(2 * pp) * ATT_BLOCK:(2 * pp + 1) * ATT_BLOCK, :]
                o_hi = o[(2 * pp + 1) * ATT_BLOCK:(2 * pp + 2) * ATT_BLOCK, :]
                yattn_s[r0:r0 + ATT_BLOCK, p_col * LANES:(p_col + 1) * LANES] = (
                    jnp.where(lane < HEAD_DIM, o_lo, o_hi).astype(BF16))

    def out_proj(rows):
        _merge(ypool_s[rows, :], yattn_s[rows, :], sga_s, sgb_s, wfold_ref, wap_ref, merged_s, rows)
        x1 = x_ref[0, rows, :] + _dot(merged_s[rows, :], wout_ref[...])
        x1_s[rows, :] = x1
        h2_s[rows, :] = ((x1 * _rms_scale(x1)) * _param(params_ref, ROW_N2, D_MODEL)).astype(BF16)

    for rows in subs:
        norm1(rows)
    for rows in subs:
        in_proj(rows)
    pool(subs[0])
    gates(subs[0], range(N_GATE_CHUNKS))
    for rows in subs[1:]:
        pool(rows)
    later_gates = [(rows, c) for rows in subs[1:] for c in range(N_GATE_CHUNKS)]
    per_block = -(-len(later_gates) // n_blocks)
    scores(0)
    for b in range(n_blocks):
        if b + 1 < n_blocks:
            scores(b + 1)
        for rows, c in later_gates[b * per_block:(b + 1) * per_block]:
            gates(rows, [c])
        softmax(b)
        weighted_values(b)

    u_s[0:PREFIX, :] = u_s[T:T + PREFIX, :]
    kd_s[:, 0:ATT_BLOCK, :] = kd_s[:, T:T + ATT_BLOCK, :]
    vd_s[:, 0:ATT_BLOCK, :] = vd_s[:, T:T + ATT_BLOCK, :]

    for rows in subs:
        out_proj(rows)
    for rows in subs:
        _ffn_act(h2_s[rows, :], wg_ref, wu_ref, act_s, rows)
    for rows in subs:
        y_ref[0, rows, :] = x1_s[rows, :] + _dot(act_s[rows, :], wd_ref[...])


def _const_spec(shape):
    nd = len(shape)
    return pl.BlockSpec(shape, lambda *_: (0,) * nd, pipeline_mode=pl.Buffered(1))


def _prompt_call(x, sinks, weights):
    B, S, _ = x.shape
    T = SEQ_TILE
    in_specs = [pl.BlockSpec(memory_space=pltpu.SMEM),
                pl.BlockSpec((1, T, D_MODEL), lambda b, j: (b, j, 0))]
    in_specs += [_const_spec(w.shape) for w in weights]
    out_shape = (jax.ShapeDtypeStruct((B, S, D_MODEL), F32),
                 jax.ShapeDtypeStruct((B, KV_WIDTH, ATT_BLOCK), F32),
                 jax.ShapeDtypeStruct((B, KV_WIDTH, ATT_BLOCK), F32),
                 jax.ShapeDtypeStruct((B, POOL_BUF, POOL_WIDTH), F32))
    out_specs = (pl.BlockSpec((1, T, D_MODEL), lambda b, j: (b, j, 0)),
                 pl.BlockSpec((1, KV_WIDTH, ATT_BLOCK), lambda b, j: (b, 0, 0)),
                 pl.BlockSpec((1, KV_WIDTH, ATT_BLOCK), lambda b, j: (b, 0, 0)),
                 pl.BlockSpec((1, POOL_BUF, POOL_WIDTH), lambda b, j: (b, 0, 0)))
    scratch = [pltpu.VMEM((T, D_MODEL), BF16),
               pltpu.VMEM((PREFIX + T, POOL_WIDTH), F32),
               pltpu.VMEM((T, Q_WIDTH), BF16),
               pltpu.VMEM((N_KV_HEADS, ATT_BLOCK + T, KV_WIDTH), BF16),
               pltpu.VMEM((N_KV_HEADS, ATT_BLOCK + T, KV_WIDTH), BF16),
               pltpu.VMEM((2, N_HEADS * ATT_BLOCK, 2 * ATT_BLOCK), F32),
               pltpu.VMEM((2, N_HEADS * ATT_BLOCK, 2 * ATT_BLOCK), BF16),
               pltpu.VMEM((T, POOL_WIDTH), BF16),
               pltpu.VMEM((T, Q_WIDTH), BF16),
               pltpu.VMEM((T, D_MODEL), F32),
               pltpu.VMEM((T, D_MODEL), F32),
               pltpu.VMEM((T, D_MODEL), BF16),
               pltpu.VMEM((T, D_MODEL), F32),
               pltpu.VMEM((T, D_MODEL), BF16),
               pltpu.VMEM((T, D_FF), BF16)]
    return pl.pallas_call(
        _prompt_kernel,
        grid=(B, S // T),
        in_specs=in_specs,
        out_specs=out_specs,
        out_shape=out_shape,
        scratch_shapes=scratch,
        compiler_params=pltpu.CompilerParams(
            dimension_semantics=("arbitrary", "arbitrary"),
            vmem_limit_bytes=VMEM_LIMIT_BYTES),
        name="prompt_layer",
    )(sinks, x, *weights)


BIG_WEIGHTS = (("w_in", D_MODEL, OFF_GB + D_MODEL), ("mix", len(POOL_WINDOWS) * POOL_GROUP_WIDTH, POOL_GROUP_WIDTH),
               ("wpp", POOL_WIDTH, D_MODEL), ("wap", Q_WIDTH, D_MODEL), ("wout", D_MODEL, D_MODEL),
               ("wg", D_MODEL, D_FF), ("wu", D_MODEL, D_FF), ("wd", D_FF, D_MODEL))
W_MIX, W_PP = 1, 2
W_FOLD = len(BIG_WEIGHTS)
HANDED_ON = tuple(w for w in range(len(BIG_WEIGHTS)) if w not in (W_MIX, W_PP)) + (W_FOLD,)
CONVERT_ROWS = 128
STAGE_SLOTS = 4
STAGE_COLS = max(cols for _, _, cols in BIG_WEIGHTS)


def _weight_chunks():
    chunks = []
    for w, (_, nrows, cols) in enumerate(BIG_WEIGHTS):
        side_by_side = max(1, STAGE_COLS // cols)
        blocks = [(r0, min(CONVERT_ROWS, nrows - r0)) for r0 in range(0, nrows, CONVERT_ROWS)]
        for first in range(0, len(blocks), side_by_side):
            group = blocks[first:first + side_by_side]
            chunks.append((w, tuple((r0, rows, k * cols) for k, (r0, rows) in enumerate(group))))
    return tuple(chunks)


WEIGHT_CHUNKS = _weight_chunks()


def _chunks_per_step(nsteps):
    first = [n for n, (w, _) in enumerate(WEIGHT_CHUNKS) if w == 0]
    rest = [n for n, (w, _) in enumerate(WEIGHT_CHUNKS) if w != 0]
    size = lambda n: sum(rows for _, rows, _ in WEIGHT_CHUNKS[n][1]) * BIG_WEIGHTS[WEIGHT_CHUNKS[n][0]][2]
    total = sum(size(n) for n in rest)
    steps = [first] + [[] for _ in range(nsteps - 1)]
    done = 0
    for n in rest:
        steps[1 + min(nsteps - 2, done * (nsteps - 1) // total)].append(n)
        done += size(n)
    return steps


def _sample_kernel(*refs):
    nw = len(BIG_WEIGHTS)
    n_in = 6
    (x_ref, ck_ref, cv_ref, sp_ref, sinkc_ref, params_ref) = refs[:n_in]
    w_hbm = refs[n_in:n_in + nw]
    pos = n_in + nw
    y_ref, knew_ref, vnew_ref, pnew_ref = refs[pos:pos + 4]
    pos += 4
    w_bf_hbm = dict(zip(HANDED_ON, refs[pos:pos + len(HANDED_ON)]))
    pos += len(HANDED_ON)
    w_v = refs[pos:pos + nw + 1]
    pos += nw + 1
    (stage, stage_sem, out_sem, mixf_s,
     u_s, kt_s, vt_s, q8_s, k8_s, v8_s, o8_s, ypre_s, sga_s, sgb_s, merged_s, act_s) = refs[pos:]
    w_in_ref, _, wpp_ref, wap_ref, wout_ref, wg_ref, wu_ref, wd_ref, wfold_ref = w_v

    SB = SAMPLE_BLOCK
    N = x_ref.shape[0]
    nsteps = N // SB
    i = pl.program_id(0)
    lane = lax.broadcasted_iota(jnp.int32, (N, LANES), 1)
    all_rows = slice(0, N)

    def chunk_copies(n):
        w, pieces = WEIGHT_CHUNKS[n]
        cols = BIG_WEIGHTS[w][2]
        slot = n % STAGE_SLOTS
        return [pltpu.make_async_copy(w_hbm[w].at[pl.ds(r0, rows), :],
                                      stage.at[slot, pl.ds(0, rows), pl.ds(c0, cols)], stage_sem.at[slot])
                for r0, rows, c0 in pieces]

    def writeback_copy(w):
        return pltpu.make_async_copy(w_v[w], w_bf_hbm[w], out_sem.at[w])

    def convert(n):
        w, pieces = WEIGHT_CHUNKS[n]
        cols = BIG_WEIGHTS[w][2]
        for r0, rows, c0 in pieces:
            staged = stage[n % STAGE_SLOTS, 0:rows, c0:c0 + cols]
            if w == W_MIX:
                mixf_s[r0:r0 + rows, :] = staged
            else:
                w_v[w][r0:r0 + rows, :] = staged.astype(BF16)
        last_r0, last_rows, _ = pieces[-1]
        if last_r0 + last_rows == BIG_WEIGHTS[w][1] and w in HANDED_ON:
            writeback_copy(w).start()

    ahead = STAGE_SLOTS - 1
    for step, chunk_ids in enumerate(_chunks_per_step(nsteps)):
        @pl.when(i == step)
        def _(chunk_ids=chunk_ids):
            for n in chunk_ids:
                if n == 0:
                    for m in range(ahead):
                        for copy in chunk_copies(m):
                            copy.start()
                if n + ahead < len(WEIGHT_CHUNKS):
                    for copy in chunk_copies(n + ahead):
                        copy.start()
                for copy in chunk_copies(n):
                    copy.wait()
                convert(n)

    @pl.when(i == 0)
    def _():
        x = x_ref[...]
        h = ((x * _rms_scale(x)) * _param(params_ref, ROW_N1, D_MODEL)).astype(BF16)
        u_s[...] = _dot(h, w_in_ref[:, OFF_U:OFF_U + POOL_WIDTH])
        for gc in range(N_GATE_CHUNKS):
            _gate_chunk(h, gc, w_in_ref, sga_s, sgb_s, all_rows)
        q = _head_norm(_dot(h, w_in_ref[:, OFF_Q:OFF_Q + Q_WIDTH]), _param(params_ref, ROW_QN, Q_WIDTH))
        kv = _dot(h, w_in_ref[:, OFF_K:OFF_K + 2 * KV_WIDTH])
        kn = _head_norm(kv[:, 0:KV_WIDTH], _param(params_ref, ROW_KN, KV_WIDTH))
        v = kv[:, KV_WIDTH:]
        kt_s[...] = kn.T
        vt_s[...] = v.T
        for head in range(N_HEADS):
            k8_s[pl.ds(head, N, stride=N_HEADS), :] = kn
            v8_s[pl.ds(head, N, stride=N_HEADS), :] = v
        for p in range(Q_COLS):
            qcol = q[:, p * LANES:(p + 1) * LANES]
            qswap = pltpu.roll(qcol, HEAD_DIM, 1)
            for e in range(HEADS_PER_COL):
                head = p * HEADS_PER_COL + e
                c = _kv_head(head)
                src = qcol if e == c else qswap
                keep = (lane < HEAD_DIM) if c == 0 else (lane >= HEAD_DIM)
                q8_s[pl.ds(head, N, stride=N_HEADS), :] = jnp.where(keep, src, 0.0)

    base = i * SB
    rows = pl.ds(pl.multiple_of(base, SB), SB)

    unew = u_s[rows, :]
    for g, w in enumerate(POOL_WINDOWS):
        cols = slice(g * POOL_GROUP_WIDTH, (g + 1) * POOL_GROUP_WIDTH)
        s = unew[:, cols]
        for r in range(POOL_BUF - (w - 1), POOL_BUF):
            s = s + sp_ref[r, :, cols]
        ypre_s[rows, cols] = (s * (1.0 / w) - unew[:, cols]).astype(BF16)
    pnew_ref[0:POOL_BUF - 1] = sp_ref[1:POOL_BUF]
    pnew_ref[POOL_BUF - 1] = unew

    hrows = pl.ds(pl.multiple_of(base * N_HEADS, SB * N_HEADS), SB * N_HEADS)
    lhs = q8_s[hrows, :]
    sc = jnp.concatenate(
        [_dot(lhs[bb * N_HEADS:(bb + 1) * N_HEADS, :].astype(BF16), ck_ref[bb].astype(BF16)) for bb in range(SB)],
        axis=0)
    sc_self = jnp.sum(lhs * k8_s[hrows, :], axis=-1, keepdims=True)
    sink = sinkc_ref[:, 0:1]
    m = jnp.maximum(jnp.maximum(jnp.max(sc, axis=-1, keepdims=True), sc_self), sink)
    p = jnp.exp(sc - m)
    p_self = jnp.exp(sc_self - m)
    denom = jnp.sum(p, axis=-1, keepdims=True) + p_self + jnp.exp(sink - m)
    inv = 1.0 / denom
    pn = p * inv
    o = jnp.concatenate(
        [_dot_nt(pn[bb * N_HEADS:(bb + 1) * N_HEADS, :].astype(BF16), cv_ref[bb].astype(BF16)) for bb in range(SB)],
        axis=0)
    o8_s[hrows, :] = o + (p_self * inv) * v8_s[hrows, :]

    lane_c = lax.broadcasted_iota(jnp.int32, (KV_WIDTH, WINDOW), 1)
    kt_all = kt_s[...]
    vt_all = vt_s[...]
    for bb in range(SB):
        put = (WINDOW - 1) - (base + bb)
        knew_ref[bb] = jnp.where(lane_c == WINDOW - 1, pltpu.roll(kt_all, put, 1),
                                 pltpu.roll(ck_ref[bb], WINDOW - 1, 1))
        vnew_ref[bb] = jnp.where(lane_c == WINDOW - 1, pltpu.roll(vt_all, put, 1),
                                 pltpu.roll(cv_ref[bb], WINDOW - 1, 1))

    @pl.when(i == nsteps - 1)
    def _():
        for g in range(len(POOL_WINDOWS)):
            grp = slice(g * POOL_GROUP_WIDTH, (g + 1) * POOL_GROUP_WIDTH)
            scale = params_ref[ROW_PSCALE:ROW_PSCALE + 1, grp]
            wfold_ref[grp, :] = _dot((mixf_s[grp, :] * scale).astype(BF16), wpp_ref[grp, :]).astype(BF16)
        writeback_copy(W_FOLD).start()
        ycols = []
        for p in range(Q_COLS):
            halves = []
            for e in range(HEADS_PER_COL):
                head = p * HEADS_PER_COL + e
                oh = o8_s[pl.ds(head, N, stride=N_HEADS), :]
                halves.append(oh if e == _kv_head(head) else pltpu.roll(oh, HEAD_DIM, 1))
            ycols.append(jnp.where(lane < HEAD_DIM, halves[0], halves[1]).astype(BF16))
        yattn = jnp.concatenate(ycols, axis=1)
        _merge(ypre_s[...], yattn, sga_s, sgb_s, wfold_ref, wap_ref, merged_s, all_rows)
        x1 = x_ref[...] + _dot(merged_s[...], wout_ref[...])
        h2 = ((x1 * _rms_scale(x1)) * _param(params_ref, ROW_N2, D_MODEL)).astype(BF16)
        _ffn_act(h2, wg_ref, wu_ref, act_s, all_rows)
        y_ref[...] = x1 + _dot(act_s[...], wd_ref[...])
        for w in HANDED_ON:
            writeback_copy(w).wait()


def _sample_call(x, ck, cv, sp, sinkc, small, big):
    N = x.shape[0]
    SB = SAMPLE_BLOCK
    W = WINDOW
    assert tuple(b.shape for b in big) == tuple((r, c) for _, r, c in BIG_WEIGHTS)
    consts = (sinkc,) + tuple(small)
    in_specs = [_const_spec(x.shape),
                pl.BlockSpec((SB, KV_WIDTH, W), lambda i: (i, 0, 0)),
                pl.BlockSpec((SB, KV_WIDTH, W), lambda i: (i, 0, 0)),
                pl.BlockSpec((POOL_BUF, SB, POOL_WIDTH), lambda i: (0, i, 0))]
    in_specs += [_const_spec(w.shape) for w in consts]
    in_specs += [pl.BlockSpec(memory_space=pl.ANY) for _ in big]
    out_shape = (jax.ShapeDtypeStruct((N, D_MODEL), F32),
                 jax.ShapeDtypeStruct((N, KV_WIDTH, W), F32),
                 jax.ShapeDtypeStruct((N, KV_WIDTH, W), F32),
                 jax.ShapeDtypeStruct((POOL_BUF, N, POOL_WIDTH), F32))
    bf_shapes = [b.shape for b in big] + [(POOL_WIDTH, D_MODEL)]
    out_shape += tuple(jax.ShapeDtypeStruct(bf_shapes[w], BF16) for w in HANDED_ON)
    out_specs = (pl.BlockSpec((N, D_MODEL), lambda i: (0, 0)),
                 pl.BlockSpec((SB, KV_WIDTH, W), lambda i: (i, 0, 0)),
                 pl.BlockSpec((SB, KV_WIDTH, W), lambda i: (i, 0, 0)),
                 pl.BlockSpec((POOL_BUF, SB, POOL_WIDTH), lambda i: (0, i, 0)))
    out_specs += tuple(pl.BlockSpec(memory_space=pl.ANY) for _ in HANDED_ON)
    scratch = [pltpu.VMEM(shape, BF16) for shape in bf_shapes]
    scratch += [pltpu.VMEM((STAGE_SLOTS, CONVERT_ROWS, STAGE_COLS), F32),
                pltpu.SemaphoreType.DMA((STAGE_SLOTS,)),
                pltpu.SemaphoreType.DMA((len(bf_shapes),)),
                pltpu.VMEM(big[W_MIX].shape, F32),
                pltpu.VMEM((N, POOL_WIDTH), F32),
               pltpu.VMEM((KV_WIDTH, N), F32),
               pltpu.VMEM((KV_WIDTH, N), F32),
               pltpu.VMEM((N * N_HEADS, LANES), F32),
               pltpu.VMEM((N * N_HEADS, LANES), F32),
               pltpu.VMEM((N * N_HEADS, LANES), F32),
               pltpu.VMEM((N * N_HEADS, LANES), F32),
               pltpu.VMEM((N, POOL_WIDTH), BF16),
               pltpu.VMEM((N, D_MODEL), F32),
               pltpu.VMEM((N, D_MODEL), F32),
               pltpu.VMEM((N, D_MODEL), BF16),
               pltpu.VMEM((N, D_FF), BF16)]
    return pl.pallas_call(
        _sample_kernel,
        grid=(N // SB,),
        in_specs=in_specs,
        out_specs=out_specs,
        out_shape=out_shape,
        scratch_shapes=scratch,
        compiler_params=pltpu.CompilerParams(
            dimension_semantics=("arbitrary",),
            vmem_limit_bytes=VMEM_LIMIT_BYTES),
        name="sample_layer",
    )(x, ck, cv, sp, *consts, *big)


def _cache_to_device_order(c):
    n, w = c.shape[0], c.shape[1]
    return jnp.transpose(c, (0, 2, 3, 1)).reshape(n, KV_WIDTH, w)


def _cache_from_device_order(c):
    n, _, w = c.shape
    return jnp.transpose(c.reshape(n, N_KV_HEADS, HEAD_DIM, w), (0, 3, 1, 2))[None]


@jax.jit
def _forward(x_prompt, x_sample, cache_k, cache_v, state_pool, norm1, w_in, q_norm, k_norm, sinks,
             pool_mix_w, pool_scale, w_pool_proj, w_attn_proj, w_out, norm2, w_gate, w_up, w_down):
    depth = w_in.shape[0]
    assert depth == 1, "single-layer trunk"
    l = 0
    N = x_sample.shape[0]
    assert cache_k.shape[2] == WINDOW and x_sample.shape[1] == 1

    def row(v):
        return jnp.pad(v, (0, D_MODEL - v.shape[0]))

    rows = {ROW_N1: norm1[l], ROW_N2: norm2[l],
            ROW_QN: jnp.tile(q_norm[l], N_HEADS) * (HEAD_DIM ** -0.5),
            ROW_KN: jnp.tile(k_norm[l], N_KV_HEADS), ROW_PSCALE: pool_scale[l]}
    params = jnp.stack([row(rows[r]) if r in rows else jnp.zeros((D_MODEL,), F32) for r in range(PARAM_ROWS)])
    big = (w_in[l], pool_mix_w[l].reshape(len(POOL_WINDOWS) * POOL_GROUP_WIDTH, POOL_GROUP_WIDTH),
           w_pool_proj[l], w_attn_proj[l], w_out[l], w_gate[l], w_up[l], w_down[l])

    sink_cols = jnp.broadcast_to(jnp.tile(sinks[l], SAMPLE_BLOCK)[:, None], (SAMPLE_BLOCK * N_HEADS, LANES))
    y_s, k_s, v_s, pool_s, w_in_b, wap_b, wout_b, wg_b, wu_b, wd_b, wfold_b = _sample_call(
        x_sample.reshape(N, D_MODEL),
        _cache_to_device_order(cache_k[l]), _cache_to_device_order(cache_v[l]),
        jnp.transpose(state_pool[l], (1, 0, 2)), sink_cols, (params,), big)

    weights = (params, w_in_b, wfold_b, wap_b, wout_b, wg_b, wu_b, wd_b)
    y_p, k_p, v_p, pool_p = _prompt_call(x_prompt, sinks[l], weights)

    return (y_p, y_s.reshape(N, 1, D_MODEL),
            _cache_from_device_order(k_p), _cache_from_device_order(v_p), pool_p[None],
            _cache_from_device_order(k_s), _cache_from_device_order(v_s),
            jnp.transpose(pool_s, (1, 0, 2))[None])


def kernel(x_prompt, x_sample, cache_k, cache_v, state_pool, norm1, w_in, q_norm, k_norm, sinks, pool_mix_w,
           pool_scale, w_pool_proj, w_attn_proj, w_out, norm2, w_gate, w_up, w_down):
    return _forward(x_prompt, x_sample, cache_k, cache_v, state_pool, norm1, w_in, q_norm, k_norm, sinks,
                    pool_mix_w, pool_scale, w_pool_proj, w_attn_proj, w_out, norm2, w_gate, w_up, w_down)
```

```python
import jax
import jax.numpy as jnp
from jax import lax
from jax.experimental import pallas as pl
from jax.experimental.pallas import tpu as pltpu

D_MODEL = 1024
POOL_WINDOWS = (2, 4, 8, 16)
POOL_GROUP_WIDTH = 128
POOL_WIDTH = 512
POOL_BUF = 15
N_HEADS = 8
N_KV_HEADS = 2
HEAD_DIM = 64
GROUP = N_HEADS // N_KV_HEADS
Q_WIDTH = 512
KV_WIDTH = 128
WINDOW = 128
D_FF = 2816
EPS = 1e-6
NEG = -1e30

LANES = 128
ATT_BLOCK = WINDOW
SEQ_TILE = 512
TILES_PER_STEP = 2
SUB_TILE = 256
FF_CHUNK = 512
GATE_CHUNK = 256
SAMPLE_BLOCK = 16
PREFIX = 16
VMEM_LIMIT_BYTES = 58 * 1024 * 1024
Q_COLS = Q_WIDTH // LANES
HEADS_PER_COL = LANES // HEAD_DIM
N_GATE_CHUNKS = D_MODEL // GATE_CHUNK

OFF_U = 0
OFF_Q = OFF_U + POOL_WIDTH
OFF_K = OFF_Q + Q_WIDTH
OFF_V = OFF_K + KV_WIDTH
OFF_GA = OFF_V + KV_WIDTH
OFF_GB = OFF_GA + D_MODEL

BF16 = jnp.bfloat16
F32 = jnp.float32

ROW_N1, ROW_N2, ROW_QN, ROW_KN, ROW_PSCALE, PARAM_ROWS = 0, 1, 2, 3, 4, 8


def _param(params_ref, row, width):
    return params_ref[row:row + 1, 0:width]


def _dot(a, b):
    return jnp.dot(a, b, preferred_element_type=F32)


def _dot_nt(a, b):
    return lax.dot_general(a, b, (((1,), (1,)), ((), ())), preferred_element_type=F32)


def _rms_scale(x):
    return lax.rsqrt(jnp.mean(x * x, axis=-1, keepdims=True) + EPS)


def _head_norm(x, gain):
    sq = x * x
    low = lax.broadcasted_iota(jnp.int32, (x.shape[0], LANES), 1) < HEAD_DIM
    cols = []
    for c in range(0, x.shape[1], LANES):
        blk = sq[:, c:c + LANES]
        ss_lo = jnp.sum(jnp.where(low, blk, 0.0), axis=-1, keepdims=True)
        ss_hi = jnp.sum(jnp.where(low, 0.0, blk), axis=-1, keepdims=True)
        cols.append(jnp.where(low, ss_lo, ss_hi))
    ss = jnp.concatenate(cols, axis=1) if len(cols) > 1 else cols[0]
    return (x * lax.rsqrt(ss * (1.0 / HEAD_DIM) + EPS)) * gain


def _kv_head(head):
    return head // GROUP


def _ffn_act(h2, wg_ref, wu_ref, act_ref, rows):
    for start in range(0, D_FF, FF_CHUNK):
        width = min(FF_CHUNK, D_FF - start)
        g = _dot(h2, wg_ref[:, start:start + width])
        u = _dot(h2, wu_ref[:, start:start + width])
        act_ref[rows, start:start + width] = (g * jax.nn.sigmoid(g) * u).astype(BF16)


def _gate_chunk(h, c, w_in_ref, sga_ref, sgb_ref, rows):
    lo = c * GATE_CHUNK
    sga_ref[rows, lo:lo + GATE_CHUNK] = jax.nn.sigmoid(_dot(h, w_in_ref[:, OFF_GA + lo:OFF_GA + lo + GATE_CHUNK]))
    sgb_ref[rows, lo:lo + GATE_CHUNK] = jax.nn.sigmoid(_dot(h, w_in_ref[:, OFF_GB + lo:OFF_GB + lo + GATE_CHUNK]))


def _merge(ypool, yattn, sga_ref, sgb_ref, wpp_ref, wap_ref, merged_ref, rows):
    for c in range(N_GATE_CHUNKS):
        lo = c * GATE_CHUNK
        pp = _dot(ypool, wpp_ref[:, lo:lo + GATE_CHUNK])
        ap = _dot(yattn, wap_ref[:, lo:lo + GATE_CHUNK])
        merged = sga_ref[rows, lo:lo + GATE_CHUNK] * pp + sgb_ref[rows, lo:lo + GATE_CHUNK] * ap
        merged_ref[rows, lo:lo + GATE_CHUNK] = merged.astype(BF16)


def _prompt_kernel(*refs):
    def body(t, carry):
        _prompt_tile(t, *refs)
        return carry
    lax.fori_loop(0, TILES_PER_STEP, body, 0)


def _prompt_tile(t, sinks_ref, x_ref, params_ref, w_in_ref, wfold_ref, wap_ref, wout_ref, wg_ref, wu_ref, wd_ref,
                 y_ref, knew_ref, vnew_ref, pnew_ref,
                 h_s, u_s, q_s, kd_s, vd_s, s_s, p_s, ypool_s, yattn_s, sga_s, sgb_s, merged_s, x1_s, h2_s, act_s):
    T = SEQ_TILE
    R = SUB_TILE
    subs = [slice(r0, r0 + R) for r0 in range(0, T, R)]
    n_blocks = T // ATT_BLOCK
    j = pl.program_id(1) * TILES_PER_STEP + t
    rows_per_kv = GROUP * ATT_BLOCK

    def in_block(rows):
        return pl.ds(pl.multiple_of(t * T + rows.start, R), R)

    @pl.when(j == 0)
    def _():
        u_s[0:PREFIX, :] = jnp.zeros((PREFIX, POOL_WIDTH), F32)
        kd_s[:, 0:ATT_BLOCK, :] = jnp.zeros((N_KV_HEADS, ATT_BLOCK, KV_WIDTH), BF16)
        vd_s[:, 0:ATT_BLOCK, :] = jnp.zeros((N_KV_HEADS, ATT_BLOCK, KV_WIDTH), BF16)

    lane_r = lax.broadcasted_iota(jnp.int32, (R, LANES), 1)
    lane = lax.broadcasted_iota(jnp.int32, (ATT_BLOCK, LANES), 1)
    row = lax.broadcasted_iota(jnp.int32, (ATT_BLOCK, 2 * ATT_BLOCK), 0)
    col = lax.broadcasted_iota(jnp.int32, (ATT_BLOCK, 2 * ATT_BLOCK), 1)
    band = (col >= row) & (col <= row + WINDOW)
    first_lo = jnp.where(j == 0, ATT_BLOCK, 0)

    def norm1(rows):
        x = x_ref[0, in_block(rows), :]
        h_s[rows, :] = ((x * _rms_scale(x)) * _param(params_ref, ROW_N1, D_MODEL)).astype(BF16)

    def in_proj(rows):
        h = h_s[rows, :]
        u_s[PREFIX + rows.start:PREFIX + rows.stop, :] = _dot(h, w_in_ref[:, OFF_U:OFF_U + POOL_WIDTH])
        q = _dot(h, w_in_ref[:, OFF_Q:OFF_Q + Q_WIDTH])
        q_s[rows, :] = _head_norm(q, _param(params_ref, ROW_QN, Q_WIDTH)).astype(BF16)
        kv = _dot(h, w_in_ref[:, OFF_K:OFF_K + 2 * KV_WIDTH])
        kn = _head_norm(kv[:, 0:KV_WIDTH], _param(params_ref, ROW_KN, KV_WIDTH))
        v = kv[:, KV_WIDTH:]
        dst_rows = slice(ATT_BLOCK + rows.start, ATT_BLOCK + rows.stop)
        for src, dst in ((kn, kd_s), (v, vd_s)):
            swapped = pltpu.roll(src, HEAD_DIM, 1)
            dst[0, dst_rows, :] = jnp.where(lane_r < HEAD_DIM, src, swapped).astype(BF16)
            dst[1, dst_rows, :] = jnp.where(lane_r < HEAD_DIM, swapped, src).astype(BF16)
        if rows.stop == T:
            knew_ref[0] = kn[R - ATT_BLOCK:, :].T
            vnew_ref[0] = v[R - ATT_BLOCK:, :].T
            pnew_ref[0] = u_s[PREFIX + T - POOL_BUF:PREFIX + T, :]

    def pool(rows):
        pos1 = j * T + rows.start + lax.broadcasted_iota(jnp.int32, (R, 1), 0) + 1
        for g, w in enumerate(POOL_WINDOWS):
            cols = slice(g * POOL_GROUP_WIDTH, (g + 1) * POOL_GROUP_WIDTH)
            a = u_s[rows.start:rows.stop + PREFIX, cols]
            s = a
            shift = 1
            while shift < w:
                s = s + pltpu.roll(s, shift, 0)
                shift *= 2
            inv_cnt = 1.0 / jnp.minimum(pos1, w).astype(F32)
            ypool_s[rows, cols] = (s[PREFIX:, :] * inv_cnt - a[PREFIX:, :]).astype(BF16)

    def gates(rows, chunks):
        h = h_s[rows, :]
        for c in chunks:
            _gate_chunk(h, c, w_in_ref, sga_s, sgb_s, rows)

    def scores(b):
        r0 = b * ATT_BLOCK
        for c in range(N_KV_HEADS):
            parts = []
            for p in range(c * Q_COLS // N_KV_HEADS, (c + 1) * Q_COLS // N_KV_HEADS):
                qcol = q_s[r0:r0 + ATT_BLOCK, p * LANES:(p + 1) * LANES]
                parts.append(jnp.where(lane < HEAD_DIM, qcol, jnp.zeros_like(qcol)))
                parts.append(jnp.where(lane >= HEAD_DIM, qcol, jnp.zeros_like(qcol)))
            s_s[b % 2, c * rows_per_kv:(c + 1) * rows_per_kv, :] = _dot_nt(
                jnp.concatenate(parts, axis=0), kd_s[c, r0:r0 + 2 * ATT_BLOCK, :])

    def softmax(b):
        valid = band & (col >= first_lo) if b == 0 else band
        for head in range(N_HEADS):
            sh = jnp.where(valid, s_s[b % 2, head * ATT_BLOCK:(head + 1) * ATT_BLOCK, :], NEG)
            sink = sinks_ref[head]
            m = jnp.maximum(jnp.max(sh, axis=-1, keepdims=True), sink)
            p = jnp.exp(sh - m)
            denom = jnp.sum(p, axis=-1, keepdims=True) + jnp.exp(sink - m)
            p_s[b % 2, head * ATT_BLOCK:(head + 1) * ATT_BLOCK, :] = (p * (1.0 / denom)).astype(BF16)

    def weighted_values(b):
        r0 = b * ATT_BLOCK
        for c in range(N_KV_HEADS):
            o = _dot(p_s[b % 2, c * rows_per_kv:(c + 1) * rows_per_kv, :], vd_s[c, r0:r0 + 2 * ATT_BLOCK, :])
            for pp in range(Q_COLS // N_KV_HEADS):
                p_col = c * Q_COLS // N_KV_HEADS + pp
                o_lo = o[(2 * pp) * ATT_BLOCK:(2 * pp + 1) * ATT_BLOCK, :]
                o_hi = o[(2 * pp + 1) * ATT_BLOCK:(2 * pp + 2) * ATT_BLOCK, :]
                yattn_s[r0:r0 + ATT_BLOCK, p_col * LANES:(p_col + 1) * LANES] = (
                    jnp.where(lane < HEAD_DIM, o_lo, o_hi).astype(BF16))

    def out_proj(rows):
        _merge(ypool_s[rows, :], yattn_s[rows, :], sga_s, sgb_s, wfold_ref, wap_ref, merged_s, rows)
        x1 = x_ref[0, in_block(rows), :] + _dot(merged_s[rows, :], wout_ref[...])
        x1_s[rows, :] = x1
        h2_s[rows, :] = ((x1 * _rms_scale(x1)) * _param(params_ref, ROW_N2, D_MODEL)).astype(BF16)

    for rows in subs:
        norm1(rows)
    for rows in subs:
        in_proj(rows)
    pool(subs[0])
    gates(subs[0], range(N_GATE_CHUNKS))
    for rows in subs[1:]:
        pool(rows)
    later_gates = [(rows, c) for rows in subs[1:] for c in range(N_GATE_CHUNKS)]
    per_block = -(-len(later_gates) // n_blocks)
    scores(0)
    for b in range(n_blocks):
        if b + 1 < n_blocks:
            scores(b + 1)
        for rows, c in later_gates[b * per_block:(b + 1) * per_block]:
            gates(rows, [c])
        softmax(b)
        weighted_values(b)

    u_s[0:PREFIX, :] = u_s[T:T + PREFIX, :]
    kd_s[:, 0:ATT_BLOCK, :] = kd_s[:, T:T + ATT_BLOCK, :]
    vd_s[:, 0:ATT_BLOCK, :] = vd_s[:, T:T + ATT_BLOCK, :]

    for rows in subs:
        out_proj(rows)
    for rows in subs:
        _ffn_act(h2_s[rows, :], wg_ref, wu_ref, act_s, rows)
    for rows in subs:
        y_ref[0, in_block(rows), :] = x1_s[rows, :] + _dot(act_s[rows, :], wd_ref[...])


def _const_spec(shape):
    nd = len(shape)
    return pl.BlockSpec(shape, lambda *_: (0,) * nd, pipeline_mode=pl.Buffered(1))


def _prompt_call(x, sinks, weights):
    B, S, _ = x.shape
    T = SEQ_TILE
    step_rows = T * TILES_PER_STEP
    in_specs = [pl.BlockSpec(memory_space=pltpu.SMEM),
                pl.BlockSpec((1, step_rows, D_MODEL), lambda b, j: (b, j, 0))]
    in_specs += [_const_spec(w.shape) for w in weights]
    out_shape = (jax.ShapeDtypeStruct((B, S, D_MODEL), F32),
                 jax.ShapeDtypeStruct((B, KV_WIDTH, ATT_BLOCK), F32),
                 jax.ShapeDtypeStruct((B, KV_WIDTH, ATT_BLOCK), F32),
                 jax.ShapeDtypeStruct((B, POOL_BUF, POOL_WIDTH), F32))
    out_specs = (pl.BlockSpec((1, step_rows, D_MODEL), lambda b, j: (b, j, 0)),
                 pl.BlockSpec((1, KV_WIDTH, ATT_BLOCK), lambda b, j: (b, 0, 0)),
                 pl.BlockSpec((1, KV_WIDTH, ATT_BLOCK), lambda b, j: (b, 0, 0)),
                 pl.BlockSpec((1, POOL_BUF, POOL_WIDTH), lambda b, j: (b, 0, 0)))
    scratch = [pltpu.VMEM((T, D_MODEL), BF16),
               pltpu.VMEM((PREFIX + T, POOL_WIDTH), F32),
               pltpu.VMEM((T, Q_WIDTH), BF16),
               pltpu.VMEM((N_KV_HEADS, ATT_BLOCK + T, KV_WIDTH), BF16),
               pltpu.VMEM((N_KV_HEADS, ATT_BLOCK + T, KV_WIDTH), BF16),
               pltpu.VMEM((2, N_HEADS * ATT_BLOCK, 2 * ATT_BLOCK), F32),
               pltpu.VMEM((2, N_HEADS * ATT_BLOCK, 2 * ATT_BLOCK), BF16),
               pltpu.VMEM((T, POOL_WIDTH), BF16),
               pltpu.VMEM((T, Q_WIDTH), BF16),
               pltpu.VMEM((T, D_MODEL), F32),
               pltpu.VMEM((T, D_MODEL), F32),
               pltpu.VMEM((T, D_MODEL), BF16),
               pltpu.VMEM((T, D_MODEL), F32),
               pltpu.VMEM((T, D_MODEL), BF16),
               pltpu.VMEM((T, D_FF), BF16)]
    return pl.pallas_call(
        _prompt_kernel,
        grid=(B, S // step_rows),
        in_specs=in_specs,
        out_specs=out_specs,
        out_shape=out_shape,
        scratch_shapes=scratch,
        compiler_params=pltpu.CompilerParams(
            dimension_semantics=("arbitrary", "arbitrary"),
            vmem_limit_bytes=VMEM_LIMIT_BYTES),
        name="prompt_layer",
    )(sinks, x, *weights)


BIG_WEIGHTS = (("w_in", D_MODEL, OFF_GB + D_MODEL), ("mix", len(POOL_WINDOWS) * POOL_GROUP_WIDTH, POOL_GROUP_WIDTH),
               ("wpp", POOL_WIDTH, D_MODEL), ("wap", Q_WIDTH, D_MODEL), ("wout", D_MODEL, D_MODEL),
               ("wg", D_MODEL, D_FF), ("wu", D_MODEL, D_FF), ("wd", D_FF, D_MODEL))
W_MIX, W_PP = 1, 2
W_FOLD = len(BIG_WEIGHTS)
HANDED_ON = tuple(w for w in range(len(BIG_WEIGHTS)) if w not in (W_MIX, W_PP)) + (W_FOLD,)
CONVERT_ROWS = 128
STAGE_SLOTS = 4
STAGE_COLS = max(cols for _, _, cols in BIG_WEIGHTS)


def _weight_chunks():
    chunks = []
    for w, (_, nrows, cols) in enumerate(BIG_WEIGHTS):
        side_by_side = max(1, STAGE_COLS // cols)
        blocks = [(r0, min(CONVERT_ROWS, nrows - r0)) for r0 in range(0, nrows, CONVERT_ROWS)]
        for first in range(0, len(blocks), side_by_side):
            group = blocks[first:first + side_by_side]
            chunks.append((w, tuple((r0, rows, k * cols) for k, (r0, rows) in enumerate(group))))
    return tuple(chunks)


WEIGHT_CHUNKS = _weight_chunks()


def _chunks_per_step(nsteps):
    first = [n for n, (w, _) in enumerate(WEIGHT_CHUNKS) if w == 0]
    rest = [n for n, (w, _) in enumerate(WEIGHT_CHUNKS) if w != 0]
    size = lambda n: sum(rows for _, rows, _ in WEIGHT_CHUNKS[n][1]) * BIG_WEIGHTS[WEIGHT_CHUNKS[n][0]][2]
    total = sum(size(n) for n in rest)
    steps = [first] + [[] for _ in range(nsteps - 1)]
    done = 0
    for n in rest:
        steps[1 + min(nsteps - 2, done * (nsteps - 1) // total)].append(n)
        done += size(n)
    return steps


def _sample_kernel(*refs):
    nw = len(BIG_WEIGHTS)
    n_in = 6
    (x_ref, ck_ref, cv_ref, sp_ref, sinkc_ref, params_ref) = refs[:n_in]
    w_hbm = refs[n_in:n_in + nw]
    pos = n_in + nw
    y_ref, knew_ref, vnew_ref, pnew_ref = refs[pos:pos + 4]
    pos += 4
    w_bf_hbm = dict(zip(HANDED_ON, refs[pos:pos + len(HANDED_ON)]))
    pos += len(HANDED_ON)
    w_v = refs[pos:pos + nw + 1]
    pos += nw + 1
    (stage, stage_sem, out_sem, mixf_s,
     u_s, kt_s, vt_s, q8_s, k8_s, v8_s, o8_s, ypre_s, sga_s, sgb_s, merged_s, act_s) = refs[pos:]
    w_in_ref, _, wpp_ref, wap_ref, wout_ref, wg_ref, wu_ref, wd_ref, wfold_ref = w_v

    SB = SAMPLE_BLOCK
    N = x_ref.shape[0]
    nsteps = N // SB
    i = pl.program_id(0)
    lane = lax.broadcasted_iota(jnp.int32, (N, LANES), 1)
    all_rows = slice(0, N)

    def chunk_copies(n):
        w, pieces = WEIGHT_CHUNKS[n]
        cols = BIG_WEIGHTS[w][2]
        slot = n % STAGE_SLOTS
        return [pltpu.make_async_copy(w_hbm[w].at[pl.ds(r0, rows), :],
                                      stage.at[slot, pl.ds(0, rows), pl.ds(c0, cols)], stage_sem.at[slot])
                for r0, rows, c0 in pieces]

    def writeback_copy(w):
        return pltpu.make_async_copy(w_v[w], w_bf_hbm[w], out_sem.at[w])

    def convert(n):
        w, pieces = WEIGHT_CHUNKS[n]
        cols = BIG_WEIGHTS[w][2]
        for r0, rows, c0 in pieces:
            staged = stage[n % STAGE_SLOTS, 0:rows, c0:c0 + cols]
            if w == W_MIX:
                mixf_s[r0:r0 + rows, :] = staged
            else:
                w_v[w][r0:r0 + rows, :] = staged.astype(BF16)
        last_r0, last_rows, _ = pieces[-1]
        if last_r0 + last_rows == BIG_WEIGHTS[w][1] and w in HANDED_ON:
            writeback_copy(w).start()

    ahead = STAGE_SLOTS - 1
    for step, chunk_ids in enumerate(_chunks_per_step(nsteps)):
        @pl.when(i == step)
        def _(chunk_ids=chunk_ids):
            for n in chunk_ids:
                if n == 0:
                    for m in range(ahead):
                        for copy in chunk_copies(m):
                            copy.start()
                if n + ahead < len(WEIGHT_CHUNKS):
                    for copy in chunk_copies(n + ahead):
                        copy.start()
                for copy in chunk_copies(n):
                    copy.wait()
                convert(n)

    @pl.when(i == 0)
    def _():
        x = x_ref[...]
        h = ((x * _rms_scale(x)) * _param(params_ref, ROW_N1, D_MODEL)).astype(BF16)
        u_s[...] = _dot(h, w_in_ref[:, OFF_U:OFF_U + POOL_WIDTH])
        for gc in range(N_GATE_CHUNKS):
            _gate_chunk(h, gc, w_in_ref, sga_s, sgb_s, all_rows)
        q = _head_norm(_dot(h, w_in_ref[:, OFF_Q:OFF_Q + Q_WIDTH]), _param(params_ref, ROW_QN, Q_WIDTH))
        kv = _dot(h, w_in_ref[:, OFF_K:OFF_K + 2 * KV_WIDTH])
        kn = _head_norm(kv[:, 0:KV_WIDTH], _param(params_ref, ROW_KN, KV_WIDTH))
        v = kv[:, KV_WIDTH:]
        kt_s[...] = kn.T
        vt_s[...] = v.T
        for head in range(N_HEADS):
            k8_s[pl.ds(head, N, stride=N_HEADS), :] = kn
            v8_s[pl.ds(head, N, stride=N_HEADS), :] = v
        for p in range(Q_COLS):
            qcol = q[:, p * LANES:(p + 1) * LANES]
            qswap = pltpu.roll(qcol, HEAD_DIM, 1)
            for e in range(HEADS_PER_COL):
                head = p * HEADS_PER_COL + e
                c = _kv_head(head)
                src = qcol if e == c else qswap
                keep = (lane < HEAD_DIM) if c == 0 else (lane >= HEAD_DIM)
                q8_s[pl.ds(head, N, stride=N_HEADS), :] = jnp.where(keep, src, 0.0)

    base = i * SB
    rows = pl.ds(pl.multiple_of(base, SB), SB)

    unew = u_s[rows, :]
    for g, w in enumerate(POOL_WINDOWS):
        cols = slice(g * POOL_GROUP_WIDTH, (g + 1) * POOL_GROUP_WIDTH)
        s = unew[:, cols]
        for r in range(POOL_BUF - (w - 1), POOL_BUF):
            s = s + sp_ref[r, :, cols]
        ypre_s[rows, cols] = (s * (1.0 / w) - unew[:, cols]).astype(BF16)
    pnew_ref[0:POOL_BUF - 1] = sp_ref[1:POOL_BUF]
    pnew_ref[POOL_BUF - 1] = unew

    hrows = pl.ds(pl.multiple_of(base * N_HEADS, SB * N_HEADS), SB * N_HEADS)
    lhs = q8_s[hrows, :]
    sc = jnp.concatenate(
        [_dot(lhs[bb * N_HEADS:(bb + 1) * N_HEADS, :].astype(BF16), ck_ref[bb].astype(BF16)) for bb in range(SB)],
        axis=0)
    sc_self = jnp.sum(lhs * k8_s[hrows, :], axis=-1, keepdims=True)
    sink = sinkc_ref[:, 0:1]
    m = jnp.maximum(jnp.maximum(jnp.max(sc, axis=-1, keepdims=True), sc_self), sink)
    p = jnp.exp(sc - m)
    p_self = jnp.exp(sc_self - m)
    denom = jnp.sum(p, axis=-1, keepdims=True) + p_self + jnp.exp(sink - m)
    inv = 1.0 / denom
    pn = p * inv
    o = jnp.concatenate(
        [_dot_nt(pn[bb * N_HEADS:(bb + 1) * N_HEADS, :].astype(BF16), cv_ref[bb].astype(BF16)) for bb in range(SB)],
        axis=0)
    o8_s[hrows, :] = o + (p_self * inv) * v8_s[hrows, :]

    lane_c = lax.broadcasted_iota(jnp.int32, (KV_WIDTH, WINDOW), 1)
    kt_all = kt_s[...]
    vt_all = vt_s[...]
    for bb in range(SB):
        put = (WINDOW - 1) - (base + bb)
        knew_ref[bb] = jnp.where(lane_c == WINDOW - 1, pltpu.roll(kt_all, put, 1),
                                 pltpu.roll(ck_ref[bb], WINDOW - 1, 1))
        vnew_ref[bb] = jnp.where(lane_c == WINDOW - 1, pltpu.roll(vt_all, put, 1),
                                 pltpu.roll(cv_ref[bb], WINDOW - 1, 1))

    @pl.when(i == nsteps - 1)
    def _():
        for g in range(len(POOL_WINDOWS)):
            grp = slice(g * POOL_GROUP_WIDTH, (g + 1) * POOL_GROUP_WIDTH)
            scale = params_ref[ROW_PSCALE:ROW_PSCALE + 1, grp]
            wfold_ref[grp, :] = _dot((mixf_s[grp, :] * scale).astype(BF16), wpp_ref[grp, :]).astype(BF16)
        writeback_copy(W_FOLD).start()
        ycols = []
        for p in range(Q_COLS):
            halves = []
            for e in range(HEADS_PER_COL):
                head = p * HEADS_PER_COL + e
                oh = o8_s[pl.ds(head, N, stride=N_HEADS), :]
                halves.append(oh if e == _kv_head(head) else pltpu.roll(oh, HEAD_DIM, 1))
            ycols.append(jnp.where(lane < HEAD_DIM, halves[0], halves[1]).astype(BF16))
        yattn = jnp.concatenate(ycols, axis=1)
        _merge(ypre_s[...], yattn, sga_s, sgb_s, wfold_ref, wap_ref, merged_s, all_rows)
        x1 = x_ref[...] + _dot(merged_s[...], wout_ref[...])
        h2 = ((x1 * _rms_scale(x1)) * _param(params_ref, ROW_N2, D_MODEL)).astype(BF16)
        _ffn_act(h2, wg_ref, wu_ref, act_s, all_rows)
        y_ref[...] = x1 + _dot(act_s[...], wd_ref[...])
        for w in HANDED_ON:
            writeback_copy(w).wait()


def _sample_call(x, ck, cv, sp, sinkc, small, big):
    N = x.shape[0]
    SB = SAMPLE_BLOCK
    W = WINDOW
    assert tuple(b.shape for b in big) == tuple((r, c) for _, r, c in BIG_WEIGHTS)
    consts = (sinkc,) + tuple(small)
    in_specs = [_const_spec(x.shape),
                pl.BlockSpec((SB, KV_WIDTH, W), lambda i: (i, 0, 0)),
                pl.BlockSpec((SB, KV_WIDTH, W), lambda i: (i, 0, 0)),
                pl.BlockSpec((POOL_BUF, SB, POOL_WIDTH), lambda i: (0, i, 0))]
    in_specs += [_const_spec(w.shape) for w in consts]
    in_specs += [pl.BlockSpec(memory_space=pl.ANY) for _ in big]
    out_shape = (jax.ShapeDtypeStruct((N, D_MODEL), F32),
                 jax.ShapeDtypeStruct((N, KV_WIDTH, W), F32),
                 jax.ShapeDtypeStruct((N, KV_WIDTH, W), F32),
                 jax.ShapeDtypeStruct((POOL_BUF, N, POOL_WIDTH), F32))
    bf_shapes = [b.shape for b in big] + [(POOL_WIDTH, D_MODEL)]
    out_shape += tuple(jax.ShapeDtypeStruct(bf_shapes[w], BF16) for w in HANDED_ON)
    out_specs = (pl.BlockSpec((N, D_MODEL), lambda i: (0, 0)),
                 pl.BlockSpec((SB, KV_WIDTH, W), lambda i: (i, 0, 0)),
                 pl.BlockSpec((SB, KV_WIDTH, W), lambda i: (i, 0, 0)),
                 pl.BlockSpec((POOL_BUF, SB, POOL_WIDTH), lambda i: (0, i, 0)))
    out_specs += tuple(pl.BlockSpec(memory_space=pl.ANY) for _ in HANDED_ON)
    scratch = [pltpu.VMEM(shape, BF16) for shape in bf_shapes]
    scratch += [pltpu.VMEM((STAGE_SLOTS, CONVERT_ROWS, STAGE_COLS), F32),
                pltpu.SemaphoreType.DMA((STAGE_SLOTS,)),
                pltpu.SemaphoreType.DMA((len(bf_shapes),)),
                pltpu.VMEM(big[W_MIX].shape, F32),
                pltpu.VMEM((N, POOL_WIDTH), F32),
               pltpu.VMEM((KV_WIDTH, N), F32),
               pltpu.VMEM((KV_WIDTH, N), F32),
               pltpu.VMEM((N * N_HEADS, LANES), F32),
               pltpu.VMEM((N * N_HEADS, LANES), F32),
               pltpu.VMEM((N * N_HEADS, LANES), F32),
               pltpu.VMEM((N * N_HEADS, LANES), F32),
               pltpu.VMEM((N, POOL_WIDTH), BF16),
               pltpu.VMEM((N, D_MODEL), F32),
               pltpu.VMEM((N, D_MODEL), F32),
               pltpu.VMEM((N, D_MODEL), BF16),
               pltpu.VMEM((N, D_FF), BF16)]
    return pl.pallas_call(
        _sample_kernel,
        grid=(N // SB,),
        in_specs=in_specs,
        out_specs=out_specs,
        out_shape=out_shape,
        scratch_shapes=scratch,
        compiler_params=pltpu.CompilerParams(
            dimension_semantics=("arbitrary",),
            vmem_limit_bytes=VMEM_LIMIT_BYTES),
        name="sample_layer",
    )(x, ck, cv, sp, *consts, *big)


def _cache_to_device_order(c):
    n, w = c.shape[0], c.shape[1]
    return jnp.transpose(c, (0, 2, 3, 1)).reshape(n, KV_WIDTH, w)


def _cache_from_device_order(c):
    n, _, w = c.shape
    return jnp.transpose(c.reshape(n, N_KV_HEADS, HEAD_DIM, w), (0, 3, 1, 2))[None]


@jax.jit
def _forward(x_prompt, x_sample, cache_k, cache_v, state_pool, norm1, w_in, q_norm, k_norm, sinks,
             pool_mix_w, pool_scale, w_pool_proj, w_attn_proj, w_out, norm2, w_gate, w_up, w_down):
    depth = w_in.shape[0]
    assert depth == 1, "single-layer trunk"
    l = 0
    N = x_sample.shape[0]
    assert cache_k.shape[2] == WINDOW and x_sample.shape[1] == 1

    def row(v):
        return jnp.pad(v, (0, D_MODEL - v.shape[0]))

    rows = {ROW_N1: norm1[l], ROW_N2: norm2[l],
            ROW_QN: jnp.tile(q_norm[l], N_HEADS) * (HEAD_DIM ** -0.5),
            ROW_KN: jnp.tile(k_norm[l], N_KV_HEADS), ROW_PSCALE: pool_scale[l]}
    params = jnp.stack([row(rows[r]) if r in rows else jnp.zeros((D_MODEL,), F32) for r in range(PARAM_ROWS)])
    big = (w_in[l], pool_mix_w[l].reshape(len(POOL_WINDOWS) * POOL_GROUP_WIDTH, POOL_GROUP_WIDTH),
           w_pool_proj[l], w_attn_proj[l], w_out[l], w_gate[l], w_up[l], w_down[l])

    sink_cols = jnp.broadcast_to(jnp.tile(sinks[l], SAMPLE_BLOCK)[:, None], (SAMPLE_BLOCK * N_HEADS, LANES))
    y_s, k_s, v_s, pool_s, w_in_b, wap_b, wout_b, wg_b, wu_b, wd_b, wfold_b = _sample_call(
        x_sample.reshape(N, D_MODEL),
        _cache_to_device_order(cache_k[l]), _cache_to_device_order(cache_v[l]),
        jnp.transpose(state_pool[l], (1, 0, 2)), sink_cols, (params,), big)

    weights = (params, w_in_b, wfold_b, wap_b, wout_b, wg_b, wu_b, wd_b)
    y_p, k_p, v_p, pool_p = _prompt_call(x_prompt, sinks[l], weights)

    return (y_p, y_s.reshape(N, 1, D_MODEL),
            _cache_from_device_order(k_p), _cache_from_device_order(v_p), pool_p[None],
            _cache_from_device_order(k_s), _cache_from_device_order(v_s),
            jnp.transpose(pool_s, (1, 0, 2))[None])


def kernel(x_prompt, x_sample, cache_k, cache_v, state_pool, norm1, w_in, q_norm, k_norm, sinks, pool_mix_w,
           pool_scale, w_pool_proj, w_attn_proj, w_out, norm2, w_gate, w_up, w_down):
    return _forward(x_prompt, x_sample, cache_k, cache_v, state_pool, norm1, w_in, q_norm, k_norm, sinks,
                    pool_mix_w, pool_scale, w_pool_proj, w_attn_proj, w_out, norm2, w_gate, w_up, w_down)
```

```python
import jax
import jax.numpy as jnp
from jax import lax
from jax.experimental import pallas as pl
from jax.experimental.pallas import tpu as pltpu

D_MODEL = 1024
POOL_WINDOWS = (2, 4, 8, 16)
POOL_GROUP_WIDTH = 128
POOL_WIDTH = 512
POOL_BUF = 15
N_HEADS = 8
N_KV_HEADS = 2
HEAD_DIM = 64
GROUP = N_HEADS // N_KV_HEADS
Q_WIDTH = 512
KV_WIDTH = 128
WINDOW = 128
D_FF = 2816
EPS = 1e-6
NEG = -1e30

LANES = 128
ATT_BLOCK = WINDOW
SEQ_TILE = 512
TILES_PER_STEP = 1
SUB_TILE = 256
FF_CHUNK = 256
GATE_CHUNK = 256
SAMPLE_BLOCK = 16
PREFIX = 16
VMEM_LIMIT_BYTES = 58 * 1024 * 1024
Q_COLS = Q_WIDTH // LANES
HEADS_PER_COL = LANES // HEAD_DIM
N_GATE_CHUNKS = D_MODEL // GATE_CHUNK

OFF_U = 0
OFF_Q = OFF_U + POOL_WIDTH
OFF_K = OFF_Q + Q_WIDTH
OFF_V = OFF_K + KV_WIDTH
OFF_GA = OFF_V + KV_WIDTH
OFF_GB = OFF_GA + D_MODEL

BF16 = jnp.bfloat16
F32 = jnp.float32

ROW_N1, ROW_N2, ROW_QN, ROW_KN, ROW_PSCALE, PARAM_ROWS = 0, 1, 2, 3, 4, 8


def _param(params_ref, row, width):
    return params_ref[row:row + 1, 0:width]


def _dot(a, b):
    return jnp.dot(a, b, preferred_element_type=F32)


def _dot_nt(a, b):
    return lax.dot_general(a, b, (((1,), (1,)), ((), ())), preferred_element_type=F32)


def _rms_scale(x):
    return lax.rsqrt(jnp.mean(x * x, axis=-1, keepdims=True) + EPS)


def _head_norm(x, gain):
    sq = x * x
    low = lax.broadcasted_iota(jnp.int32, (x.shape[0], LANES), 1) < HEAD_DIM
    cols = []
    for c in range(0, x.shape[1], LANES):
        blk = sq[:, c:c + LANES]
        ss_lo = jnp.sum(jnp.where(low, blk, 0.0), axis=-1, keepdims=True)
        ss_hi = jnp.sum(jnp.where(low, 0.0, blk), axis=-1, keepdims=True)
        cols.append(jnp.where(low, ss_lo, ss_hi))
    ss = jnp.concatenate(cols, axis=1) if len(cols) > 1 else cols[0]
    return (x * lax.rsqrt(ss * (1.0 / HEAD_DIM) + EPS)) * gain


def _kv_head(head):
    return head // GROUP


def _ffn_act(h2, wg_ref, wu_ref, act_ref, rows):
    for start in range(0, D_FF, FF_CHUNK):
        width = min(FF_CHUNK, D_FF - start)
        g = _dot(h2, wg_ref[:, start:start + width])
        u = _dot(h2, wu_ref[:, start:start + width])
        act_ref[rows, start:start + width] = (g * jax.nn.sigmoid(g) * u).astype(BF16)


def _gate_chunk(h, c, w_in_ref, sga_ref, sgb_ref, rows):
    lo = c * GATE_CHUNK
    sga_ref[rows, lo:lo + GATE_CHUNK] = jax.nn.sigmoid(_dot(h, w_in_ref[:, OFF_GA + lo:OFF_GA + lo + GATE_CHUNK]))
    sgb_ref[rows, lo:lo + GATE_CHUNK] = jax.nn.sigmoid(_dot(h, w_in_ref[:, OFF_GB + lo:OFF_GB + lo + GATE_CHUNK]))


def _merge(ypool, yattn, sga_ref, sgb_ref, wpp_ref, wap_ref, merged_ref, rows):
    for c in range(N_GATE_CHUNKS):
        lo = c * GATE_CHUNK
        pp = _dot(ypool, wpp_ref[:, lo:lo + GATE_CHUNK])
        ap = _dot(yattn, wap_ref[:, lo:lo + GATE_CHUNK])
        merged = sga_ref[rows, lo:lo + GATE_CHUNK] * pp + sgb_ref[rows, lo:lo + GATE_CHUNK] * ap
        merged_ref[rows, lo:lo + GATE_CHUNK] = merged.astype(BF16)


BIG_WEIGHTS = (("w_in", D_MODEL, OFF_GB + D_MODEL), ("mix", len(POOL_WINDOWS) * POOL_GROUP_WIDTH, POOL_GROUP_WIDTH),
               ("wpp", POOL_WIDTH, D_MODEL), ("wap", Q_WIDTH, D_MODEL), ("wout", D_MODEL, D_MODEL),
               ("wg", D_MODEL, D_FF), ("wu", D_MODEL, D_FF), ("wd", D_FF, D_MODEL))
W_MIX, W_PP = 1, 2
W_FOLD = len(BIG_WEIGHTS)
HANDED_ON = tuple(w for w in range(len(BIG_WEIGHTS)) if w not in (W_MIX, W_PP)) + (W_FOLD,)
BF16_SHAPES = tuple((rows, cols) for _, rows, cols in BIG_WEIGHTS) + ((POOL_WIDTH, D_MODEL),)
CONVERT_ROWS = 128
STAGE_SLOTS = 4
STAGE_COLS = max(cols for _, _, cols in BIG_WEIGHTS)


def _weight_chunks():
    chunks = []
    for w, (_, nrows, cols) in enumerate(BIG_WEIGHTS):
        side_by_side = max(1, STAGE_COLS // cols)
        blocks = [(r0, min(CONVERT_ROWS, nrows - r0)) for r0 in range(0, nrows, CONVERT_ROWS)]
        for first in range(0, len(blocks), side_by_side):
            group = blocks[first:first + side_by_side]
            chunks.append((w, tuple((r0, rows, k * cols) for k, (r0, rows) in enumerate(group))))
    return tuple(chunks)


WEIGHT_CHUNKS = _weight_chunks()


def _writeback_copy(w, w_v, w_bf_hbm, out_sem):
    return pltpu.make_async_copy(w_v[w], w_bf_hbm[w], out_sem.at[w])


def _convert_weights(w_hbm, w_v, w_bf_hbm, stage, stage_sem, out_sem, mixf_s, params_ref):
    def chunk_copies(n):
        w, pieces = WEIGHT_CHUNKS[n]
        cols = BIG_WEIGHTS[w][2]
        slot = n % STAGE_SLOTS
        return [pltpu.make_async_copy(w_hbm[w].at[pl.ds(r0, rows), :],
                                      stage.at[slot, pl.ds(0, rows), pl.ds(c0, cols)], stage_sem.at[slot])
                for r0, rows, c0 in pieces]

    def convert(n):
        w, pieces = WEIGHT_CHUNKS[n]
        cols = BIG_WEIGHTS[w][2]
        for r0, rows, c0 in pieces:
            staged = stage[n % STAGE_SLOTS, 0:rows, c0:c0 + cols]
            if w == W_MIX:
                mixf_s[r0:r0 + rows, :] = staged
            else:
                w_v[w][r0:r0 + rows, :] = staged.astype(BF16)
        last_r0, last_rows, _ = pieces[-1]
        if last_r0 + last_rows == BIG_WEIGHTS[w][1] and w in HANDED_ON:
            _writeback_copy(w, w_v, w_bf_hbm, out_sem).start()

    ahead = STAGE_SLOTS - 1
    for m in range(ahead):
        for copy in chunk_copies(m):
            copy.start()
    for n in range(len(WEIGHT_CHUNKS)):
        if n + ahead < len(WEIGHT_CHUNKS):
            for copy in chunk_copies(n + ahead):
                copy.start()
        for copy in chunk_copies(n):
            copy.wait()
        convert(n)

    for g in range(len(POOL_WINDOWS)):
        grp = slice(g * POOL_GROUP_WIDTH, (g + 1) * POOL_GROUP_WIDTH)
        scale = params_ref[ROW_PSCALE:ROW_PSCALE + 1, grp]
        w_v[W_FOLD][grp, :] = _dot((mixf_s[grp, :] * scale).astype(BF16), w_v[W_PP][grp, :]).astype(BF16)
    _writeback_copy(W_FOLD, w_v, w_bf_hbm, out_sem).start()


def _prompt_kernel(*refs):
    nw = len(BIG_WEIGHTS)
    sinks_ref, x_ref, params_ref = refs[:3]
    w_hbm = refs[3:3 + nw]
    pos = 3 + nw
    outs = refs[pos:pos + 4]
    pos += 4
    w_bf_hbm = dict(zip(HANDED_ON, refs[pos:pos + len(HANDED_ON)]))
    pos += len(HANDED_ON)
    w_v = refs[pos:pos + nw + 1]
    pos += nw + 1
    stage, stage_sem, out_sem, mixf_s = refs[pos:pos + 4]
    scratch = refs[pos + 4:]
    w_in_ref, _, _, wap_ref, wout_ref, wg_ref, wu_ref, wd_ref, wfold_ref = w_v

    step = pl.program_id(0) * pl.num_programs(1) + pl.program_id(1)
    n_steps = pl.num_programs(0) * pl.num_programs(1)

    @pl.when(step == 0)
    def _():
        _convert_weights(w_hbm, w_v, w_bf_hbm, stage, stage_sem, out_sem, mixf_s, params_ref)

    def body(t, carry):
        _prompt_tile(t, sinks_ref, x_ref, params_ref, w_in_ref, wfold_ref, wap_ref, wout_ref, wg_ref, wu_ref, wd_ref,
                     *outs, *scratch)
        return carry
    lax.fori_loop(0, TILES_PER_STEP, body, 0)

    @pl.when(step == n_steps - 1)
    def _():
        for w in HANDED_ON:
            _writeback_copy(w, w_v, w_bf_hbm, out_sem).wait()


def _prompt_tile(t, sinks_ref, x_ref, params_ref, w_in_ref, wfold_ref, wap_ref, wout_ref, wg_ref, wu_ref, wd_ref,
                 y_ref, knew_ref, vnew_ref, pnew_ref,
                 h_s, u_s, q_s, kd_s, vd_s, s_s, p_s, ypool_s, yattn_s, sga_s, sgb_s, merged_s, x1_s, h2_s, act_s):
    T = SEQ_TILE
    R = SUB_TILE
    subs = [slice(r0, r0 + R) for r0 in range(0, T, R)]
    n_blocks = T // ATT_BLOCK
    j = pl.program_id(1) * TILES_PER_STEP + t
    rows_per_kv = GROUP * ATT_BLOCK

    def in_block(rows):
        return pl.ds(pl.multiple_of(t * T + rows.start, R), R)

    @pl.when(j == 0)
    def _():
        u_s[0:PREFIX, :] = jnp.zeros((PREFIX, POOL_WIDTH), F32)
        kd_s[:, 0:ATT_BLOCK, :] = jnp.zeros((N_KV_HEADS, ATT_BLOCK, KV_WIDTH), BF16)
        vd_s[:, 0:ATT_BLOCK, :] = jnp.zeros((N_KV_HEADS, ATT_BLOCK, KV_WIDTH), BF16)

    lane_r = lax.broadcasted_iota(jnp.int32, (R, LANES), 1)
    lane = lax.broadcasted_iota(jnp.int32, (ATT_BLOCK, LANES), 1)
    row = lax.broadcasted_iota(jnp.int32, (ATT_BLOCK, 2 * ATT_BLOCK), 0)
    col = lax.broadcasted_iota(jnp.int32, (ATT_BLOCK, 2 * ATT_BLOCK), 1)
    band = (col >= row) & (col <= row + WINDOW)
    first_lo = jnp.where(j == 0, ATT_BLOCK, 0)

    def norm1(rows):
        x = x_ref[0, in_block(rows), :]
        h_s[rows, :] = ((x * _rms_scale(x)) * _param(params_ref, ROW_N1, D_MODEL)).astype(BF16)

    def in_proj(rows):
        h = h_s[rows, :]
        u_s[PREFIX + rows.start:PREFIX + rows.stop, :] = _dot(h, w_in_ref[:, OFF_U:OFF_U + POOL_WIDTH])
        q = _dot(h, w_in_ref[:, OFF_Q:OFF_Q + Q_WIDTH])
        q_s[rows, :] = _head_norm(q, _param(params_ref, ROW_QN, Q_WIDTH)).astype(BF16)
        kv = _dot(h, w_in_ref[:, OFF_K:OFF_K + 2 * KV_WIDTH])
        kn = _head_norm(kv[:, 0:KV_WIDTH], _param(params_ref, ROW_KN, KV_WIDTH))
        v = kv[:, KV_WIDTH:]
        dst_rows = slice(ATT_BLOCK + rows.start, ATT_BLOCK + rows.stop)
        for src, dst in ((kn, kd_s), (v, vd_s)):
            swapped = pltpu.roll(src, HEAD_DIM, 1)
            dst[0, dst_rows, :] = jnp.where(lane_r < HEAD_DIM, src, swapped).astype(BF16)
            dst[1, dst_rows, :] = jnp.where(lane_r < HEAD_DIM, swapped, src).astype(BF16)
        if rows.stop == T:
            knew_ref[0] = kn[R - ATT_BLOCK:, :].T
            vnew_ref[0] = v[R - ATT_BLOCK:, :].T
            pnew_ref[0] = u_s[PREFIX + T - POOL_BUF:PREFIX + T, :]

    def pool(rows):
        pos1 = j * T + rows.start + lax.broadcasted_iota(jnp.int32, (R, 1), 0) + 1
        for g, w in enumerate(POOL_WINDOWS):
            cols = slice(g * POOL_GROUP_WIDTH, (g + 1) * POOL_GROUP_WIDTH)
            a = u_s[rows.start:rows.stop + PREFIX, cols]
            s = a
            shift = 1
            while shift < w:
                s = s + pltpu.roll(s, shift, 0)
                shift *= 2
            inv_cnt = 1.0 / jnp.minimum(pos1, w).astype(F32)
            ypool_s[rows, cols] = (s[PREFIX:, :] * inv_cnt - a[PREFIX:, :]).astype(BF16)

    def gates(rows, chunks):
        h = h_s[rows, :]
        for c in chunks:
            _gate_chunk(h, c, w_in_ref, sga_s, sgb_s, rows)

    def scores(b):
        r0 = b * ATT_BLOCK
        for c in range(N_KV_HEADS):
            parts = []
            for p in range(c * Q_COLS // N_KV_HEADS, (c + 1) * Q_COLS // N_KV_HEADS):
                qcol = q_s[r0:r0 + ATT_BLOCK, p * LANES:(p + 1) * LANES]
                parts.append(jnp.where(lane < HEAD_DIM, qcol, jnp.zeros_like(qcol)))
                parts.append(jnp.where(lane >= HEAD_DIM, qcol, jnp.zeros_like(qcol)))
            s_s[b % 2, c * rows_per_kv:(c + 1) * rows_per_kv, :] = _dot_nt(
                jnp.concatenate(parts, axis=0), kd_s[c, r0:r0 + 2 * ATT_BLOCK, :])

    def softmax(b):
        valid = band & (col >= first_lo) if b == 0 else band
        for head in range(N_HEADS):
            sh = jnp.where(valid, s_s[b % 2, head * ATT_BLOCK:(head + 1) * ATT_BLOCK, :], NEG)
            sink = sinks_ref[head]
            m = jnp.maximum(jnp.max(sh, axis=-1, keepdims=True), sink)
            p = jnp.exp(sh - m)
            denom = jnp.sum(p, axis=-1, keepdims=True) + jnp.exp(sink - m)
            p_s[b % 2, head * ATT_BLOCK:(head + 1) * ATT_BLOCK, :] = (p * (1.0 / denom)).astype(BF16)

    def weighted_values(b):
        r0 = b * ATT_BLOCK
        for c in range(N_KV_HEADS):
            o = _dot(p_s[b % 2, c * rows_per_kv:(c + 1) * rows_per_kv, :], vd_s[c, r0:r0 + 2 * ATT_BLOCK, :])
            for pp in range(Q_COLS // N_KV_HEADS):
                p_col = c * Q_COLS // N_KV_HEADS + pp
                o_lo = o[(2 * pp) * ATT_BLOCK:(2 * pp + 1) * ATT_BLOCK, :]
                o_hi = o[(2 * pp + 1) * ATT_BLOCK:(2 * pp + 2) * ATT_BLOCK, :]
                yattn_s[r0:r0 + ATT_BLOCK, p_col * LANES:(p_col + 1) * LANES] = (
                    jnp.where(lane < HEAD_DIM, o_lo, o_hi).astype(BF16))

    def out_proj(rows):
        _merge(ypool_s[rows, :], yattn_s[rows, :], sga_s, sgb_s, wfold_ref, wap_ref, merged_s, rows)
        x1 = x_ref[0, in_block(rows), :] + _dot(merged_s[rows, :], wout_ref[...])
        x1_s[rows, :] = x1
        h2_s[rows, :] = ((x1 * _rms_scale(x1)) * _param(params_ref, ROW_N2, D_MODEL)).astype(BF16)

    for rows in subs:
        norm1(rows)
    for rows in subs:
        in_proj(rows)
    pool(subs[0])
    gates(subs[0], range(N_GATE_CHUNKS))
    for rows in subs[1:]:
        pool(rows)
    later_gates = [(rows, c) for rows in subs[1:] for c in range(N_GATE_CHUNKS)]
    per_block = -(-len(later_gates) // n_blocks)
    scores(0)
    for b in range(n_blocks):
        if b + 1 < n_blocks:
            scores(b + 1)
        for rows, c in later_gates[b * per_block:(b + 1) * per_block]:
            gates(rows, [c])
        softmax(b)
        weighted_values(b)

    u_s[0:PREFIX, :] = u_s[T:T + PREFIX, :]
    kd_s[:, 0:ATT_BLOCK, :] = kd_s[:, T:T + ATT_BLOCK, :]
    vd_s[:, 0:ATT_BLOCK, :] = vd_s[:, T:T + ATT_BLOCK, :]

    for rows in subs:
        out_proj(rows)
    for rows in subs:
        _ffn_act(h2_s[rows, :], wg_ref, wu_ref, act_s, rows)
    for rows in subs:
        y_ref[0, in_block(rows), :] = x1_s[rows, :] + _dot(act_s[rows, :], wd_ref[...])


def _const_spec(shape):
    nd = len(shape)
    return pl.BlockSpec(shape, lambda *_: (0,) * nd, pipeline_mode=pl.Buffered(1))


def _prompt_call(x, sinks, params, big):
    B, S, _ = x.shape
    T = SEQ_TILE
    step_rows = T * TILES_PER_STEP
    assert tuple(b.shape for b in big) == BF16_SHAPES[:len(BIG_WEIGHTS)]
    in_specs = [pl.BlockSpec(memory_space=pltpu.SMEM),
                pl.BlockSpec((1, step_rows, D_MODEL), lambda b, j: (b, j, 0)),
                _const_spec(params.shape)]
    in_specs += [pl.BlockSpec(memory_space=pl.ANY) for _ in big]
    out_shape = (jax.ShapeDtypeStruct((B, S, D_MODEL), F32),
                 jax.ShapeDtypeStruct((B, KV_WIDTH, ATT_BLOCK), F32),
                 jax.ShapeDtypeStruct((B, KV_WIDTH, ATT_BLOCK), F32),
                 jax.ShapeDtypeStruct((B, POOL_BUF, POOL_WIDTH), F32))
    out_shape += tuple(jax.ShapeDtypeStruct(BF16_SHAPES[w], BF16) for w in HANDED_ON)
    out_specs = (pl.BlockSpec((1, step_rows, D_MODEL), lambda b, j: (b, j, 0)),
                 pl.BlockSpec((1, KV_WIDTH, ATT_BLOCK), lambda b, j: (b, 0, 0)),
                 pl.BlockSpec((1, KV_WIDTH, ATT_BLOCK), lambda b, j: (b, 0, 0)),
                 pl.BlockSpec((1, POOL_BUF, POOL_WIDTH), lambda b, j: (b, 0, 0)))
    out_specs += tuple(pl.BlockSpec(memory_space=pl.ANY) for _ in HANDED_ON)
    scratch = [pltpu.VMEM(shape, BF16) for shape in BF16_SHAPES]
    scratch += [pltpu.VMEM((STAGE_SLOTS, CONVERT_ROWS, STAGE_COLS), F32),
                pltpu.SemaphoreType.DMA((STAGE_SLOTS,)),
                pltpu.SemaphoreType.DMA((len(BF16_SHAPES),)),
                pltpu.VMEM(BF16_SHAPES[W_MIX], F32)]
    scratch += [pltpu.VMEM((T, D_MODEL), BF16),
               pltpu.VMEM((PREFIX + T, POOL_WIDTH), F32),
               pltpu.VMEM((T, Q_WIDTH), BF16),
               pltpu.VMEM((N_KV_HEADS, ATT_BLOCK + T, KV_WIDTH), BF16),
               pltpu.VMEM((N_KV_HEADS, ATT_BLOCK + T, KV_WIDTH), BF16),
               pltpu.VMEM((2, N_HEADS * ATT_BLOCK, 2 * ATT_BLOCK), F32),
               pltpu.VMEM((2, N_HEADS * ATT_BLOCK, 2 * ATT_BLOCK), BF16),
               pltpu.VMEM((T, POOL_WIDTH), BF16),
               pltpu.VMEM((T, Q_WIDTH), BF16),
               pltpu.VMEM((T, D_MODEL), F32),
               pltpu.VMEM((T, D_MODEL), F32),
               pltpu.VMEM((T, D_MODEL), BF16),
               pltpu.VMEM((T, D_MODEL), F32),
               pltpu.VMEM((T, D_MODEL), BF16),
               pltpu.VMEM((T, D_FF), BF16)]
    return pl.pallas_call(
        _prompt_kernel,
        grid=(B, S // step_rows),
        in_specs=in_specs,
        out_specs=out_specs,
        out_shape=out_shape,
        scratch_shapes=scratch,
        compiler_params=pltpu.CompilerParams(
            dimension_semantics=("arbitrary", "arbitrary"),
            vmem_limit_bytes=VMEM_LIMIT_BYTES),
        name="prompt_layer",
    )(sinks, x, params, *big)


def _sample_kernel(*refs):
    nh = len(HANDED_ON)
    n_in = 6
    (x_ref, ck_ref, cv_ref, sp_ref, sinkc_ref, params_ref) = refs[:n_in]
    w_hbm = refs[n_in:n_in + nh]
    pos = n_in + nh
    y_ref, knew_ref, vnew_ref, pnew_ref = refs[pos:pos + 4]
    pos += 4
    w_v = refs[pos:pos + nh]
    pos += nh
    (w_sem, u_s, kt_s, vt_s, q8_s, k8_s, v8_s, o8_s, ypre_s, sga_s, sgb_s, merged_s, act_s) = refs[pos:]
    w_in_ref, wap_ref, wout_ref, wg_ref, wu_ref, wd_ref, wfold_ref = w_v

    SB = SAMPLE_BLOCK
    N = x_ref.shape[0]
    nsteps = N // SB
    i = pl.program_id(0)
    lane = lax.broadcasted_iota(jnp.int32, (N, LANES), 1)
    all_rows = slice(0, N)

    def weight_copy(k):
        return pltpu.make_async_copy(w_hbm[k], w_v[k], w_sem.at[k])

    @pl.when(i == 0)
    def _():
        for k in range(nh):
            weight_copy(k).start()
        weight_copy(0).wait()
        x = x_ref[...]
        h = ((x * _rms_scale(x)) * _param(params_ref, ROW_N1, D_MODEL)).astype(BF16)
        u_s[...] = _dot(h, w_in_ref[:, OFF_U:OFF_U + POOL_WIDTH])
        for gc in range(N_GATE_CHUNKS):
            _gate_chunk(h, gc, w_in_ref, sga_s, sgb_s, all_rows)
        q = _head_norm(_dot(h, w_in_ref[:, OFF_Q:OFF_Q + Q_WIDTH]), _param(params_ref, ROW_QN, Q_WIDTH))
        kv = _dot(h, w_in_ref[:, OFF_K:OFF_K + 2 * KV_WIDTH])
        kn = _head_norm(kv[:, 0:KV_WIDTH], _param(params_ref, ROW_KN, KV_WIDTH))
        v = kv[:, KV_WIDTH:]
        kt_s[...] = kn.T
        vt_s[...] = v.T
        for head in range(N_HEADS):
            k8_s[pl.ds(head, N, stride=N_HEADS), :] = kn
            v8_s[pl.ds(head, N, stride=N_HEADS), :] = v
        for p in range(Q_COLS):
            qcol = q[:, p * LANES:(p + 1) * LANES]
            qswap = pltpu.roll(qcol, HEAD_DIM, 1)
            for e in range(HEADS_PER_COL):
                head = p * HEADS_PER_COL + e
                c = _kv_head(head)
                src = qcol if e == c else qswap
                keep = (lane < HEAD_DIM) if c == 0 else (lane >= HEAD_DIM)
                q8_s[pl.ds(head, N, stride=N_HEADS), :] = jnp.where(keep, src, 0.0)

    base = i * SB
    rows = pl.ds(pl.multiple_of(base, SB), SB)

    unew = u_s[rows, :]
    for g, w in enumerate(POOL_WINDOWS):
        cols = slice(g * POOL_GROUP_WIDTH, (g + 1) * POOL_GROUP_WIDTH)
        s = unew[:, cols]
        for r in range(POOL_BUF - (w - 1), POOL_BUF):
            s = s + sp_ref[r, :, cols]
        ypre_s[rows, cols] = (s * (1.0 / w) - unew[:, cols]).astype(BF16)
    pnew_ref[0:POOL_BUF - 1] = sp_ref[1:POOL_BUF]
    pnew_ref[POOL_BUF - 1] = unew

    hrows = pl.ds(pl.multiple_of(base * N_HEADS, SB * N_HEADS), SB * N_HEADS)
    lhs = q8_s[hrows, :]
    sc = jnp.concatenate(
        [_dot(lhs[bb * N_HEADS:(bb + 1) * N_HEADS, :].astype(BF16), ck_ref[bb].astype(BF16)) for bb in range(SB)],
        axis=0)
    sc_self = jnp.sum(lhs * k8_s[hrows, :], axis=-1, keepdims=True)
    sink = sinkc_ref[:, 0:1]
    m = jnp.maximum(jnp.maximum(jnp.max(sc, axis=-1, keepdims=True), sc_self), sink)
    p = jnp.exp(sc - m)
    p_self = jnp.exp(sc_self - m)
    denom = jnp.sum(p, axis=-1, keepdims=True) + p_self + jnp.exp(sink - m)
    inv = 1.0 / denom
    pn = p * inv
    o = jnp.concatenate(
        [_dot_nt(pn[bb * N_HEADS:(bb + 1) * N_HEADS, :].astype(BF16), cv_ref[bb].astype(BF16)) for bb in range(SB)],
        axis=0)
    o8_s[hrows, :] = o + (p_self * inv) * v8_s[hrows, :]

    lane_c = lax.broadcasted_iota(jnp.int32, (KV_WIDTH, WINDOW), 1)
    kt_all = kt_s[...]
    vt_all = vt_s[...]
    for bb in range(SB):
        put = (WINDOW - 1) - (base + bb)
        knew_ref[bb] = jnp.where(lane_c == WINDOW - 1, pltpu.roll(kt_all, put, 1),
                                 pltpu.roll(ck_ref[bb], WINDOW - 1, 1))
        vnew_ref[bb] = jnp.where(lane_c == WINDOW - 1, pltpu.roll(vt_all, put, 1),
                                 pltpu.roll(cv_ref[bb], WINDOW - 1, 1))

    @pl.when(i == nsteps - 1)
    def _():
        for k in range(1, nh):
            weight_copy(k).wait()
        ycols = []
        for p in range(Q_COLS):
            halves = []
            for e in range(HEADS_PER_COL):
                head = p * HEADS_PER_COL + e
                oh = o8_s[pl.ds(head, N, stride=N_HEADS), :]
                halves.append(oh if e == _kv_head(head) else pltpu.roll(oh, HEAD_DIM, 1))
            ycols.append(jnp.where(lane < HEAD_DIM, halves[0], halves[1]).astype(BF16))
        yattn = jnp.concatenate(ycols, axis=1)
        _merge(ypre_s[...], yattn, sga_s, sgb_s, wfold_ref, wap_ref, merged_s, all_rows)
        x1 = x_ref[...] + _dot(merged_s[...], wout_ref[...])
        h2 = ((x1 * _rms_scale(x1)) * _param(params_ref, ROW_N2, D_MODEL)).astype(BF16)
        _ffn_act(h2, wg_ref, wu_ref, act_s, all_rows)
        y_ref[...] = x1 + _dot(act_s[...], wd_ref[...])


def _sample_call(x, ck, cv, sp, sinkc, params, big):
    N = x.shape[0]
    SB = SAMPLE_BLOCK
    W = WINDOW
    assert tuple(b.shape for b in big) == tuple(BF16_SHAPES[w] for w in HANDED_ON)
    consts = (sinkc, params)
    in_specs = [_const_spec(x.shape),
                pl.BlockSpec((SB, KV_WIDTH, W), lambda i: (i, 0, 0)),
                pl.BlockSpec((SB, KV_WIDTH, W), lambda i: (i, 0, 0)),
                pl.BlockSpec((POOL_BUF, SB, POOL_WIDTH), lambda i: (0, i, 0))]
    in_specs += [_const_spec(w.shape) for w in consts]
    in_specs += [pl.BlockSpec(memory_space=pl.ANY) for _ in big]
    out_shape = (jax.ShapeDtypeStruct((N, D_MODEL), F32),
                 jax.ShapeDtypeStruct((N, KV_WIDTH, W), F32),
                 jax.ShapeDtypeStruct((N, KV_WIDTH, W), F32),
                 jax.ShapeDtypeStruct((POOL_BUF, N, POOL_WIDTH), F32))
    out_specs = (pl.BlockSpec((N, D_MODEL), lambda i: (0, 0)),
                 pl.BlockSpec((SB, KV_WIDTH, W), lambda i: (i, 0, 0)),
                 pl.BlockSpec((SB, KV_WIDTH, W), lambda i: (i, 0, 0)),
                 pl.BlockSpec((POOL_BUF, SB, POOL_WIDTH), lambda i: (0, i, 0)))
    scratch = [pltpu.VMEM(b.shape, BF16) for b in big]
    scratch += [pltpu.SemaphoreType.DMA((len(big),)),
                pltpu.VMEM((N, POOL_WIDTH), F32),
               pltpu.VMEM((KV_WIDTH, N), F32),
               pltpu.VMEM((KV_WIDTH, N), F32),
               pltpu.VMEM((N * N_HEADS, LANES), F32),
               pltpu.VMEM((N * N_HEADS, LANES), F32),
               pltpu.VMEM((N * N_HEADS, LANES), F32),
               pltpu.VMEM((N * N_HEADS, LANES), F32),
               pltpu.VMEM((N, POOL_WIDTH), BF16),
               pltpu.VMEM((N, D_MODEL), F32),
               pltpu.VMEM((N, D_MODEL), F32),
               pltpu.VMEM((N, D_MODEL), BF16),
               pltpu.VMEM((N, D_FF), BF16)]
    return pl.pallas_call(
        _sample_kernel,
        grid=(N // SB,),
        in_specs=in_specs,
        out_specs=out_specs,
        out_shape=out_shape,
        scratch_shapes=scratch,
        compiler_params=pltpu.CompilerParams(
            dimension_semantics=("arbitrary",),
            vmem_limit_bytes=VMEM_LIMIT_BYTES),
        name="sample_layer",
    )(x, ck, cv, sp, *consts, *big)


def _cache_to_device_order(c):
    n, w = c.shape[0], c.shape[1]
    return jnp.transpose(c, (0, 2, 3, 1)).reshape(n, KV_WIDTH, w)


def _cache_from_device_order(c):
    n, _, w = c.shape
    return jnp.transpose(c.reshape(n, N_KV_HEADS, HEAD_DIM, w), (0, 3, 1, 2))[None]


@jax.jit
def _forward(x_prompt, x_sample, cache_k, cache_v, state_pool, norm1, w_in, q_norm, k_norm, sinks,
             pool_mix_w, pool_scale, w_pool_proj, w_attn_proj, w_out, norm2, w_gate, w_up, w_down):
    depth = w_in.shape[0]
    assert depth == 1, "single-layer trunk"
    l = 0
    N = x_sample.shape[0]
    assert cache_k.shape[2] == WINDOW and x_sample.shape[1] == 1

    def row(v):
        return jnp.pad(v, (0, D_MODEL - v.shape[0]))

    rows = {ROW_N1: norm1[l], ROW_N2: norm2[l],
            ROW_QN: jnp.tile(q_norm[l], N_HEADS) * (HEAD_DIM ** -0.5),
            ROW_KN: jnp.tile(k_norm[l], N_KV_HEADS), ROW_PSCALE: pool_scale[l]}
    params = jnp.stack([row(rows[r]) if r in rows else jnp.zeros((D_MODEL,), F32) for r in range(PARAM_ROWS)])
    big = (w_in[l], pool_mix_w[l].reshape(len(POOL_WINDOWS) * POOL_GROUP_WIDTH, POOL_GROUP_WIDTH),
           w_pool_proj[l], w_attn_proj[l], w_out[l], w_gate[l], w_up[l], w_down[l])

    y_p, k_p, v_p, pool_p, *weights_bf16 = _prompt_call(x_prompt, sinks[l], params, big)

    sink_cols = jnp.broadcast_to(jnp.tile(sinks[l], SAMPLE_BLOCK)[:, None], (SAMPLE_BLOCK * N_HEADS, LANES))
    y_s, k_s, v_s, pool_s = _sample_call(
        x_sample.reshape(N, D_MODEL),
        _cache_to_device_order(cache_k[l]), _cache_to_device_order(cache_v[l]),
        jnp.transpose(state_pool[l], (1, 0, 2)), sink_cols, params, tuple(weights_bf16))

    return (y_p, y_s.reshape(N, 1, D_MODEL),
            _cache_from_device_order(k_p), _cache_from_device_order(v_p), pool_p[None],
            _cache_from_device_order(k_s), _cache_from_device_order(v_s),
            jnp.transpose(pool_s, (1, 0, 2))[None])


def kernel(x_prompt, x_sample, cache_k, cache_v, state_pool, norm1, w_in, q_norm, k_norm, sinks, pool_mix_w,
           pool_scale, w_pool_proj, w_attn_proj, w_out, norm2, w_gate, w_up, w_down):
    return _forward(x_prompt, x_sample, cache_k, cache_v, state_pool, norm1, w_in, q_norm, k_norm, sinks,
                    pool_mix_w, pool_scale, w_pool_proj, w_attn_proj, w_out, norm2, w_gate, w_up, w_down)
```

```python
import jax
import jax.numpy as jnp
from jax import lax
from jax.experimental import pallas as pl
from jax.experimental.pallas import tpu as pltpu

D_MODEL = 1024
POOL_WINDOWS = (2, 4, 8, 16)
POOL_GROUP_WIDTH = 128
POOL_WIDTH = 512
POOL_BUF = 15
N_HEADS = 8
N_KV_HEADS = 2
HEAD_DIM = 64
GROUP = N_HEADS // N_KV_HEADS
Q_WIDTH = 512
KV_WIDTH = 128
WINDOW = 128
D_FF = 2816
EPS = 1e-6
NEG = -1e30

LANES = 128
ATT_BLOCK = WINDOW
SEQ_TILE = 512
TILES_PER_STEP = 1
SUB_TILE = 256
FF_CHUNK = 256
GATE_CHUNK = 256
SAMPLE_BLOCK = 16
PREFIX = 16
VMEM_LIMIT_BYTES = 58 * 1024 * 1024
Q_COLS = Q_WIDTH // LANES
HEADS_PER_COL = LANES // HEAD_DIM
N_GATE_CHUNKS = D_MODEL // GATE_CHUNK

OFF_U = 0
OFF_Q = OFF_U + POOL_WIDTH
OFF_K = OFF_Q + Q_WIDTH
OFF_V = OFF_K + KV_WIDTH
OFF_GA = OFF_V + KV_WIDTH
OFF_GB = OFF_GA + D_MODEL

BF16 = jnp.bfloat16
F32 = jnp.float32

ROW_N1, ROW_N2, ROW_QN, ROW_KN, ROW_PSCALE, PARAM_ROWS = 0, 1, 2, 3, 4, 8


def _param(params_ref, row, width):
    return params_ref[row:row + 1, 0:width]


def _dot(a, b):
    return jnp.dot(a, b, preferred_element_type=F32)


def _dot_nt(a, b):
    return lax.dot_general(a, b, (((1,), (1,)), ((), ())), preferred_element_type=F32)


def _rms_scale(x):
    return lax.rsqrt(jnp.mean(x * x, axis=-1, keepdims=True) + EPS)


def _head_norm(x, gain):
    sq = x * x
    low = lax.broadcasted_iota(jnp.int32, (x.shape[0], LANES), 1) < HEAD_DIM
    cols = []
    for c in range(0, x.shape[1], LANES):
        blk = sq[:, c:c + LANES]
        ss_lo = jnp.sum(jnp.where(low, blk, 0.0), axis=-1, keepdims=True)
        ss_hi = jnp.sum(jnp.where(low, 0.0, blk), axis=-1, keepdims=True)
        cols.append(jnp.where(low, ss_lo, ss_hi))
    ss = jnp.concatenate(cols, axis=1) if len(cols) > 1 else cols[0]
    return (x * lax.rsqrt(ss * (1.0 / HEAD_DIM) + EPS)) * gain


def _kv_head(head):
    return head // GROUP


def _ffn_act(h2, wg_ref, wu_ref, act_ref, rows):
    for start in range(0, D_FF, FF_CHUNK):
        width = min(FF_CHUNK, D_FF - start)
        g = _dot(h2, wg_ref[:, start:start + width])
        u = _dot(h2, wu_ref[:, start:start + width])
        act_ref[rows, start:start + width] = (g * jax.nn.sigmoid(g) * u).astype(BF16)


def _gate_chunk(h, c, w_in_ref, sga_ref, sgb_ref, rows):
    lo = c * GATE_CHUNK
    sga_ref[rows, lo:lo + GATE_CHUNK] = jax.nn.sigmoid(_dot(h, w_in_ref[:, OFF_GA + lo:OFF_GA + lo + GATE_CHUNK]))
    sgb_ref[rows, lo:lo + GATE_CHUNK] = jax.nn.sigmoid(_dot(h, w_in_ref[:, OFF_GB + lo:OFF_GB + lo + GATE_CHUNK]))


def _merge(ypool, yattn, sga_ref, sgb_ref, wpp_ref, wap_ref, merged_ref, rows):
    for c in range(N_GATE_CHUNKS):
        lo = c * GATE_CHUNK
        pp = _dot(ypool, wpp_ref[:, lo:lo + GATE_CHUNK])
        ap = _dot(yattn, wap_ref[:, lo:lo + GATE_CHUNK])
        merged = sga_ref[rows, lo:lo + GATE_CHUNK] * pp + sgb_ref[rows, lo:lo + GATE_CHUNK] * ap
        merged_ref[rows, lo:lo + GATE_CHUNK] = merged.astype(BF16)


BIG_WEIGHTS = (("w_in", D_MODEL, OFF_GB + D_MODEL), ("mix", len(POOL_WINDOWS) * POOL_GROUP_WIDTH, POOL_GROUP_WIDTH),
               ("wpp", POOL_WIDTH, D_MODEL), ("wap", Q_WIDTH, D_MODEL), ("wout", D_MODEL, D_MODEL),
               ("wg", D_MODEL, D_FF), ("wu", D_MODEL, D_FF), ("wd", D_FF, D_MODEL))
W_MIX, W_PP = 1, 2
W_FOLD = len(BIG_WEIGHTS)
HANDED_ON = tuple(w for w in range(len(BIG_WEIGHTS)) if w not in (W_MIX, W_PP)) + (W_FOLD,)
BF16_SHAPES = tuple((rows, cols) for _, rows, cols in BIG_WEIGHTS) + ((POOL_WIDTH, D_MODEL),)
CONVERT_ROWS = 128
STAGE_SLOTS = 4
STAGE_COLS = max(cols for _, _, cols in BIG_WEIGHTS)


def _weight_chunks():
    chunks = []
    for w, (_, nrows, cols) in enumerate(BIG_WEIGHTS):
        side_by_side = max(1, STAGE_COLS // cols)
        blocks = [(r0, min(CONVERT_ROWS, nrows - r0)) for r0 in range(0, nrows, CONVERT_ROWS)]
        for first in range(0, len(blocks), side_by_side):
            group = blocks[first:first + side_by_side]
            chunks.append((w, tuple((r0, rows, k * cols) for k, (r0, rows) in enumerate(group))))
    return tuple(chunks)


WEIGHT_CHUNKS = _weight_chunks()


def _writeback_copy(w, w_v, w_bf_hbm, out_sem):
    return pltpu.make_async_copy(w_v[w], w_bf_hbm[w], out_sem.at[w])


def _convert_weights(w_hbm, w_v, stage, stage_sem, mixf_s, params_ref):
    def chunk_copies(n):
        w, pieces = WEIGHT_CHUNKS[n]
        cols = BIG_WEIGHTS[w][2]
        slot = n % STAGE_SLOTS
        return [pltpu.make_async_copy(w_hbm[w].at[pl.ds(r0, rows), :],
                                      stage.at[slot, pl.ds(0, rows), pl.ds(c0, cols)], stage_sem.at[slot])
                for r0, rows, c0 in pieces]

    def convert(n):
        w, pieces = WEIGHT_CHUNKS[n]
        cols = BIG_WEIGHTS[w][2]
        for r0, rows, c0 in pieces:
            staged = stage[n % STAGE_SLOTS, 0:rows, c0:c0 + cols]
            if w == W_MIX:
                mixf_s[r0:r0 + rows, :] = staged
            else:
                w_v[w][r0:r0 + rows, :] = staged.astype(BF16)

    ahead = STAGE_SLOTS - 1
    for m in range(ahead):
        for copy in chunk_copies(m):
            copy.start()
    for n in range(len(WEIGHT_CHUNKS)):
        if n + ahead < len(WEIGHT_CHUNKS):
            for copy in chunk_copies(n + ahead):
                copy.start()
        for copy in chunk_copies(n):
            copy.wait()
        convert(n)

    for g in range(len(POOL_WINDOWS)):
        grp = slice(g * POOL_GROUP_WIDTH, (g + 1) * POOL_GROUP_WIDTH)
        scale = params_ref[ROW_PSCALE:ROW_PSCALE + 1, grp]
        w_v[W_FOLD][grp, :] = _dot((mixf_s[grp, :] * scale).astype(BF16), w_v[W_PP][grp, :]).astype(BF16)


def _prompt_kernel(*refs):
    nw = len(BIG_WEIGHTS)
    sinks_ref, x_ref, params_ref = refs[:3]
    w_hbm = refs[3:3 + nw]
    pos = 3 + nw
    outs = refs[pos:pos + 4]
    pos += 4
    w_bf_hbm = dict(zip(HANDED_ON, refs[pos:pos + len(HANDED_ON)]))
    pos += len(HANDED_ON)
    w_v = refs[pos:pos + nw + 1]
    pos += nw + 1
    stage, stage_sem, out_sem, mixf_s = refs[pos:pos + 4]
    scratch = refs[pos + 4:]
    w_in_ref, _, _, wap_ref, wout_ref, wg_ref, wu_ref, wd_ref, wfold_ref = w_v

    step = pl.program_id(0) * pl.num_programs(1) + pl.program_id(1)
    n_steps = pl.num_programs(0) * pl.num_programs(1)

    @pl.when(step == 0)
    def _():
        _convert_weights(w_hbm, w_v, stage, stage_sem, mixf_s, params_ref)

    @pl.when(step == 1)
    def _():
        for w in HANDED_ON:
            _writeback_copy(w, w_v, w_bf_hbm, out_sem).start()

    def body(t, carry):
        _prompt_tile(t, sinks_ref, x_ref, params_ref, w_in_ref, wfold_ref, wap_ref, wout_ref, wg_ref, wu_ref, wd_ref,
                     *outs, *scratch)
        return carry
    lax.fori_loop(0, TILES_PER_STEP, body, 0)

    @pl.when(step == n_steps - 1)
    def _():
        for w in HANDED_ON:
            _writeback_copy(w, w_v, w_bf_hbm, out_sem).wait()


def _prompt_tile(t, sinks_ref, x_ref, params_ref, w_in_ref, wfold_ref, wap_ref, wout_ref, wg_ref, wu_ref, wd_ref,
                 y_ref, knew_ref, vnew_ref, pnew_ref,
                 h_s, u_s, q_s, kd_s, vd_s, s_s, p_s, ypool_s, yattn_s, sga_s, sgb_s, merged_s, x1_s, h2_s, act_s):
    T = SEQ_TILE
    R = SUB_TILE
    subs = [slice(r0, r0 + R) for r0 in range(0, T, R)]
    n_blocks = T // ATT_BLOCK
    j = pl.program_id(1) * TILES_PER_STEP + t
    rows_per_kv = GROUP * ATT_BLOCK

    def in_block(rows):
        return pl.ds(pl.multiple_of(t * T + rows.start, R), R)

    @pl.when(j == 0)
    def _():
        u_s[0:PREFIX, :] = jnp.zeros((PREFIX, POOL_WIDTH), F32)
        kd_s[:, 0:ATT_BLOCK, :] = jnp.zeros((N_KV_HEADS, ATT_BLOCK, KV_WIDTH), BF16)
        vd_s[:, 0:ATT_BLOCK, :] = jnp.zeros((N_KV_HEADS, ATT_BLOCK, KV_WIDTH), BF16)

    lane_r = lax.broadcasted_iota(jnp.int32, (R, LANES), 1)
    lane = lax.broadcasted_iota(jnp.int32, (ATT_BLOCK, LANES), 1)
    row = lax.broadcasted_iota(jnp.int32, (ATT_BLOCK, 2 * ATT_BLOCK), 0)
    col = lax.broadcasted_iota(jnp.int32, (ATT_BLOCK, 2 * ATT_BLOCK), 1)
    band = (col >= row) & (col <= row + WINDOW)
    first_lo = jnp.where(j == 0, ATT_BLOCK, 0)

    def norm1(rows):
        x = x_ref[0, in_block(rows), :]
        h_s[rows, :] = ((x * _rms_scale(x)) * _param(params_ref, ROW_N1, D_MODEL)).astype(BF16)

    def in_proj(rows):
        h = h_s[rows, :]
        u_s[PREFIX + rows.start:PREFIX + rows.stop, :] = _dot(h, w_in_ref[:, OFF_U:OFF_U + POOL_WIDTH])
        q = _dot(h, w_in_ref[:, OFF_Q:OFF_Q + Q_WIDTH])
        q_s[rows, :] = _head_norm(q, _param(params_ref, ROW_QN, Q_WIDTH)).astype(BF16)
        kv = _dot(h, w_in_ref[:, OFF_K:OFF_K + 2 * KV_WIDTH])
        kn = _head_norm(kv[:, 0:KV_WIDTH], _param(params_ref, ROW_KN, KV_WIDTH))
        v = kv[:, KV_WIDTH:]
        dst_rows = slice(ATT_BLOCK + rows.start, ATT_BLOCK + rows.stop)
        for src, dst in ((kn, kd_s), (v, vd_s)):
            swapped = pltpu.roll(src, HEAD_DIM, 1)
            dst[0, dst_rows, :] = jnp.where(lane_r < HEAD_DIM, src, swapped).astype(BF16)
            dst[1, dst_rows, :] = jnp.where(lane_r < HEAD_DIM, swapped, src).astype(BF16)
        if rows.stop == T:
            knew_ref[0] = kn[R - ATT_BLOCK:, :].T
            vnew_ref[0] = v[R - ATT_BLOCK:, :].T
            pnew_ref[0] = u_s[PREFIX + T - POOL_BUF:PREFIX + T, :]

    def pool(rows):
        pos1 = j * T + rows.start + lax.broadcasted_iota(jnp.int32, (R, 1), 0) + 1
        for g, w in enumerate(POOL_WINDOWS):
            cols = slice(g * POOL_GROUP_WIDTH, (g + 1) * POOL_GROUP_WIDTH)
            a = u_s[rows.start:rows.stop + PREFIX, cols]
            s = a
            shift = 1
            while shift < w:
                s = s + pltpu.roll(s, shift, 0)
                shift *= 2
            inv_cnt = 1.0 / jnp.minimum(pos1, w).astype(F32)
            ypool_s[rows, cols] = (s[PREFIX:, :] * inv_cnt - a[PREFIX:, :]).astype(BF16)

    def gates(rows, chunks):
        h = h_s[rows, :]
        for c in chunks:
            _gate_chunk(h, c, w_in_ref, sga_s, sgb_s, rows)

    def scores(b):
        r0 = b * ATT_BLOCK
        for c in range(N_KV_HEADS):
            parts = []
            for p in range(c * Q_COLS // N_KV_HEADS, (c + 1) * Q_COLS // N_KV_HEADS):
                qcol = q_s[r0:r0 + ATT_BLOCK, p * LANES:(p + 1) * LANES]
                parts.append(jnp.where(lane < HEAD_DIM, qcol, jnp.zeros_like(qcol)))
                parts.append(jnp.where(lane >= HEAD_DIM, qcol, jnp.zeros_like(qcol)))
            s_s[b % 2, c * rows_per_kv:(c + 1) * rows_per_kv, :] = _dot_nt(
                jnp.concatenate(parts, axis=0), kd_s[c, r0:r0 + 2 * ATT_BLOCK, :])

    def softmax(b):
        valid = band & (col >= first_lo) if b == 0 else band
        for head in range(N_HEADS):
            sh = jnp.where(valid, s_s[b % 2, head * ATT_BLOCK:(head + 1) * ATT_BLOCK, :], NEG)
            sink = sinks_ref[head]
            m = jnp.maximum(jnp.max(sh, axis=-1, keepdims=True), sink)
            p = jnp.exp(sh - m)
            denom = jnp.sum(p, axis=-1, keepdims=True) + jnp.exp(sink - m)
            p_s[b % 2, head * ATT_BLOCK:(head + 1) * ATT_BLOCK, :] = (p * (1.0 / denom)).astype(BF16)

    def weighted_values(b):
        r0 = b * ATT_BLOCK
        for c in range(N_KV_HEADS):
            o = _dot(p_s[b % 2, c * rows_per_kv:(c + 1) * rows_per_kv, :], vd_s[c, r0:r0 + 2 * ATT_BLOCK, :])
            for pp in range(Q_COLS // N_KV_HEADS):
                p_col = c * Q_COLS // N_KV_HEADS + pp
                o_lo = o[(2 * pp) * ATT_BLOCK:(2 * pp + 1) * ATT_BLOCK, :]
                o_hi = o[(2 * pp + 1) * ATT_BLOCK:(2 * pp + 2) * ATT_BLOCK, :]
                yattn_s[r0:r0 + ATT_BLOCK, p_col * LANES:(p_col + 1) * LANES] = (
                    jnp.where(lane < HEAD_DIM, o_lo, o_hi).astype(BF16))

    def out_proj(rows):
        _merge(ypool_s[rows, :], yattn_s[rows, :], sga_s, sgb_s, wfold_ref, wap_ref, merged_s, rows)
        x1 = x_ref[0, in_block(rows), :] + _dot(merged_s[rows, :], wout_ref[...])
        x1_s[rows, :] = x1
        h2_s[rows, :] = ((x1 * _rms_scale(x1)) * _param(params_ref, ROW_N2, D_MODEL)).astype(BF16)

    for rows in subs:
        norm1(rows)
    for rows in subs:
        in_proj(rows)
    pool(subs[0])
    gates(subs[0], range(N_GATE_CHUNKS))
    for rows in subs[1:]:
        pool(rows)
    later_gates = [(rows, c) for rows in subs[1:] for c in range(N_GATE_CHUNKS)]
    per_block = -(-len(later_gates) // n_blocks)
    scores(0)
    for b in range(n_blocks):
        if b + 1 < n_blocks:
            scores(b + 1)
        for rows, c in later_gates[b * per_block:(b + 1) * per_block]:
            gates(rows, [c])
        softmax(b)
        weighted_values(b)

    u_s[0:PREFIX, :] = u_s[T:T + PREFIX, :]
    kd_s[:, 0:ATT_BLOCK, :] = kd_s[:, T:T + ATT_BLOCK, :]
    vd_s[:, 0:ATT_BLOCK, :] = vd_s[:, T:T + ATT_BLOCK, :]

    for rows in subs:
        out_proj(rows)
    for rows in subs:
        _ffn_act(h2_s[rows, :], wg_ref, wu_ref, act_s, rows)
    for rows in subs:
        y_ref[0, in_block(rows), :] = x1_s[rows, :] + _dot(act_s[rows, :], wd_ref[...])


def _const_spec(shape):
    nd = len(shape)
    return pl.BlockSpec(shape, lambda *_: (0,) * nd, pipeline_mode=pl.Buffered(1))


def _prompt_call(x, sinks, params, big):
    B, S, _ = x.shape
    T = SEQ_TILE
    step_rows = T * TILES_PER_STEP
    assert tuple(b.shape for b in big) == BF16_SHAPES[:len(BIG_WEIGHTS)]
    assert B * (S // step_rows) >= 2, "the bf16 write-back starts on the second grid step"
    in_specs = [pl.BlockSpec(memory_space=pltpu.SMEM),
                pl.BlockSpec((1, step_rows, D_MODEL), lambda b, j: (b, j, 0)),
                _const_spec(params.shape)]
    in_specs += [pl.BlockSpec(memory_space=pl.ANY) for _ in big]
    out_shape = (jax.ShapeDtypeStruct((B, S, D_MODEL), F32),
                 jax.ShapeDtypeStruct((B, KV_WIDTH, ATT_BLOCK), F32),
                 jax.ShapeDtypeStruct((B, KV_WIDTH, ATT_BLOCK), F32),
                 jax.ShapeDtypeStruct((B, POOL_BUF, POOL_WIDTH), F32))
    out_shape += tuple(jax.ShapeDtypeStruct(BF16_SHAPES[w], BF16) for w in HANDED_ON)
    out_specs = (pl.BlockSpec((1, step_rows, D_MODEL), lambda b, j: (b, j, 0)),
                 pl.BlockSpec((1, KV_WIDTH, ATT_BLOCK), lambda b, j: (b, 0, 0)),
                 pl.BlockSpec((1, KV_WIDTH, ATT_BLOCK), lambda b, j: (b, 0, 0)),
                 pl.BlockSpec((1, POOL_BUF, POOL_WIDTH), lambda b, j: (b, 0, 0)))
    out_specs += tuple(pl.BlockSpec(memory_space=pl.ANY) for _ in HANDED_ON)
    scratch = [pltpu.VMEM(shape, BF16) for shape in BF16_SHAPES]
    scratch += [pltpu.VMEM((STAGE_SLOTS, CONVERT_ROWS, STAGE_COLS), F32),
                pltpu.SemaphoreType.DMA((STAGE_SLOTS,)),
                pltpu.SemaphoreType.DMA((len(BF16_SHAPES),)),
                pltpu.VMEM(BF16_SHAPES[W_MIX], F32)]
    scratch += [pltpu.VMEM((T, D_MODEL), BF16),
               pltpu.VMEM((PREFIX + T, POOL_WIDTH), F32),
               pltpu.VMEM((T, Q_WIDTH), BF16),
               pltpu.VMEM((N_KV_HEADS, ATT_BLOCK + T, KV_WIDTH), BF16),
               pltpu.VMEM((N_KV_HEADS, ATT_BLOCK + T, KV_WIDTH), BF16),
               pltpu.VMEM((2, N_HEADS * ATT_BLOCK, 2 * ATT_BLOCK), F32),
               pltpu.VMEM((2, N_HEADS * ATT_BLOCK, 2 * ATT_BLOCK), BF16),
               pltpu.VMEM((T, POOL_WIDTH), BF16),
               pltpu.VMEM((T, Q_WIDTH), BF16),
               pltpu.VMEM((T, D_MODEL), F32),
               pltpu.VMEM((T, D_MODEL), F32),
               pltpu.VMEM((T, D_MODEL), BF16),
               pltpu.VMEM((T, D_MODEL), F32),
               pltpu.VMEM((T, D_MODEL), BF16),
               pltpu.VMEM((T, D_FF), BF16)]
    return pl.pallas_call(
        _prompt_kernel,
        grid=(B, S // step_rows),
        in_specs=in_specs,
        out_specs=out_specs,
        out_shape=out_shape,
        scratch_shapes=scratch,
        compiler_params=pltpu.CompilerParams(
            dimension_semantics=("arbitrary", "arbitrary"),
            vmem_limit_bytes=VMEM_LIMIT_BYTES),
        name="prompt_layer",
    )(sinks, x, params, *big)


def _sample_kernel(*refs):
    nh = len(HANDED_ON)
    n_in = 6
    (x_ref, ck_ref, cv_ref, sp_ref, sinkc_ref, params_ref) = refs[:n_in]
    w_hbm = refs[n_in:n_in + nh]
    pos = n_in + nh
    y_ref, knew_ref, vnew_ref, pnew_ref = refs[pos:pos + 4]
    pos += 4
    w_v = refs[pos:pos + nh]
    pos += nh
    (w_sem, u_s, kt_s, vt_s, q8_s, k8_s, v8_s, o8_s, ypre_s, sga_s, sgb_s, merged_s, act_s) = refs[pos:]
    w_in_ref, wap_ref, wout_ref, wg_ref, wu_ref, wd_ref, wfold_ref = w_v

    SB = SAMPLE_BLOCK
    N = x_ref.shape[0]
    nsteps = N // SB
    i = pl.program_id(0)
    lane = lax.broadcasted_iota(jnp.int32, (N, LANES), 1)
    all_rows = slice(0, N)

    def weight_copy(k):
        return pltpu.make_async_copy(w_hbm[k], w_v[k], w_sem.at[k])

    @pl.when(i == 0)
    def _():
        for k in range(nh):
            weight_copy(k).start()
        weight_copy(0).wait()
        x = x_ref[...]
        h = ((x * _rms_scale(x)) * _param(params_ref, ROW_N1, D_MODEL)).astype(BF16)
        u_s[...] = _dot(h, w_in_ref[:, OFF_U:OFF_U + POOL_WIDTH])
        for gc in range(N_GATE_CHUNKS):
            _gate_chunk(h, gc, w_in_ref, sga_s, sgb_s, all_rows)
        q = _head_norm(_dot(h, w_in_ref[:, OFF_Q:OFF_Q + Q_WIDTH]), _param(params_ref, ROW_QN, Q_WIDTH))
        kv = _dot(h, w_in_ref[:, OFF_K:OFF_K + 2 * KV_WIDTH])
        kn = _head_norm(kv[:, 0:KV_WIDTH], _param(params_ref, ROW_KN, KV_WIDTH))
        v = kv[:, KV_WIDTH:]
        kt_s[...] = kn.T
        vt_s[...] = v.T
        for head in range(N_HEADS):
            k8_s[pl.ds(head, N, stride=N_HEADS), :] = kn
            v8_s[pl.ds(head, N, stride=N_HEADS), :] = v
        for p in range(Q_COLS):
            qcol = q[:, p * LANES:(p + 1) * LANES]
            qswap = pltpu.roll(qcol, HEAD_DIM, 1)
            for e in range(HEADS_PER_COL):
                head = p * HEADS_PER_COL + e
                c = _kv_head(head)
                src = qcol if e == c else qswap
                keep = (lane < HEAD_DIM) if c == 0 else (lane >= HEAD_DIM)
                q8_s[pl.ds(head, N, stride=N_HEADS), :] = jnp.where(keep, src, 0.0)

    base = i * SB
    rows = pl.ds(pl.multiple_of(base, SB), SB)

    unew = u_s[rows, :]
    for g, w in enumerate(POOL_WINDOWS):
        cols = slice(g * POOL_GROUP_WIDTH, (g + 1) * POOL_GROUP_WIDTH)
        s = unew[:, cols]
        for r in range(POOL_BUF - (w - 1), POOL_BUF):
            s = s + sp_ref[r, :, cols]
        ypre_s[rows, cols] = (s * (1.0 / w) - unew[:, cols]).astype(BF16)
    pnew_ref[0:POOL_BUF - 1] = sp_ref[1:POOL_BUF]
    pnew_ref[POOL_BUF - 1] = unew

    hrows = pl.ds(pl.multiple_of(base * N_HEADS, SB * N_HEADS), SB * N_HEADS)
    lhs = q8_s[hrows, :]
    sc = jnp.concatenate(
        [_dot(lhs[bb * N_HEADS:(bb + 1) * N_HEADS, :].astype(BF16), ck_ref[bb].astype(BF16)) for bb in range(SB)],
        axis=0)
    sc_self = jnp.sum(lhs * k8_s[hrows, :], axis=-1, keepdims=True)
    sink = sinkc_ref[:, 0:1]
    m = jnp.maximum(jnp.maximum(jnp.max(sc, axis=-1, keepdims=True), sc_self), sink)
    p = jnp.exp(sc - m)
    p_self = jnp.exp(sc_self - m)
    denom = jnp.sum(p, axis=-1, keepdims=True) + p_self + jnp.exp(sink - m)
    inv = 1.0 / denom
    pn = p * inv
    o = jnp.concatenate(
        [_dot_nt(pn[bb * N_HEADS:(bb + 1) * N_HEADS, :].astype(BF16), cv_ref[bb].astype(BF16)) for bb in range(SB)],
        axis=0)
    o8_s[hrows, :] = o + (p_self * inv) * v8_s[hrows, :]

    lane_c = lax.broadcasted_iota(jnp.int32, (KV_WIDTH, WINDOW), 1)
    kt_all = kt_s[...]
    vt_all = vt_s[...]
    for bb in range(SB):
        put = (WINDOW - 1) - (base + bb)
        knew_ref[bb] = jnp.where(lane_c == WINDOW - 1, pltpu.roll(kt_all, put, 1),
                                 pltpu.roll(ck_ref[bb], WINDOW - 1, 1))
        vnew_ref[bb] = jnp.where(lane_c == WINDOW - 1, pltpu.roll(vt_all, put, 1),
                                 pltpu.roll(cv_ref[bb], WINDOW - 1, 1))

    @pl.when(i == nsteps - 1)
    def _():
        for k in range(1, nh):
            weight_copy(k).wait()
        ycols = []
        for p in range(Q_COLS):
            halves = []
            for e in range(HEADS_PER_COL):
                head = p * HEADS_PER_COL + e
                oh = o8_s[pl.ds(head, N, stride=N_HEADS), :]
                halves.append(oh if e == _kv_head(head) else pltpu.roll(oh, HEAD_DIM, 1))
            ycols.append(jnp.where(lane < HEAD_DIM, halves[0], halves[1]).astype(BF16))
        yattn = jnp.concatenate(ycols, axis=1)
        _merge(ypre_s[...], yattn, sga_s, sgb_s, wfold_ref, wap_ref, merged_s, all_rows)
        x1 = x_ref[...] + _dot(merged_s[...], wout_ref[...])
        h2 = ((x1 * _rms_scale(x1)) * _param(params_ref, ROW_N2, D_MODEL)).astype(BF16)
        _ffn_act(h2, wg_ref, wu_ref, act_s, all_rows)
        y_ref[...] = x1 + _dot(act_s[...], wd_ref[...])


def _sample_call(x, ck, cv, sp, sinkc, params, big):
    N = x.shape[0]
    SB = SAMPLE_BLOCK
    W = WINDOW
    assert tuple(b.shape for b in big) == tuple(BF16_SHAPES[w] for w in HANDED_ON)
    consts = (sinkc, params)
    in_specs = [_const_spec(x.shape),
                pl.BlockSpec((SB, KV_WIDTH, W), lambda i: (i, 0, 0)),
                pl.BlockSpec((SB, KV_WIDTH, W), lambda i: (i, 0, 0)),
                pl.BlockSpec((POOL_BUF, SB, POOL_WIDTH), lambda i: (0, i, 0))]
    in_specs += [_const_spec(w.shape) for w in consts]
    in_specs += [pl.BlockSpec(memory_space=pl.ANY) for _ in big]
    out_shape = (jax.ShapeDtypeStruct((N, D_MODEL), F32),
                 jax.ShapeDtypeStruct((N, KV_WIDTH, W), F32),
                 jax.ShapeDtypeStruct((N, KV_WIDTH, W), F32),
                 jax.ShapeDtypeStruct((POOL_BUF, N, POOL_WIDTH), F32))
    out_specs = (pl.BlockSpec((N, D_MODEL), lambda i: (0, 0)),
                 pl.BlockSpec((SB, KV_WIDTH, W), lambda i: (i, 0, 0)),
                 pl.BlockSpec((SB, KV_WIDTH, W), lambda i: (i, 0, 0)),
                 pl.BlockSpec((POOL_BUF, SB, POOL_WIDTH), lambda i: (0, i, 0)))
    scratch = [pltpu.VMEM(b.shape, BF16) for b in big]
    scratch += [pltpu.SemaphoreType.DMA((len(big),)),
                pltpu.VMEM((N, POOL_WIDTH), F32),
               pltpu.VMEM((KV_WIDTH, N), F32),
               pltpu.VMEM((KV_WIDTH, N), F32),
               pltpu.VMEM((N * N_HEADS, LANES), F32),
               pltpu.VMEM((N * N_HEADS, LANES), F32),
               pltpu.VMEM((N * N_HEADS, LANES), F32),
               pltpu.VMEM((N * N_HEADS, LANES), F32),
               pltpu.VMEM((N, POOL_WIDTH), BF16),
               pltpu.VMEM((N, D_MODEL), F32),
               pltpu.VMEM((N, D_MODEL), F32),
               pltpu.VMEM((N, D_MODEL), BF16),
               pltpu.VMEM((N, D_FF), BF16)]
    return pl.pallas_call(
        _sample_kernel,
        grid=(N // SB,),
        in_specs=in_specs,
        out_specs=out_specs,
        out_shape=out_shape,
        scratch_shapes=scratch,
        compiler_params=pltpu.CompilerParams(
            dimension_semantics=("arbitrary",),
            vmem_limit_bytes=VMEM_LIMIT_BYTES),
        name="sample_layer",
    )(x, ck, cv, sp, *consts, *big)


def _cache_to_device_order(c):
    n, w = c.shape[0], c.shape[1]
    return jnp.transpose(c, (0, 2, 3, 1)).reshape(n, KV_WIDTH, w)


def _cache_from_device_order(c):
    n, _, w = c.shape
    return jnp.transpose(c.reshape(n, N_KV_HEADS, HEAD_DIM, w), (0, 3, 1, 2))[None]


@jax.jit
def _forward(x_prompt, x_sample, cache_k, cache_v, state_pool, norm1, w_in, q_norm, k_norm, sinks,
             pool_mix_w, pool_scale, w_pool_proj, w_attn_proj, w_out, norm2, w_gate, w_up, w_down):
    depth = w_in.shape[0]
    assert depth == 1, "single-layer trunk"
    l = 0
    N = x_sample.shape[0]
    assert cache_k.shape[2] == WINDOW and x_sample.shape[1] == 1

    def row(v):
        return jnp.pad(v, (0, D_MODEL - v.shape[0]))

    rows = {ROW_N1: norm1[l], ROW_N2: norm2[l],
            ROW_QN: jnp.tile(q_norm[l], N_HEADS) * (HEAD_DIM ** -0.5),
            ROW_KN: jnp.tile(k_norm[l], N_KV_HEADS), ROW_PSCALE: pool_scale[l]}
    params = jnp.stack([row(rows[r]) if r in rows else jnp.zeros((D_MODEL,), F32) for r in range(PARAM_ROWS)])
    big = (w_in[l], pool_mix_w[l].reshape(len(POOL_WINDOWS) * POOL_GROUP_WIDTH, POOL_GROUP_WIDTH),
           w_pool_proj[l], w_attn_proj[l], w_out[l], w_gate[l], w_up[l], w_down[l])

    y_p, k_p, v_p, pool_p, *weights_bf16 = _prompt_call(x_prompt, sinks[l], params, big)

    sink_cols = jnp.broadcast_to(jnp.tile(sinks[l], SAMPLE_BLOCK)[:, None], (SAMPLE_BLOCK * N_HEADS, LANES))
    y_s, k_s, v_s, pool_s = _sample_call(
        x_sample.reshape(N, D_MODEL),
        _cache_to_device_order(cache_k[l]), _cache_to_device_order(cache_v[l]),
        jnp.transpose(state_pool[l], (1, 0, 2)), sink_cols, params, tuple(weights_bf16))

    return (y_p, y_s.reshape(N, 1, D_MODEL),
            _cache_from_device_order(k_p), _cache_from_device_order(v_p), pool_p[None],
            _cache_from_device_order(k_s), _cache_from_device_order(v_s),
            jnp.transpose(pool_s, (1, 0, 2))[None])


def kernel(x_prompt, x_sample, cache_k, cache_v, state_pool, norm1, w_in, q_norm, k_norm, sinks, pool_mix_w,
           pool_scale, w_pool_proj, w_attn_proj, w_out, norm2, w_gate, w_up, w_down):
    return _forward(x_prompt, x_sample, cache_k, cache_v, state_pool, norm1, w_in, q_norm, k_norm, sinks,
                    pool_mix_w, pool_scale, w_pool_proj, w_attn_proj, w_out, norm2, w_gate, w_up, w_down)
```

```python
import jax
import jax.numpy as jnp
from jax import lax
from jax.experimental import pallas as pl
from jax.experimental.pallas import tpu as pltpu

D_MODEL = 1024
POOL_WINDOWS = (2, 4, 8, 16)
POOL_GROUP_WIDTH = 128
POOL_WIDTH = 512
POOL_BUF = 15
N_HEADS = 8
N_KV_HEADS = 2
HEAD_DIM = 64
GROUP = N_HEADS // N_KV_HEADS
Q_WIDTH = 512
KV_WIDTH = 128
WINDOW = 128
D_FF = 2816
EPS = 1e-6
NEG = -1e30

LANES = 128
ATT_BLOCK = WINDOW
SEQ_TILE = 512
TILES_PER_STEP = 1
SUB_TILE = 256
FF_CHUNK = 256
GATE_CHUNK = 256
SAMPLE_BLOCK = 16
PREFIX = 16
VMEM_LIMIT_BYTES = 58 * 1024 * 1024
Q_COLS = Q_WIDTH // LANES
HEADS_PER_COL = LANES // HEAD_DIM
N_GATE_CHUNKS = D_MODEL // GATE_CHUNK

OFF_U = 0
OFF_Q = OFF_U + POOL_WIDTH
OFF_K = OFF_Q + Q_WIDTH
OFF_V = OFF_K + KV_WIDTH
OFF_GA = OFF_V + KV_WIDTH
OFF_GB = OFF_GA + D_MODEL

BF16 = jnp.bfloat16
F32 = jnp.float32

ROW_N1, ROW_N2, ROW_QN, ROW_KN, ROW_PSCALE, PARAM_ROWS = 0, 1, 2, 3, 4, 8


def _param(params_ref, row, width):
    return params_ref[row:row + 1, 0:width]


def _dot(a, b):
    return jnp.dot(a, b, preferred_element_type=F32)


def _dot_nt(a, b):
    return lax.dot_general(a, b, (((1,), (1,)), ((), ())), preferred_element_type=F32)


def _rms_scale(x):
    return lax.rsqrt(jnp.mean(x * x, axis=-1, keepdims=True) + EPS)


def _head_norm(x, gain):
    sq = x * x
    low = lax.broadcasted_iota(jnp.int32, (x.shape[0], LANES), 1) < HEAD_DIM
    cols = []
    for c in range(0, x.shape[1], LANES):
        blk = sq[:, c:c + LANES]
        ss_lo = jnp.sum(jnp.where(low, blk, 0.0), axis=-1, keepdims=True)
        ss_hi = jnp.sum(jnp.where(low, 0.0, blk), axis=-1, keepdims=True)
        cols.append(jnp.where(low, ss_lo, ss_hi))
    ss = jnp.concatenate(cols, axis=1) if len(cols) > 1 else cols[0]
    return (x * lax.rsqrt(ss * (1.0 / HEAD_DIM) + EPS)) * gain


def _kv_head(head):
    return head // GROUP


def _ffn_act(h2, wg_ref, wu_ref, act_ref, rows):
    for start in range(0, D_FF, FF_CHUNK):
        width = min(FF_CHUNK, D_FF - start)
        g = _dot(h2, wg_ref[:, start:start + width])
        u = _dot(h2, wu_ref[:, start:start + width])
        act_ref[rows, start:start + width] = (g * jax.nn.sigmoid(g) * u).astype(BF16)


def _gate_chunk(h, c, w_in_ref, sga_ref, sgb_ref, rows):
    lo = c * GATE_CHUNK
    sga_ref[rows, lo:lo + GATE_CHUNK] = jax.nn.sigmoid(_dot(h, w_in_ref[:, OFF_GA + lo:OFF_GA + lo + GATE_CHUNK]))
    sgb_ref[rows, lo:lo + GATE_CHUNK] = jax.nn.sigmoid(_dot(h, w_in_ref[:, OFF_GB + lo:OFF_GB + lo + GATE_CHUNK]))


def _merge(ypool, yattn, sga_ref, sgb_ref, wpp_ref, wap_ref, merged_ref, rows):
    for c in range(N_GATE_CHUNKS):
        lo = c * GATE_CHUNK
        pp = _dot(ypool, wpp_ref[:, lo:lo + GATE_CHUNK])
        ap = _dot(yattn, wap_ref[:, lo:lo + GATE_CHUNK])
        merged = sga_ref[rows, lo:lo + GATE_CHUNK] * pp + sgb_ref[rows, lo:lo + GATE_CHUNK] * ap
        merged_ref[rows, lo:lo + GATE_CHUNK] = merged.astype(BF16)


BIG_WEIGHTS = (("w_in", D_MODEL, OFF_GB + D_MODEL), ("mix", len(POOL_WINDOWS) * POOL_GROUP_WIDTH, POOL_GROUP_WIDTH),
               ("wpp", POOL_WIDTH, D_MODEL), ("wap", Q_WIDTH, D_MODEL), ("wout", D_MODEL, D_MODEL),
               ("wg", D_MODEL, D_FF), ("wu", D_MODEL, D_FF), ("wd", D_FF, D_MODEL))
W_MIX, W_PP = 1, 2
W_FOLD = len(BIG_WEIGHTS)
HANDED_ON = tuple(w for w in range(len(BIG_WEIGHTS)) if w not in (W_MIX, W_PP)) + (W_FOLD,)
BF16_SHAPES = tuple((rows, cols) for _, rows, cols in BIG_WEIGHTS) + ((POOL_WIDTH, D_MODEL),)
CONVERT_ROWS = 128
STAGE_SLOTS = 4
STAGE_COLS = max(cols for _, _, cols in BIG_WEIGHTS)


def _weight_chunks():
    chunks = []
    for w, (_, nrows, cols) in enumerate(BIG_WEIGHTS):
        side_by_side = max(1, STAGE_COLS // cols)
        blocks = [(r0, min(CONVERT_ROWS, nrows - r0)) for r0 in range(0, nrows, CONVERT_ROWS)]
        for first in range(0, len(blocks), side_by_side):
            group = blocks[first:first + side_by_side]
            chunks.append((w, tuple((r0, rows, k * cols) for k, (r0, rows) in enumerate(group))))
    return tuple(chunks)


WEIGHT_CHUNKS = _weight_chunks()


def _writeback_copy(w, w_v, w_bf_hbm, out_sem):
    return pltpu.make_async_copy(w_v[w], w_bf_hbm[w], out_sem.at[w])


def _convert_weights(w_hbm, w_v, stage, stage_sem, mixf_s, params_ref):
    def chunk_copies(n):
        w, pieces = WEIGHT_CHUNKS[n]
        cols = BIG_WEIGHTS[w][2]
        slot = n % STAGE_SLOTS
        return [pltpu.make_async_copy(w_hbm[w].at[pl.ds(r0, rows), :],
                                      stage.at[slot, pl.ds(0, rows), pl.ds(c0, cols)], stage_sem.at[slot])
                for r0, rows, c0 in pieces]

    def convert(n):
        w, pieces = WEIGHT_CHUNKS[n]
        cols = BIG_WEIGHTS[w][2]
        for r0, rows, c0 in pieces:
            staged = stage[n % STAGE_SLOTS, 0:rows, c0:c0 + cols]
            if w == W_MIX:
                mixf_s[r0:r0 + rows, :] = staged
            else:
                w_v[w][r0:r0 + rows, :] = staged.astype(BF16)

    ahead = STAGE_SLOTS - 1
    for m in range(ahead):
        for copy in chunk_copies(m):
            copy.start()
    for n in range(len(WEIGHT_CHUNKS)):
        if n + ahead < len(WEIGHT_CHUNKS):
            for copy in chunk_copies(n + ahead):
                copy.start()
        for copy in chunk_copies(n):
            copy.wait()
        convert(n)

    for g in range(len(POOL_WINDOWS)):
        grp = slice(g * POOL_GROUP_WIDTH, (g + 1) * POOL_GROUP_WIDTH)
        scale = params_ref[ROW_PSCALE:ROW_PSCALE + 1, grp]
        w_v[W_FOLD][grp, :] = _dot((mixf_s[grp, :] * scale).astype(BF16), w_v[W_PP][grp, :]).astype(BF16)


def _prompt_kernel(*refs):
    nw = len(BIG_WEIGHTS)
    sinks_ref, x_ref, params_ref = refs[:3]
    w_hbm = refs[3:3 + nw]
    pos = 3 + nw
    outs = refs[pos:pos + 4]
    pos += 4
    w_bf_hbm = dict(zip(HANDED_ON, refs[pos:pos + len(HANDED_ON)]))
    pos += len(HANDED_ON)
    w_v = refs[pos:pos + nw + 1]
    pos += nw + 1
    stage, stage_sem, out_sem, mixf_s = refs[pos:pos + 4]
    scratch = refs[pos + 4:]
    w_in_ref, _, _, wap_ref, wout_ref, wg_ref, wu_ref, wd_ref, wfold_ref = w_v

    step = pl.program_id(0) * pl.num_programs(1) + pl.program_id(1)
    n_steps = pl.num_programs(0) * pl.num_programs(1)

    @pl.when(step == 0)
    def _():
        _convert_weights(w_hbm, w_v, stage, stage_sem, mixf_s, params_ref)

    @pl.when(step == 1)
    def _():
        for w in HANDED_ON:
            _writeback_copy(w, w_v, w_bf_hbm, out_sem).start()

    def body(t, carry):
        _prompt_tile(t, sinks_ref, x_ref, params_ref, w_in_ref, wfold_ref, wap_ref, wout_ref, wg_ref, wu_ref, wd_ref,
                     *outs, *scratch)
        return carry
    lax.fori_loop(0, TILES_PER_STEP, body, 0)

    @pl.when(step == n_steps - 1)
    def _():
        for w in HANDED_ON:
            _writeback_copy(w, w_v, w_bf_hbm, out_sem).wait()


def _prompt_tile(t, sinks_ref, x_ref, params_ref, w_in_ref, wfold_ref, wap_ref, wout_ref, wg_ref, wu_ref, wd_ref,
                 y_ref, knew_ref, vnew_ref, pnew_ref,
                 h_s, u_s, q_s, kd_s, vd_s, s_s, p_s, ypool_s, yattn_s, sga_s, sgb_s, merged_s, x1_s, h2_s, act_s):
    T = SEQ_TILE
    R = SUB_TILE
    subs = [slice(r0, r0 + R) for r0 in range(0, T, R)]
    n_blocks = T // ATT_BLOCK
    j = pl.program_id(1) * TILES_PER_STEP + t
    rows_per_kv = GROUP * ATT_BLOCK

    def in_block(rows):
        return pl.ds(pl.multiple_of(t * T + rows.start, R), R)

    @pl.when(j == 0)
    def _():
        u_s[0:PREFIX, :] = jnp.zeros((PREFIX, POOL_WIDTH), F32)
        kd_s[:, 0:ATT_BLOCK, :] = jnp.zeros((N_KV_HEADS, ATT_BLOCK, KV_WIDTH), BF16)
        vd_s[:, 0:ATT_BLOCK, :] = jnp.zeros((N_KV_HEADS, ATT_BLOCK, KV_WIDTH), BF16)

    lane_r = lax.broadcasted_iota(jnp.int32, (R, LANES), 1)
    lane = lax.broadcasted_iota(jnp.int32, (ATT_BLOCK, LANES), 1)
    row = lax.broadcasted_iota(jnp.int32, (ATT_BLOCK, 2 * ATT_BLOCK), 0)
    col = lax.broadcasted_iota(jnp.int32, (ATT_BLOCK, 2 * ATT_BLOCK), 1)
    band = (col >= row) & (col <= row + WINDOW)
    first_lo = jnp.where(j == 0, ATT_BLOCK, 0)

    def norm1(rows):
        x = x_ref[0, in_block(rows), :]
        h_s[rows, :] = ((x * _rms_scale(x)) * _param(params_ref, ROW_N1, D_MODEL)).astype(BF16)

    def in_proj(rows):
        h = h_s[rows, :]
        u_s[PREFIX + rows.start:PREFIX + rows.stop, :] = _dot(h, w_in_ref[:, OFF_U:OFF_U + POOL_WIDTH])
        q = _dot(h, w_in_ref[:, OFF_Q:OFF_Q + Q_WIDTH])
        q_s[rows, :] = _head_norm(q, _param(params_ref, ROW_QN, Q_WIDTH)).astype(BF16)
        kv = _dot(h, w_in_ref[:, OFF_K:OFF_K + 2 * KV_WIDTH])
        kn = _head_norm(kv[:, 0:KV_WIDTH], _param(params_ref, ROW_KN, KV_WIDTH))
        v = kv[:, KV_WIDTH:]
        dst_rows = slice(ATT_BLOCK + rows.start, ATT_BLOCK + rows.stop)
        for src, dst in ((kn, kd_s), (v, vd_s)):
            swapped = pltpu.roll(src, HEAD_DIM, 1)
            dst[0, dst_rows, :] = jnp.where(lane_r < HEAD_DIM, src, swapped).astype(BF16)
            dst[1, dst_rows, :] = jnp.where(lane_r < HEAD_DIM, swapped, src).astype(BF16)
        if rows.stop == T:
            knew_ref[0] = kn[R - ATT_BLOCK:, :].T
            vnew_ref[0] = v[R - ATT_BLOCK:, :].T
            batch = pl.program_id(0)
            for r in range(POOL_BUF):
                src = PREFIX + T - POOL_BUF + r
                pnew_ref[r, pl.ds(batch, 1), :] = u_s[src:src + 1, :]

    def pool(rows):
        pos1 = j * T + rows.start + lax.broadcasted_iota(jnp.int32, (R, 1), 0) + 1
        for g, w in enumerate(POOL_WINDOWS):
            cols = slice(g * POOL_GROUP_WIDTH, (g + 1) * POOL_GROUP_WIDTH)
            a = u_s[rows.start:rows.stop + PREFIX, cols]
            s = a
            shift = 1
            while shift < w:
                s = s + pltpu.roll(s, shift, 0)
                shift *= 2
            inv_cnt = 1.0 / jnp.minimum(pos1, w).astype(F32)
            ypool_s[rows, cols] = (s[PREFIX:, :] * inv_cnt - a[PREFIX:, :]).astype(BF16)

    def gates(rows, chunks):
        h = h_s[rows, :]
        for c in chunks:
            _gate_chunk(h, c, w_in_ref, sga_s, sgb_s, rows)

    def scores(b):
        r0 = b * ATT_BLOCK
        for c in range(N_KV_HEADS):
            parts = []
            for p in range(c * Q_COLS // N_KV_HEADS, (c + 1) * Q_COLS // N_KV_HEADS):
                qcol = q_s[r0:r0 + ATT_BLOCK, p * LANES:(p + 1) * LANES]
                parts.append(jnp.where(lane < HEAD_DIM, qcol, jnp.zeros_like(qcol)))
                parts.append(jnp.where(lane >= HEAD_DIM, qcol, jnp.zeros_like(qcol)))
            s_s[b % 2, c * rows_per_kv:(c + 1) * rows_per_kv, :] = _dot_nt(
                jnp.concatenate(parts, axis=0), kd_s[c, r0:r0 + 2 * ATT_BLOCK, :])

    def softmax(b):
        valid = band & (col >= first_lo) if b == 0 else band
        for head in range(N_HEADS):
            sh = jnp.where(valid, s_s[b % 2, head * ATT_BLOCK:(head + 1) * ATT_BLOCK, :], NEG)
            sink = sinks_ref[head]
            m = jnp.maximum(jnp.max(sh, axis=-1, keepdims=True), sink)
            p = jnp.exp(sh - m)
            denom = jnp.sum(p, axis=-1, keepdims=True) + jnp.exp(sink - m)
            p_s[b % 2, head * ATT_BLOCK:(head + 1) * ATT_BLOCK, :] = (p * (1.0 / denom)).astype(BF16)

    def weighted_values(b):
        r0 = b * ATT_BLOCK
        for c in range(N_KV_HEADS):
            o = _dot(p_s[b % 2, c * rows_per_kv:(c + 1) * rows_per_kv, :], vd_s[c, r0:r0 + 2 * ATT_BLOCK, :])
            for pp in range(Q_COLS // N_KV_HEADS):
                p_col = c * Q_COLS // N_KV_HEADS + pp
                o_lo = o[(2 * pp) * ATT_BLOCK:(2 * pp + 1) * ATT_BLOCK, :]
                o_hi = o[(2 * pp + 1) * ATT_BLOCK:(2 * pp + 2) * ATT_BLOCK, :]
                yattn_s[r0:r0 + ATT_BLOCK, p_col * LANES:(p_col + 1) * LANES] = (
                    jnp.where(lane < HEAD_DIM, o_lo, o_hi).astype(BF16))

    def out_proj(rows):
        _merge(ypool_s[rows, :], yattn_s[rows, :], sga_s, sgb_s, wfold_ref, wap_ref, merged_s, rows)
        x1 = x_ref[0, in_block(rows), :] + _dot(merged_s[rows, :], wout_ref[...])
        x1_s[rows, :] = x1
        h2_s[rows, :] = ((x1 * _rms_scale(x1)) * _param(params_ref, ROW_N2, D_MODEL)).astype(BF16)

    for rows in subs:
        norm1(rows)
    for rows in subs:
        in_proj(rows)
    pool(subs[0])
    gates(subs[0], range(N_GATE_CHUNKS))
    for rows in subs[1:]:
        pool(rows)
    later_gates = [(rows, c) for rows in subs[1:] for c in range(N_GATE_CHUNKS)]
    per_block = -(-len(later_gates) // n_blocks)
    scores(0)
    for b in range(n_blocks):
        if b + 1 < n_blocks:
            scores(b + 1)
        for rows, c in later_gates[b * per_block:(b + 1) * per_block]:
            gates(rows, [c])
        softmax(b)
        weighted_values(b)

    u_s[0:PREFIX, :] = u_s[T:T + PREFIX, :]
    kd_s[:, 0:ATT_BLOCK, :] = kd_s[:, T:T + ATT_BLOCK, :]
    vd_s[:, 0:ATT_BLOCK, :] = vd_s[:, T:T + ATT_BLOCK, :]

    for rows in subs:
        out_proj(rows)
    for rows in subs:
        _ffn_act(h2_s[rows, :], wg_ref, wu_ref, act_s, rows)
    for rows in subs:
        y_ref[0, in_block(rows), :] = x1_s[rows, :] + _dot(act_s[rows, :], wd_ref[...])


def _const_spec(shape):
    nd = len(shape)
    return pl.BlockSpec(shape, lambda *_: (0,) * nd, pipeline_mode=pl.Buffered(1))


def _prompt_call(x, sinks, params, big):
    B, S, _ = x.shape
    T = SEQ_TILE
    step_rows = T * TILES_PER_STEP
    assert tuple(b.shape for b in big) == BF16_SHAPES[:len(BIG_WEIGHTS)]
    assert B * (S // step_rows) >= 2, "the bf16 write-back starts on the second grid step"
    in_specs = [pl.BlockSpec(memory_space=pltpu.SMEM),
                pl.BlockSpec((1, step_rows, D_MODEL), lambda b, j: (b, j, 0)),
                _const_spec(params.shape)]
    in_specs += [pl.BlockSpec(memory_space=pl.ANY) for _ in big]
    out_shape = (jax.ShapeDtypeStruct((B, S, D_MODEL), F32),
                 jax.ShapeDtypeStruct((B, KV_WIDTH, ATT_BLOCK), F32),
                 jax.ShapeDtypeStruct((B, KV_WIDTH, ATT_BLOCK), F32),
                 jax.ShapeDtypeStruct((POOL_BUF, B, POOL_WIDTH), F32))
    out_shape += tuple(jax.ShapeDtypeStruct(BF16_SHAPES[w], BF16) for w in HANDED_ON)
    out_specs = (pl.BlockSpec((1, step_rows, D_MODEL), lambda b, j: (b, j, 0)),
                 pl.BlockSpec((1, KV_WIDTH, ATT_BLOCK), lambda b, j: (b, 0, 0)),
                 pl.BlockSpec((1, KV_WIDTH, ATT_BLOCK), lambda b, j: (b, 0, 0)),
                 pl.BlockSpec((POOL_BUF, B, POOL_WIDTH), lambda b, j: (0, 0, 0)))
    out_specs += tuple(pl.BlockSpec(memory_space=pl.ANY) for _ in HANDED_ON)
    scratch = [pltpu.VMEM(shape, BF16) for shape in BF16_SHAPES]
    scratch += [pltpu.VMEM((STAGE_SLOTS, CONVERT_ROWS, STAGE_COLS), F32),
                pltpu.SemaphoreType.DMA((STAGE_SLOTS,)),
                pltpu.SemaphoreType.DMA((len(BF16_SHAPES),)),
                pltpu.VMEM(BF16_SHAPES[W_MIX], F32)]
    scratch += [pltpu.VMEM((T, D_MODEL), BF16),
               pltpu.VMEM((PREFIX + T, POOL_WIDTH), F32),
               pltpu.VMEM((T, Q_WIDTH), BF16),
               pltpu.VMEM((N_KV_HEADS, ATT_BLOCK + T, KV_WIDTH), BF16),
               pltpu.VMEM((N_KV_HEADS, ATT_BLOCK + T, KV_WIDTH), BF16),
               pltpu.VMEM((2, N_HEADS * ATT_BLOCK, 2 * ATT_BLOCK), F32),
               pltpu.VMEM((2, N_HEADS * ATT_BLOCK, 2 * ATT_BLOCK), BF16),
               pltpu.VMEM((T, POOL_WIDTH), BF16),
               pltpu.VMEM((T, Q_WIDTH), BF16),
               pltpu.VMEM((T, D_MODEL), F32),
               pltpu.VMEM((T, D_MODEL), F32),
               pltpu.VMEM((T, D_MODEL), BF16),
               pltpu.VMEM((T, D_MODEL), F32),
               pltpu.VMEM((T, D_MODEL), BF16),
               pltpu.VMEM((T, D_FF), BF16)]
    return pl.pallas_call(
        _prompt_kernel,
        grid=(B, S // step_rows),
        in_specs=in_specs,
        out_specs=out_specs,
        out_shape=out_shape,
        scratch_shapes=scratch,
        compiler_params=pltpu.CompilerParams(
            dimension_semantics=("arbitrary", "arbitrary"),
            vmem_limit_bytes=VMEM_LIMIT_BYTES),
        name="prompt_layer",
    )(sinks, x, params, *big)


def _sample_kernel(*refs):
    nh = len(HANDED_ON)
    n_in = 6
    (x_ref, ck_ref, cv_ref, sp_ref, sinks_ref, params_ref) = refs[:n_in]
    w_hbm = refs[n_in:n_in + nh]
    pos = n_in + nh
    y_ref, knew_ref, vnew_ref, pnew_ref = refs[pos:pos + 4]
    pos += 4
    w_v = refs[pos:pos + nh]
    pos += nh
    (w_sem, u_s, kt_s, vt_s, q8_s, k8_s, v8_s, o8_s, ypre_s, sga_s, sgb_s, merged_s, act_s) = refs[pos:]
    w_in_ref, wap_ref, wout_ref, wg_ref, wu_ref, wd_ref, wfold_ref = w_v

    SB = SAMPLE_BLOCK
    N = x_ref.shape[0]
    nsteps = N // SB
    i = pl.program_id(0)
    lane = lax.broadcasted_iota(jnp.int32, (N, LANES), 1)
    all_rows = slice(0, N)

    def weight_copy(k):
        return pltpu.make_async_copy(w_hbm[k], w_v[k], w_sem.at[k])

    @pl.when(i == 0)
    def _():
        for k in range(nh):
            weight_copy(k).start()
        weight_copy(0).wait()
        x = x_ref[...]
        h = ((x * _rms_scale(x)) * _param(params_ref, ROW_N1, D_MODEL)).astype(BF16)
        u_s[...] = _dot(h, w_in_ref[:, OFF_U:OFF_U + POOL_WIDTH])
        for gc in range(N_GATE_CHUNKS):
            _gate_chunk(h, gc, w_in_ref, sga_s, sgb_s, all_rows)
        q = _head_norm(_dot(h, w_in_ref[:, OFF_Q:OFF_Q + Q_WIDTH]), _param(params_ref, ROW_QN, Q_WIDTH))
        kv = _dot(h, w_in_ref[:, OFF_K:OFF_K + 2 * KV_WIDTH])
        kn = _head_norm(kv[:, 0:KV_WIDTH], _param(params_ref, ROW_KN, KV_WIDTH))
        v = kv[:, KV_WIDTH:]
        kt_s[...] = kn.T
        vt_s[...] = v.T
        for head in range(N_HEADS):
            k8_s[pl.ds(head, N, stride=N_HEADS), :] = kn
            v8_s[pl.ds(head, N, stride=N_HEADS), :] = v
        for p in range(Q_COLS):
            qcol = q[:, p * LANES:(p + 1) * LANES]
            qswap = pltpu.roll(qcol, HEAD_DIM, 1)
            for e in range(HEADS_PER_COL):
                head = p * HEADS_PER_COL + e
                c = _kv_head(head)
                src = qcol if e == c else qswap
                keep = (lane < HEAD_DIM) if c == 0 else (lane >= HEAD_DIM)
                q8_s[pl.ds(head, N, stride=N_HEADS), :] = jnp.where(keep, src, 0.0)

    base = i * SB
    rows = pl.ds(pl.multiple_of(base, SB), SB)

    unew = u_s[rows, :]
    for g, w in enumerate(POOL_WINDOWS):
        cols = slice(g * POOL_GROUP_WIDTH, (g + 1) * POOL_GROUP_WIDTH)
        s = unew[:, cols]
        for r in range(POOL_BUF - (w - 1), POOL_BUF):
            s = s + sp_ref[r, :, cols]
        ypre_s[rows, cols] = (s * (1.0 / w) - unew[:, cols]).astype(BF16)
    pnew_ref[0:POOL_BUF - 1] = sp_ref[1:POOL_BUF]
    pnew_ref[POOL_BUF - 1] = unew

    hrows = pl.ds(pl.multiple_of(base * N_HEADS, SB * N_HEADS), SB * N_HEADS)
    lhs = q8_s[hrows, :]
    sc = jnp.concatenate(
        [_dot(lhs[bb * N_HEADS:(bb + 1) * N_HEADS, :].astype(BF16), ck_ref[bb].astype(BF16)) for bb in range(SB)],
        axis=0)
    sc_self = jnp.sum(lhs * k8_s[hrows, :], axis=-1, keepdims=True)
    row_head = lax.rem(lax.broadcasted_iota(jnp.int32, (SB * N_HEADS, 1), 0), N_HEADS)
    sink = jnp.zeros((SB * N_HEADS, 1), F32)
    for head in range(N_HEADS):
        sink = jnp.where(row_head == head, sinks_ref[head], sink)
    m = jnp.maximum(jnp.maximum(jnp.max(sc, axis=-1, keepdims=True), sc_self), sink)
    p = jnp.exp(sc - m)
    p_self = jnp.exp(sc_self - m)
    denom = jnp.sum(p, axis=-1, keepdims=True) + p_self + jnp.exp(sink - m)
    inv = 1.0 / denom
    pn = p * inv
    o = jnp.concatenate(
        [_dot_nt(pn[bb * N_HEADS:(bb + 1) * N_HEADS, :].astype(BF16), cv_ref[bb].astype(BF16)) for bb in range(SB)],
        axis=0)
    o8_s[hrows, :] = o + (p_self * inv) * v8_s[hrows, :]

    lane_c = lax.broadcasted_iota(jnp.int32, (KV_WIDTH, WINDOW), 1)
    kt_all = kt_s[...]
    vt_all = vt_s[...]
    for bb in range(SB):
        put = (WINDOW - 1) - (base + bb)
        knew_ref[bb] = jnp.where(lane_c == WINDOW - 1, pltpu.roll(kt_all, put, 1),
                                 pltpu.roll(ck_ref[bb], WINDOW - 1, 1))
        vnew_ref[bb] = jnp.where(lane_c == WINDOW - 1, pltpu.roll(vt_all, put, 1),
                                 pltpu.roll(cv_ref[bb], WINDOW - 1, 1))

    @pl.when(i == nsteps - 1)
    def _():
        for k in range(1, nh):
            weight_copy(k).wait()
        ycols = []
        for p in range(Q_COLS):
            halves = []
            for e in range(HEADS_PER_COL):
                head = p * HEADS_PER_COL + e
                oh = o8_s[pl.ds(head, N, stride=N_HEADS), :]
                halves.append(oh if e == _kv_head(head) else pltpu.roll(oh, HEAD_DIM, 1))
            ycols.append(jnp.where(lane < HEAD_DIM, halves[0], halves[1]).astype(BF16))
        yattn = jnp.concatenate(ycols, axis=1)
        _merge(ypre_s[...], yattn, sga_s, sgb_s, wfold_ref, wap_ref, merged_s, all_rows)
        x1 = x_ref[...] + _dot(merged_s[...], wout_ref[...])
        h2 = ((x1 * _rms_scale(x1)) * _param(params_ref, ROW_N2, D_MODEL)).astype(BF16)
        _ffn_act(h2, wg_ref, wu_ref, act_s, all_rows)
        y_ref[...] = x1 + _dot(act_s[...], wd_ref[...])


def _sample_call(x, ck, cv, sp, sinks, params, big):
    N = x.shape[0]
    SB = SAMPLE_BLOCK
    W = WINDOW
    assert tuple(b.shape for b in big) == tuple(BF16_SHAPES[w] for w in HANDED_ON)
    consts = (sinks, params)
    in_specs = [_const_spec(x.shape),
                pl.BlockSpec((SB, KV_WIDTH, W), lambda i: (i, 0, 0)),
                pl.BlockSpec((SB, KV_WIDTH, W), lambda i: (i, 0, 0)),
                pl.BlockSpec((POOL_BUF, SB, POOL_WIDTH), lambda i: (0, i, 0)),
                pl.BlockSpec(memory_space=pltpu.SMEM),
                _const_spec(params.shape)]
    in_specs += [pl.BlockSpec(memory_space=pl.ANY) for _ in big]
    out_shape = (jax.ShapeDtypeStruct((N, D_MODEL), F32),
                 jax.ShapeDtypeStruct((N, KV_WIDTH, W), F32),
                 jax.ShapeDtypeStruct((N, KV_WIDTH, W), F32),
                 jax.ShapeDtypeStruct((POOL_BUF, N, POOL_WIDTH), F32))
    out_specs = (pl.BlockSpec((N, D_MODEL), lambda i: (0, 0)),
                 pl.BlockSpec((SB, KV_WIDTH, W), lambda i: (i, 0, 0)),
                 pl.BlockSpec((SB, KV_WIDTH, W), lambda i: (i, 0, 0)),
                 pl.BlockSpec((POOL_BUF, SB, POOL_WIDTH), lambda i: (0, i, 0)))
    scratch = [pltpu.VMEM(b.shape, BF16) for b in big]
    scratch += [pltpu.SemaphoreType.DMA((len(big),)),
                pltpu.VMEM((N, POOL_WIDTH), F32),
               pltpu.VMEM((KV_WIDTH, N), F32),
               pltpu.VMEM((KV_WIDTH, N), F32),
               pltpu.VMEM((N * N_HEADS, LANES), F32),
               pltpu.VMEM((N * N_HEADS, LANES), F32),
               pltpu.VMEM((N * N_HEADS, LANES), F32),
               pltpu.VMEM((N * N_HEADS, LANES), F32),
               pltpu.VMEM((N, POOL_WIDTH), BF16),
               pltpu.VMEM((N, D_MODEL), F32),
               pltpu.VMEM((N, D_MODEL), F32),
               pltpu.VMEM((N, D_MODEL), BF16),
               pltpu.VMEM((N, D_FF), BF16)]
    return pl.pallas_call(
        _sample_kernel,
        grid=(N // SB,),
        in_specs=in_specs,
        out_specs=out_specs,
        out_shape=out_shape,
        scratch_shapes=scratch,
        compiler_params=pltpu.CompilerParams(
            dimension_semantics=("arbitrary",),
            vmem_limit_bytes=VMEM_LIMIT_BYTES),
        name="sample_layer",
    )(x, ck, cv, sp, *consts, *big)


def _cache_to_device_order(c):
    n, w = c.shape[0], c.shape[1]
    return jnp.transpose(c, (0, 2, 3, 1)).reshape(n, KV_WIDTH, w)


def _cache_from_device_order(c):
    n, _, w = c.shape
    return jnp.transpose(c.reshape(n, N_KV_HEADS, HEAD_DIM, w), (0, 3, 1, 2))[None]


@jax.jit
def _forward(x_prompt, x_sample, cache_k, cache_v, state_pool, norm1, w_in, q_norm, k_norm, sinks,
             pool_mix_w, pool_scale, w_pool_proj, w_attn_proj, w_out, norm2, w_gate, w_up, w_down):
    depth = w_in.shape[0]
    assert depth == 1, "single-layer trunk"
    l = 0
    N = x_sample.shape[0]
    assert cache_k.shape[2] == WINDOW and x_sample.shape[1] == 1

    rows = {ROW_N1: norm1[l], ROW_N2: norm2[l],
            ROW_QN: jnp.tile(q_norm[l], N_HEADS) * (HEAD_DIM ** -0.5),
            ROW_KN: jnp.tile(k_norm[l], N_KV_HEADS), ROW_PSCALE: pool_scale[l]}
    pieces = []
    for r in range(PARAM_ROWS):
        v = rows.get(r)
        used = 0 if v is None else v.shape[0]
        pieces += ([] if v is None else [v]) + ([jnp.zeros((D_MODEL - used,), F32)] if used < D_MODEL else [])
    params = jnp.concatenate(pieces).reshape(PARAM_ROWS, D_MODEL)
    big = (w_in[l], pool_mix_w[l].reshape(len(POOL_WINDOWS) * POOL_GROUP_WIDTH, POOL_GROUP_WIDTH),
           w_pool_proj[l], w_attn_proj[l], w_out[l], w_gate[l], w_up[l], w_down[l])

    y_p, k_p, v_p, pool_p, *weights_bf16 = _prompt_call(x_prompt, sinks[l], params, big)

    y_s, k_s, v_s, pool_s = _sample_call(
        x_sample.reshape(N, D_MODEL),
        _cache_to_device_order(cache_k[l]), _cache_to_device_order(cache_v[l]),
        jnp.transpose(state_pool[l], (1, 0, 2)), sinks[l], params, tuple(weights_bf16))

    return (y_p, y_s.reshape(N, 1, D_MODEL),
            _cache_from_device_order(k_p), _cache_from_device_order(v_p), jnp.transpose(pool_p, (1, 0, 2))[None],
            _cache_from_device_order(k_s), _cache_from_device_order(v_s),
            jnp.transpose(pool_s, (1, 0, 2))[None])


def kernel(x_prompt, x_sample, cache_k, cache_v, state_pool, norm1, w_in, q_norm, k_norm, sinks, pool_mix_w,
           pool_scale, w_pool_proj, w_attn_proj, w_out, norm2, w_gate, w_up, w_down):
    return _forward(x_prompt, x_sample, cache_k, cache_v, state_pool, norm1, w_in, q_norm, k_norm, sinks,
                    pool_mix_w, pool_scale, w_pool_proj, w_attn_proj, w_out, norm2, w_gate, w_up, w_down)
```

```python
import jax
import jax.numpy as jnp
from jax import lax
from jax.experimental import pallas as pl
from jax.experimental.pallas import tpu as pltpu

D_MODEL = 1024
POOL_WINDOWS = (2, 4, 8, 16)
POOL_GROUP_WIDTH = 128
POOL_WIDTH = 512
POOL_BUF = 15
N_HEADS = 8
N_KV_HEADS = 2
HEAD_DIM = 64
GROUP = N_HEADS // N_KV_HEADS
Q_WIDTH = 512
KV_WIDTH = 128
WINDOW = 128
D_FF = 2816
EPS = 1e-6
NEG = -1e30

LANES = 128
ATT_BLOCK = WINDOW
SEQ_TILE = 512
TILES_PER_STEP = 1
SUB_TILE = 256
FF_CHUNK = 256
GATE_CHUNK = 256
SAMPLE_BLOCK = 16
PREFIX = 16
VMEM_LIMIT_BYTES = 58 * 1024 * 1024
Q_COLS = Q_WIDTH // LANES
HEADS_PER_COL = LANES // HEAD_DIM
N_GATE_CHUNKS = D_MODEL // GATE_CHUNK

OFF_U = 0
OFF_Q = OFF_U + POOL_WIDTH
OFF_K = OFF_Q + Q_WIDTH
OFF_V = OFF_K + KV_WIDTH
OFF_GA = OFF_V + KV_WIDTH
OFF_GB = OFF_GA + D_MODEL

BF16 = jnp.bfloat16
F32 = jnp.float32

ROW_N1, ROW_N2, ROW_QN, ROW_KN, ROW_PSCALE, PARAM_ROWS = 0, 1, 2, 3, 4, 8


def _param(params_ref, row, width):
    return params_ref[row:row + 1, 0:width]


def _dot(a, b):
    return jnp.dot(a, b, preferred_element_type=F32)


def _dot_nt(a, b):
    return lax.dot_general(a, b, (((1,), (1,)), ((), ())), preferred_element_type=F32)


def _rms_scale(x):
    return lax.rsqrt(jnp.mean(x * x, axis=-1, keepdims=True) + EPS)


def _head_norm(x, gain):
    sq = x * x
    low = lax.broadcasted_iota(jnp.int32, (x.shape[0], LANES), 1) < HEAD_DIM
    cols = []
    for c in range(0, x.shape[1], LANES):
        blk = sq[:, c:c + LANES]
        ss_lo = jnp.sum(jnp.where(low, blk, 0.0), axis=-1, keepdims=True)
        ss_hi = jnp.sum(jnp.where(low, 0.0, blk), axis=-1, keepdims=True)
        cols.append(jnp.where(low, ss_lo, ss_hi))
    ss = jnp.concatenate(cols, axis=1) if len(cols) > 1 else cols[0]
    return (x * lax.rsqrt(ss * (1.0 / HEAD_DIM) + EPS)) * gain


def _kv_head(head):
    return head // GROUP


def _ffn_act(h2, wg_ref, wu_ref, act_ref, rows):
    for start in range(0, D_FF, FF_CHUNK):
        width = min(FF_CHUNK, D_FF - start)
        g = _dot(h2, wg_ref[:, start:start + width])
        u = _dot(h2, wu_ref[:, start:start + width])
        act_ref[rows, start:start + width] = (g * jax.nn.sigmoid(g) * u).astype(BF16)


def _gate_chunk(h, c, w_in_ref, sga_ref, sgb_ref, rows):
    lo = c * GATE_CHUNK
    sga_ref[rows, lo:lo + GATE_CHUNK] = jax.nn.sigmoid(_dot(h, w_in_ref[:, OFF_GA + lo:OFF_GA + lo + GATE_CHUNK]))
    sgb_ref[rows, lo:lo + GATE_CHUNK] = jax.nn.sigmoid(_dot(h, w_in_ref[:, OFF_GB + lo:OFF_GB + lo + GATE_CHUNK]))


def _merge(ypool, yattn, sga_ref, sgb_ref, wpp_ref, wap_ref, merged_ref, rows):
    for c in range(N_GATE_CHUNKS):
        lo = c * GATE_CHUNK
        pp = _dot(ypool, wpp_ref[:, lo:lo + GATE_CHUNK])
        ap = _dot(yattn, wap_ref[:, lo:lo + GATE_CHUNK])
        merged = sga_ref[rows, lo:lo + GATE_CHUNK] * pp + sgb_ref[rows, lo:lo + GATE_CHUNK] * ap
        merged_ref[rows, lo:lo + GATE_CHUNK] = merged.astype(BF16)


BIG_WEIGHTS = (("w_in", D_MODEL, OFF_GB + D_MODEL), ("mix", len(POOL_WINDOWS) * POOL_GROUP_WIDTH, POOL_GROUP_WIDTH),
               ("wpp", POOL_WIDTH, D_MODEL), ("wap", Q_WIDTH, D_MODEL), ("wout", D_MODEL, D_MODEL),
               ("wg", D_MODEL, D_FF), ("wu", D_MODEL, D_FF), ("wd", D_FF, D_MODEL))
W_MIX, W_PP = 1, 2
W_FOLD = len(BIG_WEIGHTS)
HANDED_ON = tuple(w for w in range(len(BIG_WEIGHTS)) if w not in (W_MIX, W_PP)) + (W_FOLD,)
BF16_SHAPES = tuple((rows, cols) for _, rows, cols in BIG_WEIGHTS) + ((POOL_WIDTH, D_MODEL),)
CONVERT_ROWS = 128
STAGE_SLOTS = 4
STAGE_COLS = max(cols for _, _, cols in BIG_WEIGHTS)


def _weight_chunks():
    chunks = []
    for w, (_, nrows, cols) in enumerate(BIG_WEIGHTS):
        side_by_side = max(1, STAGE_COLS // cols)
        blocks = [(r0, min(CONVERT_ROWS, nrows - r0)) for r0 in range(0, nrows, CONVERT_ROWS)]
        for first in range(0, len(blocks), side_by_side):
            group = blocks[first:first + side_by_side]
            chunks.append((w, tuple((r0, rows, k * cols) for k, (r0, rows) in enumerate(group))))
    return tuple(chunks)


WEIGHT_CHUNKS = _weight_chunks()


def _writeback_copy(w, w_v, w_bf_hbm, out_sem):
    return pltpu.make_async_copy(w_v[w], w_bf_hbm[w], out_sem.at[w])


def _convert_weights(w_hbm, w_v, stage, stage_sem, mixf_s, params_ref):
    def chunk_copies(n):
        w, pieces = WEIGHT_CHUNKS[n]
        cols = BIG_WEIGHTS[w][2]
        slot = n % STAGE_SLOTS
        return [pltpu.make_async_copy(w_hbm[w].at[pl.ds(r0, rows), :],
                                      stage.at[slot, pl.ds(0, rows), pl.ds(c0, cols)], stage_sem.at[slot])
                for r0, rows, c0 in pieces]

    def convert(n):
        w, pieces = WEIGHT_CHUNKS[n]
        cols = BIG_WEIGHTS[w][2]
        for r0, rows, c0 in pieces:
            staged = stage[n % STAGE_SLOTS, 0:rows, c0:c0 + cols]
            if w == W_MIX:
                mixf_s[r0:r0 + rows, :] = staged
            else:
                w_v[w][r0:r0 + rows, :] = staged.astype(BF16)

    ahead = STAGE_SLOTS - 1
    for m in range(ahead):
        for copy in chunk_copies(m):
            copy.start()
    for n in range(len(WEIGHT_CHUNKS)):
        if n + ahead < len(WEIGHT_CHUNKS):
            for copy in chunk_copies(n + ahead):
                copy.start()
        for copy in chunk_copies(n):
            copy.wait()
        convert(n)

    for g in range(len(POOL_WINDOWS)):
        grp = slice(g * POOL_GROUP_WIDTH, (g + 1) * POOL_GROUP_WIDTH)
        scale = params_ref[ROW_PSCALE:ROW_PSCALE + 1, grp]
        w_v[W_FOLD][grp, :] = _dot((mixf_s[grp, :] * scale).astype(BF16), w_v[W_PP][grp, :]).astype(BF16)


def _prompt_kernel(*refs):
    nw = len(BIG_WEIGHTS)
    sinks_ref, x_ref, params_ref = refs[:3]
    w_hbm = refs[3:3 + nw]
    pos = 3 + nw
    outs = refs[pos:pos + 4]
    pos += 4
    w_bf_hbm = dict(zip(HANDED_ON, refs[pos:pos + len(HANDED_ON)]))
    pos += len(HANDED_ON)
    w_v = refs[pos:pos + nw + 1]
    pos += nw + 1
    stage, stage_sem, out_sem, mixf_s = refs[pos:pos + 4]
    scratch = refs[pos + 4:]
    w_in_ref, _, _, wap_ref, wout_ref, wg_ref, wu_ref, wd_ref, wfold_ref = w_v

    step = pl.program_id(0) * pl.num_programs(1) + pl.program_id(1)
    n_steps = pl.num_programs(0) * pl.num_programs(1)

    @pl.when(step == 0)
    def _():
        _convert_weights(w_hbm, w_v, stage, stage_sem, mixf_s, params_ref)

    @pl.when(step == 1)
    def _():
        for w in HANDED_ON:
            _writeback_copy(w, w_v, w_bf_hbm, out_sem).start()

    def body(t, carry):
        _prompt_tile(t, sinks_ref, x_ref, params_ref, w_in_ref, wfold_ref, wap_ref, wout_ref, wg_ref, wu_ref, wd_ref,
                     *outs, *scratch)
        return carry
    lax.fori_loop(0, TILES_PER_STEP, body, 0)

    @pl.when(step == n_steps - 1)
    def _():
        for w in HANDED_ON:
            _writeback_copy(w, w_v, w_bf_hbm, out_sem).wait()


def _prompt_tile(t, sinks_ref, x_ref, params_ref, w_in_ref, wfold_ref, wap_ref, wout_ref, wg_ref, wu_ref, wd_ref,
                 y_ref, knew_ref, vnew_ref, pnew_ref,
                 h_s, u_s, q_s, kd_s, vd_s, s_s, p_s, ypool_s, yattn_s, sga_s, sgb_s, merged_s, x1_s, h2_s, act_s):
    T = SEQ_TILE
    R = SUB_TILE
    subs = [slice(r0, r0 + R) for r0 in range(0, T, R)]
    n_blocks = T // ATT_BLOCK
    j = pl.program_id(1) * TILES_PER_STEP + t
    rows_per_kv = GROUP * ATT_BLOCK

    def in_block(rows):
        return pl.ds(pl.multiple_of(t * T + rows.start, R), R)

    @pl.when(j == 0)
    def _():
        u_s[0:PREFIX, :] = jnp.zeros((PREFIX, POOL_WIDTH), F32)
        kd_s[:, 0:ATT_BLOCK, :] = jnp.zeros((N_KV_HEADS, ATT_BLOCK, KV_WIDTH), BF16)
        vd_s[:, 0:ATT_BLOCK, :] = jnp.zeros((N_KV_HEADS, ATT_BLOCK, KV_WIDTH), BF16)

    lane_r = lax.broadcasted_iota(jnp.int32, (R, LANES), 1)
    lane = lax.broadcasted_iota(jnp.int32, (ATT_BLOCK, LANES), 1)
    row = lax.broadcasted_iota(jnp.int32, (ATT_BLOCK, 2 * ATT_BLOCK), 0)
    col = lax.broadcasted_iota(jnp.int32, (ATT_BLOCK, 2 * ATT_BLOCK), 1)
    band = (col >= row) & (col <= row + WINDOW)
    first_lo = jnp.where(j == 0, ATT_BLOCK, 0)

    def norm1(rows):
        x = x_ref[0, in_block(rows), :]
        h_s[rows, :] = ((x * _rms_scale(x)) * _param(params_ref, ROW_N1, D_MODEL)).astype(BF16)

    def in_proj(rows):
        h = h_s[rows, :]
        u_s[PREFIX + rows.start:PREFIX + rows.stop, :] = _dot(h, w_in_ref[:, OFF_U:OFF_U + POOL_WIDTH])
        q = _dot(h, w_in_ref[:, OFF_Q:OFF_Q + Q_WIDTH])
        q_s[rows, :] = _head_norm(q, _param(params_ref, ROW_QN, Q_WIDTH)).astype(BF16)
        kv = _dot(h, w_in_ref[:, OFF_K:OFF_K + 2 * KV_WIDTH])
        kn = _head_norm(kv[:, 0:KV_WIDTH], _param(params_ref, ROW_KN, KV_WIDTH))
        v = kv[:, KV_WIDTH:]
        dst_rows = slice(ATT_BLOCK + rows.start, ATT_BLOCK + rows.stop)
        for src, dst in ((kn, kd_s), (v, vd_s)):
            swapped = pltpu.roll(src, HEAD_DIM, 1)
            dst[0, dst_rows, :] = jnp.where(lane_r < HEAD_DIM, src, swapped).astype(BF16)
            dst[1, dst_rows, :] = jnp.where(lane_r < HEAD_DIM, swapped, src).astype(BF16)
        if rows.stop == T:
            knew_ref[0] = kn[R - ATT_BLOCK:, :].T
            vnew_ref[0] = v[R - ATT_BLOCK:, :].T
            batch = pl.program_id(0)
            for r in range(POOL_BUF):
                src = PREFIX + T - POOL_BUF + r
                pnew_ref[r, pl.ds(batch, 1), :] = u_s[src:src + 1, :]

    def pool(rows):
        pos1 = j * T + rows.start + lax.broadcasted_iota(jnp.int32, (R, 1), 0) + 1
        for g, w in enumerate(POOL_WINDOWS):
            cols = slice(g * POOL_GROUP_WIDTH, (g + 1) * POOL_GROUP_WIDTH)
            a = u_s[rows.start:rows.stop + PREFIX, cols]
            s = a
            shift = 1
            while shift < w:
                s = s + pltpu.roll(s, shift, 0)
                shift *= 2
            inv_cnt = 1.0 / jnp.minimum(pos1, w).astype(F32)
            ypool_s[rows, cols] = (s[PREFIX:, :] * inv_cnt - a[PREFIX:, :]).astype(BF16)

    def gates(rows, chunks):
        h = h_s[rows, :]
        for c in chunks:
            _gate_chunk(h, c, w_in_ref, sga_s, sgb_s, rows)

    def scores(b):
        r0 = b * ATT_BLOCK
        for c in range(N_KV_HEADS):
            parts = []
            for p in range(c * Q_COLS // N_KV_HEADS, (c + 1) * Q_COLS // N_KV_HEADS):
                qcol = q_s[r0:r0 + ATT_BLOCK, p * LANES:(p + 1) * LANES]
                parts.append(jnp.where(lane < HEAD_DIM, qcol, jnp.zeros_like(qcol)))
                parts.append(jnp.where(lane >= HEAD_DIM, qcol, jnp.zeros_like(qcol)))
            s_s[b % 2, c * rows_per_kv:(c + 1) * rows_per_kv, :] = _dot_nt(
                jnp.concatenate(parts, axis=0), kd_s[c, r0:r0 + 2 * ATT_BLOCK, :])

    def softmax(b):
        valid = band & (col >= first_lo) if b == 0 else band
        for head in range(N_HEADS):
            sh = jnp.where(valid, s_s[b % 2, head * ATT_BLOCK:(head + 1) * ATT_BLOCK, :], NEG)
            sink = sinks_ref[head]
            m = jnp.maximum(jnp.max(sh, axis=-1, keepdims=True), sink)
            p = jnp.exp(sh - m)
            denom = jnp.sum(p, axis=-1, keepdims=True) + jnp.exp(sink - m)
            p_s[b % 2, head * ATT_BLOCK:(head + 1) * ATT_BLOCK, :] = (p * (1.0 / denom)).astype(BF16)

    def weighted_values(b):
        r0 = b * ATT_BLOCK
        for c in range(N_KV_HEADS):
            o = _dot(p_s[b % 2, c * rows_per_kv:(c + 1) * rows_per_kv, :], vd_s[c, r0:r0 + 2 * ATT_BLOCK, :])
            for pp in range(Q_COLS // N_KV_HEADS):
                p_col = c * Q_COLS // N_KV_HEADS + pp
                o_lo = o[(2 * pp) * ATT_BLOCK:(2 * pp + 1) * ATT_BLOCK, :]
                o_hi = o[(2 * pp + 1) * ATT_BLOCK:(2 * pp + 2) * ATT_BLOCK, :]
                yattn_s[r0:r0 + ATT_BLOCK, p_col * LANES:(p_col + 1) * LANES] = (
                    jnp.where(lane < HEAD_DIM, o_lo, o_hi).astype(BF16))

    def out_proj(rows):
        _merge(ypool_s[rows, :], yattn_s[rows, :], sga_s, sgb_s, wfold_ref, wap_ref, merged_s, rows)
        x1 = x_ref[0, in_block(rows), :] + _dot(merged_s[rows, :], wout_ref[...])
        x1_s[rows, :] = x1
        h2_s[rows, :] = ((x1 * _rms_scale(x1)) * _param(params_ref, ROW_N2, D_MODEL)).astype(BF16)

    for rows in subs:
        norm1(rows)
    for rows in subs:
        in_proj(rows)
    pool(subs[0])
    gates(subs[0], range(N_GATE_CHUNKS))
    for rows in subs[1:]:
        pool(rows)
    later_gates = [(rows, c) for rows in subs[1:] for c in range(N_GATE_CHUNKS)]
    per_block = -(-len(later_gates) // n_blocks)
    scores(0)
    for b in range(n_blocks):
        if b + 1 < n_blocks:
            scores(b + 1)
        for rows, c in later_gates[b * per_block:(b + 1) * per_block]:
            gates(rows, [c])
        softmax(b)
        weighted_values(b)

    u_s[0:PREFIX, :] = u_s[T:T + PREFIX, :]
    kd_s[:, 0:ATT_BLOCK, :] = kd_s[:, T:T + ATT_BLOCK, :]
    vd_s[:, 0:ATT_BLOCK, :] = vd_s[:, T:T + ATT_BLOCK, :]

    for rows in subs:
        out_proj(rows)
    for rows in subs:
        _ffn_act(h2_s[rows, :], wg_ref, wu_ref, act_s, rows)
    for rows in subs:
        y_ref[0, in_block(rows), :] = x1_s[rows, :] + _dot(act_s[rows, :], wd_ref[...])


def _const_spec(shape):
    nd = len(shape)
    return pl.BlockSpec(shape, lambda *_: (0,) * nd, pipeline_mode=pl.Buffered(1))


def _prompt_call(x, sinks, params, big):
    B, S, _ = x.shape
    T = SEQ_TILE
    step_rows = T * TILES_PER_STEP
    assert tuple(b.shape for b in big) == BF16_SHAPES[:len(BIG_WEIGHTS)]
    assert B * (S // step_rows) >= 2, "the bf16 write-back starts on the second grid step"
    in_specs = [pl.BlockSpec(memory_space=pltpu.SMEM),
                pl.BlockSpec((1, step_rows, D_MODEL), lambda b, j: (b, j, 0)),
                _const_spec(params.shape)]
    in_specs += [pl.BlockSpec(memory_space=pl.ANY) for _ in big]
    out_shape = (jax.ShapeDtypeStruct((B, S, D_MODEL), F32),
                 jax.ShapeDtypeStruct((B, KV_WIDTH, ATT_BLOCK), F32),
                 jax.ShapeDtypeStruct((B, KV_WIDTH, ATT_BLOCK), F32),
                 jax.ShapeDtypeStruct((POOL_BUF, B, POOL_WIDTH), F32))
    out_shape += tuple(jax.ShapeDtypeStruct(BF16_SHAPES[w], BF16) for w in HANDED_ON)
    out_specs = (pl.BlockSpec((1, step_rows, D_MODEL), lambda b, j: (b, j, 0)),
                 pl.BlockSpec((1, KV_WIDTH, ATT_BLOCK), lambda b, j: (b, 0, 0)),
                 pl.BlockSpec((1, KV_WIDTH, ATT_BLOCK), lambda b, j: (b, 0, 0)),
                 pl.BlockSpec((POOL_BUF, B, POOL_WIDTH), lambda b, j: (0, 0, 0)))
    out_specs += tuple(pl.BlockSpec(memory_space=pl.ANY) for _ in HANDED_ON)
    scratch = [pltpu.VMEM(shape, BF16) for shape in BF16_SHAPES]
    scratch += [pltpu.VMEM((STAGE_SLOTS, CONVERT_ROWS, STAGE_COLS), F32),
                pltpu.SemaphoreType.DMA((STAGE_SLOTS,)),
                pltpu.SemaphoreType.DMA((len(BF16_SHAPES),)),
                pltpu.VMEM(BF16_SHAPES[W_MIX], F32)]
    scratch += [pltpu.VMEM((T, D_MODEL), BF16),
               pltpu.VMEM((PREFIX + T, POOL_WIDTH), F32),
               pltpu.VMEM((T, Q_WIDTH), BF16),
               pltpu.VMEM((N_KV_HEADS, ATT_BLOCK + T, KV_WIDTH), BF16),
               pltpu.VMEM((N_KV_HEADS, ATT_BLOCK + T, KV_WIDTH), BF16),
               pltpu.VMEM((2, N_HEADS * ATT_BLOCK, 2 * ATT_BLOCK), F32),
               pltpu.VMEM((2, N_HEADS * ATT_BLOCK, 2 * ATT_BLOCK), BF16),
               pltpu.VMEM((T, POOL_WIDTH), BF16),
               pltpu.VMEM((T, Q_WIDTH), BF16),
               pltpu.VMEM((T, D_MODEL), F32),
               pltpu.VMEM((T, D_MODEL), F32),
               pltpu.VMEM((T, D_MODEL), BF16),
               pltpu.VMEM((T, D_MODEL), F32),
               pltpu.VMEM((T, D_MODEL), BF16),
               pltpu.VMEM((T, D_FF), BF16)]
    return pl.pallas_call(
        _prompt_kernel,
        grid=(B, S // step_rows),
        in_specs=in_specs,
        out_specs=out_specs,
        out_shape=out_shape,
        scratch_shapes=scratch,
        compiler_params=pltpu.CompilerParams(
            dimension_semantics=("arbitrary", "arbitrary"),
            vmem_limit_bytes=VMEM_LIMIT_BYTES),
        name="prompt_layer",
    )(sinks, x, params, *big)


def _sample_kernel(*refs):
    nh = len(HANDED_ON)
    n_in = 6
    (x_ref, ck_ref, cv_ref, sp_ref, sinks_ref, params_ref) = refs[:n_in]
    w_hbm = refs[n_in:n_in + nh]
    pos = n_in + nh
    y_ref, knew_ref, vnew_ref, pnew_ref = refs[pos:pos + 4]
    pos += 4
    w_v = refs[pos:pos + nh]
    pos += nh
    (w_sem, u_s, kt_s, vt_s, q8_s, k8_s, v8_s, o8_s, ypre_s, sga_s, sgb_s, merged_s, act_s) = refs[pos:]
    w_in_ref, wap_ref, wout_ref, wg_ref, wu_ref, wd_ref, wfold_ref = w_v

    SB = SAMPLE_BLOCK
    col_tiles = D_MODEL // LANES
    N = x_ref.shape[0] // col_tiles
    nsteps = N // SB
    i = pl.program_id(0)
    lane = lax.broadcasted_iota(jnp.int32, (N, LANES), 1)
    all_rows = slice(0, N)

    def load_x():
        return jnp.concatenate([x_ref[pl.ds(ct, N, stride=col_tiles), :] for ct in range(col_tiles)], axis=1)

    def weight_copy(k):
        return pltpu.make_async_copy(w_hbm[k], w_v[k], w_sem.at[k])

    @pl.when(i == 0)
    def _():
        for k in range(nh):
            weight_copy(k).start()
        weight_copy(0).wait()
        x = load_x()
        h = ((x * _rms_scale(x)) * _param(params_ref, ROW_N1, D_MODEL)).astype(BF16)
        u_s[...] = _dot(h, w_in_ref[:, OFF_U:OFF_U + POOL_WIDTH])
        for gc in range(N_GATE_CHUNKS):
            _gate_chunk(h, gc, w_in_ref, sga_s, sgb_s, all_rows)
        q = _head_norm(_dot(h, w_in_ref[:, OFF_Q:OFF_Q + Q_WIDTH]), _param(params_ref, ROW_QN, Q_WIDTH))
        kv = _dot(h, w_in_ref[:, OFF_K:OFF_K + 2 * KV_WIDTH])
        kn = _head_norm(kv[:, 0:KV_WIDTH], _param(params_ref, ROW_KN, KV_WIDTH))
        v = kv[:, KV_WIDTH:]
        kt_s[...] = kn.T
        vt_s[...] = v.T
        for head in range(N_HEADS):
            k8_s[pl.ds(head, N, stride=N_HEADS), :] = kn
            v8_s[pl.ds(head, N, stride=N_HEADS), :] = v
        for p in range(Q_COLS):
            qcol = q[:, p * LANES:(p + 1) * LANES]
            qswap = pltpu.roll(qcol, HEAD_DIM, 1)
            for e in range(HEADS_PER_COL):
                head = p * HEADS_PER_COL + e
                c = _kv_head(head)
                src = qcol if e == c else qswap
                keep = (lane < HEAD_DIM) if c == 0 else (lane >= HEAD_DIM)
                q8_s[pl.ds(head, N, stride=N_HEADS), :] = jnp.where(keep, src, 0.0)

    base = i * SB
    rows = pl.ds(pl.multiple_of(base, SB), SB)

    unew = u_s[rows, :]
    for g, w in enumerate(POOL_WINDOWS):
        cols = slice(g * POOL_GROUP_WIDTH, (g + 1) * POOL_GROUP_WIDTH)
        s = unew[:, cols]
        for r in range(POOL_BUF - (w - 1), POOL_BUF):
            s = s + sp_ref[r, :, cols]
        ypre_s[rows, cols] = (s * (1.0 / w) - unew[:, cols]).astype(BF16)
    pnew_ref[0:POOL_BUF - 1] = sp_ref[1:POOL_BUF]
    pnew_ref[POOL_BUF - 1] = unew

    hrows = pl.ds(pl.multiple_of(base * N_HEADS, SB * N_HEADS), SB * N_HEADS)
    lhs = q8_s[hrows, :]
    sc = jnp.concatenate(
        [_dot(lhs[bb * N_HEADS:(bb + 1) * N_HEADS, :].astype(BF16), ck_ref[bb].astype(BF16)) for bb in range(SB)],
        axis=0)
    sc_self = jnp.sum(lhs * k8_s[hrows, :], axis=-1, keepdims=True)
    row_head = lax.rem(lax.broadcasted_iota(jnp.int32, (SB * N_HEADS, 1), 0), N_HEADS)
    sink = jnp.zeros((SB * N_HEADS, 1), F32)
    for head in range(N_HEADS):
        sink = jnp.where(row_head == head, sinks_ref[head], sink)
    m = jnp.maximum(jnp.maximum(jnp.max(sc, axis=-1, keepdims=True), sc_self), sink)
    p = jnp.exp(sc - m)
    p_self = jnp.exp(sc_self - m)
    denom = jnp.sum(p, axis=-1, keepdims=True) + p_self + jnp.exp(sink - m)
    inv = 1.0 / denom
    pn = p * inv
    o = jnp.concatenate(
        [_dot_nt(pn[bb * N_HEADS:(bb + 1) * N_HEADS, :].astype(BF16), cv_ref[bb].astype(BF16)) for bb in range(SB)],
        axis=0)
    o8_s[hrows, :] = o + (p_self * inv) * v8_s[hrows, :]

    lane_c = lax.broadcasted_iota(jnp.int32, (KV_WIDTH, WINDOW), 1)
    kt_all = kt_s[...]
    vt_all = vt_s[...]
    for bb in range(SB):
        put = (WINDOW - 1) - (base + bb)
        knew_ref[bb] = jnp.where(lane_c == WINDOW - 1, pltpu.roll(kt_all, put, 1),
                                 pltpu.roll(ck_ref[bb], WINDOW - 1, 1))
        vnew_ref[bb] = jnp.where(lane_c == WINDOW - 1, pltpu.roll(vt_all, put, 1),
                                 pltpu.roll(cv_ref[bb], WINDOW - 1, 1))

    @pl.when(i == nsteps - 1)
    def _():
        for k in range(1, nh):
            weight_copy(k).wait()
        ycols = []
        for p in range(Q_COLS):
            halves = []
            for e in range(HEADS_PER_COL):
                head = p * HEADS_PER_COL + e
                oh = o8_s[pl.ds(head, N, stride=N_HEADS), :]
                halves.append(oh if e == _kv_head(head) else pltpu.roll(oh, HEAD_DIM, 1))
            ycols.append(jnp.where(lane < HEAD_DIM, halves[0], halves[1]).astype(BF16))
        yattn = jnp.concatenate(ycols, axis=1)
        _merge(ypre_s[...], yattn, sga_s, sgb_s, wfold_ref, wap_ref, merged_s, all_rows)
        x1 = load_x() + _dot(merged_s[...], wout_ref[...])
        h2 = ((x1 * _rms_scale(x1)) * _param(params_ref, ROW_N2, D_MODEL)).astype(BF16)
        _ffn_act(h2, wg_ref, wu_ref, act_s, all_rows)
        y = x1 + _dot(act_s[...], wd_ref[...])
        for ct in range(col_tiles):
            y_ref[pl.ds(ct, N, stride=col_tiles), :] = y[:, ct * LANES:(ct + 1) * LANES]


def _sample_call(x, ck, cv, sp, sinks, params, big):
    N = x.shape[0] // (D_MODEL // LANES)
    SB = SAMPLE_BLOCK
    W = WINDOW
    assert tuple(b.shape for b in big) == tuple(BF16_SHAPES[w] for w in HANDED_ON)
    consts = (sinks, params)
    in_specs = [_const_spec(x.shape),
                pl.BlockSpec((SB, KV_WIDTH, W), lambda i: (i, 0, 0)),
                pl.BlockSpec((SB, KV_WIDTH, W), lambda i: (i, 0, 0)),
                pl.BlockSpec((POOL_BUF, SB, POOL_WIDTH), lambda i: (0, i, 0)),
                pl.BlockSpec(memory_space=pltpu.SMEM),
                _const_spec(params.shape)]
    in_specs += [pl.BlockSpec(memory_space=pl.ANY) for _ in big]
    out_shape = (jax.ShapeDtypeStruct(x.shape, F32),
                 jax.ShapeDtypeStruct((N, KV_WIDTH, W), F32),
                 jax.ShapeDtypeStruct((N, KV_WIDTH, W), F32),
                 jax.ShapeDtypeStruct((POOL_BUF, N, POOL_WIDTH), F32))
    out_specs = (pl.BlockSpec(x.shape, lambda i: (0, 0)),
                 pl.BlockSpec((SB, KV_WIDTH, W), lambda i: (i, 0, 0)),
                 pl.BlockSpec((SB, KV_WIDTH, W), lambda i: (i, 0, 0)),
                 pl.BlockSpec((POOL_BUF, SB, POOL_WIDTH), lambda i: (0, i, 0)))
    scratch = [pltpu.VMEM(b.shape, BF16) for b in big]
    scratch += [pltpu.SemaphoreType.DMA((len(big),)),
                pltpu.VMEM((N, POOL_WIDTH), F32),
               pltpu.VMEM((KV_WIDTH, N), F32),
               pltpu.VMEM((KV_WIDTH, N), F32),
               pltpu.VMEM((N * N_HEADS, LANES), F32),
               pltpu.VMEM((N * N_HEADS, LANES), F32),
               pltpu.VMEM((N * N_HEADS, LANES), F32),
               pltpu.VMEM((N * N_HEADS, LANES), F32),
               pltpu.VMEM((N, POOL_WIDTH), BF16),
               pltpu.VMEM((N, D_MODEL), F32),
               pltpu.VMEM((N, D_MODEL), F32),
               pltpu.VMEM((N, D_MODEL), BF16),
               pltpu.VMEM((N, D_FF), BF16)]
    return pl.pallas_call(
        _sample_kernel,
        grid=(N // SB,),
        in_specs=in_specs,
        out_specs=out_specs,
        out_shape=out_shape,
        scratch_shapes=scratch,
        compiler_params=pltpu.CompilerParams(
            dimension_semantics=("arbitrary",),
            vmem_limit_bytes=VMEM_LIMIT_BYTES),
        name="sample_layer",
    )(x, ck, cv, sp, *consts, *big)


def _cache_to_device_order(c):
    n, w = c.shape[0], c.shape[1]
    return jnp.transpose(c, (0, 2, 3, 1)).reshape(n, KV_WIDTH, w)


def _cache_from_device_order(c):
    n, _, w = c.shape
    return jnp.transpose(c.reshape(n, N_KV_HEADS, HEAD_DIM, w), (0, 3, 1, 2))[None]


@jax.jit
def _forward(x_prompt, x_sample, cache_k, cache_v, state_pool, norm1, w_in, q_norm, k_norm, sinks,
             pool_mix_w, pool_scale, w_pool_proj, w_attn_proj, w_out, norm2, w_gate, w_up, w_down):
    depth = w_in.shape[0]
    assert depth == 1, "single-layer trunk"
    l = 0
    N = x_sample.shape[0]
    assert cache_k.shape[2] == WINDOW and x_sample.shape[1] == 1

    rows = {ROW_N1: norm1[l], ROW_N2: norm2[l],
            ROW_QN: jnp.tile(q_norm[l], N_HEADS) * (HEAD_DIM ** -0.5),
            ROW_KN: jnp.tile(k_norm[l], N_KV_HEADS), ROW_PSCALE: pool_scale[l]}
    pieces = []
    for r in range(PARAM_ROWS):
        v = rows.get(r)
        used = 0 if v is None else v.shape[0]
        pieces += ([] if v is None else [v]) + ([jnp.zeros((D_MODEL - used,), F32)] if used < D_MODEL else [])
    params = jnp.concatenate(pieces).reshape(PARAM_ROWS, D_MODEL)
    big = (w_in[l], pool_mix_w[l].reshape(len(POOL_WINDOWS) * POOL_GROUP_WIDTH, POOL_GROUP_WIDTH),
           w_pool_proj[l], w_attn_proj[l], w_out[l], w_gate[l], w_up[l], w_down[l])

    y_p, k_p, v_p, pool_p, *weights_bf16 = _prompt_call(x_prompt, sinks[l], params, big)

    y_s, k_s, v_s, pool_s = _sample_call(
        x_sample.reshape(N * (D_MODEL // LANES), LANES),
        _cache_to_device_order(cache_k[l]), _cache_to_device_order(cache_v[l]),
        jnp.transpose(state_pool[l], (1, 0, 2)), sinks[l], params, tuple(weights_bf16))

    return (y_p, y_s.reshape(N, 1, D_MODEL),
            _cache_from_device_order(k_p), _cache_from_device_order(v_p), jnp.transpose(pool_p, (1, 0, 2))[None],
            _cache_from_device_order(k_s), _cache_from_device_order(v_s),
            jnp.transpose(pool_s, (1, 0, 2))[None])


def kernel(x_prompt, x_sample, cache_k, cache_v, state_pool, norm1, w_in, q_norm, k_norm, sinks, pool_mix_w,
           pool_scale, w_pool_proj, w_attn_proj, w_out, norm2, w_gate, w_up, w_down):
    return _forward(x_prompt, x_sample, cache_k, cache_v, state_pool, norm1, w_in, q_norm, k_norm, sinks,
                    pool_mix_w, pool_scale, w_pool_proj, w_attn_proj, w_out, norm2, w_gate, w_up, w_down)
```

```python
import jax
import jax.numpy as jnp
from jax import lax
from jax.experimental import pallas as pl
from jax.experimental.pallas import tpu as pltpu

D_MODEL = 1024
POOL_WINDOWS = (2, 4, 8, 16)
POOL_GROUP_WIDTH = 128
POOL_WIDTH = 512
POOL_BUF = 15
N_HEADS = 8
N_KV_HEADS = 2
HEAD_DIM = 64
GROUP = N_HEADS // N_KV_HEADS
Q_WIDTH = 512
KV_WIDTH = 128
WINDOW = 128
D_FF = 2816
EPS = 1e-6
NEG = -1e30
LOG2_E = 1.4426950408889634

LANES = 128
ATT_BLOCK = WINDOW
SEQ_TILE = 512
TILES_PER_STEP = 1
SUB_TILE = 256
FF_CHUNK = 256
GATE_CHUNK = 256
SAMPLE_BLOCK = 16
PREFIX = 16
VMEM_LIMIT_BYTES = 58 * 1024 * 1024
Q_COLS = Q_WIDTH // LANES
HEADS_PER_COL = LANES // HEAD_DIM
N_GATE_CHUNKS = D_MODEL // GATE_CHUNK

OFF_U = 0
OFF_Q = OFF_U + POOL_WIDTH
OFF_K = OFF_Q + Q_WIDTH
OFF_V = OFF_K + KV_WIDTH
OFF_GA = OFF_V + KV_WIDTH
OFF_GB = OFF_GA + D_MODEL

BF16 = jnp.bfloat16
F32 = jnp.float32

ROW_N1, ROW_N2, ROW_QN, ROW_KN, ROW_PSCALE, PARAM_ROWS = 0, 1, 2, 3, 4, 8


def _param(params_ref, row, width):
    return params_ref[row:row + 1, 0:width]


def _dot(a, b):
    return jnp.dot(a, b, preferred_element_type=F32)


def _dot_nt(a, b):
    return lax.dot_general(a, b, (((1,), (1,)), ((), ())), preferred_element_type=F32)


def _rms_scale(x):
    return lax.rsqrt(jnp.mean(x * x, axis=-1, keepdims=True) + EPS)


def _head_norm(x, gain):
    sq = x * x
    low = lax.broadcasted_iota(jnp.int32, (x.shape[0], LANES), 1) < HEAD_DIM
    cols = []
    for c in range(0, x.shape[1], LANES):
        blk = sq[:, c:c + LANES]
        ss_lo = jnp.sum(jnp.where(low, blk, 0.0), axis=-1, keepdims=True)
        ss_hi = jnp.sum(jnp.where(low, 0.0, blk), axis=-1, keepdims=True)
        cols.append(jnp.where(low, ss_lo, ss_hi))
    ss = jnp.concatenate(cols, axis=1) if len(cols) > 1 else cols[0]
    return (x * lax.rsqrt(ss * (1.0 / HEAD_DIM) + EPS)) * gain


def _kv_head(head):
    return head // GROUP


def _ffn_act(h2, wg_ref, wu_ref, act_ref, rows):
    for start in range(0, D_FF, FF_CHUNK):
        width = min(FF_CHUNK, D_FF - start)
        g = _dot(h2, wg_ref[:, start:start + width])
        u = _dot(h2, wu_ref[:, start:start + width])
        act_ref[rows, start:start + width] = (g * jax.nn.sigmoid(g) * u).astype(BF16)


def _gate_chunk(h, c, w_in_ref, sga_ref, sgb_ref, rows):
    lo = c * GATE_CHUNK
    sga_ref[rows, lo:lo + GATE_CHUNK] = jax.nn.sigmoid(_dot(h, w_in_ref[:, OFF_GA + lo:OFF_GA + lo + GATE_CHUNK]))
    sgb_ref[rows, lo:lo + GATE_CHUNK] = jax.nn.sigmoid(_dot(h, w_in_ref[:, OFF_GB + lo:OFF_GB + lo + GATE_CHUNK]))


def _merge(ypool, yattn, sga_ref, sgb_ref, wpp_ref, wap_ref, merged_ref, rows):
    for c in range(N_GATE_CHUNKS):
        lo = c * GATE_CHUNK
        pp = _dot(ypool, wpp_ref[:, lo:lo + GATE_CHUNK])
        ap = _dot(yattn, wap_ref[:, lo:lo + GATE_CHUNK])
        merged = sga_ref[rows, lo:lo + GATE_CHUNK] * pp + sgb_ref[rows, lo:lo + GATE_CHUNK] * ap
        merged_ref[rows, lo:lo + GATE_CHUNK] = merged.astype(BF16)


BIG_WEIGHTS = (("w_in", D_MODEL, OFF_GB + D_MODEL), ("mix", len(POOL_WINDOWS) * POOL_GROUP_WIDTH, POOL_GROUP_WIDTH),
               ("wpp", POOL_WIDTH, D_MODEL), ("wap", Q_WIDTH, D_MODEL), ("wout", D_MODEL, D_MODEL),
               ("wg", D_MODEL, D_FF), ("wu", D_MODEL, D_FF), ("wd", D_FF, D_MODEL))
W_MIX, W_PP = 1, 2
W_FOLD = len(BIG_WEIGHTS)
HANDED_ON = tuple(w for w in range(len(BIG_WEIGHTS)) if w not in (W_MIX, W_PP)) + (W_FOLD,)
BF16_SHAPES = tuple((rows, cols) for _, rows, cols in BIG_WEIGHTS) + ((POOL_WIDTH, D_MODEL),)
CONVERT_ROWS = 128
STAGE_SLOTS = 4
STAGE_COLS = max(cols for _, _, cols in BIG_WEIGHTS)


def _weight_chunks():
    chunks = []
    for w, (_, nrows, cols) in enumerate(BIG_WEIGHTS):
        side_by_side = max(1, STAGE_COLS // cols)
        blocks = [(r0, min(CONVERT_ROWS, nrows - r0)) for r0 in range(0, nrows, CONVERT_ROWS)]
        for first in range(0, len(blocks), side_by_side):
            group = blocks[first:first + side_by_side]
            chunks.append((w, tuple((r0, rows, k * cols) for k, (r0, rows) in enumerate(group))))
    return tuple(chunks)


WEIGHT_CHUNKS = _weight_chunks()


def _writeback_copy(w, w_v, w_bf_hbm, out_sem):
    return pltpu.make_async_copy(w_v[w], w_bf_hbm[w], out_sem.at[w])


def _convert_weights(w_hbm, w_v, stage, stage_sem, mixf_s, params_ref):
    def chunk_copies(n):
        w, pieces = WEIGHT_CHUNKS[n]
        cols = BIG_WEIGHTS[w][2]
        slot = n % STAGE_SLOTS
        return [pltpu.make_async_copy(w_hbm[w].at[pl.ds(r0, rows), :],
                                      stage.at[slot, pl.ds(0, rows), pl.ds(c0, cols)], stage_sem.at[slot])
                for r0, rows, c0 in pieces]

    def convert(n):
        w, pieces = WEIGHT_CHUNKS[n]
        cols = BIG_WEIGHTS[w][2]
        for r0, rows, c0 in pieces:
            staged = stage[n % STAGE_SLOTS, 0:rows, c0:c0 + cols]
            if w == W_MIX:
                mixf_s[r0:r0 + rows, :] = staged
            else:
                w_v[w][r0:r0 + rows, :] = staged.astype(BF16)

    ahead = STAGE_SLOTS - 1
    for m in range(ahead):
        for copy in chunk_copies(m):
            copy.start()
    for n in range(len(WEIGHT_CHUNKS)):
        if n + ahead < len(WEIGHT_CHUNKS):
            for copy in chunk_copies(n + ahead):
                copy.start()
        for copy in chunk_copies(n):
            copy.wait()
        convert(n)

    for g in range(len(POOL_WINDOWS)):
        grp = slice(g * POOL_GROUP_WIDTH, (g + 1) * POOL_GROUP_WIDTH)
        scale = params_ref[ROW_PSCALE:ROW_PSCALE + 1, grp]
        w_v[W_FOLD][grp, :] = _dot((mixf_s[grp, :] * scale).astype(BF16), w_v[W_PP][grp, :]).astype(BF16)


def _prompt_kernel(*refs):
    nw = len(BIG_WEIGHTS)
    sinks_ref, x_ref, params_ref = refs[:3]
    w_hbm = refs[3:3 + nw]
    pos = 3 + nw
    outs = refs[pos:pos + 4]
    pos += 4
    w_bf_hbm = dict(zip(HANDED_ON, refs[pos:pos + len(HANDED_ON)]))
    pos += len(HANDED_ON)
    w_v = refs[pos:pos + nw + 1]
    pos += nw + 1
    stage, stage_sem, out_sem, mixf_s = refs[pos:pos + 4]
    scratch = refs[pos + 4:]
    w_in_ref, _, _, wap_ref, wout_ref, wg_ref, wu_ref, wd_ref, wfold_ref = w_v

    step = pl.program_id(0) * pl.num_programs(1) + pl.program_id(1)
    n_steps = pl.num_programs(0) * pl.num_programs(1)

    @pl.when(step == 0)
    def _():
        _convert_weights(w_hbm, w_v, stage, stage_sem, mixf_s, params_ref)

    @pl.when(step == 1)
    def _():
        for w in HANDED_ON:
            _writeback_copy(w, w_v, w_bf_hbm, out_sem).start()

    def body(t, carry):
        _prompt_tile(t, sinks_ref, x_ref, params_ref, w_in_ref, wfold_ref, wap_ref, wout_ref, wg_ref, wu_ref, wd_ref,
                     *outs, *scratch)
        return carry
    lax.fori_loop(0, TILES_PER_STEP, body, 0)

    @pl.when(step == n_steps - 1)
    def _():
        for w in HANDED_ON:
            _writeback_copy(w, w_v, w_bf_hbm, out_sem).wait()


def _prompt_tile(t, sinks_ref, x_ref, params_ref, w_in_ref, wfold_ref, wap_ref, wout_ref, wg_ref, wu_ref, wd_ref,
                 y_ref, knew_ref, vnew_ref, pnew_ref,
                 h_s, u_s, q_s, kd_s, vd_s, s_s, p_s, ypool_s, yattn_s, sga_s, sgb_s, merged_s, x1_s, h2_s, act_s):
    T = SEQ_TILE
    R = SUB_TILE
    subs = [slice(r0, r0 + R) for r0 in range(0, T, R)]
    n_blocks = T // ATT_BLOCK
    j = pl.program_id(1) * TILES_PER_STEP + t
    rows_per_kv = GROUP * ATT_BLOCK

    def in_block(rows):
        return pl.ds(pl.multiple_of(t * T + rows.start, R), R)

    @pl.when(j == 0)
    def _():
        u_s[0:PREFIX, :] = jnp.zeros((PREFIX, POOL_WIDTH), F32)
        kd_s[:, 0:ATT_BLOCK, :] = jnp.zeros((N_KV_HEADS, ATT_BLOCK, KV_WIDTH), BF16)
        vd_s[:, 0:ATT_BLOCK, :] = jnp.zeros((N_KV_HEADS, ATT_BLOCK, KV_WIDTH), BF16)

    lane_r = lax.broadcasted_iota(jnp.int32, (R, LANES), 1)
    lane = lax.broadcasted_iota(jnp.int32, (ATT_BLOCK, LANES), 1)
    row = lax.broadcasted_iota(jnp.int32, (ATT_BLOCK, 2 * ATT_BLOCK), 0)
    col = lax.broadcasted_iota(jnp.int32, (ATT_BLOCK, 2 * ATT_BLOCK), 1)
    band = (col >= row) & (col <= row + WINDOW)
    first_lo = jnp.where(j == 0, ATT_BLOCK, 0)

    def norm1(rows):
        x = x_ref[0, in_block(rows), :]
        h_s[rows, :] = ((x * _rms_scale(x)) * _param(params_ref, ROW_N1, D_MODEL)).astype(BF16)

    def in_proj(rows):
        h = h_s[rows, :]
        u_s[PREFIX + rows.start:PREFIX + rows.stop, :] = _dot(h, w_in_ref[:, OFF_U:OFF_U + POOL_WIDTH])
        q = _dot(h, w_in_ref[:, OFF_Q:OFF_Q + Q_WIDTH])
        q_s[rows, :] = _head_norm(q, _param(params_ref, ROW_QN, Q_WIDTH)).astype(BF16)
        kv = _dot(h, w_in_ref[:, OFF_K:OFF_K + 2 * KV_WIDTH])
        kn = _head_norm(kv[:, 0:KV_WIDTH], _param(params_ref, ROW_KN, KV_WIDTH))
        v = kv[:, KV_WIDTH:]
        dst_rows = slice(ATT_BLOCK + rows.start, ATT_BLOCK + rows.stop)
        for src, dst in ((kn, kd_s), (v, vd_s)):
            swapped = pltpu.roll(src, HEAD_DIM, 1)
            dst[0, dst_rows, :] = jnp.where(lane_r < HEAD_DIM, src, swapped).astype(BF16)
            dst[1, dst_rows, :] = jnp.where(lane_r < HEAD_DIM, swapped, src).astype(BF16)
        if rows.stop == T:
            knew_ref[0] = kn[R - ATT_BLOCK:, :].T
            vnew_ref[0] = v[R - ATT_BLOCK:, :].T
            batch = pl.program_id(0)
            for r in range(POOL_BUF):
                src = PREFIX + T - POOL_BUF + r
                pnew_ref[r, pl.ds(batch, 1), :] = u_s[src:src + 1, :]

    def pool(rows):
        pos1 = j * T + rows.start + lax.broadcasted_iota(jnp.int32, (R, 1), 0) + 1
        for g, w in enumerate(POOL_WINDOWS):
            cols = slice(g * POOL_GROUP_WIDTH, (g + 1) * POOL_GROUP_WIDTH)
            a = u_s[rows.start:rows.stop + PREFIX, cols]
            s = a
            shift = 1
            while shift < w:
                s = s + pltpu.roll(s, shift, 0)
                shift *= 2
            inv_cnt = 1.0 / jnp.minimum(pos1, w).astype(F32)
            ypool_s[rows, cols] = (s[PREFIX:, :] * inv_cnt - a[PREFIX:, :]).astype(BF16)

    def gates(rows, chunks):
        h = h_s[rows, :]
        for c in chunks:
            _gate_chunk(h, c, w_in_ref, sga_s, sgb_s, rows)

    def scores(b):
        r0 = b * ATT_BLOCK
        for c in range(N_KV_HEADS):
            parts = []
            for p in range(c * Q_COLS // N_KV_HEADS, (c + 1) * Q_COLS // N_KV_HEADS):
                qcol = q_s[r0:r0 + ATT_BLOCK, p * LANES:(p + 1) * LANES]
                parts.append(jnp.where(lane < HEAD_DIM, qcol, jnp.zeros_like(qcol)))
                parts.append(jnp.where(lane >= HEAD_DIM, qcol, jnp.zeros_like(qcol)))
            s_s[b % 2, c * rows_per_kv:(c + 1) * rows_per_kv, :] = _dot_nt(
                jnp.concatenate(parts, axis=0), kd_s[c, r0:r0 + 2 * ATT_BLOCK, :])

    def softmax(b):
        valid = band & (col >= first_lo) if b == 0 else band
        inv_denoms = []
        for head in range(N_HEADS):
            sh = jnp.where(valid, s_s[b % 2, head * ATT_BLOCK:(head + 1) * ATT_BLOCK, :], NEG)
            sink = sinks_ref[head] * LOG2_E
            m = jnp.maximum(jnp.max(sh, axis=-1, keepdims=True), sink)
            p = jnp.exp2(sh - m)
            denom = jnp.sum(p, axis=-1, keepdims=True) + jnp.exp2(sink - m)
            p_s[b % 2, head * ATT_BLOCK:(head + 1) * ATT_BLOCK, :] = p.astype(BF16)
            inv_denoms.append(1.0 / denom)
        return inv_denoms

    def weighted_values(b, inv_denoms):
        r0 = b * ATT_BLOCK
        for c in range(N_KV_HEADS):
            o = _dot(p_s[b % 2, c * rows_per_kv:(c + 1) * rows_per_kv, :], vd_s[c, r0:r0 + 2 * ATT_BLOCK, :])
            for pp in range(Q_COLS // N_KV_HEADS):
                p_col = c * Q_COLS // N_KV_HEADS + pp
                head_lo = p_col * HEADS_PER_COL
                o_lo = o[(2 * pp) * ATT_BLOCK:(2 * pp + 1) * ATT_BLOCK, :] * inv_denoms[head_lo]
                o_hi = o[(2 * pp + 1) * ATT_BLOCK:(2 * pp + 2) * ATT_BLOCK, :] * inv_denoms[head_lo + 1]
                yattn_s[r0:r0 + ATT_BLOCK, p_col * LANES:(p_col + 1) * LANES] = (
                    jnp.where(lane < HEAD_DIM, o_lo, o_hi).astype(BF16))

    def out_proj(rows):
        _merge(ypool_s[rows, :], yattn_s[rows, :], sga_s, sgb_s, wfold_ref, wap_ref, merged_s, rows)
        x1 = x_ref[0, in_block(rows), :] + _dot(merged_s[rows, :], wout_ref[...])
        x1_s[rows, :] = x1
        h2_s[rows, :] = ((x1 * _rms_scale(x1)) * _param(params_ref, ROW_N2, D_MODEL)).astype(BF16)

    for rows in subs:
        norm1(rows)
    for rows in subs:
        in_proj(rows)
    pool(subs[0])
    gates(subs[0], range(N_GATE_CHUNKS))
    for rows in subs[1:]:
        pool(rows)
    later_gates = [(rows, c) for rows in subs[1:] for c in range(N_GATE_CHUNKS)]
    per_block = -(-len(later_gates) // n_blocks)
    scores(0)
    for b in range(n_blocks):
        if b + 1 < n_blocks:
            scores(b + 1)
        for rows, c in later_gates[b * per_block:(b + 1) * per_block]:
            gates(rows, [c])
        weighted_values(b, softmax(b))

    u_s[0:PREFIX, :] = u_s[T:T + PREFIX, :]
    kd_s[:, 0:ATT_BLOCK, :] = kd_s[:, T:T + ATT_BLOCK, :]
    vd_s[:, 0:ATT_BLOCK, :] = vd_s[:, T:T + ATT_BLOCK, :]

    for rows in subs:
        out_proj(rows)
    for rows in subs:
        _ffn_act(h2_s[rows, :], wg_ref, wu_ref, act_s, rows)
    for rows in subs:
        y_ref[0, in_block(rows), :] = x1_s[rows, :] + _dot(act_s[rows, :], wd_ref[...])


def _const_spec(shape):
    nd = len(shape)
    return pl.BlockSpec(shape, lambda *_: (0,) * nd, pipeline_mode=pl.Buffered(1))


def _prompt_call(x, sinks, params, big):
    B, S, _ = x.shape
    T = SEQ_TILE
    step_rows = T * TILES_PER_STEP
    assert tuple(b.shape for b in big) == BF16_SHAPES[:len(BIG_WEIGHTS)]
    assert B * (S // step_rows) >= 2, "the bf16 write-back starts on the second grid step"
    in_specs = [pl.BlockSpec(memory_space=pltpu.SMEM),
                pl.BlockSpec((1, step_rows, D_MODEL), lambda b, j: (b, j, 0)),
                _const_spec(params.shape)]
    in_specs += [pl.BlockSpec(memory_space=pl.ANY) for _ in big]
    out_shape = (jax.ShapeDtypeStruct((B, S, D_MODEL), F32),
                 jax.ShapeDtypeStruct((B, KV_WIDTH, ATT_BLOCK), F32),
                 jax.ShapeDtypeStruct((B, KV_WIDTH, ATT_BLOCK), F32),
                 jax.ShapeDtypeStruct((POOL_BUF, B, POOL_WIDTH), F32))
    out_shape += tuple(jax.ShapeDtypeStruct(BF16_SHAPES[w], BF16) for w in HANDED_ON)
    out_specs = (pl.BlockSpec((1, step_rows, D_MODEL), lambda b, j: (b, j, 0)),
                 pl.BlockSpec((1, KV_WIDTH, ATT_BLOCK), lambda b, j: (b, 0, 0)),
                 pl.BlockSpec((1, KV_WIDTH, ATT_BLOCK), lambda b, j: (b, 0, 0)),
                 pl.BlockSpec((POOL_BUF, B, POOL_WIDTH), lambda b, j: (0, 0, 0)))
    out_specs += tuple(pl.BlockSpec(memory_space=pl.ANY) for _ in HANDED_ON)
    scratch = [pltpu.VMEM(shape, BF16) for shape in BF16_SHAPES]
    scratch += [pltpu.VMEM((STAGE_SLOTS, CONVERT_ROWS, STAGE_COLS), F32),
                pltpu.SemaphoreType.DMA((STAGE_SLOTS,)),
                pltpu.SemaphoreType.DMA((len(BF16_SHAPES),)),
                pltpu.VMEM(BF16_SHAPES[W_MIX], F32)]
    scratch += [pltpu.VMEM((T, D_MODEL), BF16),
               pltpu.VMEM((PREFIX + T, POOL_WIDTH), F32),
               pltpu.VMEM((T, Q_WIDTH), BF16),
               pltpu.VMEM((N_KV_HEADS, ATT_BLOCK + T, KV_WIDTH), BF16),
               pltpu.VMEM((N_KV_HEADS, ATT_BLOCK + T, KV_WIDTH), BF16),
               pltpu.VMEM((2, N_HEADS * ATT_BLOCK, 2 * ATT_BLOCK), F32),
               pltpu.VMEM((2, N_HEADS * ATT_BLOCK, 2 * ATT_BLOCK), BF16),
               pltpu.VMEM((T, POOL_WIDTH), BF16),
               pltpu.VMEM((T, Q_WIDTH), BF16),
               pltpu.VMEM((T, D_MODEL), F32),
               pltpu.VMEM((T, D_MODEL), F32),
               pltpu.VMEM((T, D_MODEL), BF16),
               pltpu.VMEM((T, D_MODEL), F32),
               pltpu.VMEM((T, D_MODEL), BF16),
               pltpu.VMEM((T, D_FF), BF16)]
    return pl.pallas_call(
        _prompt_kernel,
        grid=(B, S // step_rows),
        in_specs=in_specs,
        out_specs=out_specs,
        out_shape=out_shape,
        scratch_shapes=scratch,
        compiler_params=pltpu.CompilerParams(
            dimension_semantics=("arbitrary", "arbitrary"),
            vmem_limit_bytes=VMEM_LIMIT_BYTES),
        name="prompt_layer",
    )(sinks, x, params, *big)


def _sample_kernel(*refs):
    nh = len(HANDED_ON)
    n_in = 6
    (x_ref, ck_ref, cv_ref, sp_ref, sinks_ref, params_ref) = refs[:n_in]
    w_hbm = refs[n_in:n_in + nh]
    pos = n_in + nh
    y_ref, knew_ref, vnew_ref, pnew_ref = refs[pos:pos + 4]
    pos += 4
    w_v = refs[pos:pos + nh]
    pos += nh
    (w_sem, u_s, kt_s, vt_s, q8_s, k8_s, v8_s, o8_s, ypre_s, sga_s, sgb_s, merged_s, act_s) = refs[pos:]
    w_in_ref, wap_ref, wout_ref, wg_ref, wu_ref, wd_ref, wfold_ref = w_v

    SB = SAMPLE_BLOCK
    col_tiles = D_MODEL // LANES
    N = x_ref.shape[0] // col_tiles
    nsteps = N // SB
    i = pl.program_id(0)
    lane = lax.broadcasted_iota(jnp.int32, (N, LANES), 1)
    all_rows = slice(0, N)

    def load_x():
        return jnp.concatenate([x_ref[pl.ds(ct, N, stride=col_tiles), :] for ct in range(col_tiles)], axis=1)

    def weight_copy(k):
        return pltpu.make_async_copy(w_hbm[k], w_v[k], w_sem.at[k])

    @pl.when(i == 0)
    def _():
        for k in range(nh):
            weight_copy(k).start()
        weight_copy(0).wait()
        x = load_x()
        h = ((x * _rms_scale(x)) * _param(params_ref, ROW_N1, D_MODEL)).astype(BF16)
        u_s[...] = _dot(h, w_in_ref[:, OFF_U:OFF_U + POOL_WIDTH])
        for gc in range(N_GATE_CHUNKS):
            _gate_chunk(h, gc, w_in_ref, sga_s, sgb_s, all_rows)
        q = _head_norm(_dot(h, w_in_ref[:, OFF_Q:OFF_Q + Q_WIDTH]), _param(params_ref, ROW_QN, Q_WIDTH))
        kv = _dot(h, w_in_ref[:, OFF_K:OFF_K + 2 * KV_WIDTH])
        kn = _head_norm(kv[:, 0:KV_WIDTH], _param(params_ref, ROW_KN, KV_WIDTH))
        v = kv[:, KV_WIDTH:]
        kt_s[...] = kn.T
        vt_s[...] = v.T
        for head in range(N_HEADS):
            k8_s[pl.ds(head, N, stride=N_HEADS), :] = kn
            v8_s[pl.ds(head, N, stride=N_HEADS), :] = v
        for p in range(Q_COLS):
            qcol = q[:, p * LANES:(p + 1) * LANES]
            qswap = pltpu.roll(qcol, HEAD_DIM, 1)
            for e in range(HEADS_PER_COL):
                head = p * HEADS_PER_COL + e
                c = _kv_head(head)
                src = qcol if e == c else qswap
                keep = (lane < HEAD_DIM) if c == 0 else (lane >= HEAD_DIM)
                q8_s[pl.ds(head, N, stride=N_HEADS), :] = jnp.where(keep, src, 0.0)

    base = i * SB
    rows = pl.ds(pl.multiple_of(base, SB), SB)

    unew = u_s[rows, :]
    for g, w in enumerate(POOL_WINDOWS):
        cols = slice(g * POOL_GROUP_WIDTH, (g + 1) * POOL_GROUP_WIDTH)
        s = unew[:, cols]
        for r in range(POOL_BUF - (w - 1), POOL_BUF):
            s = s + sp_ref[r, :, cols]
        ypre_s[rows, cols] = (s * (1.0 / w) - unew[:, cols]).astype(BF16)
    pnew_ref[0:POOL_BUF - 1] = sp_ref[1:POOL_BUF]
    pnew_ref[POOL_BUF - 1] = unew

    hrows = pl.ds(pl.multiple_of(base * N_HEADS, SB * N_HEADS), SB * N_HEADS)
    lhs = q8_s[hrows, :]
    sc = jnp.concatenate(
        [_dot(lhs[bb * N_HEADS:(bb + 1) * N_HEADS, :].astype(BF16), ck_ref[bb].astype(BF16)) for bb in range(SB)],
        axis=0)
    sc_self = jnp.sum(lhs * k8_s[hrows, :], axis=-1, keepdims=True)
    row_head = lax.rem(lax.broadcasted_iota(jnp.int32, (SB * N_HEADS, 1), 0), N_HEADS)
    sink = jnp.zeros((SB * N_HEADS, 1), F32)
    for head in range(N_HEADS):
        sink = jnp.where(row_head == head, sinks_ref[head] * LOG2_E, sink)
    m = jnp.maximum(jnp.maximum(jnp.max(sc, axis=-1, keepdims=True), sc_self), sink)
    p = jnp.exp2(sc - m)
    p_self = jnp.exp2(sc_self - m)
    denom = jnp.sum(p, axis=-1, keepdims=True) + p_self + jnp.exp2(sink - m)
    inv = 1.0 / denom
    pn = p * inv
    o = jnp.concatenate(
        [_dot_nt(pn[bb * N_HEADS:(bb + 1) * N_HEADS, :].astype(BF16), cv_ref[bb].astype(BF16)) for bb in range(SB)],
        axis=0)
    o8_s[hrows, :] = o + (p_self * inv) * v8_s[hrows, :]

    lane_c = lax.broadcasted_iota(jnp.int32, (KV_WIDTH, WINDOW), 1)
    kt_all = kt_s[...]
    vt_all = vt_s[...]
    for bb in range(SB):
        put = (WINDOW - 1) - (base + bb)
        knew_ref[bb] = jnp.where(lane_c == WINDOW - 1, pltpu.roll(kt_all, put, 1),
                                 pltpu.roll(ck_ref[bb], WINDOW - 1, 1))
        vnew_ref[bb] = jnp.where(lane_c == WINDOW - 1, pltpu.roll(vt_all, put, 1),
                                 pltpu.roll(cv_ref[bb], WINDOW - 1, 1))

    @pl.when(i == nsteps - 1)
    def _():
        for k in range(1, nh):
            weight_copy(k).wait()
        ycols = []
        for p in range(Q_COLS):
            halves = []
            for e in range(HEADS_PER_COL):
                head = p * HEADS_PER_COL + e
                oh = o8_s[pl.ds(head, N, stride=N_HEADS), :]
                halves.append(oh if e == _kv_head(head) else pltpu.roll(oh, HEAD_DIM, 1))
            ycols.append(jnp.where(lane < HEAD_DIM, halves[0], halves[1]).astype(BF16))
        yattn = jnp.concatenate(ycols, axis=1)
        _merge(ypre_s[...], yattn, sga_s, sgb_s, wfold_ref, wap_ref, merged_s, all_rows)
        x1 = load_x() + _dot(merged_s[...], wout_ref[...])
        h2 = ((x1 * _rms_scale(x1)) * _param(params_ref, ROW_N2, D_MODEL)).astype(BF16)
        _ffn_act(h2, wg_ref, wu_ref, act_s, all_rows)
        y = x1 + _dot(act_s[...], wd_ref[...])
        for ct in range(col_tiles):
            y_ref[pl.ds(ct, N, stride=col_tiles), :] = y[:, ct * LANES:(ct + 1) * LANES]


def _sample_call(x, ck, cv, sp, sinks, params, big):
    N = x.shape[0] // (D_MODEL // LANES)
    SB = SAMPLE_BLOCK
    W = WINDOW
    assert tuple(b.shape for b in big) == tuple(BF16_SHAPES[w] for w in HANDED_ON)
    consts = (sinks, params)
    in_specs = [_const_spec(x.shape),
                pl.BlockSpec((SB, KV_WIDTH, W), lambda i: (i, 0, 0)),
                pl.BlockSpec((SB, KV_WIDTH, W), lambda i: (i, 0, 0)),
                pl.BlockSpec((POOL_BUF, SB, POOL_WIDTH), lambda i: (0, i, 0)),
                pl.BlockSpec(memory_space=pltpu.SMEM),
                _const_spec(params.shape)]
    in_specs += [pl.BlockSpec(memory_space=pl.ANY) for _ in big]
    out_shape = (jax.ShapeDtypeStruct(x.shape, F32),
                 jax.ShapeDtypeStruct((N, KV_WIDTH, W), F32),
                 jax.ShapeDtypeStruct((N, KV_WIDTH, W), F32),
                 jax.ShapeDtypeStruct((POOL_BUF, N, POOL_WIDTH), F32))
    out_specs = (pl.BlockSpec(x.shape, lambda i: (0, 0)),
                 pl.BlockSpec((SB, KV_WIDTH, W), lambda i: (i, 0, 0)),
                 pl.BlockSpec((SB, KV_WIDTH, W), lambda i: (i, 0, 0)),
                 pl.BlockSpec((POOL_BUF, SB, POOL_WIDTH), lambda i: (0, i, 0)))
    scratch = [pltpu.VMEM(b.shape, BF16) for b in big]
    scratch += [pltpu.SemaphoreType.DMA((len(big),)),
                pltpu.VMEM((N, POOL_WIDTH), F32),
               pltpu.VMEM((KV_WIDTH, N), F32),
               pltpu.VMEM((KV_WIDTH, N), F32),
               pltpu.VMEM((N * N_HEADS, LANES), F32),
               pltpu.VMEM((N * N_HEADS, LANES), F32),
               pltpu.VMEM((N * N_HEADS, LANES), F32),
               pltpu.VMEM((N * N_HEADS, LANES), F32),
               pltpu.VMEM((N, POOL_WIDTH), BF16),
               pltpu.VMEM((N, D_MODEL), F32),
               pltpu.VMEM((N, D_MODEL), F32),
               pltpu.VMEM((N, D_MODEL), BF16),
               pltpu.VMEM((N, D_FF), BF16)]
    return pl.pallas_call(
        _sample_kernel,
        grid=(N // SB,),
        in_specs=in_specs,
        out_specs=out_specs,
        out_shape=out_shape,
        scratch_shapes=scratch,
        compiler_params=pltpu.CompilerParams(
            dimension_semantics=("arbitrary",),
            vmem_limit_bytes=VMEM_LIMIT_BYTES),
        name="sample_layer",
    )(x, ck, cv, sp, *consts, *big)


def _cache_to_device_order(c):
    n, w = c.shape[0], c.shape[1]
    return jnp.transpose(c, (0, 2, 3, 1)).reshape(n, KV_WIDTH, w)


def _cache_from_device_order(c):
    n, _, w = c.shape
    return jnp.transpose(c.reshape(n, N_KV_HEADS, HEAD_DIM, w), (0, 3, 1, 2))[None]


@jax.jit
def _forward(x_prompt, x_sample, cache_k, cache_v, state_pool, norm1, w_in, q_norm, k_norm, sinks,
             pool_mix_w, pool_scale, w_pool_proj, w_attn_proj, w_out, norm2, w_gate, w_up, w_down):
    depth = w_in.shape[0]
    assert depth == 1, "single-layer trunk"
    l = 0
    N = x_sample.shape[0]
    assert cache_k.shape[2] == WINDOW and x_sample.shape[1] == 1

    rows = {ROW_N1: norm1[l], ROW_N2: norm2[l],
            ROW_QN: jnp.tile(q_norm[l], N_HEADS) * (HEAD_DIM ** -0.5 * LOG2_E),
            ROW_KN: jnp.tile(k_norm[l], N_KV_HEADS), ROW_PSCALE: pool_scale[l]}
    pieces = []
    for r in range(PARAM_ROWS):
        v = rows.get(r)
        used = 0 if v is None else v.shape[0]
        pieces += ([] if v is None else [v]) + ([jnp.zeros((D_MODEL - used,), F32)] if used < D_MODEL else [])
    params = jnp.concatenate(pieces).reshape(PARAM_ROWS, D_MODEL)
    big = (w_in[l], pool_mix_w[l].reshape(len(POOL_WINDOWS) * POOL_GROUP_WIDTH, POOL_GROUP_WIDTH),
           w_pool_proj[l], w_attn_proj[l], w_out[l], w_gate[l], w_up[l], w_down[l])

    y_p, k_p, v_p, pool_p, *weights_bf16 = _prompt_call(x_prompt, sinks[l], params, big)

    y_s, k_s, v_s, pool_s = _sample_call(
        x_sample.reshape(N * (D_MODEL // LANES), LANES),
        _cache_to_device_order(cache_k[l]), _cache_to_device_order(cache_v[l]),
        jnp.transpose(state_pool[l], (1, 0, 2)), sinks[l], params, tuple(weights_bf16))

    return (y_p, y_s.reshape(N, 1, D_MODEL),
            _cache_from_device_order(k_p), _cache_from_device_order(v_p), jnp.transpose(pool_p, (1, 0, 2))[None],
            _cache_from_device_order(k_s), _cache_from_device_order(v_s),
            jnp.transpose(pool_s, (1, 0, 2))[None])


def kernel(x_prompt, x_sample, cache_k, cache_v, state_pool, norm1, w_in, q_norm, k_norm, sinks, pool_mix_w,
           pool_scale, w_pool_proj, w_attn_proj, w_out, norm2, w_gate, w_up, w_down):
    return _forward(x_prompt, x_sample, cache_k, cache_v, state_pool, norm1, w_in, q_norm, k_norm, sinks,
                    pool_mix_w, pool_scale, w_pool_proj, w_attn_proj, w_out, norm2, w_gate, w_up, w_down)
```

```python
import numpy as np
import jax
import jax.numpy as jnp
from jax import lax
from jax.experimental import pallas as pl
from jax.experimental.pallas import tpu as pltpu

D_MODEL = 1024
POOL_WINDOWS = (2, 4, 8, 16)
POOL_GROUP_WIDTH = 128
POOL_WIDTH = 512
POOL_BUF = 15
N_HEADS = 8
N_KV_HEADS = 2
HEAD_DIM = 64
GROUP = N_HEADS // N_KV_HEADS
Q_WIDTH = 512
KV_WIDTH = 128
WINDOW = 128
D_FF = 2816
EPS = 1e-6
NEG = -1e30
LOG2_E = 1.4426950408889634

LANES = 128
ATT_BLOCK = WINDOW
SEQ_TILE = 512
TILES_PER_STEP = 1
SUB_TILE = 256
FF_CHUNK = 256
GATE_CHUNK = 256
SAMPLE_BLOCK = 16
PREFIX = 16
VMEM_LIMIT_BYTES = 58 * 1024 * 1024
Q_COLS = Q_WIDTH // LANES
HEADS_PER_COL = LANES // HEAD_DIM
N_GATE_CHUNKS = D_MODEL // GATE_CHUNK

OFF_U = 0
OFF_Q = OFF_U + POOL_WIDTH
OFF_K = OFF_Q + Q_WIDTH
OFF_V = OFF_K + KV_WIDTH
OFF_GA = OFF_V + KV_WIDTH
OFF_GB = OFF_GA + D_MODEL

BF16 = jnp.bfloat16
F32 = jnp.float32

ROW_N1, ROW_N2, ROW_QN, ROW_KN, ROW_PSCALE, PARAM_ROWS = 0, 1, 2, 3, 4, 8


def _param(params_ref, row, width):
    return params_ref[row:row + 1, 0:width]


def _dot(a, b):
    return jnp.dot(a, b, preferred_element_type=F32)


def _dot_nt(a, b):
    return lax.dot_general(a, b, (((1,), (1,)), ((), ())), preferred_element_type=F32)


def _rms_scale(x):
    return lax.rsqrt(jnp.mean(x * x, axis=-1, keepdims=True) + EPS)


def _head_norm(x, gain):
    sq = x * x
    low = lax.broadcasted_iota(jnp.int32, (x.shape[0], LANES), 1) < HEAD_DIM
    cols = []
    for c in range(0, x.shape[1], LANES):
        blk = sq[:, c:c + LANES]
        ss_lo = jnp.sum(jnp.where(low, blk, 0.0), axis=-1, keepdims=True)
        ss_hi = jnp.sum(jnp.where(low, 0.0, blk), axis=-1, keepdims=True)
        cols.append(jnp.where(low, ss_lo, ss_hi))
    ss = jnp.concatenate(cols, axis=1) if len(cols) > 1 else cols[0]
    return (x * lax.rsqrt(ss * (1.0 / HEAD_DIM) + EPS)) * gain


def _kv_head(head):
    return head // GROUP


def _ffn_act(h2, wg_ref, wu_ref, act_ref, rows):
    for start in range(0, D_FF, FF_CHUNK):
        width = min(FF_CHUNK, D_FF - start)
        g = _dot(h2, wg_ref[:, start:start + width])
        u = _dot(h2, wu_ref[:, start:start + width])
        act_ref[rows, start:start + width] = (g * jax.nn.sigmoid(g) * u).astype(BF16)


def _gate_chunk(h, c, w_in_ref, sga_ref, sgb_ref, rows):
    lo = c * GATE_CHUNK
    sga_ref[rows, lo:lo + GATE_CHUNK] = jax.nn.sigmoid(_dot(h, w_in_ref[:, OFF_GA + lo:OFF_GA + lo + GATE_CHUNK]))
    sgb_ref[rows, lo:lo + GATE_CHUNK] = jax.nn.sigmoid(_dot(h, w_in_ref[:, OFF_GB + lo:OFF_GB + lo + GATE_CHUNK]))


def _merge(ypool, yattn, sga_ref, sgb_ref, wpp_ref, wap_ref, merged_ref, rows):
    for c in range(N_GATE_CHUNKS):
        lo = c * GATE_CHUNK
        pp = _dot(ypool, wpp_ref[:, lo:lo + GATE_CHUNK])
        ap = _dot(yattn, wap_ref[:, lo:lo + GATE_CHUNK])
        merged = sga_ref[rows, lo:lo + GATE_CHUNK] * pp + sgb_ref[rows, lo:lo + GATE_CHUNK] * ap
        merged_ref[rows, lo:lo + GATE_CHUNK] = merged.astype(BF16)


BIG_WEIGHTS = (("w_in", D_MODEL, OFF_GB + D_MODEL), ("mix", len(POOL_WINDOWS) * POOL_GROUP_WIDTH, POOL_GROUP_WIDTH),
               ("wpp", POOL_WIDTH, D_MODEL), ("wap", Q_WIDTH, D_MODEL), ("wout", D_MODEL, D_MODEL),
               ("wg", D_MODEL, D_FF), ("wu", D_MODEL, D_FF), ("wd", D_FF, D_MODEL))
W_MIX, W_PP = 1, 2
W_FOLD = len(BIG_WEIGHTS)
HANDED_ON = tuple(w for w in range(len(BIG_WEIGHTS)) if w not in (W_MIX, W_PP)) + (W_FOLD,)
BF16_SHAPES = tuple((rows, cols) for _, rows, cols in BIG_WEIGHTS) + ((POOL_WIDTH, D_MODEL),)
CONVERT_ROWS = 128
STAGE_SLOTS = 4
STAGE_COLS = max(cols for _, _, cols in BIG_WEIGHTS)


def _weight_chunks():
    chunks = []
    for w, (_, nrows, cols) in enumerate(BIG_WEIGHTS):
        side_by_side = max(1, STAGE_COLS // cols)
        blocks = [(r0, min(CONVERT_ROWS, nrows - r0)) for r0 in range(0, nrows, CONVERT_ROWS)]
        for first in range(0, len(blocks), side_by_side):
            group = blocks[first:first + side_by_side]
            chunks.append((w, tuple((r0, rows, k * cols) for k, (r0, rows) in enumerate(group))))
    return tuple(chunks)


WEIGHT_CHUNKS = _weight_chunks()


def _writeback_copy(w, w_v, w_bf_hbm, out_sem):
    return pltpu.make_async_copy(w_v[w], w_bf_hbm[w], out_sem.at[w])


def _convert_weights(w_hbm, w_v, stage, stage_sem, mixf_s, params_ref):
    def chunk_copies(n):
        w, pieces = WEIGHT_CHUNKS[n]
        cols = BIG_WEIGHTS[w][2]
        slot = n % STAGE_SLOTS
        return [pltpu.make_async_copy(w_hbm[w].at[pl.ds(r0, rows), :],
                                      stage.at[slot, pl.ds(0, rows), pl.ds(c0, cols)], stage_sem.at[slot])
                for r0, rows, c0 in pieces]

    def convert(n):
        w, pieces = WEIGHT_CHUNKS[n]
        cols = BIG_WEIGHTS[w][2]
        for r0, rows, c0 in pieces:
            staged = stage[n % STAGE_SLOTS, 0:rows, c0:c0 + cols]
            if w == W_MIX:
                mixf_s[r0:r0 + rows, :] = staged
            else:
                w_v[w][r0:r0 + rows, :] = staged.astype(BF16)

    ahead = STAGE_SLOTS - 1
    for m in range(ahead):
        for copy in chunk_copies(m):
            copy.start()
    for n in range(len(WEIGHT_CHUNKS)):
        if n + ahead < len(WEIGHT_CHUNKS):
            for copy in chunk_copies(n + ahead):
                copy.start()
        for copy in chunk_copies(n):
            copy.wait()
        convert(n)

    for g in range(len(POOL_WINDOWS)):
        grp = slice(g * POOL_GROUP_WIDTH, (g + 1) * POOL_GROUP_WIDTH)
        scale = params_ref[ROW_PSCALE:ROW_PSCALE + 1, grp]
        w_v[W_FOLD][grp, :] = _dot((mixf_s[grp, :] * scale).astype(BF16), w_v[W_PP][grp, :]).astype(BF16)


def _prompt_kernel(*refs):
    nw = len(BIG_WEIGHTS)
    sinks_ref, x_ref, params_ref, bias_ref = refs[:4]
    w_hbm = refs[4:4 + nw]
    pos = 4 + nw
    outs = refs[pos:pos + 4]
    pos += 4
    w_bf_hbm = dict(zip(HANDED_ON, refs[pos:pos + len(HANDED_ON)]))
    pos += len(HANDED_ON)
    w_v = refs[pos:pos + nw + 1]
    pos += nw + 1
    stage, stage_sem, out_sem, mixf_s = refs[pos:pos + 4]
    scratch = refs[pos + 4:]
    w_in_ref, _, _, wap_ref, wout_ref, wg_ref, wu_ref, wd_ref, wfold_ref = w_v

    step = pl.program_id(0) * pl.num_programs(1) + pl.program_id(1)
    n_steps = pl.num_programs(0) * pl.num_programs(1)

    @pl.when(step == 0)
    def _():
        _convert_weights(w_hbm, w_v, stage, stage_sem, mixf_s, params_ref)

    @pl.when(step == 1)
    def _():
        for w in HANDED_ON:
            _writeback_copy(w, w_v, w_bf_hbm, out_sem).start()

    def body(t, carry):
        _prompt_tile(t, sinks_ref, x_ref, params_ref, bias_ref,
                     w_in_ref, wfold_ref, wap_ref, wout_ref, wg_ref, wu_ref, wd_ref, *outs, *scratch)
        return carry
    lax.fori_loop(0, TILES_PER_STEP, body, 0)

    @pl.when(step == n_steps - 1)
    def _():
        for w in HANDED_ON:
            _writeback_copy(w, w_v, w_bf_hbm, out_sem).wait()


def _prompt_tile(t, sinks_ref, x_ref, params_ref, bias_ref,
                 w_in_ref, wfold_ref, wap_ref, wout_ref, wg_ref, wu_ref, wd_ref,
                 y_ref, knew_ref, vnew_ref, pnew_ref,
                 h_s, u_s, q_s, kd_s, vd_s, s_s, p_s, ypool_s, yattn_s, sga_s, sgb_s, merged_s, x1_s, h2_s, act_s):
    T = SEQ_TILE
    R = SUB_TILE
    subs = [slice(r0, r0 + R) for r0 in range(0, T, R)]
    n_blocks = T // ATT_BLOCK
    j = pl.program_id(1) * TILES_PER_STEP + t
    rows_per_kv = GROUP * ATT_BLOCK

    def in_block(rows):
        return pl.ds(pl.multiple_of(t * T + rows.start, R), R)

    @pl.when(j == 0)
    def _():
        u_s[0:PREFIX, :] = jnp.zeros((PREFIX, POOL_WIDTH), F32)
        kd_s[:, 0:ATT_BLOCK, :] = jnp.zeros((N_KV_HEADS, ATT_BLOCK, KV_WIDTH), BF16)
        vd_s[:, 0:ATT_BLOCK, :] = jnp.zeros((N_KV_HEADS, ATT_BLOCK, KV_WIDTH), BF16)

    lane_r = lax.broadcasted_iota(jnp.int32, (R, LANES), 1)
    lane = lax.broadcasted_iota(jnp.int32, (ATT_BLOCK, LANES), 1)

    def norm1(rows):
        x = x_ref[0, in_block(rows), :]
        h_s[rows, :] = ((x * _rms_scale(x)) * _param(params_ref, ROW_N1, D_MODEL)).astype(BF16)

    def in_proj(rows):
        h = h_s[rows, :]
        u_s[PREFIX + rows.start:PREFIX + rows.stop, :] = _dot(h, w_in_ref[:, OFF_U:OFF_U + POOL_WIDTH])
        q = _dot(h, w_in_ref[:, OFF_Q:OFF_Q + Q_WIDTH])
        qn = _head_norm(q, _param(params_ref, ROW_QN, Q_WIDTH)).astype(BF16)
        for p in range(Q_COLS):
            qcol = qn[:, p * LANES:(p + 1) * LANES]
            lo, hi = 2 * p * LANES, (2 * p + 1) * LANES
            q_s[rows, lo:lo + LANES] = jnp.where(lane_r < HEAD_DIM, qcol, jnp.zeros_like(qcol))
            q_s[rows, hi:hi + LANES] = jnp.where(lane_r >= HEAD_DIM, qcol, jnp.zeros_like(qcol))
        kv = _dot(h, w_in_ref[:, OFF_K:OFF_K + 2 * KV_WIDTH])
        kn = _head_norm(kv[:, 0:KV_WIDTH], _param(params_ref, ROW_KN, KV_WIDTH))
        v = kv[:, KV_WIDTH:]
        dst_rows = slice(ATT_BLOCK + rows.start, ATT_BLOCK + rows.stop)
        for src, dst in ((kn, kd_s), (v, vd_s)):
            swapped = pltpu.roll(src, HEAD_DIM, 1)
            dst[0, dst_rows, :] = jnp.where(lane_r < HEAD_DIM, src, swapped).astype(BF16)
            dst[1, dst_rows, :] = jnp.where(lane_r < HEAD_DIM, swapped, src).astype(BF16)
        if rows.stop == T:
            knew_ref[0] = kn[R - ATT_BLOCK:, :].T
            vnew_ref[0] = v[R - ATT_BLOCK:, :].T
            batch = pl.program_id(0)
            for r in range(POOL_BUF):
                src = PREFIX + T - POOL_BUF + r
                pnew_ref[r, pl.ds(batch, 1), :] = u_s[src:src + 1, :]

    def pool(rows):
        pos1 = j * T + rows.start + lax.broadcasted_iota(jnp.int32, (R, 1), 0) + 1
        for g, w in enumerate(POOL_WINDOWS):
            cols = slice(g * POOL_GROUP_WIDTH, (g + 1) * POOL_GROUP_WIDTH)
            a = u_s[rows.start:rows.stop + PREFIX, cols]
            s = a
            shift = 1
            while shift < w:
                s = s + pltpu.roll(s, shift, 0)
                shift *= 2
            inv_cnt = 1.0 / jnp.minimum(pos1, w).astype(F32)
            ypool_s[rows, cols] = (s[PREFIX:, :] * inv_cnt - a[PREFIX:, :]).astype(BF16)

    def gates(rows, chunks):
        h = h_s[rows, :]
        for c in chunks:
            _gate_chunk(h, c, w_in_ref, sga_s, sgb_s, rows)

    def scores(b):
        r0 = b * ATT_BLOCK
        for c in range(N_KV_HEADS):
            parts = [q_s[r0:r0 + ATT_BLOCK, head * LANES:(head + 1) * LANES]
                     for head in range(c * GROUP, (c + 1) * GROUP)]
            s_s[b % 2, c * rows_per_kv:(c + 1) * rows_per_kv, :] = _dot_nt(
                jnp.concatenate(parts, axis=0), kd_s[c, r0:r0 + 2 * ATT_BLOCK, :])

    def softmax(b):
        bias = bias_ref[jnp.where(j == 0, 1, 0)] if b == 0 else bias_ref[0]
        inv_denoms = []
        for head in range(N_HEADS):
            sh = s_s[b % 2, head * ATT_BLOCK:(head + 1) * ATT_BLOCK, :] + bias
            sink = sinks_ref[head] * LOG2_E
            m = jnp.maximum(jnp.max(sh, axis=-1, keepdims=True), sink)
            p = jnp.exp2(sh - m)
            denom = jnp.sum(p, axis=-1, keepdims=True) + jnp.exp2(sink - m)
            p_s[b % 2, head * ATT_BLOCK:(head + 1) * ATT_BLOCK, :] = p.astype(BF16)
            inv_denoms.append(1.0 / denom)
        return inv_denoms

    def weighted_values(b, inv_denoms):
        r0 = b * ATT_BLOCK
        for c in range(N_KV_HEADS):
            o = _dot(p_s[b % 2, c * rows_per_kv:(c + 1) * rows_per_kv, :], vd_s[c, r0:r0 + 2 * ATT_BLOCK, :])
            for pp in range(Q_COLS // N_KV_HEADS):
                p_col = c * Q_COLS // N_KV_HEADS + pp
                head_lo = p_col * HEADS_PER_COL
                o_lo = o[(2 * pp) * ATT_BLOCK:(2 * pp + 1) * ATT_BLOCK, :] * inv_denoms[head_lo]
                o_hi = o[(2 * pp + 1) * ATT_BLOCK:(2 * pp + 2) * ATT_BLOCK, :] * inv_denoms[head_lo + 1]
                yattn_s[r0:r0 + ATT_BLOCK, p_col * LANES:(p_col + 1) * LANES] = (
                    jnp.where(lane < HEAD_DIM, o_lo, o_hi).astype(BF16))

    def out_proj(rows):
        _merge(ypool_s[rows, :], yattn_s[rows, :], sga_s, sgb_s, wfold_ref, wap_ref, merged_s, rows)
        x1 = x_ref[0, in_block(rows), :] + _dot(merged_s[rows, :], wout_ref[...])
        x1_s[rows, :] = x1
        h2_s[rows, :] = ((x1 * _rms_scale(x1)) * _param(params_ref, ROW_N2, D_MODEL)).astype(BF16)

    for rows in subs:
        norm1(rows)
    for rows in subs:
        in_proj(rows)
    pool(subs[0])
    gates(subs[0], range(N_GATE_CHUNKS))
    for rows in subs[1:]:
        pool(rows)
    later_gates = [(rows, c) for rows in subs[1:] for c in range(N_GATE_CHUNKS)]
    per_block = -(-len(later_gates) // n_blocks)
    scores(0)
    for b in range(n_blocks):
        if b + 1 < n_blocks:
            scores(b + 1)
        for rows, c in later_gates[b * per_block:(b + 1) * per_block]:
            gates(rows, [c])
        weighted_values(b, softmax(b))

    u_s[0:PREFIX, :] = u_s[T:T + PREFIX, :]
    kd_s[:, 0:ATT_BLOCK, :] = kd_s[:, T:T + ATT_BLOCK, :]
    vd_s[:, 0:ATT_BLOCK, :] = vd_s[:, T:T + ATT_BLOCK, :]

    for rows in subs:
        out_proj(rows)
    for rows in subs:
        _ffn_act(h2_s[rows, :], wg_ref, wu_ref, act_s, rows)
    for rows in subs:
        y_ref[0, in_block(rows), :] = x1_s[rows, :] + _dot(act_s[rows, :], wd_ref[...])


def _const_spec(shape):
    nd = len(shape)
    return pl.BlockSpec(shape, lambda *_: (0,) * nd, pipeline_mode=pl.Buffered(1))


def _band_bias():
    row = np.arange(ATT_BLOCK)[:, None]
    col = np.arange(2 * ATT_BLOCK)[None, :]
    band = (col >= row) & (col <= row + WINDOW)
    masks = np.stack([band, band & (col >= ATT_BLOCK)])
    return jnp.asarray(np.where(masks, 0.0, NEG), dtype=F32)


def _prompt_call(x, sinks, params, big):
    B, S, _ = x.shape
    T = SEQ_TILE
    step_rows = T * TILES_PER_STEP
    assert tuple(b.shape for b in big) == BF16_SHAPES[:len(BIG_WEIGHTS)]
    assert B * (S // step_rows) >= 2, "the bf16 write-back starts on the second grid step"
    bias = _band_bias()
    in_specs = [pl.BlockSpec(memory_space=pltpu.SMEM),
                pl.BlockSpec((1, step_rows, D_MODEL), lambda b, j: (b, j, 0)),
                _const_spec(params.shape),
                _const_spec(bias.shape)]
    in_specs += [pl.BlockSpec(memory_space=pl.ANY) for _ in big]
    out_shape = (jax.ShapeDtypeStruct((B, S, D_MODEL), F32),
                 jax.ShapeDtypeStruct((B, KV_WIDTH, ATT_BLOCK), F32),
                 jax.ShapeDtypeStruct((B, KV_WIDTH, ATT_BLOCK), F32),
                 jax.ShapeDtypeStruct((POOL_BUF, B, POOL_WIDTH), F32))
    out_shape += tuple(jax.ShapeDtypeStruct(BF16_SHAPES[w], BF16) for w in HANDED_ON)
    out_specs = (pl.BlockSpec((1, step_rows, D_MODEL), lambda b, j: (b, j, 0)),
                 pl.BlockSpec((1, KV_WIDTH, ATT_BLOCK), lambda b, j: (b, 0, 0)),
                 pl.BlockSpec((1, KV_WIDTH, ATT_BLOCK), lambda b, j: (b, 0, 0)),
                 pl.BlockSpec((POOL_BUF, B, POOL_WIDTH), lambda b, j: (0, 0, 0)))
    out_specs += tuple(pl.BlockSpec(memory_space=pl.ANY) for _ in HANDED_ON)
    scratch = [pltpu.VMEM(shape, BF16) for shape in BF16_SHAPES]
    scratch += [pltpu.VMEM((STAGE_SLOTS, CONVERT_ROWS, STAGE_COLS), F32),
                pltpu.SemaphoreType.DMA((STAGE_SLOTS,)),
                pltpu.SemaphoreType.DMA((len(BF16_SHAPES),)),
                pltpu.VMEM(BF16_SHAPES[W_MIX], F32)]
    scratch += [pltpu.VMEM((T, D_MODEL), BF16),
               pltpu.VMEM((PREFIX + T, POOL_WIDTH), F32),
               pltpu.VMEM((T, N_HEADS * LANES), BF16),
               pltpu.VMEM((N_KV_HEADS, ATT_BLOCK + T, KV_WIDTH), BF16),
               pltpu.VMEM((N_KV_HEADS, ATT_BLOCK + T, KV_WIDTH), BF16),
               pltpu.VMEM((2, N_HEADS * ATT_BLOCK, 2 * ATT_BLOCK), F32),
               pltpu.VMEM((2, N_HEADS * ATT_BLOCK, 2 * ATT_BLOCK), BF16),
               pltpu.VMEM((T, POOL_WIDTH), BF16),
               pltpu.VMEM((T, Q_WIDTH), BF16),
               pltpu.VMEM((T, D_MODEL), F32),
               pltpu.VMEM((T, D_MODEL), F32),
               pltpu.VMEM((T, D_MODEL), BF16),
               pltpu.VMEM((T, D_MODEL), F32),
               pltpu.VMEM((T, D_MODEL), BF16),
               pltpu.VMEM((T, D_FF), BF16)]
    return pl.pallas_call(
        _prompt_kernel,
        grid=(B, S // step_rows),
        in_specs=in_specs,
        out_specs=out_specs,
        out_shape=out_shape,
        scratch_shapes=scratch,
        compiler_params=pltpu.CompilerParams(
            dimension_semantics=("arbitrary", "arbitrary"),
            vmem_limit_bytes=VMEM_LIMIT_BYTES),
        name="prompt_layer",
    )(sinks, x, params, bias, *big)


def _sample_kernel(*refs):
    nh = len(HANDED_ON)
    n_in = 6
    (x_ref, ck_ref, cv_ref, sp_ref, sinks_ref, params_ref) = refs[:n_in]
    w_hbm = refs[n_in:n_in + nh]
    pos = n_in + nh
    y_ref, knew_ref, vnew_ref, pnew_ref = refs[pos:pos + 4]
    pos += 4
    w_v = refs[pos:pos + nh]
    pos += nh
    (w_sem, u_s, kt_s, vt_s, q8_s, k8_s, v8_s, o8_s, ypre_s, sga_s, sgb_s, merged_s, act_s) = refs[pos:]
    w_in_ref, wap_ref, wout_ref, wg_ref, wu_ref, wd_ref, wfold_ref = w_v

    SB = SAMPLE_BLOCK
    col_tiles = D_MODEL // LANES
    N = x_ref.shape[0] // col_tiles
    nsteps = N // SB
    i = pl.program_id(0)
    lane = lax.broadcasted_iota(jnp.int32, (N, LANES), 1)
    all_rows = slice(0, N)

    def load_x():
        return jnp.concatenate([x_ref[pl.ds(ct, N, stride=col_tiles), :] for ct in range(col_tiles)], axis=1)

    def weight_copy(k):
        return pltpu.make_async_copy(w_hbm[k], w_v[k], w_sem.at[k])

    @pl.when(i == 0)
    def _():
        for k in range(nh):
            weight_copy(k).start()
        weight_copy(0).wait()
        x = load_x()
        h = ((x * _rms_scale(x)) * _param(params_ref, ROW_N1, D_MODEL)).astype(BF16)
        u_s[...] = _dot(h, w_in_ref[:, OFF_U:OFF_U + POOL_WIDTH])
        for gc in range(N_GATE_CHUNKS):
            _gate_chunk(h, gc, w_in_ref, sga_s, sgb_s, all_rows)
        q = _head_norm(_dot(h, w_in_ref[:, OFF_Q:OFF_Q + Q_WIDTH]), _param(params_ref, ROW_QN, Q_WIDTH))
        kv = _dot(h, w_in_ref[:, OFF_K:OFF_K + 2 * KV_WIDTH])
        kn = _head_norm(kv[:, 0:KV_WIDTH], _param(params_ref, ROW_KN, KV_WIDTH))
        v = kv[:, KV_WIDTH:]
        kt_s[...] = kn.T
        vt_s[...] = v.T
        for head in range(N_HEADS):
            k8_s[pl.ds(head, N, stride=N_HEADS), :] = kn
            v8_s[pl.ds(head, N, stride=N_HEADS), :] = v
        for p in range(Q_COLS):
            qcol = q[:, p * LANES:(p + 1) * LANES]
            qswap = pltpu.roll(qcol, HEAD_DIM, 1)
            for e in range(HEADS_PER_COL):
                head = p * HEADS_PER_COL + e
                c = _kv_head(head)
                src = qcol if e == c else qswap
                keep = (lane < HEAD_DIM) if c == 0 else (lane >= HEAD_DIM)
                q8_s[pl.ds(head, N, stride=N_HEADS), :] = jnp.where(keep, src, 0.0)

    base = i * SB
    rows = pl.ds(pl.multiple_of(base, SB), SB)

    unew = u_s[rows, :]
    for g, w in enumerate(POOL_WINDOWS):
        cols = slice(g * POOL_GROUP_WIDTH, (g + 1) * POOL_GROUP_WIDTH)
        s = unew[:, cols]
        for r in range(POOL_BUF - (w - 1), POOL_BUF):
            s = s + sp_ref[r, :, cols]
        ypre_s[rows, cols] = (s * (1.0 / w) - unew[:, cols]).astype(BF16)
    pnew_ref[0:POOL_BUF - 1] = sp_ref[1:POOL_BUF]
    pnew_ref[POOL_BUF - 1] = unew

    hrows = pl.ds(pl.multiple_of(base * N_HEADS, SB * N_HEADS), SB * N_HEADS)
    lhs = q8_s[hrows, :]
    sc = jnp.concatenate(
        [_dot(lhs[bb * N_HEADS:(bb + 1) * N_HEADS, :].astype(BF16), ck_ref[bb].astype(BF16)) for bb in range(SB)],
        axis=0)
    sc_self = jnp.sum(lhs * k8_s[hrows, :], axis=-1, keepdims=True)
    row_head = lax.rem(lax.broadcasted_iota(jnp.int32, (SB * N_HEADS, 1), 0), N_HEADS)
    sink = jnp.zeros((SB * N_HEADS, 1), F32)
    for head in range(N_HEADS):
        sink = jnp.where(row_head == head, sinks_ref[head] * LOG2_E, sink)
    m = jnp.maximum(jnp.maximum(jnp.max(sc, axis=-1, keepdims=True), sc_self), sink)
    p = jnp.exp2(sc - m)
    p_self = jnp.exp2(sc_self - m)
    denom = jnp.sum(p, axis=-1, keepdims=True) + p_self + jnp.exp2(sink - m)
    inv = 1.0 / denom
    pn = p * inv
    o = jnp.concatenate(
        [_dot_nt(pn[bb * N_HEADS:(bb + 1) * N_HEADS, :].astype(BF16), cv_ref[bb].astype(BF16)) for bb in range(SB)],
        axis=0)
    o8_s[hrows, :] = o + (p_self * inv) * v8_s[hrows, :]

    lane_c = lax.broadcasted_iota(jnp.int32, (KV_WIDTH, WINDOW), 1)
    kt_all = kt_s[...]
    vt_all = vt_s[...]
    for bb in range(SB):
        put = (WINDOW - 1) - (base + bb)
        knew_ref[bb] = jnp.where(lane_c == WINDOW - 1, pltpu.roll(kt_all, put, 1),
                                 pltpu.roll(ck_ref[bb], WINDOW - 1, 1))
        vnew_ref[bb] = jnp.where(lane_c == WINDOW - 1, pltpu.roll(vt_all, put, 1),
                                 pltpu.roll(cv_ref[bb], WINDOW - 1, 1))

    @pl.when(i == nsteps - 1)
    def _():
        for k in range(1, nh):
            weight_copy(k).wait()
        ycols = []
        for p in range(Q_COLS):
            halves = []
            for e in range(HEADS_PER_COL):
                head = p * HEADS_PER_COL + e
                oh = o8_s[pl.ds(head, N, stride=N_HEADS), :]
                halves.append(oh if e == _kv_head(head) else pltpu.roll(oh, HEAD_DIM, 1))
            ycols.append(jnp.where(lane < HEAD_DIM, halves[0], halves[1]).astype(BF16))
        yattn = jnp.concatenate(ycols, axis=1)
        _merge(ypre_s[...], yattn, sga_s, sgb_s, wfold_ref, wap_ref, merged_s, all_rows)
        x1 = load_x() + _dot(merged_s[...], wout_ref[...])
        h2 = ((x1 * _rms_scale(x1)) * _param(params_ref, ROW_N2, D_MODEL)).astype(BF16)
        _ffn_act(h2, wg_ref, wu_ref, act_s, all_rows)
        y = x1 + _dot(act_s[...], wd_ref[...])
        for ct in range(col_tiles):
            y_ref[pl.ds(ct, N, stride=col_tiles), :] = y[:, ct * LANES:(ct + 1) * LANES]


def _sample_call(x, ck, cv, sp, sinks, params, big):
    N = x.shape[0] // (D_MODEL // LANES)
    SB = SAMPLE_BLOCK
    W = WINDOW
    assert tuple(b.shape for b in big) == tuple(BF16_SHAPES[w] for w in HANDED_ON)
    consts = (sinks, params)
    in_specs = [_const_spec(x.shape),
                pl.BlockSpec((SB, KV_WIDTH, W), lambda i: (i, 0, 0)),
                pl.BlockSpec((SB, KV_WIDTH, W), lambda i: (i, 0, 0)),
                pl.BlockSpec((POOL_BUF, SB, POOL_WIDTH), lambda i: (0, i, 0)),
                pl.BlockSpec(memory_space=pltpu.SMEM),
                _const_spec(params.shape)]
    in_specs += [pl.BlockSpec(memory_space=pl.ANY) for _ in big]
    out_shape = (jax.ShapeDtypeStruct(x.shape, F32),
                 jax.ShapeDtypeStruct((N, KV_WIDTH, W), F32),
                 jax.ShapeDtypeStruct((N, KV_WIDTH, W), F32),
                 jax.ShapeDtypeStruct((POOL_BUF, N, POOL_WIDTH), F32))
    out_specs = (pl.BlockSpec(x.shape, lambda i: (0, 0)),
                 pl.BlockSpec((SB, KV_WIDTH, W), lambda i: (i, 0, 0)),
                 pl.BlockSpec((SB, KV_WIDTH, W), lambda i: (i, 0, 0)),
                 pl.BlockSpec((POOL_BUF, SB, POOL_WIDTH), lambda i: (0, i, 0)))
    scratch = [pltpu.VMEM(b.shape, BF16) for b in big]
    scratch += [pltpu.SemaphoreType.DMA((len(big),)),
                pltpu.VMEM((N, POOL_WIDTH), F32),
               pltpu.VMEM((KV_WIDTH, N), F32),
               pltpu.VMEM((KV_WIDTH, N), F32),
               pltpu.VMEM((N * N_HEADS, LANES), F32),
               pltpu.VMEM((N * N_HEADS, LANES), F32),
               pltpu.VMEM((N * N_HEADS, LANES), F32),
               pltpu.VMEM((N * N_HEADS, LANES), F32),
               pltpu.VMEM((N, POOL_WIDTH), BF16),
               pltpu.VMEM((N, D_MODEL), F32),
               pltpu.VMEM((N, D_MODEL), F32),
               pltpu.VMEM((N, D_MODEL), BF16),
               pltpu.VMEM((N, D_FF), BF16)]
    return pl.pallas_call(
        _sample_kernel,
        grid=(N // SB,),
        in_specs=in_specs,
        out_specs=out_specs,
        out_shape=out_shape,
        scratch_shapes=scratch,
        compiler_params=pltpu.CompilerParams(
            dimension_semantics=("arbitrary",),
            vmem_limit_bytes=VMEM_LIMIT_BYTES),
        name="sample_layer",
    )(x, ck, cv, sp, *consts, *big)


def _cache_to_device_order(c):
    n, w = c.shape[0], c.shape[1]
    return jnp.transpose(c, (0, 2, 3, 1)).reshape(n, KV_WIDTH, w)


def _cache_from_device_order(c):
    n, _, w = c.shape
    return jnp.transpose(c.reshape(n, N_KV_HEADS, HEAD_DIM, w), (0, 3, 1, 2))[None]


@jax.jit
def _forward(x_prompt, x_sample, cache_k, cache_v, state_pool, norm1, w_in, q_norm, k_norm, sinks,
             pool_mix_w, pool_scale, w_pool_proj, w_attn_proj, w_out, norm2, w_gate, w_up, w_down):
    depth = w_in.shape[0]
    assert depth == 1, "single-layer trunk"
    l = 0
    N = x_sample.shape[0]
    assert cache_k.shape[2] == WINDOW and x_sample.shape[1] == 1

    rows = {ROW_N1: norm1[l], ROW_N2: norm2[l],
            ROW_QN: jnp.tile(q_norm[l], N_HEADS) * (HEAD_DIM ** -0.5 * LOG2_E),
            ROW_KN: jnp.tile(k_norm[l], N_KV_HEADS), ROW_PSCALE: pool_scale[l]}
    pieces = []
    for r in range(PARAM_ROWS):
        v = rows.get(r)
        used = 0 if v is None else v.shape[0]
        pieces += ([] if v is None else [v]) + ([jnp.zeros((D_MODEL - used,), F32)] if used < D_MODEL else [])
    params = jnp.concatenate(pieces).reshape(PARAM_ROWS, D_MODEL)
    big = (w_in[l], pool_mix_w[l].reshape(len(POOL_WINDOWS) * POOL_GROUP_WIDTH, POOL_GROUP_WIDTH),
           w_pool_proj[l], w_attn_proj[l], w_out[l], w_gate[l], w_up[l], w_down[l])

    y_p, k_p, v_p, pool_p, *weights_bf16 = _prompt_call(x_prompt, sinks[l], params, big)

    y_s, k_s, v_s, pool_s = _sample_call(
        x_sample.reshape(N * (D_MODEL // LANES), LANES),
        _cache_to_device_order(cache_k[l]), _cache_to_device_order(cache_v[l]),
        jnp.transpose(state_pool[l], (1, 0, 2)), sinks[l], params, tuple(weights_bf16))

    return (y_p, y_s.reshape(N, 1, D_MODEL),
            _cache_from_device_order(k_p), _cache_from_device_order(v_p), jnp.transpose(pool_p, (1, 0, 2))[None],
            _cache_from_device_order(k_s), _cache_from_device_order(v_s),
            jnp.transpose(pool_s, (1, 0, 2))[None])


def kernel(x_prompt, x_sample, cache_k, cache_v, state_pool, norm1, w_in, q_norm, k_norm, sinks, pool_mix_w,
           pool_scale, w_pool_proj, w_attn_proj, w_out, norm2, w_gate, w_up, w_down):
    return _forward(x_prompt, x_sample, cache_k, cache_v, state_pool, norm1, w_in, q_norm, k_norm, sinks,
                    pool_mix_w, pool_scale, w_pool_proj, w_attn_proj, w_out, norm2, w_gate, w_up, w_down)
```

```python
import numpy as np
import jax
import jax.numpy as jnp
from jax import lax
from jax.experimental import pallas as pl
from jax.experimental.pallas import tpu as pltpu

D_MODEL = 1024
POOL_WINDOWS = (2, 4, 8, 16)
POOL_GROUP_WIDTH = 128
POOL_WIDTH = 512
POOL_BUF = 15
N_HEADS = 8
N_KV_HEADS = 2
HEAD_DIM = 64
GROUP = N_HEADS // N_KV_HEADS
Q_WIDTH = 512
KV_WIDTH = 128
WINDOW = 128
D_FF = 2816
EPS = 1e-6
NEG = -1e30
LOG2_E = 1.4426950408889634

LANES = 128
ATT_BLOCK = WINDOW
SEQ_TILE = 512
TILES_PER_STEP = 1
SUB_TILE = 256
FF_CHUNK = 256
GATE_CHUNK = 256
SAMPLE_BLOCK = 16
PREFIX = 16
VMEM_LIMIT_BYTES = 58 * 1024 * 1024
Q_COLS = Q_WIDTH // LANES
HEADS_PER_COL = LANES // HEAD_DIM
N_GATE_CHUNKS = D_MODEL // GATE_CHUNK

OFF_U = 0
OFF_Q = OFF_U + POOL_WIDTH
OFF_K = OFF_Q + Q_WIDTH
OFF_V = OFF_K + KV_WIDTH
OFF_GA = OFF_V + KV_WIDTH
OFF_GB = OFF_GA + D_MODEL

BF16 = jnp.bfloat16
F32 = jnp.float32

ROW_N1, ROW_N2, ROW_QN, ROW_KN, ROW_PSCALE, PARAM_ROWS = 0, 1, 2, 3, 4, 8


def _param(params_ref, row, width):
    return params_ref[row:row + 1, 0:width]


def _dot(a, b):
    return jnp.dot(a, b, preferred_element_type=F32)


def _dot_nt(a, b):
    return lax.dot_general(a, b, (((1,), (1,)), ((), ())), preferred_element_type=F32)


def _rms_scale(x):
    return lax.rsqrt(jnp.mean(x * x, axis=-1, keepdims=True) + EPS)


def _head_norm(x, gain):
    sq = x * x
    low = lax.broadcasted_iota(jnp.int32, (x.shape[0], LANES), 1) < HEAD_DIM
    cols = []
    for c in range(0, x.shape[1], LANES):
        blk = sq[:, c:c + LANES]
        ss_lo = jnp.sum(jnp.where(low, blk, 0.0), axis=-1, keepdims=True)
        ss_hi = jnp.sum(jnp.where(low, 0.0, blk), axis=-1, keepdims=True)
        cols.append(jnp.where(low, ss_lo, ss_hi))
    ss = jnp.concatenate(cols, axis=1) if len(cols) > 1 else cols[0]
    return (x * lax.rsqrt(ss * (1.0 / HEAD_DIM) + EPS)) * gain


def _kv_head(head):
    return head // GROUP


def _ffn_act(h2, wg_ref, wu_ref, act_ref, rows):
    for start in range(0, D_FF, FF_CHUNK):
        width = min(FF_CHUNK, D_FF - start)
        g = _dot(h2, wg_ref[:, start:start + width])
        u = _dot(h2, wu_ref[:, start:start + width])
        act_ref[rows, start:start + width] = (g * jax.nn.sigmoid(g) * u).astype(BF16)


def _gate_chunk(h, c, w_in_ref, sga_ref, sgb_ref, rows):
    lo = c * GATE_CHUNK
    sga_ref[rows, lo:lo + GATE_CHUNK] = jax.nn.sigmoid(_dot(h, w_in_ref[:, OFF_GA + lo:OFF_GA + lo + GATE_CHUNK]))
    sgb_ref[rows, lo:lo + GATE_CHUNK] = jax.nn.sigmoid(_dot(h, w_in_ref[:, OFF_GB + lo:OFF_GB + lo + GATE_CHUNK]))


def _merge(ypool, yattn, sga_ref, sgb_ref, wpp_ref, wap_ref, merged_ref, rows):
    for c in range(N_GATE_CHUNKS):
        lo = c * GATE_CHUNK
        pp = _dot(ypool, wpp_ref[:, lo:lo + GATE_CHUNK])
        ap = _dot(yattn, wap_ref[:, lo:lo + GATE_CHUNK])
        merged = sga_ref[rows, lo:lo + GATE_CHUNK] * pp + sgb_ref[rows, lo:lo + GATE_CHUNK] * ap
        merged_ref[rows, lo:lo + GATE_CHUNK] = merged.astype(BF16)


BIG_WEIGHTS = (("w_in", D_MODEL, OFF_GB + D_MODEL), ("mix", len(POOL_WINDOWS) * POOL_GROUP_WIDTH, POOL_GROUP_WIDTH),
               ("wpp", POOL_WIDTH, D_MODEL), ("wap", Q_WIDTH, D_MODEL), ("wout", D_MODEL, D_MODEL),
               ("wg", D_MODEL, D_FF), ("wu", D_MODEL, D_FF), ("wd", D_FF, D_MODEL))
W_MIX, W_PP = 1, 2
W_FOLD = len(BIG_WEIGHTS)
HANDED_ON = tuple(w for w in range(len(BIG_WEIGHTS)) if w not in (W_MIX, W_PP)) + (W_FOLD,)
BF16_SHAPES = tuple((rows, cols) for _, rows, cols in BIG_WEIGHTS) + ((POOL_WIDTH, D_MODEL),)
CONVERT_ROWS = 128
STAGE_SLOTS = 4
STAGE_COLS = max(cols for _, _, cols in BIG_WEIGHTS)


def _weight_chunks():
    chunks = []
    for w, (_, nrows, cols) in enumerate(BIG_WEIGHTS):
        side_by_side = max(1, STAGE_COLS // cols)
        blocks = [(r0, min(CONVERT_ROWS, nrows - r0)) for r0 in range(0, nrows, CONVERT_ROWS)]
        for first in range(0, len(blocks), side_by_side):
            group = blocks[first:first + side_by_side]
            chunks.append((w, tuple((r0, rows, k * cols) for k, (r0, rows) in enumerate(group))))
    return tuple(chunks)


WEIGHT_CHUNKS = _weight_chunks()


def _writeback_copy(w, w_v, w_bf_hbm, out_sem):
    return pltpu.make_async_copy(w_v[w], w_bf_hbm[w], out_sem.at[w])


def _convert_weights(w_hbm, w_v, stage, stage_sem, mixf_s, params_ref):
    def chunk_copies(n):
        w, pieces = WEIGHT_CHUNKS[n]
        cols = BIG_WEIGHTS[w][2]
        slot = n % STAGE_SLOTS
        return [pltpu.make_async_copy(w_hbm[w].at[pl.ds(r0, rows), :],
                                      stage.at[slot, pl.ds(0, rows), pl.ds(c0, cols)], stage_sem.at[slot])
                for r0, rows, c0 in pieces]

    def convert(n):
        w, pieces = WEIGHT_CHUNKS[n]
        cols = BIG_WEIGHTS[w][2]
        for r0, rows, c0 in pieces:
            staged = stage[n % STAGE_SLOTS, 0:rows, c0:c0 + cols]
            if w == W_MIX:
                mixf_s[r0:r0 + rows, :] = staged
            else:
                w_v[w][r0:r0 + rows, :] = staged.astype(BF16)

    ahead = STAGE_SLOTS - 1
    for m in range(ahead):
        for copy in chunk_copies(m):
            copy.start()
    for n in range(len(WEIGHT_CHUNKS)):
        if n + ahead < len(WEIGHT_CHUNKS):
            for copy in chunk_copies(n + ahead):
                copy.start()
        for copy in chunk_copies(n):
            copy.wait()
        convert(n)

    for g in range(len(POOL_WINDOWS)):
        grp = slice(g * POOL_GROUP_WIDTH, (g + 1) * POOL_GROUP_WIDTH)
        scale = params_ref[ROW_PSCALE:ROW_PSCALE + 1, grp]
        w_v[W_FOLD][grp, :] = _dot((mixf_s[grp, :] * scale).astype(BF16), w_v[W_PP][grp, :]).astype(BF16)


def _prompt_kernel(*refs):
    nw = len(BIG_WEIGHTS)
    sinks_ref, x_ref, params_ref, bias_ref = refs[:4]
    w_hbm = refs[4:4 + nw]
    pos = 4 + nw
    outs = refs[pos:pos + 4]
    pos += 4
    w_bf_hbm = dict(zip(HANDED_ON, refs[pos:pos + len(HANDED_ON)]))
    pos += len(HANDED_ON)
    w_v = refs[pos:pos + nw + 1]
    pos += nw + 1
    stage, stage_sem, out_sem, mixf_s = refs[pos:pos + 4]
    scratch = refs[pos + 4:]
    w_in_ref, _, _, wap_ref, wout_ref, wg_ref, wu_ref, wd_ref, wfold_ref = w_v

    step = pl.program_id(0) * pl.num_programs(1) + pl.program_id(1)
    n_steps = pl.num_programs(0) * pl.num_programs(1)

    @pl.when(step == 0)
    def _():
        _convert_weights(w_hbm, w_v, stage, stage_sem, mixf_s, params_ref)

    @pl.when(step == 1)
    def _():
        for w in HANDED_ON:
            _writeback_copy(w, w_v, w_bf_hbm, out_sem).start()

    def body(t, carry):
        _prompt_tile(t, sinks_ref, x_ref, params_ref, bias_ref,
                     w_in_ref, wfold_ref, wap_ref, wout_ref, wg_ref, wu_ref, wd_ref, *outs, *scratch)
        return carry
    lax.fori_loop(0, TILES_PER_STEP, body, 0)

    @pl.when(step == n_steps - 1)
    def _():
        for w in HANDED_ON:
            _writeback_copy(w, w_v, w_bf_hbm, out_sem).wait()


def _prompt_tile(t, sinks_ref, x_ref, params_ref, bias_ref,
                 w_in_ref, wfold_ref, wap_ref, wout_ref, wg_ref, wu_ref, wd_ref,
                 y_ref, knew_ref, vnew_ref, pnew_ref,
                 h_s, u_s, q_s, kd_s, vd_s, s_s, p_s, ypool_s, yattn_s, sga_s, sgb_s, merged_s, x1_s, h2_s, act_s):
    T = SEQ_TILE
    R = SUB_TILE
    subs = [slice(r0, r0 + R) for r0 in range(0, T, R)]
    n_blocks = T // ATT_BLOCK
    j = pl.program_id(1) * TILES_PER_STEP + t
    rows_per_kv = GROUP * ATT_BLOCK

    def in_block(rows):
        return pl.ds(pl.multiple_of(t * T + rows.start, R), R)

    @pl.when(j == 0)
    def _():
        u_s[0:PREFIX, :] = jnp.zeros((PREFIX, POOL_WIDTH), F32)
        kd_s[:, 0:ATT_BLOCK, :] = jnp.zeros((N_KV_HEADS, ATT_BLOCK, KV_WIDTH), BF16)
        vd_s[:, 0:ATT_BLOCK, :] = jnp.zeros((N_KV_HEADS, ATT_BLOCK, KV_WIDTH), BF16)

    lane_r = lax.broadcasted_iota(jnp.int32, (R, LANES), 1)
    lane = lax.broadcasted_iota(jnp.int32, (ATT_BLOCK, LANES), 1)

    def norm1(rows):
        x = x_ref[0, in_block(rows), :]
        h_s[rows, :] = ((x * _rms_scale(x)) * _param(params_ref, ROW_N1, D_MODEL)).astype(BF16)

    def in_proj(rows):
        h = h_s[rows, :]
        u_s[PREFIX + rows.start:PREFIX + rows.stop, :] = _dot(h, w_in_ref[:, OFF_U:OFF_U + POOL_WIDTH])
        q = _dot(h, w_in_ref[:, OFF_Q:OFF_Q + Q_WIDTH])
        qn = _head_norm(q, _param(params_ref, ROW_QN, Q_WIDTH)).astype(BF16)
        for p in range(Q_COLS):
            qcol = qn[:, p * LANES:(p + 1) * LANES]
            lo, hi = 2 * p * LANES, (2 * p + 1) * LANES
            q_s[rows, lo:lo + LANES] = jnp.where(lane_r < HEAD_DIM, qcol, jnp.zeros_like(qcol))
            q_s[rows, hi:hi + LANES] = jnp.where(lane_r >= HEAD_DIM, qcol, jnp.zeros_like(qcol))
        kv = _dot(h, w_in_ref[:, OFF_K:OFF_K + 2 * KV_WIDTH])
        kn = _head_norm(kv[:, 0:KV_WIDTH], _param(params_ref, ROW_KN, KV_WIDTH))
        v = kv[:, KV_WIDTH:]
        dst_rows = slice(ATT_BLOCK + rows.start, ATT_BLOCK + rows.stop)
        for src, dst in ((kn, kd_s), (v, vd_s)):
            swapped = pltpu.roll(src, HEAD_DIM, 1)
            dst[0, dst_rows, :] = jnp.where(lane_r < HEAD_DIM, src, swapped).astype(BF16)
            dst[1, dst_rows, :] = jnp.where(lane_r < HEAD_DIM, swapped, src).astype(BF16)
        if rows.stop == T:
            knew_ref[0] = kn[R - ATT_BLOCK:, :].T
            vnew_ref[0] = v[R - ATT_BLOCK:, :].T
            batch = pl.program_id(0)
            for r in range(POOL_BUF):
                src = PREFIX + T - POOL_BUF + r
                pnew_ref[r, pl.ds(batch, 1), :] = u_s[src:src + 1, :]

    def pool(rows):
        pos1 = j * T + rows.start + lax.broadcasted_iota(jnp.int32, (R, 1), 0) + 1
        for g, w in enumerate(POOL_WINDOWS):
            cols = slice(g * POOL_GROUP_WIDTH, (g + 1) * POOL_GROUP_WIDTH)
            a = u_s[rows.start:rows.stop + PREFIX, cols]
            s = a
            shift = 1
            while shift < w:
                s = s + pltpu.roll(s, shift, 0)
                shift *= 2
            inv_cnt = 1.0 / jnp.minimum(pos1, w).astype(F32)
            ypool_s[rows, cols] = (s[PREFIX:, :] * inv_cnt - a[PREFIX:, :]).astype(BF16)

    def gates(rows, chunks):
        h = h_s[rows, :]
        for c in chunks:
            _gate_chunk(h, c, w_in_ref, sga_s, sgb_s, rows)

    def scores(b):
        r0 = b * ATT_BLOCK
        for c in range(N_KV_HEADS):
            parts = [q_s[r0:r0 + ATT_BLOCK, head * LANES:(head + 1) * LANES]
                     for head in range(c * GROUP, (c + 1) * GROUP)]
            s_s[b % 2, c * rows_per_kv:(c + 1) * rows_per_kv, :] = _dot_nt(
                jnp.concatenate(parts, axis=0), kd_s[c, r0:r0 + 2 * ATT_BLOCK, :])

    def softmax(b):
        bias = bias_ref[jnp.where(j == 0, 1, 0)] if b == 0 else bias_ref[0]
        inv_denoms = []
        for head in range(N_HEADS):
            sh = s_s[b % 2, head * ATT_BLOCK:(head + 1) * ATT_BLOCK, :] + bias
            sink = sinks_ref[head] * LOG2_E
            m = jnp.maximum(jnp.max(sh, axis=-1, keepdims=True), sink)
            p = jnp.exp2(sh - m)
            denom = jnp.sum(p, axis=-1, keepdims=True) + jnp.exp2(sink - m)
            p_s[b % 2, head * ATT_BLOCK:(head + 1) * ATT_BLOCK, :] = p.astype(BF16)
            inv_denoms.append(1.0 / denom)
        return inv_denoms

    def weighted_values(b, inv_denoms):
        r0 = b * ATT_BLOCK
        for c in range(N_KV_HEADS):
            o = _dot(p_s[b % 2, c * rows_per_kv:(c + 1) * rows_per_kv, :], vd_s[c, r0:r0 + 2 * ATT_BLOCK, :])
            for pp in range(Q_COLS // N_KV_HEADS):
                p_col = c * Q_COLS // N_KV_HEADS + pp
                head_lo = p_col * HEADS_PER_COL
                o_lo = o[(2 * pp) * ATT_BLOCK:(2 * pp + 1) * ATT_BLOCK, :] * inv_denoms[head_lo]
                o_hi = o[(2 * pp + 1) * ATT_BLOCK:(2 * pp + 2) * ATT_BLOCK, :] * inv_denoms[head_lo + 1]
                yattn_s[r0:r0 + ATT_BLOCK, p_col * LANES:(p_col + 1) * LANES] = (
                    jnp.where(lane < HEAD_DIM, o_lo, o_hi).astype(BF16))

    def out_proj(rows):
        _merge(ypool_s[rows, :], yattn_s[rows, :], sga_s, sgb_s, wfold_ref, wap_ref, merged_s, rows)
        x1 = x_ref[0, in_block(rows), :] + _dot(merged_s[rows, :], wout_ref[...])
        x1_s[rows, :] = x1
        h2_s[rows, :] = ((x1 * _rms_scale(x1)) * _param(params_ref, ROW_N2, D_MODEL)).astype(BF16)

    for rows in subs:
        norm1(rows)
    for rows in subs:
        in_proj(rows)
    pool(subs[0])
    gates(subs[0], range(N_GATE_CHUNKS))
    for rows in subs[1:]:
        pool(rows)
    later_gates = [(rows, c) for rows in subs[1:] for c in range(N_GATE_CHUNKS)]
    per_block = -(-len(later_gates) // n_blocks)
    scores(0)
    pending = None
    for b in range(n_blocks):
        if b + 1 < n_blocks:
            scores(b + 1)
        for rows, c in later_gates[b * per_block:(b + 1) * per_block]:
            gates(rows, [c])
        inv_denoms = softmax(b)
        if pending is not None:
            weighted_values(*pending)
        pending = (b, inv_denoms)
    weighted_values(*pending)

    u_s[0:PREFIX, :] = u_s[T:T + PREFIX, :]
    kd_s[:, 0:ATT_BLOCK, :] = kd_s[:, T:T + ATT_BLOCK, :]
    vd_s[:, 0:ATT_BLOCK, :] = vd_s[:, T:T + ATT_BLOCK, :]

    for rows in subs:
        out_proj(rows)
    for rows in subs:
        _ffn_act(h2_s[rows, :], wg_ref, wu_ref, act_s, rows)
    for rows in subs:
        y_ref[0, in_block(rows), :] = x1_s[rows, :] + _dot(act_s[rows, :], wd_ref[...])


def _const_spec(shape):
    nd = len(shape)
    return pl.BlockSpec(shape, lambda *_: (0,) * nd, pipeline_mode=pl.Buffered(1))


def _band_bias():
    row = np.arange(ATT_BLOCK)[:, None]
    col = np.arange(2 * ATT_BLOCK)[None, :]
    band = (col >= row) & (col <= row + WINDOW)
    masks = np.stack([band, band & (col >= ATT_BLOCK)])
    return jnp.asarray(np.where(masks, 0.0, NEG), dtype=F32)


def _prompt_call(x, sinks, params, big):
    B, S, _ = x.shape
    T = SEQ_TILE
    step_rows = T * TILES_PER_STEP
    assert tuple(b.shape for b in big) == BF16_SHAPES[:len(BIG_WEIGHTS)]
    assert B * (S // step_rows) >= 2, "the bf16 write-back starts on the second grid step"
    bias = _band_bias()
    in_specs = [pl.BlockSpec(memory_space=pltpu.SMEM),
                pl.BlockSpec((1, step_rows, D_MODEL), lambda b, j: (b, j, 0)),
                _const_spec(params.shape),
                _const_spec(bias.shape)]
    in_specs += [pl.BlockSpec(memory_space=pl.ANY) for _ in big]
    out_shape = (jax.ShapeDtypeStruct((B, S, D_MODEL), F32),
                 jax.ShapeDtypeStruct((B, KV_WIDTH, ATT_BLOCK), F32),
                 jax.ShapeDtypeStruct((B, KV_WIDTH, ATT_BLOCK), F32),
                 jax.ShapeDtypeStruct((POOL_BUF, B, POOL_WIDTH), F32))
    out_shape += tuple(jax.ShapeDtypeStruct(BF16_SHAPES[w], BF16) for w in HANDED_ON)
    out_specs = (pl.BlockSpec((1, step_rows, D_MODEL), lambda b, j: (b, j, 0)),
                 pl.BlockSpec((1, KV_WIDTH, ATT_BLOCK), lambda b, j: (b, 0, 0)),
                 pl.BlockSpec((1, KV_WIDTH, ATT_BLOCK), lambda b, j: (b, 0, 0)),
                 pl.BlockSpec((POOL_BUF, B, POOL_WIDTH), lambda b, j: (0, 0, 0)))
    out_specs += tuple(pl.BlockSpec(memory_space=pl.ANY) for _ in HANDED_ON)
    scratch = [pltpu.VMEM(shape, BF16) for shape in BF16_SHAPES]
    scratch += [pltpu.VMEM((STAGE_SLOTS, CONVERT_ROWS, STAGE_COLS), F32),
                pltpu.SemaphoreType.DMA((STAGE_SLOTS,)),
                pltpu.SemaphoreType.DMA((len(BF16_SHAPES),)),
                pltpu.VMEM(BF16_SHAPES[W_MIX], F32)]
    scratch += [pltpu.VMEM((T, D_MODEL), BF16),
               pltpu.VMEM((PREFIX + T, POOL_WIDTH), F32),
               pltpu.VMEM((T, N_HEADS * LANES), BF16),
               pltpu.VMEM((N_KV_HEADS, ATT_BLOCK + T, KV_WIDTH), BF16),
               pltpu.VMEM((N_KV_HEADS, ATT_BLOCK + T, KV_WIDTH), BF16),
               pltpu.VMEM((2, N_HEADS * ATT_BLOCK, 2 * ATT_BLOCK), F32),
               pltpu.VMEM((2, N_HEADS * ATT_BLOCK, 2 * ATT_BLOCK), BF16),
               pltpu.VMEM((T, POOL_WIDTH), BF16),
               pltpu.VMEM((T, Q_WIDTH), BF16),
               pltpu.VMEM((T, D_MODEL), F32),
               pltpu.VMEM((T, D_MODEL), F32),
               pltpu.VMEM((T, D_MODEL), BF16),
               pltpu.VMEM((T, D_MODEL), F32),
               pltpu.VMEM((T, D_MODEL), BF16),
               pltpu.VMEM((T, D_FF), BF16)]
    return pl.pallas_call(
        _prompt_kernel,
        grid=(B, S // step_rows),
        in_specs=in_specs,
        out_specs=out_specs,
        out_shape=out_shape,
        scratch_shapes=scratch,
        compiler_params=pltpu.CompilerParams(
            dimension_semantics=("arbitrary", "arbitrary"),
            vmem_limit_bytes=VMEM_LIMIT_BYTES),
        name="prompt_layer",
    )(sinks, x, params, bias, *big)


def _sample_kernel(*refs):
    nh = len(HANDED_ON)
    n_in = 6
    (x_ref, ck_ref, cv_ref, sp_ref, sinks_ref, params_ref) = refs[:n_in]
    w_hbm = refs[n_in:n_in + nh]
    pos = n_in + nh
    y_ref, knew_ref, vnew_ref, pnew_ref = refs[pos:pos + 4]
    pos += 4
    w_v = refs[pos:pos + nh]
    pos += nh
    (w_sem, u_s, kt_s, vt_s, q8_s, k8_s, v8_s, o8_s, ypre_s, sga_s, sgb_s, merged_s, act_s) = refs[pos:]
    w_in_ref, wap_ref, wout_ref, wg_ref, wu_ref, wd_ref, wfold_ref = w_v

    SB = SAMPLE_BLOCK
    col_tiles = D_MODEL // LANES
    N = x_ref.shape[0] // col_tiles
    nsteps = N // SB
    i = pl.program_id(0)
    lane = lax.broadcasted_iota(jnp.int32, (N, LANES), 1)
    all_rows = slice(0, N)

    def load_x():
        return jnp.concatenate([x_ref[pl.ds(ct, N, stride=col_tiles), :] for ct in range(col_tiles)], axis=1)

    def weight_copy(k):
        return pltpu.make_async_copy(w_hbm[k], w_v[k], w_sem.at[k])

    @pl.when(i == 0)
    def _():
        for k in range(nh):
            weight_copy(k).start()
        weight_copy(0).wait()
        x = load_x()
        h = ((x * _rms_scale(x)) * _param(params_ref, ROW_N1, D_MODEL)).astype(BF16)
        u_s[...] = _dot(h, w_in_ref[:, OFF_U:OFF_U + POOL_WIDTH])
        for gc in range(N_GATE_CHUNKS):
            _gate_chunk(h, gc, w_in_ref, sga_s, sgb_s, all_rows)
        q = _head_norm(_dot(h, w_in_ref[:, OFF_Q:OFF_Q + Q_WIDTH]), _param(params_ref, ROW_QN, Q_WIDTH))
        kv = _dot(h, w_in_ref[:, OFF_K:OFF_K + 2 * KV_WIDTH])
        kn = _head_norm(kv[:, 0:KV_WIDTH], _param(params_ref, ROW_KN, KV_WIDTH))
        v = kv[:, KV_WIDTH:]
        kt_s[...] = kn.T
        vt_s[...] = v.T
        for head in range(N_HEADS):
            k8_s[pl.ds(head, N, stride=N_HEADS), :] = kn
            v8_s[pl.ds(head, N, stride=N_HEADS), :] = v
        for p in range(Q_COLS):
            qcol = q[:, p * LANES:(p + 1) * LANES]
            qswap = pltpu.roll(qcol, HEAD_DIM, 1)
            for e in range(HEADS_PER_COL):
                head = p * HEADS_PER_COL + e
                c = _kv_head(head)
                src = qcol if e == c else qswap
                keep = (lane < HEAD_DIM) if c == 0 else (lane >= HEAD_DIM)
                q8_s[pl.ds(head, N, stride=N_HEADS), :] = jnp.where(keep, src, 0.0)

    base = i * SB
    rows = pl.ds(pl.multiple_of(base, SB), SB)

    unew = u_s[rows, :]
    for g, w in enumerate(POOL_WINDOWS):
        cols = slice(g * POOL_GROUP_WIDTH, (g + 1) * POOL_GROUP_WIDTH)
        s = unew[:, cols]
        for r in range(POOL_BUF - (w - 1), POOL_BUF):
            s = s + sp_ref[r, :, cols]
        ypre_s[rows, cols] = (s * (1.0 / w) - unew[:, cols]).astype(BF16)
    pnew_ref[0:POOL_BUF - 1] = sp_ref[1:POOL_BUF]
    pnew_ref[POOL_BUF - 1] = unew

    hrows = pl.ds(pl.multiple_of(base * N_HEADS, SB * N_HEADS), SB * N_HEADS)
    lhs = q8_s[hrows, :]
    sc = jnp.concatenate(
        [_dot(lhs[bb * N_HEADS:(bb + 1) * N_HEADS, :].astype(BF16), ck_ref[bb].astype(BF16)) for bb in range(SB)],
        axis=0)
    sc_self = jnp.sum(lhs * k8_s[hrows, :], axis=-1, keepdims=True)
    row_head = lax.rem(lax.broadcasted_iota(jnp.int32, (SB * N_HEADS, 1), 0), N_HEADS)
    sink = jnp.zeros((SB * N_HEADS, 1), F32)
    for head in range(N_HEADS):
        sink = jnp.where(row_head == head, sinks_ref[head] * LOG2_E, sink)
    m = jnp.maximum(jnp.maximum(jnp.max(sc, axis=-1, keepdims=True), sc_self), sink)
    p = jnp.exp2(sc - m)
    p_self = jnp.exp2(sc_self - m)
    denom = jnp.sum(p, axis=-1, keepdims=True) + p_self + jnp.exp2(sink - m)
    inv = 1.0 / denom
    pn = p * inv
    o = jnp.concatenate(
        [_dot_nt(pn[bb * N_HEADS:(bb + 1) * N_HEADS, :].astype(BF16), cv_ref[bb].astype(BF16)) for bb in range(SB)],
        axis=0)
    o8_s[hrows, :] = o + (p_self * inv) * v8_s[hrows, :]

    lane_c = lax.broadcasted_iota(jnp.int32, (KV_WIDTH, WINDOW), 1)
    kt_all = kt_s[...]
    vt_all = vt_s[...]
    for bb in range(SB):
        put = (WINDOW - 1) - (base + bb)
        knew_ref[bb] = jnp.where(lane_c == WINDOW - 1, pltpu.roll(kt_all, put, 1),
                                 pltpu.roll(ck_ref[bb], WINDOW - 1, 1))
        vnew_ref[bb] = jnp.where(lane_c == WINDOW - 1, pltpu.roll(vt_all, put, 1),
                                 pltpu.roll(cv_ref[bb], WINDOW - 1, 1))

    @pl.when(i == nsteps - 1)
    def _():
        for k in range(1, nh):
            weight_copy(k).wait()
        ycols = []
        for p in range(Q_COLS):
            halves = []
            for e in range(HEADS_PER_COL):
                head = p * HEADS_PER_COL + e
                oh = o8_s[pl.ds(head, N, stride=N_HEADS), :]
                halves.append(oh if e == _kv_head(head) else pltpu.roll(oh, HEAD_DIM, 1))
            ycols.append(jnp.where(lane < HEAD_DIM, halves[0], halves[1]).astype(BF16))
        yattn = jnp.concatenate(ycols, axis=1)
        _merge(ypre_s[...], yattn, sga_s, sgb_s, wfold_ref, wap_ref, merged_s, all_rows)
        x1 = load_x() + _dot(merged_s[...], wout_ref[...])
        h2 = ((x1 * _rms_scale(x1)) * _param(params_ref, ROW_N2, D_MODEL)).astype(BF16)
        _ffn_act(h2, wg_ref, wu_ref, act_s, all_rows)
        y = x1 + _dot(act_s[...], wd_ref[...])
        for ct in range(col_tiles):
            y_ref[pl.ds(ct, N, stride=col_tiles), :] = y[:, ct * LANES:(ct + 1) * LANES]


def _sample_call(x, ck, cv, sp, sinks, params, big):
    N = x.shape[0] // (D_MODEL // LANES)
    SB = SAMPLE_BLOCK
    W = WINDOW
    assert tuple(b.shape for b in big) == tuple(BF16_SHAPES[w] for w in HANDED_ON)
    consts = (sinks, params)
    in_specs = [_const_spec(x.shape),
                pl.BlockSpec((SB, KV_WIDTH, W), lambda i: (i, 0, 0)),
                pl.BlockSpec((SB, KV_WIDTH, W), lambda i: (i, 0, 0)),
                pl.BlockSpec((POOL_BUF, SB, POOL_WIDTH), lambda i: (0, i, 0)),
                pl.BlockSpec(memory_space=pltpu.SMEM),
                _const_spec(params.shape)]
    in_specs += [pl.BlockSpec(memory_space=pl.ANY) for _ in big]
    out_shape = (jax.ShapeDtypeStruct(x.shape, F32),
                 jax.ShapeDtypeStruct((N, KV_WIDTH, W), F32),
                 jax.ShapeDtypeStruct((N, KV_WIDTH, W), F32),
                 jax.ShapeDtypeStruct((POOL_BUF, N, POOL_WIDTH), F32))
    out_specs = (pl.BlockSpec(x.shape, lambda i: (0, 0)),
                 pl.BlockSpec((SB, KV_WIDTH, W), lambda i: (i, 0, 0)),
                 pl.BlockSpec((SB, KV_WIDTH, W), lambda i: (i, 0, 0)),
                 pl.BlockSpec((POOL_BUF, SB, POOL_WIDTH), lambda i: (0, i, 0)))
    scratch = [pltpu.VMEM(b.shape, BF16) for b in big]
    scratch += [pltpu.SemaphoreType.DMA((len(big),)),
                pltpu.VMEM((N, POOL_WIDTH), F32),
               pltpu.VMEM((KV_WIDTH, N), F32),
               pltpu.VMEM((KV_WIDTH, N), F32),
               pltpu.VMEM((N * N_HEADS, LANES), F32),
               pltpu.VMEM((N * N_HEADS, LANES), F32),
               pltpu.VMEM((N * N_HEADS, LANES), F32),
               pltpu.VMEM((N * N_HEADS, LANES), F32),
               pltpu.VMEM((N, POOL_WIDTH), BF16),
               pltpu.VMEM((N, D_MODEL), F32),
               pltpu.VMEM((N, D_MODEL), F32),
               pltpu.VMEM((N, D_MODEL), BF16),
               pltpu.VMEM((N, D_FF), BF16)]
    return pl.pallas_call(
        _sample_kernel,
        grid=(N // SB,),
        in_specs=in_specs,
        out_specs=out_specs,
        out_shape=out_shape,
        scratch_shapes=scratch,
        compiler_params=pltpu.CompilerParams(
            dimension_semantics=("arbitrary",),
            vmem_limit_bytes=VMEM_LIMIT_BYTES),
        name="sample_layer",
    )(x, ck, cv, sp, *consts, *big)


def _cache_to_device_order(c):
    n, w = c.shape[0], c.shape[1]
    return jnp.transpose(c, (0, 2, 3, 1)).reshape(n, KV_WIDTH, w)


def _cache_from_device_order(c):
    n, _, w = c.shape
    return jnp.transpose(c.reshape(n, N_KV_HEADS, HEAD_DIM, w), (0, 3, 1, 2))[None]


@jax.jit
def _forward(x_prompt, x_sample, cache_k, cache_v, state_pool, norm1, w_in, q_norm, k_norm, sinks,
             pool_mix_w, pool_scale, w_pool_proj, w_attn_proj, w_out, norm2, w_gate, w_up, w_down):
    depth = w_in.shape[0]
    assert depth == 1, "single-layer trunk"
    l = 0
    N = x_sample.shape[0]
    assert cache_k.shape[2] == WINDOW and x_sample.shape[1] == 1

    rows = {ROW_N1: norm1[l], ROW_N2: norm2[l],
            ROW_QN: jnp.tile(q_norm[l], N_HEADS) * (HEAD_DIM ** -0.5 * LOG2_E),
            ROW_KN: jnp.tile(k_norm[l], N_KV_HEADS), ROW_PSCALE: pool_scale[l]}
    pieces = []
    for r in range(PARAM_ROWS):
        v = rows.get(r)
        used = 0 if v is None else v.shape[0]
        pieces += ([] if v is None else [v]) + ([jnp.zeros((D_MODEL - used,), F32)] if used < D_MODEL else [])
    params = jnp.concatenate(pieces).reshape(PARAM_ROWS, D_MODEL)
    big = (w_in[l], pool_mix_w[l].reshape(len(POOL_WINDOWS) * POOL_GROUP_WIDTH, POOL_GROUP_WIDTH),
           w_pool_proj[l], w_attn_proj[l], w_out[l], w_gate[l], w_up[l], w_down[l])

    y_p, k_p, v_p, pool_p, *weights_bf16 = _prompt_call(x_prompt, sinks[l], params, big)

    y_s, k_s, v_s, pool_s = _sample_call(
        x_sample.reshape(N * (D_MODEL // LANES), LANES),
        _cache_to_device_order(cache_k[l]), _cache_to_device_order(cache_v[l]),
        jnp.transpose(state_pool[l], (1, 0, 2)), sinks[l], params, tuple(weights_bf16))

    return (y_p, y_s.reshape(N, 1, D_MODEL),
            _cache_from_device_order(k_p), _cache_from_device_order(v_p), jnp.transpose(pool_p, (1, 0, 2))[None],
            _cache_from_device_order(k_s), _cache_from_device_order(v_s),
            jnp.transpose(pool_s, (1, 0, 2))[None])


def kernel(x_prompt, x_sample, cache_k, cache_v, state_pool, norm1, w_in, q_norm, k_norm, sinks, pool_mix_w,
           pool_scale, w_pool_proj, w_attn_proj, w_out, norm2, w_gate, w_up, w_down):
    return _forward(x_prompt, x_sample, cache_k, cache_v, state_pool, norm1, w_in, q_norm, k_norm, sinks,
                    pool_mix_w, pool_scale, w_pool_proj, w_attn_proj, w_out, norm2, w_gate, w_up, w_down)
```

```python
import numpy as np
import jax
import jax.numpy as jnp
from jax import lax
from jax.experimental import pallas as pl
from jax.experimental.pallas import tpu as pltpu

D_MODEL = 1024
POOL_WINDOWS = (2, 4, 8, 16)
POOL_GROUP_WIDTH = 128
POOL_WIDTH = 512
POOL_BUF = 15
N_HEADS = 8
N_KV_HEADS = 2
HEAD_DIM = 64
GROUP = N_HEADS // N_KV_HEADS
Q_WIDTH = 512
KV_WIDTH = 128
WINDOW = 128
D_FF = 2816
EPS = 1e-6
NEG = -1e30
LOG2_E = 1.4426950408889634

LANES = 128
ATT_BLOCK = WINDOW
SEQ_TILE = 512
TILES_PER_STEP = 1
SUB_TILE = 256
FF_CHUNK = 256
GATE_CHUNK = 256
SAMPLE_BLOCK = 16
PREFIX = 16
VMEM_LIMIT_BYTES = 58 * 1024 * 1024
Q_COLS = Q_WIDTH // LANES
HEADS_PER_COL = LANES // HEAD_DIM
N_GATE_CHUNKS = D_MODEL // GATE_CHUNK

OFF_U = 0
OFF_Q = OFF_U + POOL_WIDTH
OFF_K = OFF_Q + Q_WIDTH
OFF_V = OFF_K + KV_WIDTH
OFF_GA = OFF_V + KV_WIDTH
OFF_GB = OFF_GA + D_MODEL

BF16 = jnp.bfloat16
F32 = jnp.float32

ROW_N1, ROW_N2, ROW_QN, ROW_KN, ROW_PSCALE, PARAM_ROWS = 0, 1, 2, 3, 4, 8


def _param(params_ref, row, width):
    return params_ref[row:row + 1, 0:width]


def _dot(a, b):
    return jnp.dot(a, b, preferred_element_type=F32)


def _dot_nt(a, b):
    return lax.dot_general(a, b, (((1,), (1,)), ((), ())), preferred_element_type=F32)


def _rms_scale(x):
    return lax.rsqrt(jnp.mean(x * x, axis=-1, keepdims=True) + EPS)


def _head_norm(x, gain):
    sq = x * x
    low = lax.broadcasted_iota(jnp.int32, (x.shape[0], LANES), 1) < HEAD_DIM
    cols = []
    for c in range(0, x.shape[1], LANES):
        blk = sq[:, c:c + LANES]
        ss_lo = jnp.sum(jnp.where(low, blk, 0.0), axis=-1, keepdims=True)
        ss_hi = jnp.sum(jnp.where(low, 0.0, blk), axis=-1, keepdims=True)
        cols.append(jnp.where(low, ss_lo, ss_hi))
    ss = jnp.concatenate(cols, axis=1) if len(cols) > 1 else cols[0]
    return (x * lax.rsqrt(ss * (1.0 / HEAD_DIM) + EPS)) * gain


def _kv_head(head):
    return head // GROUP


def _ffn_act(h2, wg_ref, wu_ref, act_ref, rows):
    for start in range(0, D_FF, FF_CHUNK):
        width = min(FF_CHUNK, D_FF - start)
        g = _dot(h2, wg_ref[:, start:start + width])
        u = _dot(h2, wu_ref[:, start:start + width])
        act_ref[rows, start:start + width] = (g * jax.nn.sigmoid(g) * u).astype(BF16)


def _gate_chunk(h, c, w_in_ref, sga_ref, sgb_ref, rows):
    lo = c * GATE_CHUNK
    sga_ref[rows, lo:lo + GATE_CHUNK] = jax.nn.sigmoid(_dot(h, w_in_ref[:, OFF_GA + lo:OFF_GA + lo + GATE_CHUNK]))
    sgb_ref[rows, lo:lo + GATE_CHUNK] = jax.nn.sigmoid(_dot(h, w_in_ref[:, OFF_GB + lo:OFF_GB + lo + GATE_CHUNK]))


def _merge(ypool, yattn, sga_ref, sgb_ref, wpp_ref, wap_ref, merged_ref, rows):
    for c in range(N_GATE_CHUNKS):
        lo = c * GATE_CHUNK
        pp = _dot(ypool, wpp_ref[:, lo:lo + GATE_CHUNK])
        ap = _dot(yattn, wap_ref[:, lo:lo + GATE_CHUNK])
        merged = sga_ref[rows, lo:lo + GATE_CHUNK] * pp + sgb_ref[rows, lo:lo + GATE_CHUNK] * ap
        merged_ref[rows, lo:lo + GATE_CHUNK] = merged.astype(BF16)


BIG_WEIGHTS = (("w_in", D_MODEL, OFF_GB + D_MODEL), ("mix", len(POOL_WINDOWS) * POOL_GROUP_WIDTH, POOL_GROUP_WIDTH),
               ("wpp", POOL_WIDTH, D_MODEL), ("wap", Q_WIDTH, D_MODEL), ("wout", D_MODEL, D_MODEL),
               ("wg", D_MODEL, D_FF), ("wu", D_MODEL, D_FF), ("wd", D_FF, D_MODEL))
W_MIX, W_PP = 1, 2
W_FOLD = len(BIG_WEIGHTS)
HANDED_ON = tuple(w for w in range(len(BIG_WEIGHTS)) if w not in (W_MIX, W_PP)) + (W_FOLD,)
BF16_SHAPES = tuple((rows, cols) for _, rows, cols in BIG_WEIGHTS) + ((POOL_WIDTH, D_MODEL),)
CONVERT_ROWS = 128
STAGE_SLOTS = 4
STAGE_COLS = max(cols for _, _, cols in BIG_WEIGHTS)


def _weight_chunks():
    chunks = []
    for w, (_, nrows, cols) in enumerate(BIG_WEIGHTS):
        side_by_side = max(1, STAGE_COLS // cols)
        blocks = [(r0, min(CONVERT_ROWS, nrows - r0)) for r0 in range(0, nrows, CONVERT_ROWS)]
        for first in range(0, len(blocks), side_by_side):
            group = blocks[first:first + side_by_side]
            chunks.append((w, tuple((r0, rows, k * cols) for k, (r0, rows) in enumerate(group))))
    return tuple(chunks)


WEIGHT_CHUNKS = _weight_chunks()


def _writeback_copy(w, w_v, w_bf_hbm, out_sem):
    return pltpu.make_async_copy(w_v[w], w_bf_hbm[w], out_sem.at[w])


def _convert_weights(w_hbm, w_v, stage, stage_sem, mixf_s, params_ref):
    def chunk_copies(n):
        w, pieces = WEIGHT_CHUNKS[n]
        cols = BIG_WEIGHTS[w][2]
        slot = n % STAGE_SLOTS
        return [pltpu.make_async_copy(w_hbm[w].at[pl.ds(r0, rows), :],
                                      stage.at[slot, pl.ds(0, rows), pl.ds(c0, cols)], stage_sem.at[slot])
                for r0, rows, c0 in pieces]

    def convert(n):
        w, pieces = WEIGHT_CHUNKS[n]
        cols = BIG_WEIGHTS[w][2]
        for r0, rows, c0 in pieces:
            staged = stage[n % STAGE_SLOTS, 0:rows, c0:c0 + cols]
            if w == W_MIX:
                mixf_s[r0:r0 + rows, :] = staged
            else:
                w_v[w][r0:r0 + rows, :] = staged.astype(BF16)

    ahead = STAGE_SLOTS - 1
    for m in range(ahead):
        for copy in chunk_copies(m):
            copy.start()
    for n in range(len(WEIGHT_CHUNKS)):
        if n + ahead < len(WEIGHT_CHUNKS):
            for copy in chunk_copies(n + ahead):
                copy.start()
        for copy in chunk_copies(n):
            copy.wait()
        convert(n)

    for g in range(len(POOL_WINDOWS)):
        grp = slice(g * POOL_GROUP_WIDTH, (g + 1) * POOL_GROUP_WIDTH)
        scale = params_ref[ROW_PSCALE:ROW_PSCALE + 1, grp]
        w_v[W_FOLD][grp, :] = _dot((mixf_s[grp, :] * scale).astype(BF16), w_v[W_PP][grp, :]).astype(BF16)


def _prompt_kernel(*refs):
    nw = len(BIG_WEIGHTS)
    sinks_ref, x_ref, params_ref, bias_ref = refs[:4]
    w_hbm = refs[4:4 + nw]
    pos = 4 + nw
    outs = refs[pos:pos + 4]
    pos += 4
    w_bf_hbm = dict(zip(HANDED_ON, refs[pos:pos + len(HANDED_ON)]))
    pos += len(HANDED_ON)
    w_v = refs[pos:pos + nw + 1]
    pos += nw + 1
    stage, stage_sem, out_sem, mixf_s = refs[pos:pos + 4]
    scratch = refs[pos + 4:]
    w_in_ref, _, _, wap_ref, wout_ref, wg_ref, wu_ref, wd_ref, wfold_ref = w_v

    step = pl.program_id(0) * pl.num_programs(1) + pl.program_id(1)
    n_steps = pl.num_programs(0) * pl.num_programs(1)

    @pl.when(step == 0)
    def _():
        _convert_weights(w_hbm, w_v, stage, stage_sem, mixf_s, params_ref)

    @pl.when(step == 1)
    def _():
        for w in HANDED_ON:
            _writeback_copy(w, w_v, w_bf_hbm, out_sem).start()

    def body(t, carry):
        _prompt_tile(t, sinks_ref, x_ref, params_ref, bias_ref,
                     w_in_ref, wfold_ref, wap_ref, wout_ref, wg_ref, wu_ref, wd_ref, *outs, *scratch)
        return carry
    lax.fori_loop(0, TILES_PER_STEP, body, 0)

    @pl.when(step == n_steps - 1)
    def _():
        for w in HANDED_ON:
            _writeback_copy(w, w_v, w_bf_hbm, out_sem).wait()


def _prompt_tile(t, sinks_ref, x_ref, params_ref, bias_ref,
                 w_in_ref, wfold_ref, wap_ref, wout_ref, wg_ref, wu_ref, wd_ref,
                 y_ref, knew_ref, vnew_ref, pnew_ref,
                 h_s, u_s, q_s, kd_s, vd_s, s_s, p_s, ypool_s, yattn_s, sga_s, sgb_s, merged_s, x1_s, h2_s, act_s):
    T = SEQ_TILE
    R = SUB_TILE
    subs = [slice(r0, r0 + R) for r0 in range(0, T, R)]
    n_blocks = T // ATT_BLOCK
    j = pl.program_id(1) * TILES_PER_STEP + t
    rows_per_kv = GROUP * ATT_BLOCK

    def in_block(rows):
        return pl.ds(pl.multiple_of(t * T + rows.start, R), R)

    @pl.when(j == 0)
    def _():
        u_s[0:PREFIX, :] = jnp.zeros((PREFIX, POOL_WIDTH), F32)
        kd_s[:, 0:ATT_BLOCK, :] = jnp.zeros((N_KV_HEADS, ATT_BLOCK, KV_WIDTH), BF16)
        vd_s[:, 0:ATT_BLOCK, :] = jnp.zeros((N_KV_HEADS, ATT_BLOCK, KV_WIDTH), BF16)

    lane_r = lax.broadcasted_iota(jnp.int32, (R, LANES), 1)
    lane = lax.broadcasted_iota(jnp.int32, (ATT_BLOCK, LANES), 1)

    def norm1(rows):
        x = x_ref[0, in_block(rows), :]
        h_s[rows, :] = ((x * _rms_scale(x)) * _param(params_ref, ROW_N1, D_MODEL)).astype(BF16)

    def in_proj(rows):
        h = h_s[rows, :]
        u_s[PREFIX + rows.start:PREFIX + rows.stop, :] = _dot(h, w_in_ref[:, OFF_U:OFF_U + POOL_WIDTH])
        q = _dot(h, w_in_ref[:, OFF_Q:OFF_Q + Q_WIDTH])
        qn = _head_norm(q, _param(params_ref, ROW_QN, Q_WIDTH)).astype(BF16)
        for p in range(Q_COLS):
            qcol = qn[:, p * LANES:(p + 1) * LANES]
            lo, hi = 2 * p * LANES, (2 * p + 1) * LANES
            q_s[rows, lo:lo + LANES] = jnp.where(lane_r < HEAD_DIM, qcol, jnp.zeros_like(qcol))
            q_s[rows, hi:hi + LANES] = jnp.where(lane_r >= HEAD_DIM, qcol, jnp.zeros_like(qcol))
        kv = _dot(h, w_in_ref[:, OFF_K:OFF_K + 2 * KV_WIDTH])
        kn = _head_norm(kv[:, 0:KV_WIDTH], _param(params_ref, ROW_KN, KV_WIDTH))
        v = kv[:, KV_WIDTH:]
        dst_rows = slice(ATT_BLOCK + rows.start, ATT_BLOCK + rows.stop)
        for src, dst in ((kn, kd_s), (v, vd_s)):
            swapped = pltpu.roll(src, HEAD_DIM, 1)
            dst[0, dst_rows, :] = jnp.where(lane_r < HEAD_DIM, src, swapped).astype(BF16)
            dst[1, dst_rows, :] = jnp.where(lane_r < HEAD_DIM, swapped, src).astype(BF16)
        if rows.stop == T:
            knew_ref[0] = kn[R - ATT_BLOCK:, :].T
            vnew_ref[0] = v[R - ATT_BLOCK:, :].T
            batch = pl.program_id(0)
            for r in range(POOL_BUF):
                src = PREFIX + T - POOL_BUF + r
                pnew_ref[r, pl.ds(batch, 1), :] = u_s[src:src + 1, :]

    def pool(rows):
        pos1 = j * T + rows.start + lax.broadcasted_iota(jnp.int32, (R, 1), 0) + 1
        for g, w in enumerate(POOL_WINDOWS):
            cols = slice(g * POOL_GROUP_WIDTH, (g + 1) * POOL_GROUP_WIDTH)
            a = u_s[rows.start:rows.stop + PREFIX, cols]
            s = a
            shift = 1
            while shift < w:
                s = s + pltpu.roll(s, shift, 0)
                shift *= 2
            inv_cnt = 1.0 / jnp.minimum(pos1, w).astype(F32)
            ypool_s[rows, cols] = (s[PREFIX:, :] * inv_cnt - a[PREFIX:, :]).astype(BF16)

    def gates(rows, chunks):
        h = h_s[rows, :]
        for c in chunks:
            _gate_chunk(h, c, w_in_ref, sga_s, sgb_s, rows)

    def scores(b):
        r0 = b * ATT_BLOCK
        for c in range(N_KV_HEADS):
            parts = [q_s[r0:r0 + ATT_BLOCK, head * LANES:(head + 1) * LANES]
                     for head in range(c * GROUP, (c + 1) * GROUP)]
            s_s[b % 2, c * rows_per_kv:(c + 1) * rows_per_kv, :] = _dot_nt(
                jnp.concatenate(parts, axis=0), kd_s[c, r0:r0 + 2 * ATT_BLOCK, :])

    def softmax(b):
        bias = bias_ref[jnp.where(j == 0, 1, 0)] if b == 0 else bias_ref[0]
        inv_denoms = []
        for head in range(N_HEADS):
            sh = s_s[b % 2, head * ATT_BLOCK:(head + 1) * ATT_BLOCK, :] + bias
            sink = sinks_ref[head] * LOG2_E
            m = jnp.maximum(jnp.max(sh, axis=-1, keepdims=True), sink)
            p = jnp.exp2(sh - m)
            denom = jnp.sum(p, axis=-1, keepdims=True) + jnp.exp2(sink - m)
            p_s[b % 2, head * ATT_BLOCK:(head + 1) * ATT_BLOCK, :] = p.astype(BF16)
            inv_denoms.append(1.0 / denom)
        return inv_denoms

    def weighted_values(b, inv_denoms):
        r0 = b * ATT_BLOCK
        for c in range(N_KV_HEADS):
            o = _dot(p_s[b % 2, c * rows_per_kv:(c + 1) * rows_per_kv, :], vd_s[c, r0:r0 + 2 * ATT_BLOCK, :])
            for pp in range(Q_COLS // N_KV_HEADS):
                p_col = c * Q_COLS // N_KV_HEADS + pp
                head_lo = p_col * HEADS_PER_COL
                o_lo = o[(2 * pp) * ATT_BLOCK:(2 * pp + 1) * ATT_BLOCK, :] * inv_denoms[head_lo]
                o_hi = o[(2 * pp + 1) * ATT_BLOCK:(2 * pp + 2) * ATT_BLOCK, :] * inv_denoms[head_lo + 1]
                yattn_s[r0:r0 + ATT_BLOCK, p_col * LANES:(p_col + 1) * LANES] = (
                    jnp.where(lane < HEAD_DIM, o_lo, o_hi).astype(BF16))

    def out_proj(rows):
        _merge(ypool_s[rows, :], yattn_s[rows, :], sga_s, sgb_s, wfold_ref, wap_ref, merged_s, rows)
        x1 = x_ref[0, in_block(rows), :] + _dot(merged_s[rows, :], wout_ref[...])
        x1_s[rows, :] = x1
        h2_s[rows, :] = ((x1 * _rms_scale(x1)) * _param(params_ref, ROW_N2, D_MODEL)).astype(BF16)

    for rows in subs:
        norm1(rows)
    for rows in subs:
        in_proj(rows)
    pool(subs[0])
    gates(subs[0], range(N_GATE_CHUNKS))
    for rows in subs[1:]:
        pool(rows)
    later_gates = [(rows, c) for rows in subs[1:] for c in range(N_GATE_CHUNKS)]
    per_block = -(-len(later_gates) // n_blocks)
    scores(0)
    for b in range(n_blocks):
        if b + 1 < n_blocks:
            scores(b + 1)
        for rows, c in later_gates[b * per_block:(b + 1) * per_block]:
            gates(rows, [c])
        weighted_values(b, softmax(b))

    u_s[0:PREFIX, :] = u_s[T:T + PREFIX, :]
    kd_s[:, 0:ATT_BLOCK, :] = kd_s[:, T:T + ATT_BLOCK, :]
    vd_s[:, 0:ATT_BLOCK, :] = vd_s[:, T:T + ATT_BLOCK, :]

    for rows in subs:
        out_proj(rows)
    for rows in subs:
        _ffn_act(h2_s[rows, :], wg_ref, wu_ref, act_s, rows)
    for rows in subs:
        y_ref[0, in_block(rows), :] = x1_s[rows, :] + _dot(act_s[rows, :], wd_ref[...])


def _const_spec(shape):
    nd = len(shape)
    return pl.BlockSpec(shape, lambda *_: (0,) * nd, pipeline_mode=pl.Buffered(1))


def _band_bias():
    row = np.arange(ATT_BLOCK)[:, None]
    col = np.arange(2 * ATT_BLOCK)[None, :]
    band = (col >= row) & (col <= row + WINDOW)
    masks = np.stack([band, band & (col >= ATT_BLOCK)])
    return jnp.asarray(np.where(masks, 0.0, NEG), dtype=F32)


def _prompt_call(x, sinks, params, big):
    B, S, _ = x.shape
    T = SEQ_TILE
    step_rows = T * TILES_PER_STEP
    assert tuple(b.shape for b in big) == BF16_SHAPES[:len(BIG_WEIGHTS)]
    assert B * (S // step_rows) >= 2, "the bf16 write-back starts on the second grid step"
    bias = _band_bias()
    in_specs = [pl.BlockSpec(memory_space=pltpu.SMEM),
                pl.BlockSpec((1, step_rows, D_MODEL), lambda b, j: (b, j, 0)),
                _const_spec(params.shape),
                _const_spec(bias.shape)]
    in_specs += [pl.BlockSpec(memory_space=pl.ANY) for _ in big]
    out_shape = (jax.ShapeDtypeStruct((B, S, D_MODEL), F32),
                 jax.ShapeDtypeStruct((B, KV_WIDTH, ATT_BLOCK), F32),
                 jax.ShapeDtypeStruct((B, KV_WIDTH, ATT_BLOCK), F32),
                 jax.ShapeDtypeStruct((POOL_BUF, B, POOL_WIDTH), F32))
    out_shape += tuple(jax.ShapeDtypeStruct(BF16_SHAPES[w], BF16) for w in HANDED_ON)
    out_specs = (pl.BlockSpec((1, step_rows, D_MODEL), lambda b, j: (b, j, 0)),
                 pl.BlockSpec((1, KV_WIDTH, ATT_BLOCK), lambda b, j: (b, 0, 0)),
                 pl.BlockSpec((1, KV_WIDTH, ATT_BLOCK), lambda b, j: (b, 0, 0)),
                 pl.BlockSpec((POOL_BUF, B, POOL_WIDTH), lambda b, j: (0, 0, 0)))
    out_specs += tuple(pl.BlockSpec(memory_space=pl.ANY) for _ in HANDED_ON)
    scratch = [pltpu.VMEM(shape, BF16) for shape in BF16_SHAPES]
    scratch += [pltpu.VMEM((STAGE_SLOTS, CONVERT_ROWS, STAGE_COLS), F32),
                pltpu.SemaphoreType.DMA((STAGE_SLOTS,)),
                pltpu.SemaphoreType.DMA((len(BF16_SHAPES),)),
                pltpu.VMEM(BF16_SHAPES[W_MIX], F32)]
    scratch += [pltpu.VMEM((T, D_MODEL), BF16),
               pltpu.VMEM((PREFIX + T, POOL_WIDTH), F32),
               pltpu.VMEM((T, N_HEADS * LANES), BF16),
               pltpu.VMEM((N_KV_HEADS, ATT_BLOCK + T, KV_WIDTH), BF16),
               pltpu.VMEM((N_KV_HEADS, ATT_BLOCK + T, KV_WIDTH), BF16),
               pltpu.VMEM((2, N_HEADS * ATT_BLOCK, 2 * ATT_BLOCK), F32),
               pltpu.VMEM((2, N_HEADS * ATT_BLOCK, 2 * ATT_BLOCK), BF16),
               pltpu.VMEM((T, POOL_WIDTH), BF16),
               pltpu.VMEM((T, Q_WIDTH), BF16),
               pltpu.VMEM((T, D_MODEL), F32),
               pltpu.VMEM((T, D_MODEL), F32),
               pltpu.VMEM((T, D_MODEL), BF16),
               pltpu.VMEM((T, D_MODEL), F32),
               pltpu.VMEM((T, D_MODEL), BF16),
               pltpu.VMEM((T, D_FF), BF16)]
    return pl.pallas_call(
        _prompt_kernel,
        grid=(B, S // step_rows),
        in_specs=in_specs,
        out_specs=out_specs,
        out_shape=out_shape,
        scratch_shapes=scratch,
        compiler_params=pltpu.CompilerParams(
            dimension_semantics=("arbitrary", "arbitrary"),
            vmem_limit_bytes=VMEM_LIMIT_BYTES),
        name="prompt_layer",
    )(sinks, x, params, bias, *big)


def _sample_kernel(*refs):
    nh = len(HANDED_ON)
    n_in = 6
    (x_ref, ck_ref, cv_ref, sp_ref, sinks_ref, params_ref) = refs[:n_in]
    w_hbm = refs[n_in:n_in + nh]
    pos = n_in + nh
    y_ref, knew_ref, vnew_ref, pnew_ref = refs[pos:pos + 4]
    pos += 4
    w_v = refs[pos:pos + nh]
    pos += nh
    (w_sem, u_s, kt_s, vt_s, q8_s, k8_s, v8_s, o8_s, ypre_s, sga_s, sgb_s, merged_s, act_s) = refs[pos:]
    w_in_ref, wap_ref, wout_ref, wg_ref, wu_ref, wd_ref, wfold_ref = w_v

    SB = SAMPLE_BLOCK
    col_tiles = D_MODEL // LANES
    N = x_ref.shape[0] // col_tiles
    nsteps = N // SB
    i = pl.program_id(0)
    lane = lax.broadcasted_iota(jnp.int32, (N, LANES), 1)
    all_rows = slice(0, N)

    def load_x():
        return jnp.concatenate([x_ref[pl.ds(ct, N, stride=col_tiles), :] for ct in range(col_tiles)], axis=1)

    def weight_copy(k):
        return pltpu.make_async_copy(w_hbm[k], w_v[k], w_sem.at[k])

    @pl.when(i == 0)
    def _():
        for k in range(nh):
            weight_copy(k).start()
        weight_copy(0).wait()
        x = load_x()
        h = ((x * _rms_scale(x)) * _param(params_ref, ROW_N1, D_MODEL)).astype(BF16)
        u_s[...] = _dot(h, w_in_ref[:, OFF_U:OFF_U + POOL_WIDTH])
        for gc in range(N_GATE_CHUNKS):
            _gate_chunk(h, gc, w_in_ref, sga_s, sgb_s, all_rows)
        q = _head_norm(_dot(h, w_in_ref[:, OFF_Q:OFF_Q + Q_WIDTH]), _param(params_ref, ROW_QN, Q_WIDTH))
        kv = _dot(h, w_in_ref[:, OFF_K:OFF_K + 2 * KV_WIDTH])
        kn = _head_norm(kv[:, 0:KV_WIDTH], _param(params_ref, ROW_KN, KV_WIDTH))
        v = kv[:, KV_WIDTH:]
        kt_s[...] = kn.T
        vt_s[...] = v.T
        for head in range(N_HEADS):
            k8_s[pl.ds(head, N, stride=N_HEADS), :] = kn
            v8_s[pl.ds(head, N, stride=N_HEADS), :] = v
        for p in range(Q_COLS):
            qcol = q[:, p * LANES:(p + 1) * LANES]
            qswap = pltpu.roll(qcol, HEAD_DIM, 1)
            for e in range(HEADS_PER_COL):
                head = p * HEADS_PER_COL + e
                c = _kv_head(head)
                src = qcol if e == c else qswap
                keep = (lane < HEAD_DIM) if c == 0 else (lane >= HEAD_DIM)
                q8_s[pl.ds(head, N, stride=N_HEADS), :] = jnp.where(keep, src, 0.0)

    base = i * SB
    rows = pl.ds(pl.multiple_of(base, SB), SB)

    unew = u_s[rows, :]
    for g, w in enumerate(POOL_WINDOWS):
        cols = slice(g * POOL_GROUP_WIDTH, (g + 1) * POOL_GROUP_WIDTH)
        s = unew[:, cols]
        for r in range(POOL_BUF - (w - 1), POOL_BUF):
            s = s + sp_ref[r, :, cols]
        ypre_s[rows, cols] = (s * (1.0 / w) - unew[:, cols]).astype(BF16)
    pnew_ref[0:POOL_BUF - 1] = sp_ref[1:POOL_BUF]
    pnew_ref[POOL_BUF - 1] = unew

    hrows = pl.ds(pl.multiple_of(base * N_HEADS, SB * N_HEADS), SB * N_HEADS)
    lhs = q8_s[hrows, :]
    sc = jnp.concatenate(
        [_dot(lhs[bb * N_HEADS:(bb + 1) * N_HEADS, :].astype(BF16), ck_ref[bb].astype(BF16)) for bb in range(SB)],
        axis=0)
    sc_self = jnp.sum(lhs * k8_s[hrows, :], axis=-1, keepdims=True)
    row_head = lax.rem(lax.broadcasted_iota(jnp.int32, (SB * N_HEADS, 1), 0), N_HEADS)
    sink = jnp.zeros((SB * N_HEADS, 1), F32)
    for head in range(N_HEADS):
        sink = jnp.where(row_head == head, sinks_ref[head] * LOG2_E, sink)
    m = jnp.maximum(jnp.maximum(jnp.max(sc, axis=-1, keepdims=True), sc_self), sink)
    p = jnp.exp2(sc - m)
    p_self = jnp.exp2(sc_self - m)
    denom = jnp.sum(p, axis=-1, keepdims=True) + p_self + jnp.exp2(sink - m)
    inv = 1.0 / denom
    pn = p * inv
    o = jnp.concatenate(
        [_dot_nt(pn[bb * N_HEADS:(bb + 1) * N_HEADS, :].astype(BF16), cv_ref[bb].astype(BF16)) for bb in range(SB)],
        axis=0)
    o8_s[hrows, :] = o + (p_self * inv) * v8_s[hrows, :]

    lane_c = lax.broadcasted_iota(jnp.int32, (KV_WIDTH, WINDOW), 1)
    kt_all = kt_s[...]
    vt_all = vt_s[...]
    for bb in range(SB):
        put = (WINDOW - 1) - (base + bb)
        knew_ref[bb] = jnp.where(lane_c == WINDOW - 1, pltpu.roll(kt_all, put, 1),
                                 pltpu.roll(ck_ref[bb], WINDOW - 1, 1))
        vnew_ref[bb] = jnp.where(lane_c == WINDOW - 1, pltpu.roll(vt_all, put, 1),
                                 pltpu.roll(cv_ref[bb], WINDOW - 1, 1))

    @pl.when(i == nsteps - 1)
    def _():
        for k in range(1, nh):
            weight_copy(k).wait()
        ycols = []
        for p in range(Q_COLS):
            halves = []
            for e in range(HEADS_PER_COL):
                head = p * HEADS_PER_COL + e
                oh = o8_s[pl.ds(head, N, stride=N_HEADS), :]
                halves.append(oh if e == _kv_head(head) else pltpu.roll(oh, HEAD_DIM, 1))
            ycols.append(jnp.where(lane < HEAD_DIM, halves[0], halves[1]).astype(BF16))
        yattn = jnp.concatenate(ycols, axis=1)
        _merge(ypre_s[...], yattn, sga_s, sgb_s, wfold_ref, wap_ref, merged_s, all_rows)
        x1 = load_x() + _dot(merged_s[...], wout_ref[...])
        h2 = ((x1 * _rms_scale(x1)) * _param(params_ref, ROW_N2, D_MODEL)).astype(BF16)
        _ffn_act(h2, wg_ref, wu_ref, act_s, all_rows)
        y = x1 + _dot(act_s[...], wd_ref[...])
        for ct in range(col_tiles):
            y_ref[pl.ds(ct, N, stride=col_tiles), :] = y[:, ct * LANES:(ct + 1) * LANES]


def _sample_call(x, ck, cv, sp, sinks, params, big):
    N = x.shape[0] // (D_MODEL // LANES)
    SB = SAMPLE_BLOCK
    W = WINDOW
    assert tuple(b.shape for b in big) == tuple(BF16_SHAPES[w] for w in HANDED_ON)
    consts = (sinks, params)
    in_specs = [_const_spec(x.shape),
                pl.BlockSpec((SB, KV_WIDTH, W), lambda i: (i, 0, 0)),
                pl.BlockSpec((SB, KV_WIDTH, W), lambda i: (i, 0, 0)),
                pl.BlockSpec((POOL_BUF, SB, POOL_WIDTH), lambda i: (0, i, 0)),
                pl.BlockSpec(memory_space=pltpu.SMEM),
                _const_spec(params.shape)]
    in_specs += [pl.BlockSpec(memory_space=pl.ANY) for _ in big]
    out_shape = (jax.ShapeDtypeStruct(x.shape, F32),
                 jax.ShapeDtypeStruct((N, KV_WIDTH, W), F32),
                 jax.ShapeDtypeStruct((N, KV_WIDTH, W), F32),
                 jax.ShapeDtypeStruct((POOL_BUF, N, POOL_WIDTH), F32))
    out_specs = (pl.BlockSpec(x.shape, lambda i: (0, 0)),
                 pl.BlockSpec((SB, KV_WIDTH, W), lambda i: (i, 0, 0)),
                 pl.BlockSpec((SB, KV_WIDTH, W), lambda i: (i, 0, 0)),
                 pl.BlockSpec((POOL_BUF, SB, POOL_WIDTH), lambda i: (0, i, 0)))
    scratch = [pltpu.VMEM(b.shape, BF16) for b in big]
    scratch += [pltpu.SemaphoreType.DMA((len(big),)),
                pltpu.VMEM((N, POOL_WIDTH), F32),
               pltpu.VMEM((KV_WIDTH, N), F32),
               pltpu.VMEM((KV_WIDTH, N), F32),
               pltpu.VMEM((N * N_HEADS, LANES), F32),
               pltpu.VMEM((N * N_HEADS, LANES), F32),
               pltpu.VMEM((N * N_HEADS, LANES), F32),
               pltpu.VMEM((N * N_HEADS, LANES), F32),
               pltpu.VMEM((N, POOL_WIDTH), BF16),
               pltpu.VMEM((N, D_MODEL), F32),
               pltpu.VMEM((N, D_MODEL), F32),
               pltpu.VMEM((N, D_MODEL), BF16),
               pltpu.VMEM((N, D_FF), BF16)]
    return pl.pallas_call(
        _sample_kernel,
        grid=(N // SB,),
        in_specs=in_specs,
        out_specs=out_specs,
        out_shape=out_shape,
        scratch_shapes=scratch,
        compiler_params=pltpu.CompilerParams(
            dimension_semantics=("arbitrary",),
            vmem_limit_bytes=VMEM_LIMIT_BYTES),
        name="sample_layer",
    )(x, ck, cv, sp, *consts, *big)


def _cache_to_device_order(c):
    n, w = c.shape[0], c.shape[1]
    return jnp.transpose(c, (0, 2, 3, 1)).reshape(n, KV_WIDTH, w)


def _cache_from_device_order(c):
    n, _, w = c.shape
    return jnp.transpose(c.reshape(n, N_KV_HEADS, HEAD_DIM, w), (0, 3, 1, 2))[None]


@jax.jit
def _forward(x_prompt, x_sample, cache_k, cache_v, state_pool, norm1, w_in, q_norm, k_norm, sinks,
             pool_mix_w, pool_scale, w_pool_proj, w_attn_proj, w_out, norm2, w_gate, w_up, w_down):
    depth = w_in.shape[0]
    assert depth == 1, "single-layer trunk"
    l = 0
    N = x_sample.shape[0]
    assert cache_k.shape[2] == WINDOW and x_sample.shape[1] == 1

    col = lax.broadcasted_iota(jnp.int32, (1, D_MODEL), 1)
    row = lax.broadcasted_iota(jnp.int32, (PARAM_ROWS, 1), 0)

    def widened(v, width):
        return jnp.where(col < width, jnp.tile(v, D_MODEL // v.shape[0])[None, :], 0.0)

    rows = {ROW_N1: norm1[l][None, :], ROW_N2: norm2[l][None, :],
            ROW_QN: widened(q_norm[l], Q_WIDTH) * (HEAD_DIM ** -0.5 * LOG2_E),
            ROW_KN: widened(k_norm[l], KV_WIDTH), ROW_PSCALE: widened(pool_scale[l], POOL_WIDTH)}
    params = jnp.zeros((PARAM_ROWS, D_MODEL), F32)
    for r, v in rows.items():
        params = jnp.where(row == r, v, params)
    big = (w_in[l], pool_mix_w[l].reshape(len(POOL_WINDOWS) * POOL_GROUP_WIDTH, POOL_GROUP_WIDTH),
           w_pool_proj[l], w_attn_proj[l], w_out[l], w_gate[l], w_up[l], w_down[l])

    y_p, k_p, v_p, pool_p, *weights_bf16 = _prompt_call(x_prompt, sinks[l], params, big)

    y_s, k_s, v_s, pool_s = _sample_call(
        x_sample.reshape(N * (D_MODEL // LANES), LANES),
        _cache_to_device_order(cache_k[l]), _cache_to_device_order(cache_v[l]),
        jnp.transpose(state_pool[l], (1, 0, 2)), sinks[l], params, tuple(weights_bf16))

    return (y_p, y_s.reshape(N, 1, D_MODEL),
            _cache_from_device_order(k_p), _cache_from_device_order(v_p), jnp.transpose(pool_p, (1, 0, 2))[None],
            _cache_from_device_order(k_s), _cache_from_device_order(v_s),
            jnp.transpose(pool_s, (1, 0, 2))[None])


def kernel(x_prompt, x_sample, cache_k, cache_v, state_pool, norm1, w_in, q_norm, k_norm, sinks, pool_mix_w,
           pool_scale, w_pool_proj, w_attn_proj, w_out, norm2, w_gate, w_up, w_down):
    return _forward(x_prompt, x_sample, cache_k, cache_v, state_pool, norm1, w_in, q_norm, k_norm, sinks,
                    pool_mix_w, pool_scale, w_pool_proj, w_attn_proj, w_out, norm2, w_gate, w_up, w_down)
```

```python
import numpy as np
import jax
import jax.numpy as jnp
from jax import lax
from jax.experimental import pallas as pl
from jax.experimental.pallas import tpu as pltpu

D_MODEL = 1024
POOL_WINDOWS = (2, 4, 8, 16)
POOL_GROUP_WIDTH = 128
POOL_WIDTH = 512
POOL_BUF = 15
N_HEADS = 8
N_KV_HEADS = 2
HEAD_DIM = 64
GROUP = N_HEADS // N_KV_HEADS
Q_WIDTH = 512
KV_WIDTH = 128
WINDOW = 128
D_FF = 2816
EPS = 1e-6
NEG = -1e30
LOG2_E = 1.4426950408889634

LANES = 128
ATT_BLOCK = WINDOW
SEQ_TILE = 512
TILES_PER_STEP = 1
SUB_TILE = 256
FF_CHUNK = 256
GATE_CHUNK = 256
SAMPLE_BLOCK = 16
PREFIX = 16
VMEM_LIMIT_BYTES = 58 * 1024 * 1024
Q_COLS = Q_WIDTH // LANES
HEADS_PER_COL = LANES // HEAD_DIM
N_GATE_CHUNKS = D_MODEL // GATE_CHUNK

OFF_U = 0
OFF_Q = OFF_U + POOL_WIDTH
OFF_K = OFF_Q + Q_WIDTH
OFF_V = OFF_K + KV_WIDTH
OFF_GA = OFF_V + KV_WIDTH
OFF_GB = OFF_GA + D_MODEL

BF16 = jnp.bfloat16
F32 = jnp.float32

ROW_N1, ROW_N2, ROW_QN, ROW_KN, ROW_PSCALE, PARAM_ROWS = 0, 1, 2, 3, 4, 8


def _param(params_ref, row, width):
    return params_ref[row:row + 1, 0:width]


def _dot(a, b):
    return jnp.dot(a, b, preferred_element_type=F32)


def _dot_nt(a, b):
    return lax.dot_general(a, b, (((1,), (1,)), ((), ())), preferred_element_type=F32)


def _rms_scale(x):
    return lax.rsqrt(jnp.mean(x * x, axis=-1, keepdims=True) + EPS)


def _head_norm(x, gain):
    sq = x * x
    low = lax.broadcasted_iota(jnp.int32, (x.shape[0], LANES), 1) < HEAD_DIM
    cols = []
    for c in range(0, x.shape[1], LANES):
        blk = sq[:, c:c + LANES]
        ss_lo = jnp.sum(jnp.where(low, blk, 0.0), axis=-1, keepdims=True)
        ss_hi = jnp.sum(jnp.where(low, 0.0, blk), axis=-1, keepdims=True)
        cols.append(jnp.where(low, ss_lo, ss_hi))
    ss = jnp.concatenate(cols, axis=1) if len(cols) > 1 else cols[0]
    return (x * lax.rsqrt(ss * (1.0 / HEAD_DIM) + EPS)) * gain


def _kv_head(head):
    return head // GROUP


def _ffn_act(h2, wg_ref, wu_ref, act_ref, rows):
    for start in range(0, D_FF, FF_CHUNK):
        width = min(FF_CHUNK, D_FF - start)
        g = _dot(h2, wg_ref[:, start:start + width])
        u = _dot(h2, wu_ref[:, start:start + width])
        act_ref[rows, start:start + width] = (g * jax.nn.sigmoid(g) * u).astype(BF16)


def _gate_chunk(h, c, w_in_ref, sga_ref, sgb_ref, rows):
    lo = c * GATE_CHUNK
    sga_ref[rows, lo:lo + GATE_CHUNK] = jax.nn.sigmoid(_dot(h, w_in_ref[:, OFF_GA + lo:OFF_GA + lo + GATE_CHUNK]))
    sgb_ref[rows, lo:lo + GATE_CHUNK] = jax.nn.sigmoid(_dot(h, w_in_ref[:, OFF_GB + lo:OFF_GB + lo + GATE_CHUNK]))


def _merge(ypool, yattn, sga_ref, sgb_ref, wpp_ref, wap_ref, merged_ref, rows):
    for c in range(N_GATE_CHUNKS):
        lo = c * GATE_CHUNK
        pp = _dot(ypool, wpp_ref[:, lo:lo + GATE_CHUNK])
        ap = _dot(yattn, wap_ref[:, lo:lo + GATE_CHUNK])
        merged = sga_ref[rows, lo:lo + GATE_CHUNK] * pp + sgb_ref[rows, lo:lo + GATE_CHUNK] * ap
        merged_ref[rows, lo:lo + GATE_CHUNK] = merged.astype(BF16)


BIG_WEIGHTS = (("w_in", D_MODEL, OFF_GB + D_MODEL), ("mix", len(POOL_WINDOWS) * POOL_GROUP_WIDTH, POOL_GROUP_WIDTH),
               ("wpp", POOL_WIDTH, D_MODEL), ("wap", Q_WIDTH, D_MODEL), ("wout", D_MODEL, D_MODEL),
               ("wg", D_MODEL, D_FF), ("wu", D_MODEL, D_FF), ("wd", D_FF, D_MODEL))
W_MIX, W_PP = 1, 2
W_FOLD = len(BIG_WEIGHTS)
HANDED_ON = tuple(w for w in range(len(BIG_WEIGHTS)) if w not in (W_MIX, W_PP)) + (W_FOLD,)
BF16_SHAPES = tuple((rows, cols) for _, rows, cols in BIG_WEIGHTS) + ((POOL_WIDTH, D_MODEL),)
CONVERT_ROWS = 128
STAGE_SLOTS = 4
STAGE_COLS = max(cols for _, _, cols in BIG_WEIGHTS)


def _weight_chunks():
    chunks = []
    for w, (_, nrows, cols) in enumerate(BIG_WEIGHTS):
        side_by_side = max(1, STAGE_COLS // cols)
        blocks = [(r0, min(CONVERT_ROWS, nrows - r0)) for r0 in range(0, nrows, CONVERT_ROWS)]
        for first in range(0, len(blocks), side_by_side):
            group = blocks[first:first + side_by_side]
            chunks.append((w, tuple((r0, rows, k * cols) for k, (r0, rows) in enumerate(group))))
    return tuple(chunks)


WEIGHT_CHUNKS = _weight_chunks()


def _writeback_copy(w, w_v, w_bf_hbm, out_sem):
    return pltpu.make_async_copy(w_v[w], w_bf_hbm[w], out_sem.at[w])


def _convert_weights(w_hbm, w_v, stage, stage_sem, mixf_s, params_ref):
    def chunk_copies(n):
        w, pieces = WEIGHT_CHUNKS[n]
        cols = BIG_WEIGHTS[w][2]
        slot = n % STAGE_SLOTS
        return [pltpu.make_async_copy(w_hbm[w].at[pl.ds(r0, rows), :],
                                      stage.at[slot, pl.ds(0, rows), pl.ds(c0, cols)], stage_sem.at[slot])
                for r0, rows, c0 in pieces]

    def convert(n):
        w, pieces = WEIGHT_CHUNKS[n]
        cols = BIG_WEIGHTS[w][2]
        for r0, rows, c0 in pieces:
            staged = stage[n % STAGE_SLOTS, 0:rows, c0:c0 + cols]
            if w == W_MIX:
                mixf_s[r0:r0 + rows, :] = staged
            else:
                w_v[w][r0:r0 + rows, :] = staged.astype(BF16)

    ahead = STAGE_SLOTS - 1
    for m in range(ahead):
        for copy in chunk_copies(m):
            copy.start()
    for n in range(len(WEIGHT_CHUNKS)):
        if n + ahead < len(WEIGHT_CHUNKS):
            for copy in chunk_copies(n + ahead):
                copy.start()
        for copy in chunk_copies(n):
            copy.wait()
        convert(n)

    for g in range(len(POOL_WINDOWS)):
        grp = slice(g * POOL_GROUP_WIDTH, (g + 1) * POOL_GROUP_WIDTH)
        scale = params_ref[ROW_PSCALE:ROW_PSCALE + 1, grp]
        w_v[W_FOLD][grp, :] = _dot((mixf_s[grp, :] * scale).astype(BF16), w_v[W_PP][grp, :]).astype(BF16)


def _prompt_kernel(*refs):
    nw = len(BIG_WEIGHTS)
    sinks_ref, x_ref, params_ref, bias_ref = refs[:4]
    w_hbm = refs[4:4 + nw]
    pos = 4 + nw
    outs = refs[pos:pos + 4]
    pos += 4
    w_bf_hbm = dict(zip(HANDED_ON, refs[pos:pos + len(HANDED_ON)]))
    pos += len(HANDED_ON)
    w_v = refs[pos:pos + nw + 1]
    pos += nw + 1
    stage, stage_sem, out_sem, mixf_s = refs[pos:pos + 4]
    scratch = refs[pos + 4:]
    w_in_ref, _, _, wap_ref, wout_ref, wg_ref, wu_ref, wd_ref, wfold_ref = w_v

    step = pl.program_id(0) * pl.num_programs(1) + pl.program_id(1)
    n_steps = pl.num_programs(0) * pl.num_programs(1)

    @pl.when(step == 0)
    def _():
        _convert_weights(w_hbm, w_v, stage, stage_sem, mixf_s, params_ref)

    @pl.when(step == 1)
    def _():
        for w in HANDED_ON:
            _writeback_copy(w, w_v, w_bf_hbm, out_sem).start()

    def body(t, carry):
        _prompt_tile(t, sinks_ref, x_ref, params_ref, bias_ref,
                     w_in_ref, wfold_ref, wap_ref, wout_ref, wg_ref, wu_ref, wd_ref, *outs, *scratch)
        return carry
    lax.fori_loop(0, TILES_PER_STEP, body, 0)

    @pl.when(step == n_steps - 1)
    def _():
        for w in HANDED_ON:
            _writeback_copy(w, w_v, w_bf_hbm, out_sem).wait()


def _prompt_tile(t, sinks_ref, x_ref, params_ref, bias_ref,
                 w_in_ref, wfold_ref, wap_ref, wout_ref, wg_ref, wu_ref, wd_ref,
                 y_ref, knew_ref, vnew_ref, pnew_ref,
                 h_s, u_s, q_s, kd_s, vd_s, s_s, p_s, ypool_s, yattn_s, sga_s, sgb_s, merged_s, x1_s, h2_s, act_s):
    T = SEQ_TILE
    R = SUB_TILE
    subs = [slice(r0, r0 + R) for r0 in range(0, T, R)]
    n_blocks = T // ATT_BLOCK
    j = pl.program_id(1) * TILES_PER_STEP + t
    rows_per_kv = GROUP * ATT_BLOCK

    def in_block(rows):
        return pl.ds(pl.multiple_of(t * T + rows.start, R), R)

    @pl.when(j == 0)
    def _():
        u_s[0:PREFIX, :] = jnp.zeros((PREFIX, POOL_WIDTH), F32)
        kd_s[:, 0:ATT_BLOCK, :] = jnp.zeros((N_KV_HEADS, ATT_BLOCK, KV_WIDTH), BF16)
        vd_s[:, 0:ATT_BLOCK, :] = jnp.zeros((N_KV_HEADS, ATT_BLOCK, KV_WIDTH), BF16)

    lane_r = lax.broadcasted_iota(jnp.int32, (R, LANES), 1)
    lane = lax.broadcasted_iota(jnp.int32, (ATT_BLOCK, LANES), 1)

    def norm1(rows):
        x = x_ref[0, in_block(rows), :]
        h_s[rows, :] = ((x * _rms_scale(x)) * _param(params_ref, ROW_N1, D_MODEL)).astype(BF16)

    def in_proj(rows):
        h = h_s[rows, :]
        u_s[PREFIX + rows.start:PREFIX + rows.stop, :] = _dot(h, w_in_ref[:, OFF_U:OFF_U + POOL_WIDTH])
        q = _dot(h, w_in_ref[:, OFF_Q:OFF_Q + Q_WIDTH])
        qn = _head_norm(q, _param(params_ref, ROW_QN, Q_WIDTH)).astype(BF16)
        for p in range(Q_COLS):
            qcol = qn[:, p * LANES:(p + 1) * LANES]
            lo, hi = 2 * p * LANES, (2 * p + 1) * LANES
            q_s[rows, lo:lo + LANES] = jnp.where(lane_r < HEAD_DIM, qcol, jnp.zeros_like(qcol))
            q_s[rows, hi:hi + LANES] = jnp.where(lane_r >= HEAD_DIM, qcol, jnp.zeros_like(qcol))
        kv = _dot(h, w_in_ref[:, OFF_K:OFF_K + 2 * KV_WIDTH])
        kn = _head_norm(kv[:, 0:KV_WIDTH], _param(params_ref, ROW_KN, KV_WIDTH))
        v = kv[:, KV_WIDTH:]
        dst_rows = slice(ATT_BLOCK + rows.start, ATT_BLOCK + rows.stop)
        for src, dst in ((kn, kd_s), (v, vd_s)):
            swapped = pltpu.roll(src, HEAD_DIM, 1)
            dst[0, dst_rows, :] = jnp.where(lane_r < HEAD_DIM, src, swapped).astype(BF16)
            dst[1, dst_rows, :] = jnp.where(lane_r < HEAD_DIM, swapped, src).astype(BF16)
        if rows.stop == T:
            knew_ref[0] = kn[R - ATT_BLOCK:, :].T
            vnew_ref[0] = v[R - ATT_BLOCK:, :].T
            batch = pl.program_id(0)
            for r in range(POOL_BUF):
                src = PREFIX + T - POOL_BUF + r
                pnew_ref[r, pl.ds(batch, 1), :] = u_s[src:src + 1, :]

    def pool(rows):
        pos1 = j * T + rows.start + lax.broadcasted_iota(jnp.int32, (R, 1), 0) + 1
        for g, w in enumerate(POOL_WINDOWS):
            cols = slice(g * POOL_GROUP_WIDTH, (g + 1) * POOL_GROUP_WIDTH)
            a = u_s[rows.start:rows.stop + PREFIX, cols]
            s = a
            shift = 1
            while shift < w:
                s = s + pltpu.roll(s, shift, 0)
                shift *= 2
            inv_cnt = 1.0 / jnp.minimum(pos1, w).astype(F32)
            ypool_s[rows, cols] = (s[PREFIX:, :] * inv_cnt - a[PREFIX:, :]).astype(BF16)

    def gates(rows, chunks):
        h = h_s[rows, :]
        for c in chunks:
            _gate_chunk(h, c, w_in_ref, sga_s, sgb_s, rows)

    def scores(b):
        r0 = b * ATT_BLOCK
        for c in range(N_KV_HEADS):
            parts = [q_s[r0:r0 + ATT_BLOCK, head * LANES:(head + 1) * LANES]
                     for head in range(c * GROUP, (c + 1) * GROUP)]
            s_s[b % 2, c * rows_per_kv:(c + 1) * rows_per_kv, :] = _dot_nt(
                jnp.concatenate(parts, axis=0), kd_s[c, r0:r0 + 2 * ATT_BLOCK, :])

    def softmax(b):
        bias = bias_ref[jnp.where(j == 0, 1, 0)] if b == 0 else bias_ref[0]
        inv_denoms = []
        for head in range(N_HEADS):
            sh = s_s[b % 2, head * ATT_BLOCK:(head + 1) * ATT_BLOCK, :] + bias
            sink = sinks_ref[head] * LOG2_E
            m = jnp.maximum(jnp.max(sh, axis=-1, keepdims=True), sink)
            p = jnp.exp2(sh - m)
            denom = jnp.sum(p, axis=-1, keepdims=True) + jnp.exp2(sink - m)
            p_s[b % 2, head * ATT_BLOCK:(head + 1) * ATT_BLOCK, :] = p.astype(BF16)
            inv_denoms.append(1.0 / denom)
        return inv_denoms

    def weighted_values(b, inv_denoms):
        r0 = b * ATT_BLOCK
        for c in range(N_KV_HEADS):
            o = _dot(p_s[b % 2, c * rows_per_kv:(c + 1) * rows_per_kv, :], vd_s[c, r0:r0 + 2 * ATT_BLOCK, :])
            for pp in range(Q_COLS // N_KV_HEADS):
                p_col = c * Q_COLS // N_KV_HEADS + pp
                head_lo = p_col * HEADS_PER_COL
                o_lo = o[(2 * pp) * ATT_BLOCK:(2 * pp + 1) * ATT_BLOCK, :] * inv_denoms[head_lo]
                o_hi = o[(2 * pp + 1) * ATT_BLOCK:(2 * pp + 2) * ATT_BLOCK, :] * inv_denoms[head_lo + 1]
                yattn_s[r0:r0 + ATT_BLOCK, p_col * LANES:(p_col + 1) * LANES] = (
                    jnp.where(lane < HEAD_DIM, o_lo, o_hi).astype(BF16))

    def out_proj(rows):
        _merge(ypool_s[rows, :], yattn_s[rows, :], sga_s, sgb_s, wfold_ref, wap_ref, merged_s, rows)
        x1 = x_ref[0, in_block(rows), :] + _dot(merged_s[rows, :], wout_ref[...])
        x1_s[rows, :] = x1
        h2_s[rows, :] = ((x1 * _rms_scale(x1)) * _param(params_ref, ROW_N2, D_MODEL)).astype(BF16)

    assert len(subs) == 2 and n_blocks == 4
    first, second = subs
    half = N_GATE_CHUNKS // 2
    for rows in subs:
        norm1(rows)
    in_proj(first)
    in_proj(second)
    pool(first)
    fillers = [lambda: gates(first, range(N_GATE_CHUNKS)),
               lambda: gates(second, range(half)),
               lambda: (gates(second, range(half, N_GATE_CHUNKS)), pool(second)),
               lambda: None]
    scores(0)
    for b in range(n_blocks):
        if b + 1 < n_blocks:
            scores(b + 1)
        fillers[b]()
        weighted_values(b, softmax(b))

    u_s[0:PREFIX, :] = u_s[T:T + PREFIX, :]
    kd_s[:, 0:ATT_BLOCK, :] = kd_s[:, T:T + ATT_BLOCK, :]
    vd_s[:, 0:ATT_BLOCK, :] = vd_s[:, T:T + ATT_BLOCK, :]

    for rows in subs:
        out_proj(rows)
    for rows in subs:
        _ffn_act(h2_s[rows, :], wg_ref, wu_ref, act_s, rows)
    for rows in subs:
        y_ref[0, in_block(rows), :] = x1_s[rows, :] + _dot(act_s[rows, :], wd_ref[...])


def _const_spec(shape):
    nd = len(shape)
    return pl.BlockSpec(shape, lambda *_: (0,) * nd, pipeline_mode=pl.Buffered(1))


def _band_bias():
    row = np.arange(ATT_BLOCK)[:, None]
    col = np.arange(2 * ATT_BLOCK)[None, :]
    band = (col >= row) & (col <= row + WINDOW)
    masks = np.stack([band, band & (col >= ATT_BLOCK)])
    return jnp.asarray(np.where(masks, 0.0, NEG), dtype=F32)


def _prompt_call(x, sinks, params, big):
    B, S, _ = x.shape
    T = SEQ_TILE
    step_rows = T * TILES_PER_STEP
    assert tuple(b.shape for b in big) == BF16_SHAPES[:len(BIG_WEIGHTS)]
    assert B * (S // step_rows) >= 2, "the bf16 write-back starts on the second grid step"
    bias = _band_bias()
    in_specs = [pl.BlockSpec(memory_space=pltpu.SMEM),
                pl.BlockSpec((1, step_rows, D_MODEL), lambda b, j: (b, j, 0)),
                _const_spec(params.shape),
                _const_spec(bias.shape)]
    in_specs += [pl.BlockSpec(memory_space=pl.ANY) for _ in big]
    out_shape = (jax.ShapeDtypeStruct((B, S, D_MODEL), F32),
                 jax.ShapeDtypeStruct((B, KV_WIDTH, ATT_BLOCK), F32),
                 jax.ShapeDtypeStruct((B, KV_WIDTH, ATT_BLOCK), F32),
                 jax.ShapeDtypeStruct((POOL_BUF, B, POOL_WIDTH), F32))
    out_shape += tuple(jax.ShapeDtypeStruct(BF16_SHAPES[w], BF16) for w in HANDED_ON)
    out_specs = (pl.BlockSpec((1, step_rows, D_MODEL), lambda b, j: (b, j, 0)),
                 pl.BlockSpec((1, KV_WIDTH, ATT_BLOCK), lambda b, j: (b, 0, 0)),
                 pl.BlockSpec((1, KV_WIDTH, ATT_BLOCK), lambda b, j: (b, 0, 0)),
                 pl.BlockSpec((POOL_BUF, B, POOL_WIDTH), lambda b, j: (0, 0, 0)))
    out_specs += tuple(pl.BlockSpec(memory_space=pl.ANY) for _ in HANDED_ON)
    scratch = [pltpu.VMEM(shape, BF16) for shape in BF16_SHAPES]
    scratch += [pltpu.VMEM((STAGE_SLOTS, CONVERT_ROWS, STAGE_COLS), F32),
                pltpu.SemaphoreType.DMA((STAGE_SLOTS,)),
                pltpu.SemaphoreType.DMA((len(BF16_SHAPES),)),
                pltpu.VMEM(BF16_SHAPES[W_MIX], F32)]
    scratch += [pltpu.VMEM((T, D_MODEL), BF16),
               pltpu.VMEM((PREFIX + T, POOL_WIDTH), F32),
               pltpu.VMEM((T, N_HEADS * LANES), BF16),
               pltpu.VMEM((N_KV_HEADS, ATT_BLOCK + T, KV_WIDTH), BF16),
               pltpu.VMEM((N_KV_HEADS, ATT_BLOCK + T, KV_WIDTH), BF16),
               pltpu.VMEM((2, N_HEADS * ATT_BLOCK, 2 * ATT_BLOCK), F32),
               pltpu.VMEM((2, N_HEADS * ATT_BLOCK, 2 * ATT_BLOCK), BF16),
               pltpu.VMEM((T, POOL_WIDTH), BF16),
               pltpu.VMEM((T, Q_WIDTH), BF16),
               pltpu.VMEM((T, D_MODEL), F32),
               pltpu.VMEM((T, D_MODEL), F32),
               pltpu.VMEM((T, D_MODEL), BF16),
               pltpu.VMEM((T, D_MODEL), F32),
               pltpu.VMEM((T, D_MODEL), BF16),
               pltpu.VMEM((T, D_FF), BF16)]
    return pl.pallas_call(
        _prompt_kernel,
        grid=(B, S // step_rows),
        in_specs=in_specs,
        out_specs=out_specs,
        out_shape=out_shape,
        scratch_shapes=scratch,
        compiler_params=pltpu.CompilerParams(
            dimension_semantics=("arbitrary", "arbitrary"),
            vmem_limit_bytes=VMEM_LIMIT_BYTES),
        name="prompt_layer",
    )(sinks, x, params, bias, *big)


def _sample_kernel(*refs):
    nh = len(HANDED_ON)
    n_in = 6
    (x_ref, ck_ref, cv_ref, sp_ref, sinks_ref, params_ref) = refs[:n_in]
    w_hbm = refs[n_in:n_in + nh]
    pos = n_in + nh
    y_ref, knew_ref, vnew_ref, pnew_ref = refs[pos:pos + 4]
    pos += 4
    w_v = refs[pos:pos + nh]
    pos += nh
    (w_sem, u_s, kt_s, vt_s, q8_s, k8_s, v8_s, o8_s, ypre_s, sga_s, sgb_s, merged_s, act_s) = refs[pos:]
    w_in_ref, wap_ref, wout_ref, wg_ref, wu_ref, wd_ref, wfold_ref = w_v

    SB = SAMPLE_BLOCK
    col_tiles = D_MODEL // LANES
    N = x_ref.shape[0] // col_tiles
    nsteps = N // SB
    i = pl.program_id(0)
    lane = lax.broadcasted_iota(jnp.int32, (N, LANES), 1)
    all_rows = slice(0, N)

    def load_x():
        return jnp.concatenate([x_ref[pl.ds(ct, N, stride=col_tiles), :] for ct in range(col_tiles)], axis=1)

    def weight_copy(k):
        return pltpu.make_async_copy(w_hbm[k], w_v[k], w_sem.at[k])

    @pl.when(i == 0)
    def _():
        for k in range(nh):
            weight_copy(k).start()
        weight_copy(0).wait()
        x = load_x()
        h = ((x * _rms_scale(x)) * _param(params_ref, ROW_N1, D_MODEL)).astype(BF16)
        u_s[...] = _dot(h, w_in_ref[:, OFF_U:OFF_U + POOL_WIDTH])
        for gc in range(N_GATE_CHUNKS):
            _gate_chunk(h, gc, w_in_ref, sga_s, sgb_s, all_rows)
        q = _head_norm(_dot(h, w_in_ref[:, OFF_Q:OFF_Q + Q_WIDTH]), _param(params_ref, ROW_QN, Q_WIDTH))
        kv = _dot(h, w_in_ref[:, OFF_K:OFF_K + 2 * KV_WIDTH])
        kn = _head_norm(kv[:, 0:KV_WIDTH], _param(params_ref, ROW_KN, KV_WIDTH))
        v = kv[:, KV_WIDTH:]
        kt_s[...] = kn.T
        vt_s[...] = v.T
        for head in range(N_HEADS):
            k8_s[pl.ds(head, N, stride=N_HEADS), :] = kn
            v8_s[pl.ds(head, N, stride=N_HEADS), :] = v
        for p in range(Q_COLS):
            qcol = q[:, p * LANES:(p + 1) * LANES]
            qswap = pltpu.roll(qcol, HEAD_DIM, 1)
            for e in range(HEADS_PER_COL):
                head = p * HEADS_PER_COL + e
                c = _kv_head(head)
                src = qcol if e == c else qswap
                keep = (lane < HEAD_DIM) if c == 0 else (lane >= HEAD_DIM)
                q8_s[pl.ds(head, N, stride=N_HEADS), :] = jnp.where(keep, src, 0.0)

    base = i * SB
    rows = pl.ds(pl.multiple_of(base, SB), SB)

    unew = u_s[rows, :]
    for g, w in enumerate(POOL_WINDOWS):
        cols = slice(g * POOL_GROUP_WIDTH, (g + 1) * POOL_GROUP_WIDTH)
        s = unew[:, cols]
        for r in range(POOL_BUF - (w - 1), POOL_BUF):
            s = s + sp_ref[r, :, cols]
        ypre_s[rows, cols] = (s * (1.0 / w) - unew[:, cols]).astype(BF16)
    pnew_ref[0:POOL_BUF - 1] = sp_ref[1:POOL_BUF]
    pnew_ref[POOL_BUF - 1] = unew

    hrows = pl.ds(pl.multiple_of(base * N_HEADS, SB * N_HEADS), SB * N_HEADS)
    lhs = q8_s[hrows, :]
    sc = jnp.concatenate(
        [_dot(lhs[bb * N_HEADS:(bb + 1) * N_HEADS, :].astype(BF16), ck_ref[bb].astype(BF16)) for bb in range(SB)],
        axis=0)
    sc_self = jnp.sum(lhs * k8_s[hrows, :], axis=-1, keepdims=True)
    row_head = lax.rem(lax.broadcasted_iota(jnp.int32, (SB * N_HEADS, 1), 0), N_HEADS)
    sink = jnp.zeros((SB * N_HEADS, 1), F32)
    for head in range(N_HEADS):
        sink = jnp.where(row_head == head, sinks_ref[head] * LOG2_E, sink)
    m = jnp.maximum(jnp.maximum(jnp.max(sc, axis=-1, keepdims=True), sc_self), sink)
    p = jnp.exp2(sc - m)
    p_self = jnp.exp2(sc_self - m)
    denom = jnp.sum(p, axis=-1, keepdims=True) + p_self + jnp.exp2(sink - m)
    inv = 1.0 / denom
    pn = p * inv
    o = jnp.concatenate(
        [_dot_nt(pn[bb * N_HEADS:(bb + 1) * N_HEADS, :].astype(BF16), cv_ref[bb].astype(BF16)) for bb in range(SB)],
        axis=0)
    o8_s[hrows, :] = o + (p_self * inv) * v8_s[hrows, :]

    lane_c = lax.broadcasted_iota(jnp.int32, (KV_WIDTH, WINDOW), 1)
    kt_all = kt_s[...]
    vt_all = vt_s[...]
    for bb in range(SB):
        put = (WINDOW - 1) - (base + bb)
        knew_ref[bb] = jnp.where(lane_c == WINDOW - 1, pltpu.roll(kt_all, put, 1),
                                 pltpu.roll(ck_ref[bb], WINDOW - 1, 1))
        vnew_ref[bb] = jnp.where(lane_c == WINDOW - 1, pltpu.roll(vt_all, put, 1),
                                 pltpu.roll(cv_ref[bb], WINDOW - 1, 1))

    @pl.when(i == nsteps - 1)
    def _():
        for k in range(1, nh):
            weight_copy(k).wait()
        ycols = []
        for p in range(Q_COLS):
            halves = []
            for e in range(HEADS_PER_COL):
                head = p * HEADS_PER_COL + e
                oh = o8_s[pl.ds(head, N, stride=N_HEADS), :]
                halves.append(oh if e == _kv_head(head) else pltpu.roll(oh, HEAD_DIM, 1))
            ycols.append(jnp.where(lane < HEAD_DIM, halves[0], halves[1]).astype(BF16))
        yattn = jnp.concatenate(ycols, axis=1)
        _merge(ypre_s[...], yattn, sga_s, sgb_s, wfold_ref, wap_ref, merged_s, all_rows)
        x1 = load_x() + _dot(merged_s[...], wout_ref[...])
        h2 = ((x1 * _rms_scale(x1)) * _param(params_ref, ROW_N2, D_MODEL)).astype(BF16)
        _ffn_act(h2, wg_ref, wu_ref, act_s, all_rows)
        y = x1 + _dot(act_s[...], wd_ref[...])
        for ct in range(col_tiles):
            y_ref[pl.ds(ct, N, stride=col_tiles), :] = y[:, ct * LANES:(ct + 1) * LANES]


def _sample_call(x, ck, cv, sp, sinks, params, big):
    N = x.shape[0] // (D_MODEL // LANES)
    SB = SAMPLE_BLOCK
    W = WINDOW
    assert tuple(b.shape for b in big) == tuple(BF16_SHAPES[w] for w in HANDED_ON)
    consts = (sinks, params)
    in_specs = [_const_spec(x.shape),
                pl.BlockSpec((SB, KV_WIDTH, W), lambda i: (i, 0, 0)),
                pl.BlockSpec((SB, KV_WIDTH, W), lambda i: (i, 0, 0)),
                pl.BlockSpec((POOL_BUF, SB, POOL_WIDTH), lambda i: (0, i, 0)),
                pl.BlockSpec(memory_space=pltpu.SMEM),
                _const_spec(params.shape)]
    in_specs += [pl.BlockSpec(memory_space=pl.ANY) for _ in big]
    out_shape = (jax.ShapeDtypeStruct(x.shape, F32),
                 jax.ShapeDtypeStruct((N, KV_WIDTH, W), F32),
                 jax.ShapeDtypeStruct((N, KV_WIDTH, W), F32),
                 jax.ShapeDtypeStruct((POOL_BUF, N, POOL_WIDTH), F32))
    out_specs = (pl.BlockSpec(x.shape, lambda i: (0, 0)),
                 pl.BlockSpec((SB, KV_WIDTH, W), lambda i: (i, 0, 0)),
                 pl.BlockSpec((SB, KV_WIDTH, W), lambda i: (i, 0, 0)),
                 pl.BlockSpec((POOL_BUF, SB, POOL_WIDTH), lambda i: (0, i, 0)))
    scratch = [pltpu.VMEM(b.shape, BF16) for b in big]
    scratch += [pltpu.SemaphoreType.DMA((len(big),)),
                pltpu.VMEM((N, POOL_WIDTH), F32),
               pltpu.VMEM((KV_WIDTH, N), F32),
               pltpu.VMEM((KV_WIDTH, N), F32),
               pltpu.VMEM((N * N_HEADS, LANES), F32),
               pltpu.VMEM((N * N_HEADS, LANES), F32),
               pltpu.VMEM((N * N_HEADS, LANES), F32),
               pltpu.VMEM((N * N_HEADS, LANES), F32),
               pltpu.VMEM((N, POOL_WIDTH), BF16),
               pltpu.VMEM((N, D_MODEL), F32),
               pltpu.VMEM((N, D_MODEL), F32),
               pltpu.VMEM((N, D_MODEL), BF16),
               pltpu.VMEM((N, D_FF), BF16)]
    return pl.pallas_call(
        _sample_kernel,
        grid=(N // SB,),
        in_specs=in_specs,
        out_specs=out_specs,
        out_shape=out_shape,
        scratch_shapes=scratch,
        compiler_params=pltpu.CompilerParams(
            dimension_semantics=("arbitrary",),
            vmem_limit_bytes=VMEM_LIMIT_BYTES),
        name="sample_layer",
    )(x, ck, cv, sp, *consts, *big)


def _cache_to_device_order(c):
    n, w = c.shape[0], c.shape[1]
    return jnp.transpose(c, (0, 2, 3, 1)).reshape(n, KV_WIDTH, w)


def _cache_from_device_order(c):
    n, _, w = c.shape
    return jnp.transpose(c.reshape(n, N_KV_HEADS, HEAD_DIM, w), (0, 3, 1, 2))[None]


@jax.jit
def _forward(x_prompt, x_sample, cache_k, cache_v, state_pool, norm1, w_in, q_norm, k_norm, sinks,
             pool_mix_w, pool_scale, w_pool_proj, w_attn_proj, w_out, norm2, w_gate, w_up, w_down):
    depth = w_in.shape[0]
    assert depth == 1, "single-layer trunk"
    l = 0
    N = x_sample.shape[0]
    assert cache_k.shape[2] == WINDOW and x_sample.shape[1] == 1

    col = lax.broadcasted_iota(jnp.int32, (1, D_MODEL), 1)
    row = lax.broadcasted_iota(jnp.int32, (PARAM_ROWS, 1), 0)

    def widened(v, width):
        return jnp.where(col < width, jnp.tile(v, D_MODEL // v.shape[0])[None, :], 0.0)

    rows = {ROW_N1: norm1[l][None, :], ROW_N2: norm2[l][None, :],
            ROW_QN: widened(q_norm[l], Q_WIDTH) * (HEAD_DIM ** -0.5 * LOG2_E),
            ROW_KN: widened(k_norm[l], KV_WIDTH), ROW_PSCALE: widened(pool_scale[l], POOL_WIDTH)}
    params = jnp.zeros((PARAM_ROWS, D_MODEL), F32)
    for r, v in rows.items():
        params = jnp.where(row == r, v, params)
    big = (w_in[l], pool_mix_w[l].reshape(len(POOL_WINDOWS) * POOL_GROUP_WIDTH, POOL_GROUP_WIDTH),
           w_pool_proj[l], w_attn_proj[l], w_out[l], w_gate[l], w_up[l], w_down[l])

    y_p, k_p, v_p, pool_p, *weights_bf16 = _prompt_call(x_prompt, sinks[l], params, big)

    y_s, k_s, v_s, pool_s = _sample_call(
        x_sample.reshape(N * (D_MODEL // LANES), LANES),
        _cache_to_device_order(cache_k[l]), _cache_to_device_order(cache_v[l]),
        jnp.transpose(state_pool[l], (1, 0, 2)), sinks[l], params, tuple(weights_bf16))

    return (y_p, y_s.reshape(N, 1, D_MODEL),
            _cache_from_device_order(k_p), _cache_from_device_order(v_p), jnp.transpose(pool_p, (1, 0, 2))[None],
            _cache_from_device_order(k_s), _cache_from_device_order(v_s),
            jnp.transpose(pool_s, (1, 0, 2))[None])


def kernel(x_prompt, x_sample, cache_k, cache_v, state_pool, norm1, w_in, q_norm, k_norm, sinks, pool_mix_w,
           pool_scale, w_pool_proj, w_attn_proj, w_out, norm2, w_gate, w_up, w_down):
    return _forward(x_prompt, x_sample, cache_k, cache_v, state_pool, norm1, w_in, q_norm, k_norm, sinks,
                    pool_mix_w, pool_scale, w_pool_proj, w_attn_proj, w_out, norm2, w_gate, w_up, w_down)
```

```python
import numpy as np
import jax
import jax.numpy as jnp
from jax import lax
from jax.experimental import pallas as pl
from jax.experimental.pallas import tpu as pltpu

D_MODEL = 1024
POOL_WINDOWS = (2, 4, 8, 16)
POOL_GROUP_WIDTH = 128
POOL_WIDTH = 512
POOL_BUF = 15
N_HEADS = 8
N_KV_HEADS = 2
HEAD_DIM = 64
GROUP = N_HEADS // N_KV_HEADS
Q_WIDTH = 512
KV_WIDTH = 128
WINDOW = 128
D_FF = 2816
EPS = 1e-6
NEG = -1e30
LOG2_E = 1.4426950408889634

LANES = 128
ATT_BLOCK = WINDOW
SEQ_TILE = 512
TILES_PER_STEP = 1
SUB_TILE = 256
FF_CHUNK = 256
GATE_CHUNK = 256
SAMPLE_BLOCK = 32
PREFIX = 16
VMEM_LIMIT_BYTES = 58 * 1024 * 1024
Q_COLS = Q_WIDTH // LANES
HEADS_PER_COL = LANES // HEAD_DIM
N_GATE_CHUNKS = D_MODEL // GATE_CHUNK

OFF_U = 0
OFF_Q = OFF_U + POOL_WIDTH
OFF_K = OFF_Q + Q_WIDTH
OFF_V = OFF_K + KV_WIDTH
OFF_GA = OFF_V + KV_WIDTH
OFF_GB = OFF_GA + D_MODEL

BF16 = jnp.bfloat16
F32 = jnp.float32

ROW_N1, ROW_N2, ROW_QN, ROW_KN, ROW_PSCALE, PARAM_ROWS = 0, 1, 2, 3, 4, 8


def _param(params_ref, row, width):
    return params_ref[row:row + 1, 0:width]


def _dot(a, b):
    return jnp.dot(a, b, preferred_element_type=F32)


def _dot_nt(a, b):
    return lax.dot_general(a, b, (((1,), (1,)), ((), ())), preferred_element_type=F32)


def _rms_scale(x):
    return lax.rsqrt(jnp.mean(x * x, axis=-1, keepdims=True) + EPS)


def _head_norm(x, gain):
    sq = x * x
    low = lax.broadcasted_iota(jnp.int32, (x.shape[0], LANES), 1) < HEAD_DIM
    cols = []
    for c in range(0, x.shape[1], LANES):
        blk = sq[:, c:c + LANES]
        ss_lo = jnp.sum(jnp.where(low, blk, 0.0), axis=-1, keepdims=True)
        ss_hi = jnp.sum(jnp.where(low, 0.0, blk), axis=-1, keepdims=True)
        cols.append(jnp.where(low, ss_lo, ss_hi))
    ss = jnp.concatenate(cols, axis=1) if len(cols) > 1 else cols[0]
    return (x * lax.rsqrt(ss * (1.0 / HEAD_DIM) + EPS)) * gain


def _kv_head(head):
    return head // GROUP


def _ffn_act(h2, wg_ref, wu_ref, act_ref, rows):
    for start in range(0, D_FF, FF_CHUNK):
        width = min(FF_CHUNK, D_FF - start)
        g = _dot(h2, wg_ref[:, start:start + width])
        u = _dot(h2, wu_ref[:, start:start + width])
        act_ref[rows, start:start + width] = (g * jax.nn.sigmoid(g) * u).astype(BF16)


def _gate_chunk(h, c, w_in_ref, sga_ref, sgb_ref, rows):
    lo = c * GATE_CHUNK
    sga_ref[rows, lo:lo + GATE_CHUNK] = jax.nn.sigmoid(_dot(h, w_in_ref[:, OFF_GA + lo:OFF_GA + lo + GATE_CHUNK]))
    sgb_ref[rows, lo:lo + GATE_CHUNK] = jax.nn.sigmoid(_dot(h, w_in_ref[:, OFF_GB + lo:OFF_GB + lo + GATE_CHUNK]))


def _merge(ypool, yattn, sga_ref, sgb_ref, wpp_ref, wap_ref, merged_ref, rows):
    for c in range(N_GATE_CHUNKS):
        lo = c * GATE_CHUNK
        pp = _dot(ypool, wpp_ref[:, lo:lo + GATE_CHUNK])
        ap = _dot(yattn, wap_ref[:, lo:lo + GATE_CHUNK])
        merged = sga_ref[rows, lo:lo + GATE_CHUNK] * pp + sgb_ref[rows, lo:lo + GATE_CHUNK] * ap
        merged_ref[rows, lo:lo + GATE_CHUNK] = merged.astype(BF16)


BIG_WEIGHTS = (("w_in", D_MODEL, OFF_GB + D_MODEL), ("mix", len(POOL_WINDOWS) * POOL_GROUP_WIDTH, POOL_GROUP_WIDTH),
               ("wpp", POOL_WIDTH, D_MODEL), ("wap", Q_WIDTH, D_MODEL), ("wout", D_MODEL, D_MODEL),
               ("wg", D_MODEL, D_FF), ("wu", D_MODEL, D_FF), ("wd", D_FF, D_MODEL))
W_MIX, W_PP = 1, 2
W_FOLD = len(BIG_WEIGHTS)
HANDED_ON = tuple(w for w in range(len(BIG_WEIGHTS)) if w not in (W_MIX, W_PP)) + (W_FOLD,)
BF16_SHAPES = tuple((rows, cols) for _, rows, cols in BIG_WEIGHTS) + ((POOL_WIDTH, D_MODEL),)
CONVERT_ROWS = 128
STAGE_SLOTS = 4
STAGE_COLS = max(cols for _, _, cols in BIG_WEIGHTS)


def _weight_chunks():
    chunks = []
    for w, (_, nrows, cols) in enumerate(BIG_WEIGHTS):
        side_by_side = max(1, STAGE_COLS // cols)
        blocks = [(r0, min(CONVERT_ROWS, nrows - r0)) for r0 in range(0, nrows, CONVERT_ROWS)]
        for first in range(0, len(blocks), side_by_side):
            group = blocks[first:first + side_by_side]
            chunks.append((w, tuple((r0, rows, k * cols) for k, (r0, rows) in enumerate(group))))
    return tuple(chunks)


WEIGHT_CHUNKS = _weight_chunks()


def _writeback_copy(w, w_v, w_bf_hbm, out_sem):
    return pltpu.make_async_copy(w_v[w], w_bf_hbm[w], out_sem.at[w])


def _convert_weights(w_hbm, w_v, stage, stage_sem, mixf_s, params_ref):
    def chunk_copies(n):
        w, pieces = WEIGHT_CHUNKS[n]
        cols = BIG_WEIGHTS[w][2]
        slot = n % STAGE_SLOTS
        return [pltpu.make_async_copy(w_hbm[w].at[pl.ds(r0, rows), :],
                                      stage.at[slot, pl.ds(0, rows), pl.ds(c0, cols)], stage_sem.at[slot])
                for r0, rows, c0 in pieces]

    def convert(n):
        w, pieces = WEIGHT_CHUNKS[n]
        cols = BIG_WEIGHTS[w][2]
        for r0, rows, c0 in pieces:
            staged = stage[n % STAGE_SLOTS, 0:rows, c0:c0 + cols]
            if w == W_MIX:
                mixf_s[r0:r0 + rows, :] = staged
            else:
                w_v[w][r0:r0 + rows, :] = staged.astype(BF16)

    ahead = STAGE_SLOTS - 1
    for m in range(ahead):
        for copy in chunk_copies(m):
            copy.start()
    for n in range(len(WEIGHT_CHUNKS)):
        if n + ahead < len(WEIGHT_CHUNKS):
            for copy in chunk_copies(n + ahead):
                copy.start()
        for copy in chunk_copies(n):
            copy.wait()
        convert(n)

    for g in range(len(POOL_WINDOWS)):
        grp = slice(g * POOL_GROUP_WIDTH, (g + 1) * POOL_GROUP_WIDTH)
        scale = params_ref[ROW_PSCALE:ROW_PSCALE + 1, grp]
        w_v[W_FOLD][grp, :] = _dot((mixf_s[grp, :] * scale).astype(BF16), w_v[W_PP][grp, :]).astype(BF16)


def _prompt_kernel(*refs):
    nw = len(BIG_WEIGHTS)
    sinks_ref, x_ref, params_ref, bias_ref = refs[:4]
    w_hbm = refs[4:4 + nw]
    pos = 4 + nw
    outs = refs[pos:pos + 4]
    pos += 4
    w_bf_hbm = dict(zip(HANDED_ON, refs[pos:pos + len(HANDED_ON)]))
    pos += len(HANDED_ON)
    w_v = refs[pos:pos + nw + 1]
    pos += nw + 1
    stage, stage_sem, out_sem, mixf_s = refs[pos:pos + 4]
    scratch = refs[pos + 4:]
    w_in_ref, _, _, wap_ref, wout_ref, wg_ref, wu_ref, wd_ref, wfold_ref = w_v

    step = pl.program_id(0) * pl.num_programs(1) + pl.program_id(1)
    n_steps = pl.num_programs(0) * pl.num_programs(1)

    @pl.when(step == 0)
    def _():
        _convert_weights(w_hbm, w_v, stage, stage_sem, mixf_s, params_ref)

    @pl.when(step == 1)
    def _():
        for w in HANDED_ON:
            _writeback_copy(w, w_v, w_bf_hbm, out_sem).start()

    def body(t, carry):
        _prompt_tile(t, sinks_ref, x_ref, params_ref, bias_ref,
                     w_in_ref, wfold_ref, wap_ref, wout_ref, wg_ref, wu_ref, wd_ref, *outs, *scratch)
        return carry
    lax.fori_loop(0, TILES_PER_STEP, body, 0)

    @pl.when(step == n_steps - 1)
    def _():
        for w in HANDED_ON:
            _writeback_copy(w, w_v, w_bf_hbm, out_sem).wait()


def _prompt_tile(t, sinks_ref, x_ref, params_ref, bias_ref,
                 w_in_ref, wfold_ref, wap_ref, wout_ref, wg_ref, wu_ref, wd_ref,
                 y_ref, knew_ref, vnew_ref, pnew_ref,
                 h_s, u_s, q_s, kd_s, vd_s, s_s, p_s, ypool_s, yattn_s, sga_s, sgb_s, merged_s, x1_s, h2_s, act_s):
    T = SEQ_TILE
    R = SUB_TILE
    subs = [slice(r0, r0 + R) for r0 in range(0, T, R)]
    n_blocks = T // ATT_BLOCK
    j = pl.program_id(1) * TILES_PER_STEP + t
    rows_per_kv = GROUP * ATT_BLOCK

    def in_block(rows):
        return pl.ds(pl.multiple_of(t * T + rows.start, R), R)

    @pl.when(j == 0)
    def _():
        u_s[0:PREFIX, :] = jnp.zeros((PREFIX, POOL_WIDTH), F32)
        kd_s[:, 0:ATT_BLOCK, :] = jnp.zeros((N_KV_HEADS, ATT_BLOCK, KV_WIDTH), BF16)
        vd_s[:, 0:ATT_BLOCK, :] = jnp.zeros((N_KV_HEADS, ATT_BLOCK, KV_WIDTH), BF16)

    lane_r = lax.broadcasted_iota(jnp.int32, (R, LANES), 1)
    lane = lax.broadcasted_iota(jnp.int32, (ATT_BLOCK, LANES), 1)

    def norm1(rows):
        x = x_ref[0, in_block(rows), :]
        h_s[rows, :] = ((x * _rms_scale(x)) * _param(params_ref, ROW_N1, D_MODEL)).astype(BF16)

    def in_proj(rows):
        h = h_s[rows, :]
        u_s[PREFIX + rows.start:PREFIX + rows.stop, :] = _dot(h, w_in_ref[:, OFF_U:OFF_U + POOL_WIDTH])
        q = _dot(h, w_in_ref[:, OFF_Q:OFF_Q + Q_WIDTH])
        qn = _head_norm(q, _param(params_ref, ROW_QN, Q_WIDTH)).astype(BF16)
        for p in range(Q_COLS):
            qcol = qn[:, p * LANES:(p + 1) * LANES]
            lo, hi = 2 * p * LANES, (2 * p + 1) * LANES
            q_s[rows, lo:lo + LANES] = jnp.where(lane_r < HEAD_DIM, qcol, jnp.zeros_like(qcol))
            q_s[rows, hi:hi + LANES] = jnp.where(lane_r >= HEAD_DIM, qcol, jnp.zeros_like(qcol))
        kv = _dot(h, w_in_ref[:, OFF_K:OFF_K + 2 * KV_WIDTH])
        kn = _head_norm(kv[:, 0:KV_WIDTH], _param(params_ref, ROW_KN, KV_WIDTH))
        v = kv[:, KV_WIDTH:]
        dst_rows = slice(ATT_BLOCK + rows.start, ATT_BLOCK + rows.stop)
        for src, dst in ((kn, kd_s), (v, vd_s)):
            swapped = pltpu.roll(src, HEAD_DIM, 1)
            dst[0, dst_rows, :] = jnp.where(lane_r < HEAD_DIM, src, swapped).astype(BF16)
            dst[1, dst_rows, :] = jnp.where(lane_r < HEAD_DIM, swapped, src).astype(BF16)
        if rows.stop == T:
            knew_ref[0] = kn[R - ATT_BLOCK:, :].T
            vnew_ref[0] = v[R - ATT_BLOCK:, :].T
            batch = pl.program_id(0)
            for r in range(POOL_BUF):
                src = PREFIX + T - POOL_BUF + r
                pnew_ref[r, pl.ds(batch, 1), :] = u_s[src:src + 1, :]

    def pool(rows):
        pos1 = j * T + rows.start + lax.broadcasted_iota(jnp.int32, (R, 1), 0) + 1
        for g, w in enumerate(POOL_WINDOWS):
            cols = slice(g * POOL_GROUP_WIDTH, (g + 1) * POOL_GROUP_WIDTH)
            a = u_s[rows.start:rows.stop + PREFIX, cols]
            s = a
            shift = 1
            while shift < w:
                s = s + pltpu.roll(s, shift, 0)
                shift *= 2
            inv_cnt = 1.0 / jnp.minimum(pos1, w).astype(F32)
            ypool_s[rows, cols] = (s[PREFIX:, :] * inv_cnt - a[PREFIX:, :]).astype(BF16)

    def gates(rows, chunks):
        h = h_s[rows, :]
        for c in chunks:
            _gate_chunk(h, c, w_in_ref, sga_s, sgb_s, rows)

    def scores(b):
        r0 = b * ATT_BLOCK
        for c in range(N_KV_HEADS):
            parts = [q_s[r0:r0 + ATT_BLOCK, head * LANES:(head + 1) * LANES]
                     for head in range(c * GROUP, (c + 1) * GROUP)]
            s_s[b % 2, c * rows_per_kv:(c + 1) * rows_per_kv, :] = _dot_nt(
                jnp.concatenate(parts, axis=0), kd_s[c, r0:r0 + 2 * ATT_BLOCK, :])

    def softmax(b):
        bias = bias_ref[jnp.where(j == 0, 1, 0)] if b == 0 else bias_ref[0]
        inv_denoms = []
        for head in range(N_HEADS):
            sh = s_s[b % 2, head * ATT_BLOCK:(head + 1) * ATT_BLOCK, :] + bias
            sink = sinks_ref[head] * LOG2_E
            m = jnp.maximum(jnp.max(sh, axis=-1, keepdims=True), sink)
            p = jnp.exp2(sh - m)
            denom = jnp.sum(p, axis=-1, keepdims=True) + jnp.exp2(sink - m)
            p_s[b % 2, head * ATT_BLOCK:(head + 1) * ATT_BLOCK, :] = p.astype(BF16)
            inv_denoms.append(1.0 / denom)
        return inv_denoms

    def weighted_values(b, inv_denoms):
        r0 = b * ATT_BLOCK
        for c in range(N_KV_HEADS):
            o = _dot(p_s[b % 2, c * rows_per_kv:(c + 1) * rows_per_kv, :], vd_s[c, r0:r0 + 2 * ATT_BLOCK, :])
            for pp in range(Q_COLS // N_KV_HEADS):
                p_col = c * Q_COLS // N_KV_HEADS + pp
                head_lo = p_col * HEADS_PER_COL
                o_lo = o[(2 * pp) * ATT_BLOCK:(2 * pp + 1) * ATT_BLOCK, :] * inv_denoms[head_lo]
                o_hi = o[(2 * pp + 1) * ATT_BLOCK:(2 * pp + 2) * ATT_BLOCK, :] * inv_denoms[head_lo + 1]
                yattn_s[r0:r0 + ATT_BLOCK, p_col * LANES:(p_col + 1) * LANES] = (
                    jnp.where(lane < HEAD_DIM, o_lo, o_hi).astype(BF16))

    def out_proj(rows):
        _merge(ypool_s[rows, :], yattn_s[rows, :], sga_s, sgb_s, wfold_ref, wap_ref, merged_s, rows)
        x1 = x_ref[0, in_block(rows), :] + _dot(merged_s[rows, :], wout_ref[...])
        x1_s[rows, :] = x1
        h2_s[rows, :] = ((x1 * _rms_scale(x1)) * _param(params_ref, ROW_N2, D_MODEL)).astype(BF16)

    for rows in subs:
        norm1(rows)
    for rows in subs:
        in_proj(rows)
    pool(subs[0])
    gates(subs[0], range(N_GATE_CHUNKS))
    for rows in subs[1:]:
        pool(rows)
    later_gates = [(rows, c) for rows in subs[1:] for c in range(N_GATE_CHUNKS)]
    per_block = -(-len(later_gates) // n_blocks)
    scores(0)
    for b in range(n_blocks):
        if b + 1 < n_blocks:
            scores(b + 1)
        for rows, c in later_gates[b * per_block:(b + 1) * per_block]:
            gates(rows, [c])
        weighted_values(b, softmax(b))

    u_s[0:PREFIX, :] = u_s[T:T + PREFIX, :]
    kd_s[:, 0:ATT_BLOCK, :] = kd_s[:, T:T + ATT_BLOCK, :]
    vd_s[:, 0:ATT_BLOCK, :] = vd_s[:, T:T + ATT_BLOCK, :]

    for rows in subs:
        out_proj(rows)
    for rows in subs:
        _ffn_act(h2_s[rows, :], wg_ref, wu_ref, act_s, rows)
    for rows in subs:
        y_ref[0, in_block(rows), :] = x1_s[rows, :] + _dot(act_s[rows, :], wd_ref[...])


def _const_spec(shape):
    nd = len(shape)
    return pl.BlockSpec(shape, lambda *_: (0,) * nd, pipeline_mode=pl.Buffered(1))


def _band_bias():
    row = np.arange(ATT_BLOCK)[:, None]
    col = np.arange(2 * ATT_BLOCK)[None, :]
    band = (col >= row) & (col <= row + WINDOW)
    masks = np.stack([band, band & (col >= ATT_BLOCK)])
    return jnp.asarray(np.where(masks, 0.0, NEG), dtype=F32)


def _prompt_call(x, sinks, params, big):
    B, S, _ = x.shape
    T = SEQ_TILE
    step_rows = T * TILES_PER_STEP
    assert tuple(b.shape for b in big) == BF16_SHAPES[:len(BIG_WEIGHTS)]
    assert B * (S // step_rows) >= 2, "the bf16 write-back starts on the second grid step"
    bias = _band_bias()
    in_specs = [pl.BlockSpec(memory_space=pltpu.SMEM),
                pl.BlockSpec((1, step_rows, D_MODEL), lambda b, j: (b, j, 0)),
                _const_spec(params.shape),
                _const_spec(bias.shape)]
    in_specs += [pl.BlockSpec(memory_space=pl.ANY) for _ in big]
    out_shape = (jax.ShapeDtypeStruct((B, S, D_MODEL), F32),
                 jax.ShapeDtypeStruct((B, KV_WIDTH, ATT_BLOCK), F32),
                 jax.ShapeDtypeStruct((B, KV_WIDTH, ATT_BLOCK), F32),
                 jax.ShapeDtypeStruct((POOL_BUF, B, POOL_WIDTH), F32))
    out_shape += tuple(jax.ShapeDtypeStruct(BF16_SHAPES[w], BF16) for w in HANDED_ON)
    out_specs = (pl.BlockSpec((1, step_rows, D_MODEL), lambda b, j: (b, j, 0)),
                 pl.BlockSpec((1, KV_WIDTH, ATT_BLOCK), lambda b, j: (b, 0, 0)),
                 pl.BlockSpec((1, KV_WIDTH, ATT_BLOCK), lambda b, j: (b, 0, 0)),
                 pl.BlockSpec((POOL_BUF, B, POOL_WIDTH), lambda b, j: (0, 0, 0)))
    out_specs += tuple(pl.BlockSpec(memory_space=pl.ANY) for _ in HANDED_ON)
    scratch = [pltpu.VMEM(shape, BF16) for shape in BF16_SHAPES]
    scratch += [pltpu.VMEM((STAGE_SLOTS, CONVERT_ROWS, STAGE_COLS), F32),
                pltpu.SemaphoreType.DMA((STAGE_SLOTS,)),
                pltpu.SemaphoreType.DMA((len(BF16_SHAPES),)),
                pltpu.VMEM(BF16_SHAPES[W_MIX], F32)]
    scratch += [pltpu.VMEM((T, D_MODEL), BF16),
               pltpu.VMEM((PREFIX + T, POOL_WIDTH), F32),
               pltpu.VMEM((T, N_HEADS * LANES), BF16),
               pltpu.VMEM((N_KV_HEADS, ATT_BLOCK + T, KV_WIDTH), BF16),
               pltpu.VMEM((N_KV_HEADS, ATT_BLOCK + T, KV_WIDTH), BF16),
               pltpu.VMEM((2, N_HEADS * ATT_BLOCK, 2 * ATT_BLOCK), F32),
               pltpu.VMEM((2, N_HEADS * ATT_BLOCK, 2 * ATT_BLOCK), BF16),
               pltpu.VMEM((T, POOL_WIDTH), BF16),
               pltpu.VMEM((T, Q_WIDTH), BF16),
               pltpu.VMEM((T, D_MODEL), F32),
               pltpu.VMEM((T, D_MODEL), F32),
               pltpu.VMEM((T, D_MODEL), BF16),
               pltpu.VMEM((T, D_MODEL), F32),
               pltpu.VMEM((T, D_MODEL), BF16),
               pltpu.VMEM((T, D_FF), BF16)]
    return pl.pallas_call(
        _prompt_kernel,
        grid=(B, S // step_rows),
        in_specs=in_specs,
        out_specs=out_specs,
        out_shape=out_shape,
        scratch_shapes=scratch,
        compiler_params=pltpu.CompilerParams(
            dimension_semantics=("arbitrary", "arbitrary"),
            vmem_limit_bytes=VMEM_LIMIT_BYTES),
        name="prompt_layer",
    )(sinks, x, params, bias, *big)


def _sample_kernel(*refs):
    nh = len(HANDED_ON)
    n_in = 6
    (x_ref, ck_ref, cv_ref, sp_ref, sinks_ref, params_ref) = refs[:n_in]
    w_hbm = refs[n_in:n_in + nh]
    pos = n_in + nh
    y_ref, knew_ref, vnew_ref, pnew_ref = refs[pos:pos + 4]
    pos += 4
    w_v = refs[pos:pos + nh]
    pos += nh
    (w_sem, u_s, kt_s, vt_s, q8_s, k8_s, v8_s, o8_s, ypre_s, sga_s, sgb_s, merged_s, act_s) = refs[pos:]
    w_in_ref, wap_ref, wout_ref, wg_ref, wu_ref, wd_ref, wfold_ref = w_v

    SB = SAMPLE_BLOCK
    col_tiles = D_MODEL // LANES
    N = x_ref.shape[0] // col_tiles
    nsteps = N // SB
    i = pl.program_id(0)
    lane = lax.broadcasted_iota(jnp.int32, (N, LANES), 1)
    all_rows = slice(0, N)

    def load_x():
        return jnp.concatenate([x_ref[pl.ds(ct, N, stride=col_tiles), :] for ct in range(col_tiles)], axis=1)

    def weight_copy(k):
        return pltpu.make_async_copy(w_hbm[k], w_v[k], w_sem.at[k])

    @pl.when(i == 0)
    def _():
        for k in range(nh):
            weight_copy(k).start()
        weight_copy(0).wait()
        x = load_x()
        h = ((x * _rms_scale(x)) * _param(params_ref, ROW_N1, D_MODEL)).astype(BF16)
        u_s[...] = _dot(h, w_in_ref[:, OFF_U:OFF_U + POOL_WIDTH])
        for gc in range(N_GATE_CHUNKS):
            _gate_chunk(h, gc, w_in_ref, sga_s, sgb_s, all_rows)
        q = _head_norm(_dot(h, w_in_ref[:, OFF_Q:OFF_Q + Q_WIDTH]), _param(params_ref, ROW_QN, Q_WIDTH))
        kv = _dot(h, w_in_ref[:, OFF_K:OFF_K + 2 * KV_WIDTH])
        kn = _head_norm(kv[:, 0:KV_WIDTH], _param(params_ref, ROW_KN, KV_WIDTH))
        v = kv[:, KV_WIDTH:]
        kt_s[...] = kn.T
        vt_s[...] = v.T
        for head in range(N_HEADS):
            k8_s[pl.ds(head, N, stride=N_HEADS), :] = kn
            v8_s[pl.ds(head, N, stride=N_HEADS), :] = v
        for p in range(Q_COLS):
            qcol = q[:, p * LANES:(p + 1) * LANES]
            qswap = pltpu.roll(qcol, HEAD_DIM, 1)
            for e in range(HEADS_PER_COL):
                head = p * HEADS_PER_COL + e
                c = _kv_head(head)
                src = qcol if e == c else qswap
                keep = (lane < HEAD_DIM) if c == 0 else (lane >= HEAD_DIM)
                q8_s[pl.ds(head, N, stride=N_HEADS), :] = jnp.where(keep, src, 0.0)

    base = i * SB
    rows = pl.ds(pl.multiple_of(base, SB), SB)

    unew = u_s[rows, :]
    for g, w in enumerate(POOL_WINDOWS):
        cols = slice(g * POOL_GROUP_WIDTH, (g + 1) * POOL_GROUP_WIDTH)
        s = unew[:, cols]
        for r in range(POOL_BUF - (w - 1), POOL_BUF):
            s = s + sp_ref[r, :, cols]
        ypre_s[rows, cols] = (s * (1.0 / w) - unew[:, cols]).astype(BF16)
    pnew_ref[0:POOL_BUF - 1] = sp_ref[1:POOL_BUF]
    pnew_ref[POOL_BUF - 1] = unew

    hrows = pl.ds(pl.multiple_of(base * N_HEADS, SB * N_HEADS), SB * N_HEADS)
    lhs = q8_s[hrows, :]
    sc = jnp.concatenate(
        [_dot(lhs[bb * N_HEADS:(bb + 1) * N_HEADS, :].astype(BF16), ck_ref[bb].astype(BF16)) for bb in range(SB)],
        axis=0)
    sc_self = jnp.sum(lhs * k8_s[hrows, :], axis=-1, keepdims=True)
    row_head = lax.rem(lax.broadcasted_iota(jnp.int32, (SB * N_HEADS, 1), 0), N_HEADS)
    sink = jnp.zeros((SB * N_HEADS, 1), F32)
    for head in range(N_HEADS):
        sink = jnp.where(row_head == head, sinks_ref[head] * LOG2_E, sink)
    m = jnp.maximum(jnp.maximum(jnp.max(sc, axis=-1, keepdims=True), sc_self), sink)
    p = jnp.exp2(sc - m)
    p_self = jnp.exp2(sc_self - m)
    denom = jnp.sum(p, axis=-1, keepdims=True) + p_self + jnp.exp2(sink - m)
    inv = 1.0 / denom
    pn = p * inv
    o = jnp.concatenate(
        [_dot_nt(pn[bb * N_HEADS:(bb + 1) * N_HEADS, :].astype(BF16), cv_ref[bb].astype(BF16)) for bb in range(SB)],
        axis=0)
    o8_s[hrows, :] = o + (p_self * inv) * v8_s[hrows, :]

    lane_c = lax.broadcasted_iota(jnp.int32, (KV_WIDTH, WINDOW), 1)
    kt_all = kt_s[...]
    vt_all = vt_s[...]
    for bb in range(SB):
        put = (WINDOW - 1) - (base + bb)
        knew_ref[bb] = jnp.where(lane_c == WINDOW - 1, pltpu.roll(kt_all, put, 1),
                                 pltpu.roll(ck_ref[bb], WINDOW - 1, 1))
        vnew_ref[bb] = jnp.where(lane_c == WINDOW - 1, pltpu.roll(vt_all, put, 1),
                                 pltpu.roll(cv_ref[bb], WINDOW - 1, 1))

    @pl.when(i == nsteps - 1)
    def _():
        for k in range(1, nh):
            weight_copy(k).wait()
        ycols = []
        for p in range(Q_COLS):
            halves = []
            for e in range(HEADS_PER_COL):
                head = p * HEADS_PER_COL + e
                oh = o8_s[pl.ds(head, N, stride=N_HEADS), :]
                halves.append(oh if e == _kv_head(head) else pltpu.roll(oh, HEAD_DIM, 1))
            ycols.append(jnp.where(lane < HEAD_DIM, halves[0], halves[1]).astype(BF16))
        yattn = jnp.concatenate(ycols, axis=1)
        _merge(ypre_s[...], yattn, sga_s, sgb_s, wfold_ref, wap_ref, merged_s, all_rows)
        x1 = load_x() + _dot(merged_s[...], wout_ref[...])
        h2 = ((x1 * _rms_scale(x1)) * _param(params_ref, ROW_N2, D_MODEL)).astype(BF16)
        _ffn_act(h2, wg_ref, wu_ref, act_s, all_rows)
        y = x1 + _dot(act_s[...], wd_ref[...])
        for ct in range(col_tiles):
            y_ref[pl.ds(ct, N, stride=col_tiles), :] = y[:, ct * LANES:(ct + 1) * LANES]


def _sample_call(x, ck, cv, sp, sinks, params, big):
    N = x.shape[0] // (D_MODEL // LANES)
    SB = SAMPLE_BLOCK
    W = WINDOW
    assert tuple(b.shape for b in big) == tuple(BF16_SHAPES[w] for w in HANDED_ON)
    consts = (sinks, params)
    in_specs = [_const_spec(x.shape),
                pl.BlockSpec((SB, KV_WIDTH, W), lambda i: (i, 0, 0)),
                pl.BlockSpec((SB, KV_WIDTH, W), lambda i: (i, 0, 0)),
                pl.BlockSpec((POOL_BUF, SB, POOL_WIDTH), lambda i: (0, i, 0)),
                pl.BlockSpec(memory_space=pltpu.SMEM),
                _const_spec(params.shape)]
    in_specs += [pl.BlockSpec(memory_space=pl.ANY) for _ in big]
    out_shape = (jax.ShapeDtypeStruct(x.shape, F32),
                 jax.ShapeDtypeStruct((N, KV_WIDTH, W), F32),
                 jax.ShapeDtypeStruct((N, KV_WIDTH, W), F32),
                 jax.ShapeDtypeStruct((POOL_BUF, N, POOL_WIDTH), F32))
    out_specs = (pl.BlockSpec(x.shape, lambda i: (0, 0)),
                 pl.BlockSpec((SB, KV_WIDTH, W), lambda i: (i, 0, 0)),
                 pl.BlockSpec((SB, KV_WIDTH, W), lambda i: (i, 0, 0)),
                 pl.BlockSpec((POOL_BUF, SB, POOL_WIDTH), lambda i: (0, i, 0)))
    scratch = [pltpu.VMEM(b.shape, BF16) for b in big]
    scratch += [pltpu.SemaphoreType.DMA((len(big),)),
                pltpu.VMEM((N, POOL_WIDTH), F32),
               pltpu.VMEM((KV_WIDTH, N), F32),
               pltpu.VMEM((KV_WIDTH, N), F32),
               pltpu.VMEM((N * N_HEADS, LANES), F32),
               pltpu.VMEM((N * N_HEADS, LANES), F32),
               pltpu.VMEM((N * N_HEADS, LANES), F32),
               pltpu.VMEM((N * N_HEADS, LANES), F32),
               pltpu.VMEM((N, POOL_WIDTH), BF16),
               pltpu.VMEM((N, D_MODEL), F32),
               pltpu.VMEM((N, D_MODEL), F32),
               pltpu.VMEM((N, D_MODEL), BF16),
               pltpu.VMEM((N, D_FF), BF16)]
    return pl.pallas_call(
        _sample_kernel,
        grid=(N // SB,),
        in_specs=in_specs,
        out_specs=out_specs,
        out_shape=out_shape,
        scratch_shapes=scratch,
        compiler_params=pltpu.CompilerParams(
            dimension_semantics=("arbitrary",),
            vmem_limit_bytes=VMEM_LIMIT_BYTES),
        name="sample_layer",
    )(x, ck, cv, sp, *consts, *big)


def _cache_to_device_order(c):
    n, w = c.shape[0], c.shape[1]
    return jnp.transpose(c, (0, 2, 3, 1)).reshape(n, KV_WIDTH, w)


def _cache_from_device_order(c):
    n, _, w = c.shape
    return jnp.transpose(c.reshape(n, N_KV_HEADS, HEAD_DIM, w), (0, 3, 1, 2))[None]


@jax.jit
def _forward(x_prompt, x_sample, cache_k, cache_v, state_pool, norm1, w_in, q_norm, k_norm, sinks,
             pool_mix_w, pool_scale, w_pool_proj, w_attn_proj, w_out, norm2, w_gate, w_up, w_down):
    depth = w_in.shape[0]
    assert depth == 1, "single-layer trunk"
    l = 0
    N = x_sample.shape[0]
    assert cache_k.shape[2] == WINDOW and x_sample.shape[1] == 1

    col = lax.broadcasted_iota(jnp.int32, (1, D_MODEL), 1)
    row = lax.broadcasted_iota(jnp.int32, (PARAM_ROWS, 1), 0)

    def widened(v, width):
        return jnp.where(col < width, jnp.tile(v, D_MODEL // v.shape[0])[None, :], 0.0)

    rows = {ROW_N1: norm1[l][None, :], ROW_N2: norm2[l][None, :],
            ROW_QN: widened(q_norm[l], Q_WIDTH) * (HEAD_DIM ** -0.5 * LOG2_E),
            ROW_KN: widened(k_norm[l], KV_WIDTH), ROW_PSCALE: widened(pool_scale[l], POOL_WIDTH)}
    params = jnp.zeros((PARAM_ROWS, D_MODEL), F32)
    for r, v in rows.items():
        params = jnp.where(row == r, v, params)
    big = (w_in[l], pool_mix_w[l].reshape(len(POOL_WINDOWS) * POOL_GROUP_WIDTH, POOL_GROUP_WIDTH),
           w_pool_proj[l], w_attn_proj[l], w_out[l], w_gate[l], w_up[l], w_down[l])

    y_p, k_p, v_p, pool_p, *weights_bf16 = _prompt_call(x_prompt, sinks[l], params, big)

    y_s, k_s, v_s, pool_s = _sample_call(
        x_sample.reshape(N * (D_MODEL // LANES), LANES),
        _cache_to_device_order(cache_k[l]), _cache_to_device_order(cache_v[l]),
        jnp.transpose(state_pool[l], (1, 0, 2)), sinks[l], params, tuple(weights_bf16))

    return (y_p, y_s.reshape(N, 1, D_MODEL),
            _cache_from_device_order(k_p), _cache_from_device_order(v_p), jnp.transpose(pool_p, (1, 0, 2))[None],
            _cache_from_device_order(k_s), _cache_from_device_order(v_s),
            jnp.transpose(pool_s, (1, 0, 2))[None])


def kernel(x_prompt, x_sample, cache_k, cache_v, state_pool, norm1, w_in, q_norm, k_norm, sinks, pool_mix_w,
           pool_scale, w_pool_proj, w_attn_proj, w_out, norm2, w_gate, w_up, w_down):
    return _forward(x_prompt, x_sample, cache_k, cache_v, state_pool, norm1, w_in, q_norm, k_norm, sinks,
                    pool_mix_w, pool_scale, w_pool_proj, w_attn_proj, w_out, norm2, w_gate, w_up, w_down)
```

```python
import numpy as np
import jax
import jax.numpy as jnp
from jax import lax
from jax.experimental import pallas as pl
from jax.experimental.pallas import tpu as pltpu

D_MODEL = 1024
POOL_WINDOWS = (2, 4, 8, 16)
POOL_GROUP_WIDTH = 128
POOL_WIDTH = 512
POOL_BUF = 15
N_HEADS = 8
N_KV_HEADS = 2
HEAD_DIM = 64
GROUP = N_HEADS // N_KV_HEADS
Q_WIDTH = 512
KV_WIDTH = 128
WINDOW = 128
D_FF = 2816
EPS = 1e-6
NEG = -1e30
LOG2_E = 1.4426950408889634

LANES = 128
ATT_BLOCK = WINDOW
SEQ_TILE = 512
TILES_PER_STEP = 1
SUB_TILE = 256
FF_CHUNK = 256
GATE_CHUNK = 256
SAMPLE_BLOCK = 32
PREFIX = 16
VMEM_LIMIT_BYTES = 58 * 1024 * 1024
Q_COLS = Q_WIDTH // LANES
HEADS_PER_COL = LANES // HEAD_DIM
N_GATE_CHUNKS = D_MODEL // GATE_CHUNK

OFF_U = 0
OFF_Q = OFF_U + POOL_WIDTH
OFF_K = OFF_Q + Q_WIDTH
OFF_V = OFF_K + KV_WIDTH
OFF_GA = OFF_V + KV_WIDTH
OFF_GB = OFF_GA + D_MODEL

BF16 = jnp.bfloat16
F32 = jnp.float32

ROW_N1, ROW_N2, ROW_QN, ROW_KN, ROW_PSCALE, PARAM_ROWS = 0, 1, 2, 3, 4, 8


def _param(params_ref, row, width):
    return params_ref[row:row + 1, 0:width]


def _dot(a, b):
    return jnp.dot(a, b, preferred_element_type=F32)


def _dot_nt(a, b):
    return lax.dot_general(a, b, (((1,), (1,)), ((), ())), preferred_element_type=F32)


def _rms_scale(x):
    return lax.rsqrt(jnp.mean(x * x, axis=-1, keepdims=True) + EPS)


def _head_norm(x, gain):
    sq = x * x
    low = lax.broadcasted_iota(jnp.int32, (x.shape[0], LANES), 1) < HEAD_DIM
    cols = []
    for c in range(0, x.shape[1], LANES):
        blk = sq[:, c:c + LANES]
        ss_lo = jnp.sum(jnp.where(low, blk, 0.0), axis=-1, keepdims=True)
        ss_hi = jnp.sum(jnp.where(low, 0.0, blk), axis=-1, keepdims=True)
        cols.append(jnp.where(low, ss_lo, ss_hi))
    ss = jnp.concatenate(cols, axis=1) if len(cols) > 1 else cols[0]
    return (x * lax.rsqrt(ss * (1.0 / HEAD_DIM) + EPS)) * gain


def _kv_head(head):
    return head // GROUP


def _ffn_act(h2, wg_ref, wu_ref, act_ref, rows):
    for start in range(0, D_FF, FF_CHUNK):
        width = min(FF_CHUNK, D_FF - start)
        g = _dot(h2, wg_ref[:, start:start + width])
        u = _dot(h2, wu_ref[:, start:start + width])
        act_ref[rows, start:start + width] = (g * jax.nn.sigmoid(g) * u).astype(BF16)


def _gate_chunk(h, c, w_in_ref, sga_ref, sgb_ref, rows):
    lo = c * GATE_CHUNK
    sga_ref[rows, lo:lo + GATE_CHUNK] = jax.nn.sigmoid(_dot(h, w_in_ref[:, OFF_GA + lo:OFF_GA + lo + GATE_CHUNK]))
    sgb_ref[rows, lo:lo + GATE_CHUNK] = jax.nn.sigmoid(_dot(h, w_in_ref[:, OFF_GB + lo:OFF_GB + lo + GATE_CHUNK]))


def _merge(ypool, yattn, sga_ref, sgb_ref, wpp_ref, wap_ref, merged_ref, rows):
    for c in range(N_GATE_CHUNKS):
        lo = c * GATE_CHUNK
        pp = _dot(ypool, wpp_ref[:, lo:lo + GATE_CHUNK])
        ap = _dot(yattn, wap_ref[:, lo:lo + GATE_CHUNK])
        merged = sga_ref[rows, lo:lo + GATE_CHUNK] * pp + sgb_ref[rows, lo:lo + GATE_CHUNK] * ap
        merged_ref[rows, lo:lo + GATE_CHUNK] = merged.astype(BF16)


BIG_WEIGHTS = (("w_in", D_MODEL, OFF_GB + D_MODEL), ("mix", len(POOL_WINDOWS) * POOL_GROUP_WIDTH, POOL_GROUP_WIDTH),
               ("wpp", POOL_WIDTH, D_MODEL), ("wap", Q_WIDTH, D_MODEL), ("wout", D_MODEL, D_MODEL),
               ("wg", D_MODEL, D_FF), ("wu", D_MODEL, D_FF), ("wd", D_FF, D_MODEL))
W_MIX, W_PP = 1, 2
W_FOLD = len(BIG_WEIGHTS)
HANDED_ON = tuple(w for w in range(len(BIG_WEIGHTS)) if w not in (W_MIX, W_PP)) + (W_FOLD,)
BF16_SHAPES = tuple((rows, cols) for _, rows, cols in BIG_WEIGHTS) + ((POOL_WIDTH, D_MODEL),)
CONVERT_ROWS = 128
STAGE_SLOTS = 4
STAGE_COLS = max(cols for _, _, cols in BIG_WEIGHTS)


def _weight_chunks():
    chunks = []
    for w, (_, nrows, cols) in enumerate(BIG_WEIGHTS):
        side_by_side = max(1, STAGE_COLS // cols)
        blocks = [(r0, min(CONVERT_ROWS, nrows - r0)) for r0 in range(0, nrows, CONVERT_ROWS)]
        for first in range(0, len(blocks), side_by_side):
            group = blocks[first:first + side_by_side]
            chunks.append((w, tuple((r0, rows, k * cols) for k, (r0, rows) in enumerate(group))))
    return tuple(chunks)


WEIGHT_CHUNKS = _weight_chunks()


def _writeback_copy(w, w_v, w_bf_hbm, out_sem):
    return pltpu.make_async_copy(w_v[w], w_bf_hbm[w], out_sem.at[w])


def _convert_weights(w_hbm, w_v, stage, stage_sem, mixf_s, params_ref):
    def chunk_copies(n):
        w, pieces = WEIGHT_CHUNKS[n]
        cols = BIG_WEIGHTS[w][2]
        slot = n % STAGE_SLOTS
        return [pltpu.make_async_copy(w_hbm[w].at[pl.ds(r0, rows), :],
                                      stage.at[slot, pl.ds(0, rows), pl.ds(c0, cols)], stage_sem.at[slot])
                for r0, rows, c0 in pieces]

    def convert(n):
        w, pieces = WEIGHT_CHUNKS[n]
        cols = BIG_WEIGHTS[w][2]
        for r0, rows, c0 in pieces:
            staged = stage[n % STAGE_SLOTS, 0:rows, c0:c0 + cols]
            if w == W_MIX:
                mixf_s[r0:r0 + rows, :] = staged
            else:
                w_v[w][r0:r0 + rows, :] = staged.astype(BF16)

    ahead = STAGE_SLOTS - 1
    for m in range(ahead):
        for copy in chunk_copies(m):
            copy.start()
    for n in range(len(WEIGHT_CHUNKS)):
        if n + ahead < len(WEIGHT_CHUNKS):
            for copy in chunk_copies(n + ahead):
                copy.start()
        for copy in chunk_copies(n):
            copy.wait()
        convert(n)

    for g in range(len(POOL_WINDOWS)):
        grp = slice(g * POOL_GROUP_WIDTH, (g + 1) * POOL_GROUP_WIDTH)
        scale = params_ref[ROW_PSCALE:ROW_PSCALE + 1, grp]
        w_v[W_FOLD][grp, :] = _dot((mixf_s[grp, :] * scale).astype(BF16), w_v[W_PP][grp, :]).astype(BF16)


def _prompt_kernel(*refs):
    nw = len(BIG_WEIGHTS)
    sinks_ref, x_ref, params_ref, bias_ref = refs[:4]
    w_hbm = refs[4:4 + nw]
    pos = 4 + nw
    outs = refs[pos:pos + 4]
    pos += 4
    w_bf_hbm = dict(zip(HANDED_ON, refs[pos:pos + len(HANDED_ON)]))
    pos += len(HANDED_ON)
    w_v = refs[pos:pos + nw + 1]
    pos += nw + 1
    stage, stage_sem, out_sem, mixf_s = refs[pos:pos + 4]
    scratch = refs[pos + 4:]
    w_in_ref, _, _, wap_ref, wout_ref, wg_ref, wu_ref, wd_ref, wfold_ref = w_v

    step = pl.program_id(0) * pl.num_programs(1) + pl.program_id(1)
    n_steps = pl.num_programs(0) * pl.num_programs(1)

    @pl.when(step == 0)
    def _():
        _convert_weights(w_hbm, w_v, stage, stage_sem, mixf_s, params_ref)

    @pl.when(step == 1)
    def _():
        for w in HANDED_ON:
            _writeback_copy(w, w_v, w_bf_hbm, out_sem).start()

    def body(t, carry):
        _prompt_tile(t, sinks_ref, x_ref, params_ref, bias_ref,
                     w_in_ref, wfold_ref, wap_ref, wout_ref, wg_ref, wu_ref, wd_ref, *outs, *scratch)
        return carry
    lax.fori_loop(0, TILES_PER_STEP, body, 0)

    @pl.when(step == n_steps - 1)
    def _():
        for w in HANDED_ON:
            _writeback_copy(w, w_v, w_bf_hbm, out_sem).wait()


def _prompt_tile(t, sinks_ref, x_ref, params_ref, bias_ref,
                 w_in_ref, wfold_ref, wap_ref, wout_ref, wg_ref, wu_ref, wd_ref,
                 y_ref, knew_ref, vnew_ref, pnew_ref,
                 h_s, u_s, q_s, kd_s, vd_s, s_s, p_s, ypool_s, yattn_s, sga_s, sgb_s, merged_s, x1_s, h2_s, act_s):
    T = SEQ_TILE
    R = SUB_TILE
    subs = [slice(r0, r0 + R) for r0 in range(0, T, R)]
    n_blocks = T // ATT_BLOCK
    j = pl.program_id(1) * TILES_PER_STEP + t
    rows_per_kv = GROUP * ATT_BLOCK

    def in_block(rows):
        return pl.ds(pl.multiple_of(t * T + rows.start, R), R)

    @pl.when(j == 0)
    def _():
        u_s[0:PREFIX, :] = jnp.zeros((PREFIX, POOL_WIDTH), F32)
        kd_s[:, 0:ATT_BLOCK, :] = jnp.zeros((N_KV_HEADS, ATT_BLOCK, KV_WIDTH), BF16)
        vd_s[:, 0:ATT_BLOCK, :] = jnp.zeros((N_KV_HEADS, ATT_BLOCK, KV_WIDTH), BF16)

    lane_r = lax.broadcasted_iota(jnp.int32, (R, LANES), 1)
    lane = lax.broadcasted_iota(jnp.int32, (ATT_BLOCK, LANES), 1)

    def norm1(rows):
        x = x_ref[0, in_block(rows), :]
        h_s[rows, :] = ((x * _rms_scale(x)) * _param(params_ref, ROW_N1, D_MODEL)).astype(BF16)

    def in_proj(rows):
        h = h_s[rows, :]
        u_s[PREFIX + rows.start:PREFIX + rows.stop, :] = _dot(h, w_in_ref[:, OFF_U:OFF_U + POOL_WIDTH])
        q = _dot(h, w_in_ref[:, OFF_Q:OFF_Q + Q_WIDTH])
        qn = _head_norm(q, _param(params_ref, ROW_QN, Q_WIDTH)).astype(BF16)
        for p in range(Q_COLS):
            qcol = qn[:, p * LANES:(p + 1) * LANES]
            lo, hi = 2 * p * LANES, (2 * p + 1) * LANES
            q_s[rows, lo:lo + LANES] = jnp.where(lane_r < HEAD_DIM, qcol, jnp.zeros_like(qcol))
            q_s[rows, hi:hi + LANES] = jnp.where(lane_r >= HEAD_DIM, qcol, jnp.zeros_like(qcol))
        kv = _dot(h, w_in_ref[:, OFF_K:OFF_K + 2 * KV_WIDTH])
        kn = _head_norm(kv[:, 0:KV_WIDTH], _param(params_ref, ROW_KN, KV_WIDTH))
        v = kv[:, KV_WIDTH:]
        dst_rows = slice(ATT_BLOCK + rows.start, ATT_BLOCK + rows.stop)
        for src, dst in ((kn, kd_s), (v, vd_s)):
            swapped = pltpu.roll(src, HEAD_DIM, 1)
            dst[0, dst_rows, :] = jnp.where(lane_r < HEAD_DIM, src, swapped).astype(BF16)
            dst[1, dst_rows, :] = jnp.where(lane_r < HEAD_DIM, swapped, src).astype(BF16)
        if rows.stop == T:
            knew_ref[0] = kn[R - ATT_BLOCK:, :].T
            vnew_ref[0] = v[R - ATT_BLOCK:, :].T
            batch = pl.program_id(0)
            for r in range(POOL_BUF):
                src = PREFIX + T - POOL_BUF + r
                pnew_ref[r, pl.ds(batch, 1), :] = u_s[src:src + 1, :]

    def pool(rows):
        pos1 = j * T + rows.start + lax.broadcasted_iota(jnp.int32, (R, 1), 0) + 1
        for g, w in enumerate(POOL_WINDOWS):
            cols = slice(g * POOL_GROUP_WIDTH, (g + 1) * POOL_GROUP_WIDTH)
            a = u_s[rows.start:rows.stop + PREFIX, cols]
            s = a
            shift = 1
            while shift < w:
                s = s + pltpu.roll(s, shift, 0)
                shift *= 2
            inv_cnt = 1.0 / jnp.minimum(pos1, w).astype(F32)
            ypool_s[rows, cols] = (s[PREFIX:, :] * inv_cnt - a[PREFIX:, :]).astype(BF16)

    def gates(rows, chunks):
        h = h_s[rows, :]
        for c in chunks:
            _gate_chunk(h, c, w_in_ref, sga_s, sgb_s, rows)

    def scores(b):
        r0 = b * ATT_BLOCK
        for c in range(N_KV_HEADS):
            parts = [q_s[r0:r0 + ATT_BLOCK, head * LANES:(head + 1) * LANES]
                     for head in range(c * GROUP, (c + 1) * GROUP)]
            s_s[b % 2, c * rows_per_kv:(c + 1) * rows_per_kv, :] = _dot_nt(
                jnp.concatenate(parts, axis=0), kd_s[c, r0:r0 + 2 * ATT_BLOCK, :])

    def softmax(b):
        bias = bias_ref[jnp.where(j == 0, 1, 0)] if b == 0 else bias_ref[0]
        inv_denoms = []
        for head in range(N_HEADS):
            sh = s_s[b % 2, head * ATT_BLOCK:(head + 1) * ATT_BLOCK, :] + bias
            sink = sinks_ref[head] * LOG2_E
            m = jnp.maximum(jnp.max(sh, axis=-1, keepdims=True), sink)
            p = jnp.exp2(sh - m)
            denom = jnp.sum(p, axis=-1, keepdims=True) + jnp.exp2(sink - m)
            p_s[b % 2, head * ATT_BLOCK:(head + 1) * ATT_BLOCK, :] = p.astype(BF16)
            inv_denoms.append(1.0 / denom)
        return inv_denoms

    def weighted_values(b, inv_denoms):
        r0 = b * ATT_BLOCK
        for c in range(N_KV_HEADS):
            o = _dot(p_s[b % 2, c * rows_per_kv:(c + 1) * rows_per_kv, :], vd_s[c, r0:r0 + 2 * ATT_BLOCK, :])
            for pp in range(Q_COLS // N_KV_HEADS):
                p_col = c * Q_COLS // N_KV_HEADS + pp
                head_lo = p_col * HEADS_PER_COL
                o_lo = o[(2 * pp) * ATT_BLOCK:(2 * pp + 1) * ATT_BLOCK, :] * inv_denoms[head_lo]
                o_hi = o[(2 * pp + 1) * ATT_BLOCK:(2 * pp + 2) * ATT_BLOCK, :] * inv_denoms[head_lo + 1]
                yattn_s[r0:r0 + ATT_BLOCK, p_col * LANES:(p_col + 1) * LANES] = (
                    jnp.where(lane < HEAD_DIM, o_lo, o_hi).astype(BF16))

    def out_proj(rows):
        _merge(ypool_s[rows, :], yattn_s[rows, :], sga_s, sgb_s, wfold_ref, wap_ref, merged_s, rows)
        x1 = x_ref[0, in_block(rows), :] + _dot(merged_s[rows, :], wout_ref[...])
        x1_s[rows, :] = x1
        h2_s[rows, :] = ((x1 * _rms_scale(x1)) * _param(params_ref, ROW_N2, D_MODEL)).astype(BF16)

    for rows in subs:
        norm1(rows)
    for rows in subs:
        in_proj(rows)
    pool(subs[0])
    gates(subs[0], range(N_GATE_CHUNKS))
    for rows in subs[1:]:
        pool(rows)
    later_gates = [(rows, c) for rows in subs[1:] for c in range(N_GATE_CHUNKS)]
    per_block = -(-len(later_gates) // n_blocks)
    scores(0)
    for b in range(n_blocks):
        if b + 1 < n_blocks:
            scores(b + 1)
        for rows, c in later_gates[b * per_block:(b + 1) * per_block]:
            gates(rows, [c])
        weighted_values(b, softmax(b))

    u_s[0:PREFIX, :] = u_s[T:T + PREFIX, :]
    kd_s[:, 0:ATT_BLOCK, :] = kd_s[:, T:T + ATT_BLOCK, :]
    vd_s[:, 0:ATT_BLOCK, :] = vd_s[:, T:T + ATT_BLOCK, :]

    for rows in subs:
        out_proj(rows)
    for rows in subs:
        _ffn_act(h2_s[rows, :], wg_ref, wu_ref, act_s, rows)
    for rows in subs:
        y_ref[0, in_block(rows), :] = x1_s[rows, :] + _dot(act_s[rows, :], wd_ref[...])


def _const_spec(shape):
    nd = len(shape)
    return pl.BlockSpec(shape, lambda *_: (0,) * nd, pipeline_mode=pl.Buffered(1))


def _band_bias():
    row = np.arange(ATT_BLOCK)[:, None]
    col = np.arange(2 * ATT_BLOCK)[None, :]
    band = (col >= row) & (col <= row + WINDOW)
    masks = np.stack([band, band & (col >= ATT_BLOCK)])
    return jnp.asarray(np.where(masks, 0.0, NEG), dtype=F32)


def _prompt_call(x, sinks, params, big):
    B, S, _ = x.shape
    T = SEQ_TILE
    step_rows = T * TILES_PER_STEP
    assert tuple(b.shape for b in big) == BF16_SHAPES[:len(BIG_WEIGHTS)]
    assert B * (S // step_rows) >= 2, "the bf16 write-back starts on the second grid step"
    bias = _band_bias()
    in_specs = [pl.BlockSpec(memory_space=pltpu.SMEM),
                pl.BlockSpec((1, step_rows, D_MODEL), lambda b, j: (b, j, 0)),
                _const_spec(params.shape),
                _const_spec(bias.shape)]
    in_specs += [pl.BlockSpec(memory_space=pl.ANY) for _ in big]
    out_shape = (jax.ShapeDtypeStruct((B, S, D_MODEL), F32),
                 jax.ShapeDtypeStruct((B, KV_WIDTH, ATT_BLOCK), F32),
                 jax.ShapeDtypeStruct((B, KV_WIDTH, ATT_BLOCK), F32),
                 jax.ShapeDtypeStruct((POOL_BUF, B, POOL_WIDTH), F32))
    out_shape += tuple(jax.ShapeDtypeStruct(BF16_SHAPES[w], BF16) for w in HANDED_ON)
    out_specs = (pl.BlockSpec((1, step_rows, D_MODEL), lambda b, j: (b, j, 0)),
                 pl.BlockSpec((1, KV_WIDTH, ATT_BLOCK), lambda b, j: (b, 0, 0)),
                 pl.BlockSpec((1, KV_WIDTH, ATT_BLOCK), lambda b, j: (b, 0, 0)),
                 pl.BlockSpec((POOL_BUF, B, POOL_WIDTH), lambda b, j: (0, 0, 0)))
    out_specs += tuple(pl.BlockSpec(memory_space=pl.ANY) for _ in HANDED_ON)
    scratch = [pltpu.VMEM(shape, BF16) for shape in BF16_SHAPES]
    scratch += [pltpu.VMEM((STAGE_SLOTS, CONVERT_ROWS, STAGE_COLS), F32),
                pltpu.SemaphoreType.DMA((STAGE_SLOTS,)),
                pltpu.SemaphoreType.DMA((len(BF16_SHAPES),)),
                pltpu.VMEM(BF16_SHAPES[W_MIX], F32)]
    scratch += [pltpu.VMEM((T, D_MODEL), BF16),
               pltpu.VMEM((PREFIX + T, POOL_WIDTH), F32),
               pltpu.VMEM((T, N_HEADS * LANES), BF16),
               pltpu.VMEM((N_KV_HEADS, ATT_BLOCK + T, KV_WIDTH), BF16),
               pltpu.VMEM((N_KV_HEADS, ATT_BLOCK + T, KV_WIDTH), BF16),
               pltpu.VMEM((2, N_HEADS * ATT_BLOCK, 2 * ATT_BLOCK), F32),
               pltpu.VMEM((2, N_HEADS * ATT_BLOCK, 2 * ATT_BLOCK), BF16),
               pltpu.VMEM((T, POOL_WIDTH), BF16),
               pltpu.VMEM((T, Q_WIDTH), BF16),
               pltpu.VMEM((T, D_MODEL), F32),
               pltpu.VMEM((T, D_MODEL), F32),
               pltpu.VMEM((T, D_MODEL), BF16),
               pltpu.VMEM((T, D_MODEL), F32),
               pltpu.VMEM((T, D_MODEL), BF16),
               pltpu.VMEM((T, D_FF), BF16)]
    return pl.pallas_call(
        _prompt_kernel,
        grid=(B, S // step_rows),
        in_specs=in_specs,
        out_specs=out_specs,
        out_shape=out_shape,
        scratch_shapes=scratch,
        compiler_params=pltpu.CompilerParams(
            dimension_semantics=("arbitrary", "arbitrary"),
            vmem_limit_bytes=VMEM_LIMIT_BYTES),
        name="prompt_layer",
    )(sinks, x, params, bias, *big)


def _sample_kernel(*refs):
    nh = len(HANDED_ON)
    n_in = 6
    (x_ref, ck_ref, cv_ref, sp_ref, sinks_ref, params_ref) = refs[:n_in]
    w_hbm = refs[n_in:n_in + nh]
    pos = n_in + nh
    y_ref, knew_ref, vnew_ref, pnew_ref = refs[pos:pos + 4]
    pos += 4
    w_v = refs[pos:pos + nh]
    pos += nh
    (w_sem, u_s, kt_s, vt_s, q8_s, k8_s, v8_s, o8_s, ypre_s, sga_s, sgb_s, merged_s, act_s) = refs[pos:]
    w_in_ref, wap_ref, wout_ref, wg_ref, wu_ref, wd_ref, wfold_ref = w_v

    SB = SAMPLE_BLOCK
    col_tiles = D_MODEL // LANES
    N = x_ref.shape[0] // col_tiles
    nsteps = N // SB
    i = pl.program_id(0)
    lane = lax.broadcasted_iota(jnp.int32, (N, LANES), 1)
    all_rows = slice(0, N)

    def load_x():
        return jnp.concatenate([x_ref[pl.ds(ct, N, stride=col_tiles), :] for ct in range(col_tiles)], axis=1)

    def weight_copy(k):
        return pltpu.make_async_copy(w_hbm[k], w_v[k], w_sem.at[k])

    @pl.when(i == 0)
    def _():
        weight_copy(0).start()
        weight_copy(0).wait()
        for k in range(1, nh):
            weight_copy(k).start()
        x = load_x()
        h = ((x * _rms_scale(x)) * _param(params_ref, ROW_N1, D_MODEL)).astype(BF16)
        u_s[...] = _dot(h, w_in_ref[:, OFF_U:OFF_U + POOL_WIDTH])
        for gc in range(N_GATE_CHUNKS):
            _gate_chunk(h, gc, w_in_ref, sga_s, sgb_s, all_rows)
        q = _head_norm(_dot(h, w_in_ref[:, OFF_Q:OFF_Q + Q_WIDTH]), _param(params_ref, ROW_QN, Q_WIDTH))
        kv = _dot(h, w_in_ref[:, OFF_K:OFF_K + 2 * KV_WIDTH])
        kn = _head_norm(kv[:, 0:KV_WIDTH], _param(params_ref, ROW_KN, KV_WIDTH))
        v = kv[:, KV_WIDTH:]
        kt_s[...] = kn.T
        vt_s[...] = v.T
        for head in range(N_HEADS):
            k8_s[pl.ds(head, N, stride=N_HEADS), :] = kn
            v8_s[pl.ds(head, N, stride=N_HEADS), :] = v
        for p in range(Q_COLS):
            qcol = q[:, p * LANES:(p + 1) * LANES]
            qswap = pltpu.roll(qcol, HEAD_DIM, 1)
            for e in range(HEADS_PER_COL):
                head = p * HEADS_PER_COL + e
                c = _kv_head(head)
                src = qcol if e == c else qswap
                keep = (lane < HEAD_DIM) if c == 0 else (lane >= HEAD_DIM)
                q8_s[pl.ds(head, N, stride=N_HEADS), :] = jnp.where(keep, src, 0.0)

    base = i * SB
    rows = pl.ds(pl.multiple_of(base, SB), SB)

    unew = u_s[rows, :]
    for g, w in enumerate(POOL_WINDOWS):
        cols = slice(g * POOL_GROUP_WIDTH, (g + 1) * POOL_GROUP_WIDTH)
        s = unew[:, cols]
        for r in range(POOL_BUF - (w - 1), POOL_BUF):
            s = s + sp_ref[r, :, cols]
        ypre_s[rows, cols] = (s * (1.0 / w) - unew[:, cols]).astype(BF16)
    pnew_ref[0:POOL_BUF - 1] = sp_ref[1:POOL_BUF]
    pnew_ref[POOL_BUF - 1] = unew

    hrows = pl.ds(pl.multiple_of(base * N_HEADS, SB * N_HEADS), SB * N_HEADS)
    lhs = q8_s[hrows, :]
    sc = jnp.concatenate(
        [_dot(lhs[bb * N_HEADS:(bb + 1) * N_HEADS, :].astype(BF16), ck_ref[bb].astype(BF16)) for bb in range(SB)],
        axis=0)
    sc_self = jnp.sum(lhs * k8_s[hrows, :], axis=-1, keepdims=True)
    row_head = lax.rem(lax.broadcasted_iota(jnp.int32, (SB * N_HEADS, 1), 0), N_HEADS)
    sink = jnp.zeros((SB * N_HEADS, 1), F32)
    for head in range(N_HEADS):
        sink = jnp.where(row_head == head, sinks_ref[head] * LOG2_E, sink)
    m = jnp.maximum(jnp.maximum(jnp.max(sc, axis=-1, keepdims=True), sc_self), sink)
    p = jnp.exp2(sc - m)
    p_self = jnp.exp2(sc_self - m)
    denom = jnp.sum(p, axis=-1, keepdims=True) + p_self + jnp.exp2(sink - m)
    inv = 1.0 / denom
    pn = p * inv
    o = jnp.concatenate(
        [_dot_nt(pn[bb * N_HEADS:(bb + 1) * N_HEADS, :].astype(BF16), cv_ref[bb].astype(BF16)) for bb in range(SB)],
        axis=0)
    o8_s[hrows, :] = o + (p_self * inv) * v8_s[hrows, :]

    lane_c = lax.broadcasted_iota(jnp.int32, (KV_WIDTH, WINDOW), 1)
    kt_all = kt_s[...]
    vt_all = vt_s[...]
    for bb in range(SB):
        put = (WINDOW - 1) - (base + bb)
        knew_ref[bb] = jnp.where(lane_c == WINDOW - 1, pltpu.roll(kt_all, put, 1),
                                 pltpu.roll(ck_ref[bb], WINDOW - 1, 1))
        vnew_ref[bb] = jnp.where(lane_c == WINDOW - 1, pltpu.roll(vt_all, put, 1),
                                 pltpu.roll(cv_ref[bb], WINDOW - 1, 1))

    @pl.when(i == nsteps - 1)
    def _():
        for k in range(1, nh):
            weight_copy(k).wait()
        ycols = []
        for p in range(Q_COLS):
            halves = []
            for e in range(HEADS_PER_COL):
                head = p * HEADS_PER_COL + e
                oh = o8_s[pl.ds(head, N, stride=N_HEADS), :]
                halves.append(oh if e == _kv_head(head) else pltpu.roll(oh, HEAD_DIM, 1))
            ycols.append(jnp.where(lane < HEAD_DIM, halves[0], halves[1]).astype(BF16))
        yattn = jnp.concatenate(ycols, axis=1)
        _merge(ypre_s[...], yattn, sga_s, sgb_s, wfold_ref, wap_ref, merged_s, all_rows)
        x1 = load_x() + _dot(merged_s[...], wout_ref[...])
        h2 = ((x1 * _rms_scale(x1)) * _param(params_ref, ROW_N2, D_MODEL)).astype(BF16)
        _ffn_act(h2, wg_ref, wu_ref, act_s, all_rows)
        y = x1 + _dot(act_s[...], wd_ref[...])
        for ct in range(col_tiles):
            y_ref[pl.ds(ct, N, stride=col_tiles), :] = y[:, ct * LANES:(ct + 1) * LANES]


def _sample_call(x, ck, cv, sp, sinks, params, big):
    N = x.shape[0] // (D_MODEL // LANES)
    SB = SAMPLE_BLOCK
    W = WINDOW
    assert tuple(b.shape for b in big) == tuple(BF16_SHAPES[w] for w in HANDED_ON)
    consts = (sinks, params)
    in_specs = [_const_spec(x.shape),
                pl.BlockSpec((SB, KV_WIDTH, W), lambda i: (i, 0, 0)),
                pl.BlockSpec((SB, KV_WIDTH, W), lambda i: (i, 0, 0)),
                pl.BlockSpec((POOL_BUF, SB, POOL_WIDTH), lambda i: (0, i, 0)),
                pl.BlockSpec(memory_space=pltpu.SMEM),
                _const_spec(params.shape)]
    in_specs += [pl.BlockSpec(memory_space=pl.ANY) for _ in big]
    out_shape = (jax.ShapeDtypeStruct(x.shape, F32),
                 jax.ShapeDtypeStruct((N, KV_WIDTH, W), F32),
                 jax.ShapeDtypeStruct((N, KV_WIDTH, W), F32),
                 jax.ShapeDtypeStruct((POOL_BUF, N, POOL_WIDTH), F32))
    out_specs = (pl.BlockSpec(x.shape, lambda i: (0, 0)),
                 pl.BlockSpec((SB, KV_WIDTH, W), lambda i: (i, 0, 0)),
                 pl.BlockSpec((SB, KV_WIDTH, W), lambda i: (i, 0, 0)),
                 pl.BlockSpec((POOL_BUF, SB, POOL_WIDTH), lambda i: (0, i, 0)))
    scratch = [pltpu.VMEM(b.shape, BF16) for b in big]
    scratch += [pltpu.SemaphoreType.DMA((len(big),)),
                pltpu.VMEM((N, POOL_WIDTH), F32),
               pltpu.VMEM((KV_WIDTH, N), F32),
               pltpu.VMEM((KV_WIDTH, N), F32),
               pltpu.VMEM((N * N_HEADS, LANES), F32),
               pltpu.VMEM((N * N_HEADS, LANES), F32),
               pltpu.VMEM((N * N_HEADS, LANES), F32),
               pltpu.VMEM((N * N_HEADS, LANES), F32),
               pltpu.VMEM((N, POOL_WIDTH), BF16),
               pltpu.VMEM((N, D_MODEL), F32),
               pltpu.VMEM((N, D_MODEL), F32),
               pltpu.VMEM((N, D_MODEL), BF16),
               pltpu.VMEM((N, D_FF), BF16)]
    return pl.pallas_call(
        _sample_kernel,
        grid=(N // SB,),
        in_specs=in_specs,
        out_specs=out_specs,
        out_shape=out_shape,
        scratch_shapes=scratch,
        compiler_params=pltpu.CompilerParams(
            dimension_semantics=("arbitrary",),
            vmem_limit_bytes=VMEM_LIMIT_BYTES),
        name="sample_layer",
    )(x, ck, cv, sp, *consts, *big)


def _cache_to_device_order(c):
    n, w = c.shape[0], c.shape[1]
    return jnp.transpose(c, (0, 2, 3, 1)).reshape(n, KV_WIDTH, w)


def _cache_from_device_order(c):
    n, _, w = c.shape
    return jnp.transpose(c.reshape(n, N_KV_HEADS, HEAD_DIM, w), (0, 3, 1, 2))[None]


@jax.jit
def _forward(x_prompt, x_sample, cache_k, cache_v, state_pool, norm1, w_in, q_norm, k_norm, sinks,
             pool_mix_w, pool_scale, w_pool_proj, w_attn_proj, w_out, norm2, w_gate, w_up, w_down):
    depth = w_in.shape[0]
    assert depth == 1, "single-layer trunk"
    l = 0
    N = x_sample.shape[0]
    assert cache_k.shape[2] == WINDOW and x_sample.shape[1] == 1

    col = lax.broadcasted_iota(jnp.int32, (1, D_MODEL), 1)
    row = lax.broadcasted_iota(jnp.int32, (PARAM_ROWS, 1), 0)

    def widened(v, width):
        return jnp.where(col < width, jnp.tile(v, D_MODEL // v.shape[0])[None, :], 0.0)

    rows = {ROW_N1: norm1[l][None, :], ROW_N2: norm2[l][None, :],
            ROW_QN: widened(q_norm[l], Q_WIDTH) * (HEAD_DIM ** -0.5 * LOG2_E),
            ROW_KN: widened(k_norm[l], KV_WIDTH), ROW_PSCALE: widened(pool_scale[l], POOL_WIDTH)}
    params = jnp.zeros((PARAM_ROWS, D_MODEL), F32)
    for r, v in rows.items():
        params = jnp.where(row == r, v, params)
    big = (w_in[l], pool_mix_w[l].reshape(len(POOL_WINDOWS) * POOL_GROUP_WIDTH, POOL_GROUP_WIDTH),
           w_pool_proj[l], w_attn_proj[l], w_out[l], w_gate[l], w_up[l], w_down[l])

    y_p, k_p, v_p, pool_p, *weights_bf16 = _prompt_call(x_prompt, sinks[l], params, big)

    y_s, k_s, v_s, pool_s = _sample_call(
        x_sample.reshape(N * (D_MODEL // LANES), LANES),
        _cache_to_device_order(cache_k[l]), _cache_to_device_order(cache_v[l]),
        jnp.transpose(state_pool[l], (1, 0, 2)), sinks[l], params, tuple(weights_bf16))

    return (y_p, y_s.reshape(N, 1, D_MODEL),
            _cache_from_device_order(k_p), _cache_from_device_order(v_p), jnp.transpose(pool_p, (1, 0, 2))[None],
            _cache_from_device_order(k_s), _cache_from_device_order(v_s),
            jnp.transpose(pool_s, (1, 0, 2))[None])


def kernel(x_prompt, x_sample, cache_k, cache_v, state_pool, norm1, w_in, q_norm, k_norm, sinks, pool_mix_w,
           pool_scale, w_pool_proj, w_attn_proj, w_out, norm2, w_gate, w_up, w_down):
    return _forward(x_prompt, x_sample, cache_k, cache_v, state_pool, norm1, w_in, q_norm, k_norm, sinks,
                    pool_mix_w, pool_scale, w_pool_proj, w_attn_proj, w_out, norm2, w_gate, w_up, w_down)
```

```python
import numpy as np
import jax
import jax.numpy as jnp
from jax import lax
from jax.experimental import pallas as pl
from jax.experimental.pallas import tpu as pltpu

D_MODEL = 1024
POOL_WINDOWS = (2, 4, 8, 16)
POOL_GROUP_WIDTH = 128
POOL_WIDTH = 512
POOL_BUF = 15
N_HEADS = 8
N_KV_HEADS = 2
HEAD_DIM = 64
GROUP = N_HEADS // N_KV_HEADS
Q_WIDTH = 512
KV_WIDTH = 128
WINDOW = 128
D_FF = 2816
EPS = 1e-6
NEG = -1e30
LOG2_E = 1.4426950408889634

LANES = 128
ATT_BLOCK = WINDOW
SEQ_TILE = 512
TILES_PER_STEP = 1
SUB_TILE = 256
FF_CHUNK = 256
GATE_CHUNK = 256
SAMPLE_BLOCK = 32
W_IN_PIECES = 4
PREFIX = 16
VMEM_LIMIT_BYTES = 58 * 1024 * 1024
Q_COLS = Q_WIDTH // LANES
HEADS_PER_COL = LANES // HEAD_DIM
N_GATE_CHUNKS = D_MODEL // GATE_CHUNK

OFF_U = 0
OFF_Q = OFF_U + POOL_WIDTH
OFF_K = OFF_Q + Q_WIDTH
OFF_V = OFF_K + KV_WIDTH
OFF_GA = OFF_V + KV_WIDTH
OFF_GB = OFF_GA + D_MODEL

BF16 = jnp.bfloat16
F32 = jnp.float32

ROW_N1, ROW_N2, ROW_QN, ROW_KN, ROW_PSCALE, PARAM_ROWS = 0, 1, 2, 3, 4, 8


def _param(params_ref, row, width):
    return params_ref[row:row + 1, 0:width]


def _dot(a, b):
    return jnp.dot(a, b, preferred_element_type=F32)


def _dot_nt(a, b):
    return lax.dot_general(a, b, (((1,), (1,)), ((), ())), preferred_element_type=F32)


def _rms_scale(x):
    return lax.rsqrt(jnp.mean(x * x, axis=-1, keepdims=True) + EPS)


def _head_norm(x, gain):
    sq = x * x
    low = lax.broadcasted_iota(jnp.int32, (x.shape[0], LANES), 1) < HEAD_DIM
    cols = []
    for c in range(0, x.shape[1], LANES):
        blk = sq[:, c:c + LANES]
        ss_lo = jnp.sum(jnp.where(low, blk, 0.0), axis=-1, keepdims=True)
        ss_hi = jnp.sum(jnp.where(low, 0.0, blk), axis=-1, keepdims=True)
        cols.append(jnp.where(low, ss_lo, ss_hi))
    ss = jnp.concatenate(cols, axis=1) if len(cols) > 1 else cols[0]
    return (x * lax.rsqrt(ss * (1.0 / HEAD_DIM) + EPS)) * gain


def _kv_head(head):
    return head // GROUP


def _ffn_act(h2, wg_ref, wu_ref, act_ref, rows):
    for start in range(0, D_FF, FF_CHUNK):
        width = min(FF_CHUNK, D_FF - start)
        g = _dot(h2, wg_ref[:, start:start + width])
        u = _dot(h2, wu_ref[:, start:start + width])
        act_ref[rows, start:start + width] = (g * jax.nn.sigmoid(g) * u).astype(BF16)


def _gate_chunk(h, c, w_in_ref, sga_ref, sgb_ref, rows):
    lo = c * GATE_CHUNK
    sga_ref[rows, lo:lo + GATE_CHUNK] = jax.nn.sigmoid(_dot(h, w_in_ref[:, OFF_GA + lo:OFF_GA + lo + GATE_CHUNK]))
    sgb_ref[rows, lo:lo + GATE_CHUNK] = jax.nn.sigmoid(_dot(h, w_in_ref[:, OFF_GB + lo:OFF_GB + lo + GATE_CHUNK]))


def _merge(ypool, yattn, sga_ref, sgb_ref, wpp_ref, wap_ref, merged_ref, rows):
    for c in range(N_GATE_CHUNKS):
        lo = c * GATE_CHUNK
        pp = _dot(ypool, wpp_ref[:, lo:lo + GATE_CHUNK])
        ap = _dot(yattn, wap_ref[:, lo:lo + GATE_CHUNK])
        merged = sga_ref[rows, lo:lo + GATE_CHUNK] * pp + sgb_ref[rows, lo:lo + GATE_CHUNK] * ap
        merged_ref[rows, lo:lo + GATE_CHUNK] = merged.astype(BF16)


BIG_WEIGHTS = (("w_in", D_MODEL, OFF_GB + D_MODEL), ("mix", len(POOL_WINDOWS) * POOL_GROUP_WIDTH, POOL_GROUP_WIDTH),
               ("wpp", POOL_WIDTH, D_MODEL), ("wap", Q_WIDTH, D_MODEL), ("wout", D_MODEL, D_MODEL),
               ("wg", D_MODEL, D_FF), ("wu", D_MODEL, D_FF), ("wd", D_FF, D_MODEL))
W_MIX, W_PP = 1, 2
W_FOLD = len(BIG_WEIGHTS)
HANDED_ON = tuple(w for w in range(len(BIG_WEIGHTS)) if w not in (W_MIX, W_PP)) + (W_FOLD,)
BF16_SHAPES = tuple((rows, cols) for _, rows, cols in BIG_WEIGHTS) + ((POOL_WIDTH, D_MODEL),)
CONVERT_ROWS = 128
STAGE_SLOTS = 4
STAGE_COLS = max(cols for _, _, cols in BIG_WEIGHTS)


def _weight_chunks():
    chunks = []
    for w, (_, nrows, cols) in enumerate(BIG_WEIGHTS):
        side_by_side = max(1, STAGE_COLS // cols)
        blocks = [(r0, min(CONVERT_ROWS, nrows - r0)) for r0 in range(0, nrows, CONVERT_ROWS)]
        for first in range(0, len(blocks), side_by_side):
            group = blocks[first:first + side_by_side]
            chunks.append((w, tuple((r0, rows, k * cols) for k, (r0, rows) in enumerate(group))))
    return tuple(chunks)


WEIGHT_CHUNKS = _weight_chunks()


def _writeback_copy(w, w_v, w_bf_hbm, out_sem):
    return pltpu.make_async_copy(w_v[w], w_bf_hbm[w], out_sem.at[w])


def _convert_weights(w_hbm, w_v, stage, stage_sem, mixf_s, params_ref):
    def chunk_copies(n):
        w, pieces = WEIGHT_CHUNKS[n]
        cols = BIG_WEIGHTS[w][2]
        slot = n % STAGE_SLOTS
        return [pltpu.make_async_copy(w_hbm[w].at[pl.ds(r0, rows), :],
                                      stage.at[slot, pl.ds(0, rows), pl.ds(c0, cols)], stage_sem.at[slot])
                for r0, rows, c0 in pieces]

    def convert(n):
        w, pieces = WEIGHT_CHUNKS[n]
        cols = BIG_WEIGHTS[w][2]
        for r0, rows, c0 in pieces:
            staged = stage[n % STAGE_SLOTS, 0:rows, c0:c0 + cols]
            if w == W_MIX:
                mixf_s[r0:r0 + rows, :] = staged
            else:
                w_v[w][r0:r0 + rows, :] = staged.astype(BF16)

    ahead = STAGE_SLOTS - 1
    for m in range(ahead):
        for copy in chunk_copies(m):
            copy.start()
    for n in range(len(WEIGHT_CHUNKS)):
        if n + ahead < len(WEIGHT_CHUNKS):
            for copy in chunk_copies(n + ahead):
                copy.start()
        for copy in chunk_copies(n):
            copy.wait()
        convert(n)

    for g in range(len(POOL_WINDOWS)):
        grp = slice(g * POOL_GROUP_WIDTH, (g + 1) * POOL_GROUP_WIDTH)
        scale = params_ref[ROW_PSCALE:ROW_PSCALE + 1, grp]
        w_v[W_FOLD][grp, :] = _dot((mixf_s[grp, :] * scale).astype(BF16), w_v[W_PP][grp, :]).astype(BF16)


def _prompt_kernel(*refs):
    nw = len(BIG_WEIGHTS)
    sinks_ref, x_ref, params_ref, bias_ref = refs[:4]
    w_hbm = refs[4:4 + nw]
    pos = 4 + nw
    outs = refs[pos:pos + 4]
    pos += 4
    w_bf_hbm = dict(zip(HANDED_ON, refs[pos:pos + len(HANDED_ON)]))
    pos += len(HANDED_ON)
    w_v = refs[pos:pos + nw + 1]
    pos += nw + 1
    stage, stage_sem, out_sem, mixf_s = refs[pos:pos + 4]
    scratch = refs[pos + 4:]
    w_in_ref, _, _, wap_ref, wout_ref, wg_ref, wu_ref, wd_ref, wfold_ref = w_v

    step = pl.program_id(0) * pl.num_programs(1) + pl.program_id(1)
    n_steps = pl.num_programs(0) * pl.num_programs(1)

    @pl.when(step == 0)
    def _():
        _convert_weights(w_hbm, w_v, stage, stage_sem, mixf_s, params_ref)

    @pl.when(step == 1)
    def _():
        for w in HANDED_ON:
            _writeback_copy(w, w_v, w_bf_hbm, out_sem).start()

    def body(t, carry):
        _prompt_tile(t, sinks_ref, x_ref, params_ref, bias_ref,
                     w_in_ref, wfold_ref, wap_ref, wout_ref, wg_ref, wu_ref, wd_ref, *outs, *scratch)
        return carry
    lax.fori_loop(0, TILES_PER_STEP, body, 0)

    @pl.when(step == n_steps - 1)
    def _():
        for w in HANDED_ON:
            _writeback_copy(w, w_v, w_bf_hbm, out_sem).wait()


def _prompt_tile(t, sinks_ref, x_ref, params_ref, bias_ref,
                 w_in_ref, wfold_ref, wap_ref, wout_ref, wg_ref, wu_ref, wd_ref,
                 y_ref, knew_ref, vnew_ref, pnew_ref,
                 h_s, u_s, q_s, kd_s, vd_s, s_s, p_s, ypool_s, yattn_s, sga_s, sgb_s, merged_s, x1_s, h2_s, act_s):
    T = SEQ_TILE
    R = SUB_TILE
    subs = [slice(r0, r0 + R) for r0 in range(0, T, R)]
    n_blocks = T // ATT_BLOCK
    j = pl.program_id(1) * TILES_PER_STEP + t
    rows_per_kv = GROUP * ATT_BLOCK

    def in_block(rows):
        return pl.ds(pl.multiple_of(t * T + rows.start, R), R)

    @pl.when(j == 0)
    def _():
        u_s[0:PREFIX, :] = jnp.zeros((PREFIX, POOL_WIDTH), F32)
        kd_s[:, 0:ATT_BLOCK, :] = jnp.zeros((N_KV_HEADS, ATT_BLOCK, KV_WIDTH), BF16)
        vd_s[:, 0:ATT_BLOCK, :] = jnp.zeros((N_KV_HEADS, ATT_BLOCK, KV_WIDTH), BF16)

    lane_r = lax.broadcasted_iota(jnp.int32, (R, LANES), 1)
    lane = lax.broadcasted_iota(jnp.int32, (ATT_BLOCK, LANES), 1)

    def norm1(rows):
        x = x_ref[0, in_block(rows), :]
        h_s[rows, :] = ((x * _rms_scale(x)) * _param(params_ref, ROW_N1, D_MODEL)).astype(BF16)

    def in_proj(rows):
        h = h_s[rows, :]
        u_s[PREFIX + rows.start:PREFIX + rows.stop, :] = _dot(h, w_in_ref[:, OFF_U:OFF_U + POOL_WIDTH])
        q = _dot(h, w_in_ref[:, OFF_Q:OFF_Q + Q_WIDTH])
        qn = _head_norm(q, _param(params_ref, ROW_QN, Q_WIDTH)).astype(BF16)
        for p in range(Q_COLS):
            qcol = qn[:, p * LANES:(p + 1) * LANES]
            lo, hi = 2 * p * LANES, (2 * p + 1) * LANES
            q_s[rows, lo:lo + LANES] = jnp.where(lane_r < HEAD_DIM, qcol, jnp.zeros_like(qcol))
            q_s[rows, hi:hi + LANES] = jnp.where(lane_r >= HEAD_DIM, qcol, jnp.zeros_like(qcol))
        kv = _dot(h, w_in_ref[:, OFF_K:OFF_K + 2 * KV_WIDTH])
        kn = _head_norm(kv[:, 0:KV_WIDTH], _param(params_ref, ROW_KN, KV_WIDTH))
        v = kv[:, KV_WIDTH:]
        dst_rows = slice(ATT_BLOCK + rows.start, ATT_BLOCK + rows.stop)
        for src, dst in ((kn, kd_s), (v, vd_s)):
            swapped = pltpu.roll(src, HEAD_DIM, 1)
            dst[0, dst_rows, :] = jnp.where(lane_r < HEAD_DIM, src, swapped).astype(BF16)
            dst[1, dst_rows, :] = jnp.where(lane_r < HEAD_DIM, swapped, src).astype(BF16)
        if rows.stop == T:
            knew_ref[0] = kn[R - ATT_BLOCK:, :].T
            vnew_ref[0] = v[R - ATT_BLOCK:, :].T
            batch = pl.program_id(0)
            for r in range(POOL_BUF):
                src = PREFIX + T - POOL_BUF + r
                pnew_ref[r, pl.ds(batch, 1), :] = u_s[src:src + 1, :]

    def pool(rows):
        pos1 = j * T + rows.start + lax.broadcasted_iota(jnp.int32, (R, 1), 0) + 1
        for g, w in enumerate(POOL_WINDOWS):
            cols = slice(g * POOL_GROUP_WIDTH, (g + 1) * POOL_GROUP_WIDTH)
            a = u_s[rows.start:rows.stop + PREFIX, cols]
            s = a
            shift = 1
            while shift < w:
                s = s + pltpu.roll(s, shift, 0)
                shift *= 2
            inv_cnt = 1.0 / jnp.minimum(pos1, w).astype(F32)
            ypool_s[rows, cols] = (s[PREFIX:, :] * inv_cnt - a[PREFIX:, :]).astype(BF16)

    def gates(rows, chunks):
        h = h_s[rows, :]
        for c in chunks:
            _gate_chunk(h, c, w_in_ref, sga_s, sgb_s, rows)

    def scores(b):
        r0 = b * ATT_BLOCK
        for c in range(N_KV_HEADS):
            parts = [q_s[r0:r0 + ATT_BLOCK, head * LANES:(head + 1) * LANES]
                     for head in range(c * GROUP, (c + 1) * GROUP)]
            s_s[b % 2, c * rows_per_kv:(c + 1) * rows_per_kv, :] = _dot_nt(
                jnp.concatenate(parts, axis=0), kd_s[c, r0:r0 + 2 * ATT_BLOCK, :])

    def softmax(b):
        bias = bias_ref[jnp.where(j == 0, 1, 0)] if b == 0 else bias_ref[0]
        inv_denoms = []
        for head in range(N_HEADS):
            sh = s_s[b % 2, head * ATT_BLOCK:(head + 1) * ATT_BLOCK, :] + bias
            sink = sinks_ref[head] * LOG2_E
            m = jnp.maximum(jnp.max(sh, axis=-1, keepdims=True), sink)
            p = jnp.exp2(sh - m)
            denom = jnp.sum(p, axis=-1, keepdims=True) + jnp.exp2(sink - m)
            p_s[b % 2, head * ATT_BLOCK:(head + 1) * ATT_BLOCK, :] = p.astype(BF16)
            inv_denoms.append(1.0 / denom)
        return inv_denoms

    def weighted_values(b, inv_denoms):
        r0 = b * ATT_BLOCK
        for c in range(N_KV_HEADS):
            o = _dot(p_s[b % 2, c * rows_per_kv:(c + 1) * rows_per_kv, :], vd_s[c, r0:r0 + 2 * ATT_BLOCK, :])
            for pp in range(Q_COLS // N_KV_HEADS):
                p_col = c * Q_COLS // N_KV_HEADS + pp
                head_lo = p_col * HEADS_PER_COL
                o_lo = o[(2 * pp) * ATT_BLOCK:(2 * pp + 1) * ATT_BLOCK, :] * inv_denoms[head_lo]
                o_hi = o[(2 * pp + 1) * ATT_BLOCK:(2 * pp + 2) * ATT_BLOCK, :] * inv_denoms[head_lo + 1]
                yattn_s[r0:r0 + ATT_BLOCK, p_col * LANES:(p_col + 1) * LANES] = (
                    jnp.where(lane < HEAD_DIM, o_lo, o_hi).astype(BF16))

    def out_proj(rows):
        _merge(ypool_s[rows, :], yattn_s[rows, :], sga_s, sgb_s, wfold_ref, wap_ref, merged_s, rows)
        x1 = x_ref[0, in_block(rows), :] + _dot(merged_s[rows, :], wout_ref[...])
        x1_s[rows, :] = x1
        h2_s[rows, :] = ((x1 * _rms_scale(x1)) * _param(params_ref, ROW_N2, D_MODEL)).astype(BF16)

    for rows in subs:
        norm1(rows)
    for rows in subs:
        in_proj(rows)
    pool(subs[0])
    gates(subs[0], range(N_GATE_CHUNKS))
    for rows in subs[1:]:
        pool(rows)
    later_gates = [(rows, c) for rows in subs[1:] for c in range(N_GATE_CHUNKS)]
    per_block = -(-len(later_gates) // n_blocks)
    scores(0)
    for b in range(n_blocks):
        if b + 1 < n_blocks:
            scores(b + 1)
        for rows, c in later_gates[b * per_block:(b + 1) * per_block]:
            gates(rows, [c])
        weighted_values(b, softmax(b))

    u_s[0:PREFIX, :] = u_s[T:T + PREFIX, :]
    kd_s[:, 0:ATT_BLOCK, :] = kd_s[:, T:T + ATT_BLOCK, :]
    vd_s[:, 0:ATT_BLOCK, :] = vd_s[:, T:T + ATT_BLOCK, :]

    for rows in subs:
        out_proj(rows)
    for rows in subs:
        _ffn_act(h2_s[rows, :], wg_ref, wu_ref, act_s, rows)
    for rows in subs:
        y_ref[0, in_block(rows), :] = x1_s[rows, :] + _dot(act_s[rows, :], wd_ref[...])


def _const_spec(shape):
    nd = len(shape)
    return pl.BlockSpec(shape, lambda *_: (0,) * nd, pipeline_mode=pl.Buffered(1))


def _band_bias():
    row = np.arange(ATT_BLOCK)[:, None]
    col = np.arange(2 * ATT_BLOCK)[None, :]
    band = (col >= row) & (col <= row + WINDOW)
    masks = np.stack([band, band & (col >= ATT_BLOCK)])
    return jnp.asarray(np.where(masks, 0.0, NEG), dtype=F32)


def _prompt_call(x, sinks, params, big):
    B, S, _ = x.shape
    T = SEQ_TILE
    step_rows = T * TILES_PER_STEP
    assert tuple(b.shape for b in big) == BF16_SHAPES[:len(BIG_WEIGHTS)]
    assert B * (S // step_rows) >= 2, "the bf16 write-back starts on the second grid step"
    bias = _band_bias()
    in_specs = [pl.BlockSpec(memory_space=pltpu.SMEM),
                pl.BlockSpec((1, step_rows, D_MODEL), lambda b, j: (b, j, 0)),
                _const_spec(params.shape),
                _const_spec(bias.shape)]
    in_specs += [pl.BlockSpec(memory_space=pl.ANY) for _ in big]
    out_shape = (jax.ShapeDtypeStruct((B, S, D_MODEL), F32),
                 jax.ShapeDtypeStruct((B, KV_WIDTH, ATT_BLOCK), F32),
                 jax.ShapeDtypeStruct((B, KV_WIDTH, ATT_BLOCK), F32),
                 jax.ShapeDtypeStruct((POOL_BUF, B, POOL_WIDTH), F32))
    out_shape += tuple(jax.ShapeDtypeStruct(BF16_SHAPES[w], BF16) for w in HANDED_ON)
    out_specs = (pl.BlockSpec((1, step_rows, D_MODEL), lambda b, j: (b, j, 0)),
                 pl.BlockSpec((1, KV_WIDTH, ATT_BLOCK), lambda b, j: (b, 0, 0)),
                 pl.BlockSpec((1, KV_WIDTH, ATT_BLOCK), lambda b, j: (b, 0, 0)),
                 pl.BlockSpec((POOL_BUF, B, POOL_WIDTH), lambda b, j: (0, 0, 0)))
    out_specs += tuple(pl.BlockSpec(memory_space=pl.ANY) for _ in HANDED_ON)
    scratch = [pltpu.VMEM(shape, BF16) for shape in BF16_SHAPES]
    scratch += [pltpu.VMEM((STAGE_SLOTS, CONVERT_ROWS, STAGE_COLS), F32),
                pltpu.SemaphoreType.DMA((STAGE_SLOTS,)),
                pltpu.SemaphoreType.DMA((len(BF16_SHAPES),)),
                pltpu.VMEM(BF16_SHAPES[W_MIX], F32)]
    scratch += [pltpu.VMEM((T, D_MODEL), BF16),
               pltpu.VMEM((PREFIX + T, POOL_WIDTH), F32),
               pltpu.VMEM((T, N_HEADS * LANES), BF16),
               pltpu.VMEM((N_KV_HEADS, ATT_BLOCK + T, KV_WIDTH), BF16),
               pltpu.VMEM((N_KV_HEADS, ATT_BLOCK + T, KV_WIDTH), BF16),
               pltpu.VMEM((2, N_HEADS * ATT_BLOCK, 2 * ATT_BLOCK), F32),
               pltpu.VMEM((2, N_HEADS * ATT_BLOCK, 2 * ATT_BLOCK), BF16),
               pltpu.VMEM((T, POOL_WIDTH), BF16),
               pltpu.VMEM((T, Q_WIDTH), BF16),
               pltpu.VMEM((T, D_MODEL), F32),
               pltpu.VMEM((T, D_MODEL), F32),
               pltpu.VMEM((T, D_MODEL), BF16),
               pltpu.VMEM((T, D_MODEL), F32),
               pltpu.VMEM((T, D_MODEL), BF16),
               pltpu.VMEM((T, D_FF), BF16)]
    return pl.pallas_call(
        _prompt_kernel,
        grid=(B, S // step_rows),
        in_specs=in_specs,
        out_specs=out_specs,
        out_shape=out_shape,
        scratch_shapes=scratch,
        compiler_params=pltpu.CompilerParams(
            dimension_semantics=("arbitrary", "arbitrary"),
            vmem_limit_bytes=VMEM_LIMIT_BYTES),
        name="prompt_layer",
    )(sinks, x, params, bias, *big)


def _sample_kernel(*refs):
    nh = len(HANDED_ON)
    n_in = 6
    (x_ref, ck_ref, cv_ref, sp_ref, sinks_ref, params_ref) = refs[:n_in]
    w_hbm = refs[n_in:n_in + nh]
    pos = n_in + nh
    y_ref, knew_ref, vnew_ref, pnew_ref = refs[pos:pos + 4]
    pos += 4
    w_v = refs[pos:pos + nh]
    pos += nh
    (w_sem, u_s, kt_s, vt_s, q8_s, k8_s, v8_s, o8_s, ypre_s, sga_s, sgb_s, merged_s, act_s) = refs[pos:]
    w_in_ref, wap_ref, wout_ref, wg_ref, wu_ref, wd_ref, wfold_ref = w_v

    SB = SAMPLE_BLOCK
    col_tiles = D_MODEL // LANES
    N = x_ref.shape[0] // col_tiles
    nsteps = N // SB
    i = pl.program_id(0)
    lane = lax.broadcasted_iota(jnp.int32, (N, LANES), 1)
    all_rows = slice(0, N)

    def load_x():
        return jnp.concatenate([x_ref[pl.ds(ct, N, stride=col_tiles), :] for ct in range(col_tiles)], axis=1)

    def weight_copy(k):
        return pltpu.make_async_copy(w_hbm[k], w_v[k], w_sem.at[k])

    def w_in_piece(c):
        rows = pl.ds(c * (D_MODEL // W_IN_PIECES), D_MODEL // W_IN_PIECES)
        return pltpu.make_async_copy(w_hbm[0].at[rows, :], w_v[0].at[rows, :], w_sem.at[nh + c])

    @pl.when(i == 0)
    def _():
        for c in range(W_IN_PIECES):
            w_in_piece(c).start()
        for k in range(1, nh):
            weight_copy(k).start()
        for c in range(W_IN_PIECES):
            w_in_piece(c).wait()
        x = load_x()
        h = ((x * _rms_scale(x)) * _param(params_ref, ROW_N1, D_MODEL)).astype(BF16)
        u_s[...] = _dot(h, w_in_ref[:, OFF_U:OFF_U + POOL_WIDTH])
        for gc in range(N_GATE_CHUNKS):
            _gate_chunk(h, gc, w_in_ref, sga_s, sgb_s, all_rows)
        q = _head_norm(_dot(h, w_in_ref[:, OFF_Q:OFF_Q + Q_WIDTH]), _param(params_ref, ROW_QN, Q_WIDTH))
        kv = _dot(h, w_in_ref[:, OFF_K:OFF_K + 2 * KV_WIDTH])
        kn = _head_norm(kv[:, 0:KV_WIDTH], _param(params_ref, ROW_KN, KV_WIDTH))
        v = kv[:, KV_WIDTH:]
        kt_s[...] = kn.T
        vt_s[...] = v.T
        for head in range(N_HEADS):
            k8_s[pl.ds(head, N, stride=N_HEADS), :] = kn
            v8_s[pl.ds(head, N, stride=N_HEADS), :] = v
        for p in range(Q_COLS):
            qcol = q[:, p * LANES:(p + 1) * LANES]
            qswap = pltpu.roll(qcol, HEAD_DIM, 1)
            for e in range(HEADS_PER_COL):
                head = p * HEADS_PER_COL + e
                c = _kv_head(head)
                src = qcol if e == c else qswap
                keep = (lane < HEAD_DIM) if c == 0 else (lane >= HEAD_DIM)
                q8_s[pl.ds(head, N, stride=N_HEADS), :] = jnp.where(keep, src, 0.0)

    base = i * SB
    rows = pl.ds(pl.multiple_of(base, SB), SB)

    unew = u_s[rows, :]
    for g, w in enumerate(POOL_WINDOWS):
        cols = slice(g * POOL_GROUP_WIDTH, (g + 1) * POOL_GROUP_WIDTH)
        s = unew[:, cols]
        for r in range(POOL_BUF - (w - 1), POOL_BUF):
            s = s + sp_ref[r, :, cols]
        ypre_s[rows, cols] = (s * (1.0 / w) - unew[:, cols]).astype(BF16)
    pnew_ref[0:POOL_BUF - 1] = sp_ref[1:POOL_BUF]
    pnew_ref[POOL_BUF - 1] = unew

    hrows = pl.ds(pl.multiple_of(base * N_HEADS, SB * N_HEADS), SB * N_HEADS)
    lhs = q8_s[hrows, :]
    sc = jnp.concatenate(
        [_dot(lhs[bb * N_HEADS:(bb + 1) * N_HEADS, :].astype(BF16), ck_ref[bb].astype(BF16)) for bb in range(SB)],
        axis=0)
    sc_self = jnp.sum(lhs * k8_s[hrows, :], axis=-1, keepdims=True)
    row_head = lax.rem(lax.broadcasted_iota(jnp.int32, (SB * N_HEADS, 1), 0), N_HEADS)
    sink = jnp.zeros((SB * N_HEADS, 1), F32)
    for head in range(N_HEADS):
        sink = jnp.where(row_head == head, sinks_ref[head] * LOG2_E, sink)
    m = jnp.maximum(jnp.maximum(jnp.max(sc, axis=-1, keepdims=True), sc_self), sink)
    p = jnp.exp2(sc - m)
    p_self = jnp.exp2(sc_self - m)
    denom = jnp.sum(p, axis=-1, keepdims=True) + p_self + jnp.exp2(sink - m)
    inv = 1.0 / denom
    pn = p * inv
    o = jnp.concatenate(
        [_dot_nt(pn[bb * N_HEADS:(bb + 1) * N_HEADS, :].astype(BF16), cv_ref[bb].astype(BF16)) for bb in range(SB)],
        axis=0)
    o8_s[hrows, :] = o + (p_self * inv) * v8_s[hrows, :]

    lane_c = lax.broadcasted_iota(jnp.int32, (KV_WIDTH, WINDOW), 1)
    kt_all = kt_s[...]
    vt_all = vt_s[...]
    for bb in range(SB):
        put = (WINDOW - 1) - (base + bb)
        knew_ref[bb] = jnp.where(lane_c == WINDOW - 1, pltpu.roll(kt_all, put, 1),
                                 pltpu.roll(ck_ref[bb], WINDOW - 1, 1))
        vnew_ref[bb] = jnp.where(lane_c == WINDOW - 1, pltpu.roll(vt_all, put, 1),
                                 pltpu.roll(cv_ref[bb], WINDOW - 1, 1))

    @pl.when(i == nsteps - 1)
    def _():
        for k in range(1, nh):
            weight_copy(k).wait()
        ycols = []
        for p in range(Q_COLS):
            halves = []
            for e in range(HEADS_PER_COL):
                head = p * HEADS_PER_COL + e
                oh = o8_s[pl.ds(head, N, stride=N_HEADS), :]
                halves.append(oh if e == _kv_head(head) else pltpu.roll(oh, HEAD_DIM, 1))
            ycols.append(jnp.where(lane < HEAD_DIM, halves[0], halves[1]).astype(BF16))
        yattn = jnp.concatenate(ycols, axis=1)
        _merge(ypre_s[...], yattn, sga_s, sgb_s, wfold_ref, wap_ref, merged_s, all_rows)
        x1 = load_x() + _dot(merged_s[...], wout_ref[...])
        h2 = ((x1 * _rms_scale(x1)) * _param(params_ref, ROW_N2, D_MODEL)).astype(BF16)
        _ffn_act(h2, wg_ref, wu_ref, act_s, all_rows)
        y = x1 + _dot(act_s[...], wd_ref[...])
        for ct in range(col_tiles):
            y_ref[pl.ds(ct, N, stride=col_tiles), :] = y[:, ct * LANES:(ct + 1) * LANES]


def _sample_call(x, ck, cv, sp, sinks, params, big):
    N = x.shape[0] // (D_MODEL // LANES)
    SB = SAMPLE_BLOCK
    W = WINDOW
    assert tuple(b.shape for b in big) == tuple(BF16_SHAPES[w] for w in HANDED_ON)
    consts = (sinks, params)
    in_specs = [_const_spec(x.shape),
                pl.BlockSpec((SB, KV_WIDTH, W), lambda i: (i, 0, 0)),
                pl.BlockSpec((SB, KV_WIDTH, W), lambda i: (i, 0, 0)),
                pl.BlockSpec((POOL_BUF, SB, POOL_WIDTH), lambda i: (0, i, 0)),
                pl.BlockSpec(memory_space=pltpu.SMEM),
                _const_spec(params.shape)]
    in_specs += [pl.BlockSpec(memory_space=pl.ANY) for _ in big]
    out_shape = (jax.ShapeDtypeStruct(x.shape, F32),
                 jax.ShapeDtypeStruct((N, KV_WIDTH, W), F32),
                 jax.ShapeDtypeStruct((N, KV_WIDTH, W), F32),
                 jax.ShapeDtypeStruct((POOL_BUF, N, POOL_WIDTH), F32))
    out_specs = (pl.BlockSpec(x.shape, lambda i: (0, 0)),
                 pl.BlockSpec((SB, KV_WIDTH, W), lambda i: (i, 0, 0)),
                 pl.BlockSpec((SB, KV_WIDTH, W), lambda i: (i, 0, 0)),
                 pl.BlockSpec((POOL_BUF, SB, POOL_WIDTH), lambda i: (0, i, 0)))
    scratch = [pltpu.VMEM(b.shape, BF16) for b in big]
    scratch += [pltpu.SemaphoreType.DMA((len(big) + W_IN_PIECES,)),
                pltpu.VMEM((N, POOL_WIDTH), F32),
               pltpu.VMEM((KV_WIDTH, N), F32),
               pltpu.VMEM((KV_WIDTH, N), F32),
               pltpu.VMEM((N * N_HEADS, LANES), F32),
               pltpu.VMEM((N * N_HEADS, LANES), F32),
               pltpu.VMEM((N * N_HEADS, LANES), F32),
               pltpu.VMEM((N * N_HEADS, LANES), F32),
               pltpu.VMEM((N, POOL_WIDTH), BF16),
               pltpu.VMEM((N, D_MODEL), F32),
               pltpu.VMEM((N, D_MODEL), F32),
               pltpu.VMEM((N, D_MODEL), BF16),
               pltpu.VMEM((N, D_FF), BF16)]
    return pl.pallas_call(
        _sample_kernel,
        grid=(N // SB,),
        in_specs=in_specs,
        out_specs=out_specs,
        out_shape=out_shape,
        scratch_shapes=scratch,
        compiler_params=pltpu.CompilerParams(
            dimension_semantics=("arbitrary",),
            vmem_limit_bytes=VMEM_LIMIT_BYTES),
        name="sample_layer",
    )(x, ck, cv, sp, *consts, *big)


def _cache_to_device_order(c):
    n, w = c.shape[0], c.shape[1]
    return jnp.transpose(c, (0, 2, 3, 1)).reshape(n, KV_WIDTH, w)


def _cache_from_device_order(c):
    n, _, w = c.shape
    return jnp.transpose(c.reshape(n, N_KV_HEADS, HEAD_DIM, w), (0, 3, 1, 2))[None]


@jax.jit
def _forward(x_prompt, x_sample, cache_k, cache_v, state_pool, norm1, w_in, q_norm, k_norm, sinks,
             pool_mix_w, pool_scale, w_pool_proj, w_attn_proj, w_out, norm2, w_gate, w_up, w_down):
    depth = w_in.shape[0]
    assert depth == 1, "single-layer trunk"
    l = 0
    N = x_sample.shape[0]
    assert cache_k.shape[2] == WINDOW and x_sample.shape[1] == 1

    col = lax.broadcasted_iota(jnp.int32, (1, D_MODEL), 1)
    row = lax.broadcasted_iota(jnp.int32, (PARAM_ROWS, 1), 0)

    def widened(v, width):
        return jnp.where(col < width, jnp.tile(v, D_MODEL // v.shape[0])[None, :], 0.0)

    rows = {ROW_N1: norm1[l][None, :], ROW_N2: norm2[l][None, :],
            ROW_QN: widened(q_norm[l], Q_WIDTH) * (HEAD_DIM ** -0.5 * LOG2_E),
            ROW_KN: widened(k_norm[l], KV_WIDTH), ROW_PSCALE: widened(pool_scale[l], POOL_WIDTH)}
    params = jnp.zeros((PARAM_ROWS, D_MODEL), F32)
    for r, v in rows.items():
        params = jnp.where(row == r, v, params)
    big = (w_in[l], pool_mix_w[l].reshape(len(POOL_WINDOWS) * POOL_GROUP_WIDTH, POOL_GROUP_WIDTH),
           w_pool_proj[l], w_attn_proj[l], w_out[l], w_gate[l], w_up[l], w_down[l])

    y_p, k_p, v_p, pool_p, *weights_bf16 = _prompt_call(x_prompt, sinks[l], params, big)

    y_s, k_s, v_s, pool_s = _sample_call(
        x_sample.reshape(N * (D_MODEL // LANES), LANES),
        _cache_to_device_order(cache_k[l]), _cache_to_device_order(cache_v[l]),
        jnp.transpose(state_pool[l], (1, 0, 2)), sinks[l], params, tuple(weights_bf16))

    return (y_p, y_s.reshape(N, 1, D_MODEL),
            _cache_from_device_order(k_p), _cache_from_device_order(v_p), jnp.transpose(pool_p, (1, 0, 2))[None],
            _cache_from_device_order(k_s), _cache_from_device_order(v_s),
            jnp.transpose(pool_s, (1, 0, 2))[None])


def kernel(x_prompt, x_sample, cache_k, cache_v, state_pool, norm1, w_in, q_norm, k_norm, sinks, pool_mix_w,
           pool_scale, w_pool_proj, w_attn_proj, w_out, norm2, w_gate, w_up, w_down):
    return _forward(x_prompt, x_sample, cache_k, cache_v, state_pool, norm1, w_in, q_norm, k_norm, sinks,
                    pool_mix_w, pool_scale, w_pool_proj, w_attn_proj, w_out, norm2, w_gate, w_up, w_down)
```

```python
import numpy as np
import jax
import jax.numpy as jnp
from jax import lax
from jax.experimental import pallas as pl
from jax.experimental.pallas import tpu as pltpu

D_MODEL = 1024
POOL_WINDOWS = (2, 4, 8, 16)
POOL_GROUP_WIDTH = 128
POOL_WIDTH = 512
POOL_BUF = 15
N_HEADS = 8
N_KV_HEADS = 2
HEAD_DIM = 64
GROUP = N_HEADS // N_KV_HEADS
Q_WIDTH = 512
KV_WIDTH = 128
WINDOW = 128
D_FF = 2816
EPS = 1e-6
NEG = -1e30
LOG2_E = 1.4426950408889634

LANES = 128
ATT_BLOCK = WINDOW
SEQ_TILE = 512
TILES_PER_STEP = 1
SUB_TILE = 256
FF_CHUNK = 256
GATE_CHUNK = 256
SAMPLE_BLOCK = 32
CACHE_RING = 3
PREFIX = 16
VMEM_LIMIT_BYTES = 58 * 1024 * 1024
Q_COLS = Q_WIDTH // LANES
HEADS_PER_COL = LANES // HEAD_DIM
N_GATE_CHUNKS = D_MODEL // GATE_CHUNK

OFF_U = 0
OFF_Q = OFF_U + POOL_WIDTH
OFF_K = OFF_Q + Q_WIDTH
OFF_V = OFF_K + KV_WIDTH
OFF_GA = OFF_V + KV_WIDTH
OFF_GB = OFF_GA + D_MODEL

BF16 = jnp.bfloat16
F32 = jnp.float32

ROW_N1, ROW_N2, ROW_QN, ROW_KN, ROW_PSCALE, PARAM_ROWS = 0, 1, 2, 3, 4, 8


def _param(params_ref, row, width):
    return params_ref[row:row + 1, 0:width]


def _dot(a, b):
    return jnp.dot(a, b, preferred_element_type=F32)


def _dot_nt(a, b):
    return lax.dot_general(a, b, (((1,), (1,)), ((), ())), preferred_element_type=F32)


def _rms_scale(x):
    return lax.rsqrt(jnp.mean(x * x, axis=-1, keepdims=True) + EPS)


def _head_norm(x, gain):
    sq = x * x
    low = lax.broadcasted_iota(jnp.int32, (x.shape[0], LANES), 1) < HEAD_DIM
    cols = []
    for c in range(0, x.shape[1], LANES):
        blk = sq[:, c:c + LANES]
        ss_lo = jnp.sum(jnp.where(low, blk, 0.0), axis=-1, keepdims=True)
        ss_hi = jnp.sum(jnp.where(low, 0.0, blk), axis=-1, keepdims=True)
        cols.append(jnp.where(low, ss_lo, ss_hi))
    ss = jnp.concatenate(cols, axis=1) if len(cols) > 1 else cols[0]
    return (x * lax.rsqrt(ss * (1.0 / HEAD_DIM) + EPS)) * gain


def _kv_head(head):
    return head // GROUP


def _ffn_act(h2, wg_ref, wu_ref, act_ref, rows):
    for start in range(0, D_FF, FF_CHUNK):
        width = min(FF_CHUNK, D_FF - start)
        g = _dot(h2, wg_ref[:, start:start + width])
        u = _dot(h2, wu_ref[:, start:start + width])
        act_ref[rows, start:start + width] = (g * jax.nn.sigmoid(g) * u).astype(BF16)


def _gate_chunk(h, c, w_in_ref, sga_ref, sgb_ref, rows):
    lo = c * GATE_CHUNK
    sga_ref[rows, lo:lo + GATE_CHUNK] = jax.nn.sigmoid(_dot(h, w_in_ref[:, OFF_GA + lo:OFF_GA + lo + GATE_CHUNK]))
    sgb_ref[rows, lo:lo + GATE_CHUNK] = jax.nn.sigmoid(_dot(h, w_in_ref[:, OFF_GB + lo:OFF_GB + lo + GATE_CHUNK]))


def _merge(ypool, yattn, sga_ref, sgb_ref, wpp_ref, wap_ref, merged_ref, rows):
    for c in range(N_GATE_CHUNKS):
        lo = c * GATE_CHUNK
        pp = _dot(ypool, wpp_ref[:, lo:lo + GATE_CHUNK])
        ap = _dot(yattn, wap_ref[:, lo:lo + GATE_CHUNK])
        merged = sga_ref[rows, lo:lo + GATE_CHUNK] * pp + sgb_ref[rows, lo:lo + GATE_CHUNK] * ap
        merged_ref[rows, lo:lo + GATE_CHUNK] = merged.astype(BF16)


BIG_WEIGHTS = (("w_in", D_MODEL, OFF_GB + D_MODEL), ("mix", len(POOL_WINDOWS) * POOL_GROUP_WIDTH, POOL_GROUP_WIDTH),
               ("wpp", POOL_WIDTH, D_MODEL), ("wap", Q_WIDTH, D_MODEL), ("wout", D_MODEL, D_MODEL),
               ("wg", D_MODEL, D_FF), ("wu", D_MODEL, D_FF), ("wd", D_FF, D_MODEL))
W_MIX, W_PP = 1, 2
W_FOLD = len(BIG_WEIGHTS)
HANDED_ON = tuple(w for w in range(len(BIG_WEIGHTS)) if w not in (W_MIX, W_PP)) + (W_FOLD,)
BF16_SHAPES = tuple((rows, cols) for _, rows, cols in BIG_WEIGHTS) + ((POOL_WIDTH, D_MODEL),)
CONVERT_ROWS = 128
STAGE_SLOTS = 4
STAGE_COLS = max(cols for _, _, cols in BIG_WEIGHTS)


def _weight_chunks():
    chunks = []
    for w, (_, nrows, cols) in enumerate(BIG_WEIGHTS):
        side_by_side = max(1, STAGE_COLS // cols)
        blocks = [(r0, min(CONVERT_ROWS, nrows - r0)) for r0 in range(0, nrows, CONVERT_ROWS)]
        for first in range(0, len(blocks), side_by_side):
            group = blocks[first:first + side_by_side]
            chunks.append((w, tuple((r0, rows, k * cols) for k, (r0, rows) in enumerate(group))))
    return tuple(chunks)


WEIGHT_CHUNKS = _weight_chunks()


def _writeback_copy(w, w_v, w_bf_hbm, out_sem):
    return pltpu.make_async_copy(w_v[w], w_bf_hbm[w], out_sem.at[w])


def _convert_weights(w_hbm, w_v, stage, stage_sem, mixf_s, params_ref):
    def chunk_copies(n):
        w, pieces = WEIGHT_CHUNKS[n]
        cols = BIG_WEIGHTS[w][2]
        slot = n % STAGE_SLOTS
        return [pltpu.make_async_copy(w_hbm[w].at[pl.ds(r0, rows), :],
                                      stage.at[slot, pl.ds(0, rows), pl.ds(c0, cols)], stage_sem.at[slot])
                for r0, rows, c0 in pieces]

    def convert(n):
        w, pieces = WEIGHT_CHUNKS[n]
        cols = BIG_WEIGHTS[w][2]
        for r0, rows, c0 in pieces:
            staged = stage[n % STAGE_SLOTS, 0:rows, c0:c0 + cols]
            if w == W_MIX:
                mixf_s[r0:r0 + rows, :] = staged
            else:
                w_v[w][r0:r0 + rows, :] = staged.astype(BF16)

    ahead = STAGE_SLOTS - 1
    for m in range(ahead):
        for copy in chunk_copies(m):
            copy.start()
    for n in range(len(WEIGHT_CHUNKS)):
        if n + ahead < len(WEIGHT_CHUNKS):
            for copy in chunk_copies(n + ahead):
                copy.start()
        for copy in chunk_copies(n):
            copy.wait()
        convert(n)

    for g in range(len(POOL_WINDOWS)):
        grp = slice(g * POOL_GROUP_WIDTH, (g + 1) * POOL_GROUP_WIDTH)
        scale = params_ref[ROW_PSCALE:ROW_PSCALE + 1, grp]
        w_v[W_FOLD][grp, :] = _dot((mixf_s[grp, :] * scale).astype(BF16), w_v[W_PP][grp, :]).astype(BF16)


def _prompt_kernel(*refs):
    nw = len(BIG_WEIGHTS)
    sinks_ref, x_ref, params_ref, bias_ref = refs[:4]
    w_hbm = refs[4:4 + nw]
    pos = 4 + nw
    outs = refs[pos:pos + 4]
    pos += 4
    w_bf_hbm = dict(zip(HANDED_ON, refs[pos:pos + len(HANDED_ON)]))
    pos += len(HANDED_ON)
    w_v = refs[pos:pos + nw + 1]
    pos += nw + 1
    stage, stage_sem, out_sem, mixf_s = refs[pos:pos + 4]
    scratch = refs[pos + 4:]
    w_in_ref, _, _, wap_ref, wout_ref, wg_ref, wu_ref, wd_ref, wfold_ref = w_v

    step = pl.program_id(0) * pl.num_programs(1) + pl.program_id(1)
    n_steps = pl.num_programs(0) * pl.num_programs(1)

    @pl.when(step == 0)
    def _():
        _convert_weights(w_hbm, w_v, stage, stage_sem, mixf_s, params_ref)

    @pl.when(step == 1)
    def _():
        for w in HANDED_ON:
            _writeback_copy(w, w_v, w_bf_hbm, out_sem).start()

    def body(t, carry):
        _prompt_tile(t, sinks_ref, x_ref, params_ref, bias_ref,
                     w_in_ref, wfold_ref, wap_ref, wout_ref, wg_ref, wu_ref, wd_ref, *outs, *scratch)
        return carry
    lax.fori_loop(0, TILES_PER_STEP, body, 0)

    @pl.when(step == n_steps - 1)
    def _():
        for w in HANDED_ON:
            _writeback_copy(w, w_v, w_bf_hbm, out_sem).wait()


def _prompt_tile(t, sinks_ref, x_ref, params_ref, bias_ref,
                 w_in_ref, wfold_ref, wap_ref, wout_ref, wg_ref, wu_ref, wd_ref,
                 y_ref, knew_ref, vnew_ref, pnew_ref,
                 h_s, u_s, q_s, kd_s, vd_s, s_s, p_s, ypool_s, yattn_s, sga_s, sgb_s, merged_s, x1_s, h2_s, act_s):
    T = SEQ_TILE
    R = SUB_TILE
    subs = [slice(r0, r0 + R) for r0 in range(0, T, R)]
    n_blocks = T // ATT_BLOCK
    j = pl.program_id(1) * TILES_PER_STEP + t
    rows_per_kv = GROUP * ATT_BLOCK

    def in_block(rows):
        return pl.ds(pl.multiple_of(t * T + rows.start, R), R)

    @pl.when(j == 0)
    def _():
        u_s[0:PREFIX, :] = jnp.zeros((PREFIX, POOL_WIDTH), F32)
        kd_s[:, 0:ATT_BLOCK, :] = jnp.zeros((N_KV_HEADS, ATT_BLOCK, KV_WIDTH), BF16)
        vd_s[:, 0:ATT_BLOCK, :] = jnp.zeros((N_KV_HEADS, ATT_BLOCK, KV_WIDTH), BF16)

    lane_r = lax.broadcasted_iota(jnp.int32, (R, LANES), 1)
    lane = lax.broadcasted_iota(jnp.int32, (ATT_BLOCK, LANES), 1)

    def norm1(rows):
        x = x_ref[0, in_block(rows), :]
        h_s[rows, :] = ((x * _rms_scale(x)) * _param(params_ref, ROW_N1, D_MODEL)).astype(BF16)

    def in_proj(rows):
        h = h_s[rows, :]
        u_s[PREFIX + rows.start:PREFIX + rows.stop, :] = _dot(h, w_in_ref[:, OFF_U:OFF_U + POOL_WIDTH])
        q = _dot(h, w_in_ref[:, OFF_Q:OFF_Q + Q_WIDTH])
        qn = _head_norm(q, _param(params_ref, ROW_QN, Q_WIDTH)).astype(BF16)
        for p in range(Q_COLS):
            qcol = qn[:, p * LANES:(p + 1) * LANES]
            lo, hi = 2 * p * LANES, (2 * p + 1) * LANES
            q_s[rows, lo:lo + LANES] = jnp.where(lane_r < HEAD_DIM, qcol, jnp.zeros_like(qcol))
            q_s[rows, hi:hi + LANES] = jnp.where(lane_r >= HEAD_DIM, qcol, jnp.zeros_like(qcol))
        kv = _dot(h, w_in_ref[:, OFF_K:OFF_K + 2 * KV_WIDTH])
        kn = _head_norm(kv[:, 0:KV_WIDTH], _param(params_ref, ROW_KN, KV_WIDTH))
        v = kv[:, KV_WIDTH:]
        dst_rows = slice(ATT_BLOCK + rows.start, ATT_BLOCK + rows.stop)
        for src, dst in ((kn, kd_s), (v, vd_s)):
            swapped = pltpu.roll(src, HEAD_DIM, 1)
            dst[0, dst_rows, :] = jnp.where(lane_r < HEAD_DIM, src, swapped).astype(BF16)
            dst[1, dst_rows, :] = jnp.where(lane_r < HEAD_DIM, swapped, src).astype(BF16)
        if rows.stop == T:
            knew_ref[0] = kn[R - ATT_BLOCK:, :].T
            vnew_ref[0] = v[R - ATT_BLOCK:, :].T
            batch = pl.program_id(0)
            for r in range(POOL_BUF):
                src = PREFIX + T - POOL_BUF + r
                pnew_ref[r, pl.ds(batch, 1), :] = u_s[src:src + 1, :]

    def pool(rows):
        pos1 = j * T + rows.start + lax.broadcasted_iota(jnp.int32, (R, 1), 0) + 1
        for g, w in enumerate(POOL_WINDOWS):
            cols = slice(g * POOL_GROUP_WIDTH, (g + 1) * POOL_GROUP_WIDTH)
            a = u_s[rows.start:rows.stop + PREFIX, cols]
            s = a
            shift = 1
            while shift < w:
                s = s + pltpu.roll(s, shift, 0)
                shift *= 2
            inv_cnt = 1.0 / jnp.minimum(pos1, w).astype(F32)
            ypool_s[rows, cols] = (s[PREFIX:, :] * inv_cnt - a[PREFIX:, :]).astype(BF16)

    def gates(rows, chunks):
        h = h_s[rows, :]
        for c in chunks:
            _gate_chunk(h, c, w_in_ref, sga_s, sgb_s, rows)

    def scores(b):
        r0 = b * ATT_BLOCK
        for c in range(N_KV_HEADS):
            parts = [q_s[r0:r0 + ATT_BLOCK, head * LANES:(head + 1) * LANES]
                     for head in range(c * GROUP, (c + 1) * GROUP)]
            s_s[b % 2, c * rows_per_kv:(c + 1) * rows_per_kv, :] = _dot_nt(
                jnp.concatenate(parts, axis=0), kd_s[c, r0:r0 + 2 * ATT_BLOCK, :])

    def softmax(b):
        bias = bias_ref[jnp.where(j == 0, 1, 0)] if b == 0 else bias_ref[0]
        inv_denoms = []
        for head in range(N_HEADS):
            sh = s_s[b % 2, head * ATT_BLOCK:(head + 1) * ATT_BLOCK, :] + bias
            sink = sinks_ref[head] * LOG2_E
            m = jnp.maximum(jnp.max(sh, axis=-1, keepdims=True), sink)
            p = jnp.exp2(sh - m)
            denom = jnp.sum(p, axis=-1, keepdims=True) + jnp.exp2(sink - m)
            p_s[b % 2, head * ATT_BLOCK:(head + 1) * ATT_BLOCK, :] = p.astype(BF16)
            inv_denoms.append(1.0 / denom)
        return inv_denoms

    def weighted_values(b, inv_denoms):
        r0 = b * ATT_BLOCK
        for c in range(N_KV_HEADS):
            o = _dot(p_s[b % 2, c * rows_per_kv:(c + 1) * rows_per_kv, :], vd_s[c, r0:r0 + 2 * ATT_BLOCK, :])
            for pp in range(Q_COLS // N_KV_HEADS):
                p_col = c * Q_COLS // N_KV_HEADS + pp
                head_lo = p_col * HEADS_PER_COL
                o_lo = o[(2 * pp) * ATT_BLOCK:(2 * pp + 1) * ATT_BLOCK, :] * inv_denoms[head_lo]
                o_hi = o[(2 * pp + 1) * ATT_BLOCK:(2 * pp + 2) * ATT_BLOCK, :] * inv_denoms[head_lo + 1]
                yattn_s[r0:r0 + ATT_BLOCK, p_col * LANES:(p_col + 1) * LANES] = (
                    jnp.where(lane < HEAD_DIM, o_lo, o_hi).astype(BF16))

    def out_proj(rows):
        _merge(ypool_s[rows, :], yattn_s[rows, :], sga_s, sgb_s, wfold_ref, wap_ref, merged_s, rows)
        x1 = x_ref[0, in_block(rows), :] + _dot(merged_s[rows, :], wout_ref[...])
        x1_s[rows, :] = x1
        h2_s[rows, :] = ((x1 * _rms_scale(x1)) * _param(params_ref, ROW_N2, D_MODEL)).astype(BF16)

    for rows in subs:
        norm1(rows)
    for rows in subs:
        in_proj(rows)
    pool(subs[0])
    gates(subs[0], range(N_GATE_CHUNKS))
    for rows in subs[1:]:
        pool(rows)
    later_gates = [(rows, c) for rows in subs[1:] for c in range(N_GATE_CHUNKS)]
    per_block = -(-len(later_gates) // n_blocks)
    scores(0)
    for b in range(n_blocks):
        if b + 1 < n_blocks:
            scores(b + 1)
        for rows, c in later_gates[b * per_block:(b + 1) * per_block]:
            gates(rows, [c])
        weighted_values(b, softmax(b))

    u_s[0:PREFIX, :] = u_s[T:T + PREFIX, :]
    kd_s[:, 0:ATT_BLOCK, :] = kd_s[:, T:T + ATT_BLOCK, :]
    vd_s[:, 0:ATT_BLOCK, :] = vd_s[:, T:T + ATT_BLOCK, :]

    for rows in subs:
        out_proj(rows)
    for rows in subs:
        _ffn_act(h2_s[rows, :], wg_ref, wu_ref, act_s, rows)
    for rows in subs:
        y_ref[0, in_block(rows), :] = x1_s[rows, :] + _dot(act_s[rows, :], wd_ref[...])


def _const_spec(shape):
    nd = len(shape)
    return pl.BlockSpec(shape, lambda *_: (0,) * nd, pipeline_mode=pl.Buffered(1))


def _band_bias():
    row = np.arange(ATT_BLOCK)[:, None]
    col = np.arange(2 * ATT_BLOCK)[None, :]
    band = (col >= row) & (col <= row + WINDOW)
    masks = np.stack([band, band & (col >= ATT_BLOCK)])
    return jnp.asarray(np.where(masks, 0.0, NEG), dtype=F32)


def _prompt_call(x, sinks, params, big):
    B, S, _ = x.shape
    T = SEQ_TILE
    step_rows = T * TILES_PER_STEP
    assert tuple(b.shape for b in big) == BF16_SHAPES[:len(BIG_WEIGHTS)]
    assert B * (S // step_rows) >= 2, "the bf16 write-back starts on the second grid step"
    bias = _band_bias()
    in_specs = [pl.BlockSpec(memory_space=pltpu.SMEM),
                pl.BlockSpec((1, step_rows, D_MODEL), lambda b, j: (b, j, 0)),
                _const_spec(params.shape),
                _const_spec(bias.shape)]
    in_specs += [pl.BlockSpec(memory_space=pl.ANY) for _ in big]
    out_shape = (jax.ShapeDtypeStruct((B, S, D_MODEL), F32),
                 jax.ShapeDtypeStruct((B, KV_WIDTH, ATT_BLOCK), F32),
                 jax.ShapeDtypeStruct((B, KV_WIDTH, ATT_BLOCK), F32),
                 jax.ShapeDtypeStruct((POOL_BUF, B, POOL_WIDTH), F32))
    out_shape += tuple(jax.ShapeDtypeStruct(BF16_SHAPES[w], BF16) for w in HANDED_ON)
    out_specs = (pl.BlockSpec((1, step_rows, D_MODEL), lambda b, j: (b, j, 0)),
                 pl.BlockSpec((1, KV_WIDTH, ATT_BLOCK), lambda b, j: (b, 0, 0)),
                 pl.BlockSpec((1, KV_WIDTH, ATT_BLOCK), lambda b, j: (b, 0, 0)),
                 pl.BlockSpec((POOL_BUF, B, POOL_WIDTH), lambda b, j: (0, 0, 0)))
    out_specs += tuple(pl.BlockSpec(memory_space=pl.ANY) for _ in HANDED_ON)
    scratch = [pltpu.VMEM(shape, BF16) for shape in BF16_SHAPES]
    scratch += [pltpu.VMEM((STAGE_SLOTS, CONVERT_ROWS, STAGE_COLS), F32),
                pltpu.SemaphoreType.DMA((STAGE_SLOTS,)),
                pltpu.SemaphoreType.DMA((len(BF16_SHAPES),)),
                pltpu.VMEM(BF16_SHAPES[W_MIX], F32)]
    scratch += [pltpu.VMEM((T, D_MODEL), BF16),
               pltpu.VMEM((PREFIX + T, POOL_WIDTH), F32),
               pltpu.VMEM((T, N_HEADS * LANES), BF16),
               pltpu.VMEM((N_KV_HEADS, ATT_BLOCK + T, KV_WIDTH), BF16),
               pltpu.VMEM((N_KV_HEADS, ATT_BLOCK + T, KV_WIDTH), BF16),
               pltpu.VMEM((2, N_HEADS * ATT_BLOCK, 2 * ATT_BLOCK), F32),
               pltpu.VMEM((2, N_HEADS * ATT_BLOCK, 2 * ATT_BLOCK), BF16),
               pltpu.VMEM((T, POOL_WIDTH), BF16),
               pltpu.VMEM((T, Q_WIDTH), BF16),
               pltpu.VMEM((T, D_MODEL), F32),
               pltpu.VMEM((T, D_MODEL), F32),
               pltpu.VMEM((T, D_MODEL), BF16),
               pltpu.VMEM((T, D_MODEL), F32),
               pltpu.VMEM((T, D_MODEL), BF16),
               pltpu.VMEM((T, D_FF), BF16)]
    return pl.pallas_call(
        _prompt_kernel,
        grid=(B, S // step_rows),
        in_specs=in_specs,
        out_specs=out_specs,
        out_shape=out_shape,
        scratch_shapes=scratch,
        compiler_params=pltpu.CompilerParams(
            dimension_semantics=("arbitrary", "arbitrary"),
            vmem_limit_bytes=VMEM_LIMIT_BYTES),
        name="prompt_layer",
    )(sinks, x, params, bias, *big)


def _sample_kernel(*refs):
    nh = len(HANDED_ON)
    n_in = 6
    (x_ref, ck_hbm, cv_hbm, sp_ref, sinks_ref, params_ref) = refs[:n_in]
    w_hbm = refs[n_in:n_in + nh]
    pos = n_in + nh
    y_ref, knew_ref, vnew_ref, pnew_ref = refs[pos:pos + 4]
    pos += 4
    w_v = refs[pos:pos + nh]
    pos += nh
    (w_sem, u_s, kt_s, vt_s, q8_s, k8_s, v8_s, o8_s, ypre_s, sga_s, sgb_s, merged_s, act_s,
     ck_ring, cv_ring, c_sem) = refs[pos:]
    w_in_ref, wap_ref, wout_ref, wg_ref, wu_ref, wd_ref, wfold_ref = w_v

    SB = SAMPLE_BLOCK
    col_tiles = D_MODEL // LANES
    N = x_ref.shape[0] // col_tiles
    nsteps = N // SB
    i = pl.program_id(0)
    lane = lax.broadcasted_iota(jnp.int32, (N, LANES), 1)
    all_rows = slice(0, N)

    def load_x():
        return jnp.concatenate([x_ref[pl.ds(ct, N, stride=col_tiles), :] for ct in range(col_tiles)], axis=1)

    def weight_copy(k):
        return pltpu.make_async_copy(w_hbm[k], w_v[k], w_sem.at[k])

    def cache_copies(block, slot):
        src = pl.ds(block * SB, SB)
        return (pltpu.make_async_copy(ck_hbm.at[src], ck_ring.at[slot], c_sem.at[0, slot]),
                pltpu.make_async_copy(cv_hbm.at[src], cv_ring.at[slot], c_sem.at[1, slot]))

    @pl.when(i == 0)
    def _():
        for b in range(min(CACHE_RING, nsteps)):
            for copy in cache_copies(b, b):
                copy.start()
        for k in range(nh):
            weight_copy(k).start()
        weight_copy(0).wait()
        x = load_x()
        h = ((x * _rms_scale(x)) * _param(params_ref, ROW_N1, D_MODEL)).astype(BF16)
        u_s[...] = _dot(h, w_in_ref[:, OFF_U:OFF_U + POOL_WIDTH])
        for gc in range(N_GATE_CHUNKS):
            _gate_chunk(h, gc, w_in_ref, sga_s, sgb_s, all_rows)
        q = _head_norm(_dot(h, w_in_ref[:, OFF_Q:OFF_Q + Q_WIDTH]), _param(params_ref, ROW_QN, Q_WIDTH))
        kv = _dot(h, w_in_ref[:, OFF_K:OFF_K + 2 * KV_WIDTH])
        kn = _head_norm(kv[:, 0:KV_WIDTH], _param(params_ref, ROW_KN, KV_WIDTH))
        v = kv[:, KV_WIDTH:]
        kt_s[...] = kn.T
        vt_s[...] = v.T
        for head in range(N_HEADS):
            k8_s[pl.ds(head, N, stride=N_HEADS), :] = kn
            v8_s[pl.ds(head, N, stride=N_HEADS), :] = v
        for p in range(Q_COLS):
            qcol = q[:, p * LANES:(p + 1) * LANES]
            qswap = pltpu.roll(qcol, HEAD_DIM, 1)
            for e in range(HEADS_PER_COL):
                head = p * HEADS_PER_COL + e
                c = _kv_head(head)
                src = qcol if e == c else qswap
                keep = (lane < HEAD_DIM) if c == 0 else (lane >= HEAD_DIM)
                q8_s[pl.ds(head, N, stride=N_HEADS), :] = jnp.where(keep, src, 0.0)

    base = i * SB
    rows = pl.ds(pl.multiple_of(base, SB), SB)

    ahead = i + (CACHE_RING - 1)

    @pl.when((i > 0) & (ahead < nsteps))
    def _():
        for copy in cache_copies(ahead, lax.rem(ahead, CACHE_RING)):
            copy.start()

    slot = lax.rem(i, CACHE_RING)
    for copy in cache_copies(i, slot):
        copy.wait()
    ck_ref = ck_ring.at[slot]
    cv_ref = cv_ring.at[slot]

    unew = u_s[rows, :]
    for g, w in enumerate(POOL_WINDOWS):
        cols = slice(g * POOL_GROUP_WIDTH, (g + 1) * POOL_GROUP_WIDTH)
        s = unew[:, cols]
        for r in range(POOL_BUF - (w - 1), POOL_BUF):
            s = s + sp_ref[r, :, cols]
        ypre_s[rows, cols] = (s * (1.0 / w) - unew[:, cols]).astype(BF16)
    pnew_ref[0:POOL_BUF - 1] = sp_ref[1:POOL_BUF]
    pnew_ref[POOL_BUF - 1] = unew

    hrows = pl.ds(pl.multiple_of(base * N_HEADS, SB * N_HEADS), SB * N_HEADS)
    lhs = q8_s[hrows, :]
    sc = jnp.concatenate(
        [_dot(lhs[bb * N_HEADS:(bb + 1) * N_HEADS, :].astype(BF16), ck_ref[bb].astype(BF16)) for bb in range(SB)],
        axis=0)
    sc_self = jnp.sum(lhs * k8_s[hrows, :], axis=-1, keepdims=True)
    row_head = lax.rem(lax.broadcasted_iota(jnp.int32, (SB * N_HEADS, 1), 0), N_HEADS)
    sink = jnp.zeros((SB * N_HEADS, 1), F32)
    for head in range(N_HEADS):
        sink = jnp.where(row_head == head, sinks_ref[head] * LOG2_E, sink)
    m = jnp.maximum(jnp.maximum(jnp.max(sc, axis=-1, keepdims=True), sc_self), sink)
    p = jnp.exp2(sc - m)
    p_self = jnp.exp2(sc_self - m)
    denom = jnp.sum(p, axis=-1, keepdims=True) + p_self + jnp.exp2(sink - m)
    inv = 1.0 / denom
    pn = p * inv
    o = jnp.concatenate(
        [_dot_nt(pn[bb * N_HEADS:(bb + 1) * N_HEADS, :].astype(BF16), cv_ref[bb].astype(BF16)) for bb in range(SB)],
        axis=0)
    o8_s[hrows, :] = o + (p_self * inv) * v8_s[hrows, :]

    lane_c = lax.broadcasted_iota(jnp.int32, (KV_WIDTH, WINDOW), 1)
    kt_all = kt_s[...]
    vt_all = vt_s[...]
    for bb in range(SB):
        put = (WINDOW - 1) - (base + bb)
        knew_ref[bb] = jnp.where(lane_c == WINDOW - 1, pltpu.roll(kt_all, put, 1),
                                 pltpu.roll(ck_ref[bb], WINDOW - 1, 1))
        vnew_ref[bb] = jnp.where(lane_c == WINDOW - 1, pltpu.roll(vt_all, put, 1),
                                 pltpu.roll(cv_ref[bb], WINDOW - 1, 1))

    @pl.when(i == nsteps - 1)
    def _():
        for k in range(1, nh):
            weight_copy(k).wait()
        ycols = []
        for p in range(Q_COLS):
            halves = []
            for e in range(HEADS_PER_COL):
                head = p * HEADS_PER_COL + e
                oh = o8_s[pl.ds(head, N, stride=N_HEADS), :]
                halves.append(oh if e == _kv_head(head) else pltpu.roll(oh, HEAD_DIM, 1))
            ycols.append(jnp.where(lane < HEAD_DIM, halves[0], halves[1]).astype(BF16))
        yattn = jnp.concatenate(ycols, axis=1)
        _merge(ypre_s[...], yattn, sga_s, sgb_s, wfold_ref, wap_ref, merged_s, all_rows)
        x1 = load_x() + _dot(merged_s[...], wout_ref[...])
        h2 = ((x1 * _rms_scale(x1)) * _param(params_ref, ROW_N2, D_MODEL)).astype(BF16)
        _ffn_act(h2, wg_ref, wu_ref, act_s, all_rows)
        y = x1 + _dot(act_s[...], wd_ref[...])
        for ct in range(col_tiles):
            y_ref[pl.ds(ct, N, stride=col_tiles), :] = y[:, ct * LANES:(ct + 1) * LANES]


def _sample_call(x, ck, cv, sp, sinks, params, big):
    N = x.shape[0] // (D_MODEL // LANES)
    SB = SAMPLE_BLOCK
    W = WINDOW
    assert tuple(b.shape for b in big) == tuple(BF16_SHAPES[w] for w in HANDED_ON)
    consts = (sinks, params)
    in_specs = [_const_spec(x.shape),
                pl.BlockSpec(memory_space=pl.ANY),
                pl.BlockSpec(memory_space=pl.ANY),
                pl.BlockSpec((POOL_BUF, SB, POOL_WIDTH), lambda i: (0, i, 0)),
                pl.BlockSpec(memory_space=pltpu.SMEM),
                _const_spec(params.shape)]
    in_specs += [pl.BlockSpec(memory_space=pl.ANY) for _ in big]
    out_shape = (jax.ShapeDtypeStruct(x.shape, F32),
                 jax.ShapeDtypeStruct((N, KV_WIDTH, W), F32),
                 jax.ShapeDtypeStruct((N, KV_WIDTH, W), F32),
                 jax.ShapeDtypeStruct((POOL_BUF, N, POOL_WIDTH), F32))
    out_specs = (pl.BlockSpec(x.shape, lambda i: (0, 0)),
                 pl.BlockSpec((SB, KV_WIDTH, W), lambda i: (i, 0, 0)),
                 pl.BlockSpec((SB, KV_WIDTH, W), lambda i: (i, 0, 0)),
                 pl.BlockSpec((POOL_BUF, SB, POOL_WIDTH), lambda i: (0, i, 0)))
    scratch = [pltpu.VMEM(b.shape, BF16) for b in big]
    scratch += [pltpu.SemaphoreType.DMA((len(big),)),
                pltpu.VMEM((N, POOL_WIDTH), F32),
               pltpu.VMEM((KV_WIDTH, N), F32),
               pltpu.VMEM((KV_WIDTH, N), F32),
               pltpu.VMEM((N * N_HEADS, LANES), F32),
               pltpu.VMEM((N * N_HEADS, LANES), F32),
               pltpu.VMEM((N * N_HEADS, LANES), F32),
               pltpu.VMEM((N * N_HEADS, LANES), F32),
               pltpu.VMEM((N, POOL_WIDTH), BF16),
               pltpu.VMEM((N, D_MODEL), F32),
               pltpu.VMEM((N, D_MODEL), F32),
               pltpu.VMEM((N, D_MODEL), BF16),
               pltpu.VMEM((N, D_FF), BF16),
               pltpu.VMEM((CACHE_RING, SB, KV_WIDTH, W), F32),
               pltpu.VMEM((CACHE_RING, SB, KV_WIDTH, W), F32),
               pltpu.SemaphoreType.DMA((2, CACHE_RING))]
    return pl.pallas_call(
        _sample_kernel,
        grid=(N // SB,),
        in_specs=in_specs,
        out_specs=out_specs,
        out_shape=out_shape,
        scratch_shapes=scratch,
        compiler_params=pltpu.CompilerParams(
            dimension_semantics=("arbitrary",),
            vmem_limit_bytes=VMEM_LIMIT_BYTES),
        name="sample_layer",
    )(x, ck, cv, sp, *consts, *big)


def _cache_to_device_order(c):
    n, w = c.shape[0], c.shape[1]
    return jnp.transpose(c, (0, 2, 3, 1)).reshape(n, KV_WIDTH, w)


def _cache_from_device_order(c):
    n, _, w = c.shape
    return jnp.transpose(c.reshape(n, N_KV_HEADS, HEAD_DIM, w), (0, 3, 1, 2))[None]


@jax.jit
def _forward(x_prompt, x_sample, cache_k, cache_v, state_pool, norm1, w_in, q_norm, k_norm, sinks,
             pool_mix_w, pool_scale, w_pool_proj, w_attn_proj, w_out, norm2, w_gate, w_up, w_down):
    depth = w_in.shape[0]
    assert depth == 1, "single-layer trunk"
    l = 0
    N = x_sample.shape[0]
    assert cache_k.shape[2] == WINDOW and x_sample.shape[1] == 1

    col = lax.broadcasted_iota(jnp.int32, (1, D_MODEL), 1)
    row = lax.broadcasted_iota(jnp.int32, (PARAM_ROWS, 1), 0)

    def widened(v, width):
        return jnp.where(col < width, jnp.tile(v, D_MODEL // v.shape[0])[None, :], 0.0)

    rows = {ROW_N1: norm1[l][None, :], ROW_N2: norm2[l][None, :],
            ROW_QN: widened(q_norm[l], Q_WIDTH) * (HEAD_DIM ** -0.5 * LOG2_E),
            ROW_KN: widened(k_norm[l], KV_WIDTH), ROW_PSCALE: widened(pool_scale[l], POOL_WIDTH)}
    params = jnp.zeros((PARAM_ROWS, D_MODEL), F32)
    for r, v in rows.items():
        params = jnp.where(row == r, v, params)
    big = (w_in[l], pool_mix_w[l].reshape(len(POOL_WINDOWS) * POOL_GROUP_WIDTH, POOL_GROUP_WIDTH),
           w_pool_proj[l], w_attn_proj[l], w_out[l], w_gate[l], w_up[l], w_down[l])

    y_p, k_p, v_p, pool_p, *weights_bf16 = _prompt_call(x_prompt, sinks[l], params, big)

    y_s, k_s, v_s, pool_s = _sample_call(
        x_sample.reshape(N * (D_MODEL // LANES), LANES),
        _cache_to_device_order(cache_k[l]), _cache_to_device_order(cache_v[l]),
        jnp.transpose(state_pool[l], (1, 0, 2)), sinks[l], params, tuple(weights_bf16))

    return (y_p, y_s.reshape(N, 1, D_MODEL),
            _cache_from_device_order(k_p), _cache_from_device_order(v_p), jnp.transpose(pool_p, (1, 0, 2))[None],
            _cache_from_device_order(k_s), _cache_from_device_order(v_s),
            jnp.transpose(pool_s, (1, 0, 2))[None])


def kernel(x_prompt, x_sample, cache_k, cache_v, state_pool, norm1, w_in, q_norm, k_norm, sinks, pool_mix_w,
           pool_scale, w_pool_proj, w_attn_proj, w_out, norm2, w_gate, w_up, w_down):
    return _forward(x_prompt, x_sample, cache_k, cache_v, state_pool, norm1, w_in, q_norm, k_norm, sinks,
                    pool_mix_w, pool_scale, w_pool_proj, w_attn_proj, w_out, norm2, w_gate, w_up, w_down)
```
